```python
import math
import jax, jax.numpy as jnp
from jax import lax
import numpy as np

D_MODEL = 1024
BATCH = 4
SEQ = 4096
DEPTH = 2

RMS_EPS = 1e-6
NEG_INF = -1e30

POOL_WIDTH = D_MODEL // 2
POOL_WINDOWS = (2, 4, 8, 16)
POOL_GROUPS = len(POOL_WINDOWS)
POOL_GW = POOL_WIDTH // POOL_GROUPS
RWKV_HEAD_DIM = 64
RWKV_WIDTH = D_MODEL // 2
RWKV_HEADS = RWKV_WIDTH // RWKV_HEAD_DIM
DECAY_LORA = 32
AAA_LORA = 32
GATE_LORA = 96
RWKV_GN_EPS = 64e-5
NSA_HEAD_DIM = 64
NSA_WIDTH = D_MODEL
NSA_HEADS = NSA_WIDTH // NSA_HEAD_DIM
NSA_KV_HEADS = 4
NSA_GROUP = NSA_HEADS // NSA_KV_HEADS
NSA_KV_WIDTH = NSA_KV_HEADS * NSA_HEAD_DIM
N_NSA_BRANCHES = 3
CMP_BLOCK = 32
CMP_STRIDE = 16
CMP_HIDDEN = 256
SLC_BLOCK = 64
N_SELECT = 8
WINDOW = 512
NSA_Q_BLOCK = 64
FORCE_BONUS = 1e3
N_BUCKETS = 32
MAX_EXACT = 16
MAX_DISTANCE = 128
N_BRANCHES = 3
POOL_COLS = POOL_WIDTH
RWKV_COLS = 3 * RWKV_WIDTH + DECAY_LORA + AAA_LORA + GATE_LORA
NSA_COLS = NSA_WIDTH + 2 * N_NSA_BRANCHES * NSA_KV_WIDTH + N_NSA_BRANCHES * NSA_HEADS
IN_COLS = POOL_COLS + RWKV_COLS + NSA_COLS
BRANCH_ROWS = POOL_WIDTH + RWKV_WIDTH + NSA_WIDTH
N_GROUPS = 4
EXPERTS_PER_GROUP = 8
N_EXPERTS = N_GROUPS * EXPERTS_PER_GROUP
TOP_K_EXPERTS = 2
D_EXPERT = 256

kernel_name = 'hybrid_pool_rwkv7_nsa_hmoe_block'


def rms_norm(x, g):
    xf = x.astype(jnp.float32)
    y = xf * lax.rsqrt(jnp.mean(xf * xf, axis=-1, keepdims=True) + RMS_EPS)
    return (y * g).astype(x.dtype)


def split_cols(t, sizes):
    return jnp.split(t, [int(o) for o in np.cumsum(sizes)[:-1]], axis=-1)


def masked_softmax(logits, mask):
    lf = jnp.where(mask, logits.astype(jnp.float32), NEG_INF)
    return jax.nn.softmax(lf, axis=-1) * mask


def t5_bucket(dist):
    n = jnp.maximum(dist, 0)
    nf = jnp.maximum(n, 1).astype(jnp.float32)
    large = MAX_EXACT + (jnp.log(nf / MAX_EXACT) / math.log(MAX_DISTANCE / MAX_EXACT)
                         * (N_BUCKETS - MAX_EXACT)).astype(jnp.int32)
    large = jnp.minimum(large, N_BUCKETS - 1)
    return jnp.where(n < MAX_EXACT, n, large)


def pool_mixer(u, w_grp, scale):
    B, S, _ = u.shape
    uf = u.astype(jnp.float32)
    cs = jnp.concatenate([jnp.zeros((B, 1, POOL_WIDTH), jnp.float32), jnp.cumsum(uf, axis=1)], axis=1)
    t = jnp.arange(S)
    means = []
    for gi, win in enumerate(POOL_WINDOWS):
        c = cs[..., gi * POOL_GW:(gi + 1) * POOL_GW]
        lo = jnp.maximum(t + 1 - win, 0)
        cnt = (t + 1 - lo).astype(jnp.float32)
        means.append((c[:, t + 1] - c[:, lo]) / cnt[None, :, None])
    pooled = (jnp.stack(means, axis=2) - uf.reshape(B, S, POOL_GROUPS, POOL_GW)).astype(u.dtype)
    mixed = jnp.einsum('bsgc,gcd->bsgd', pooled, w_grp)
    return mixed.reshape(B, S, POOL_WIDTH) * scale


def rwkv7_mixer(u, mu, w0, w_up, a0, a_up, g_up, k_k, k_a, r_k, gn_w, gn_b):
    B, S, _ = u.shape
    H, N = RWKV_HEADS, RWKV_HEAD_DIM
    u_prev = jnp.pad(u, ((0, 0), (1, 0), (0, 0)))[:, :-1]
    u = u + (u_prev - u) * mu
    r, k, v, wd, ad, gd = split_cols(u, [RWKV_WIDTH] * 3 + [DECAY_LORA, AAA_LORA, GATE_LORA])
    w_log = -jax.nn.softplus(-(w0 + jnp.tanh(wd) @ w_up)) - 0.5
    decay = jnp.exp(-jnp.exp(w_log.astype(jnp.float32)))
    a = jax.nn.sigmoid(a0 + ad @ a_up)
    g = jax.nn.sigmoid(gd) @ g_up
    heads = lambda t: t.reshape(B, S, H, N).astype(jnp.float32)
    kk = heads(k * k_k)
    kk = kk / jnp.maximum(jnp.sqrt(jnp.sum(kk * kk, axis=-1, keepdims=True)), 1e-12)
    k = k * (1.0 + (a - 1.0) * k_a)
    r_h, k_h, v_h, w_h, a_h = heads(r), heads(k), heads(v), heads(decay), heads(a)

    def step(state, inp):
        r_t, w_t, k_t, v_t, kk_t, a_t = inp
        sa = jnp.einsum('bhvk,bhk->bhv', state, -kk_t)
        state = (state * w_t[:, :, None, :] + sa[..., None] * (kk_t * a_t)[:, :, None, :]
                 + v_t[..., None] * k_t[:, :, None, :])
        return state, jnp.einsum('bhvk,bhk->bhv', state, r_t)

    xs = tuple(jnp.moveaxis(t, 1, 0) for t in (r_h, w_h, k_h, v_h, kk, a_h))
    _, y = lax.scan(step, jnp.zeros((B, H, N, N), jnp.float32), xs)
    y = jnp.moveaxis(y, 0, 1)
    mean = jnp.mean(y, axis=-1, keepdims=True)
    var = jnp.mean(jnp.square(y - mean), axis=-1, keepdims=True)
    yn = (y - mean) * lax.rsqrt(var + RWKV_GN_EPS) * gn_w.reshape(H, N) + gn_b.reshape(H, N)
    bonus = jnp.sum(r_h * k_h * r_k, axis=-1, keepdims=True) * v_h
    return ((yn + bonus).reshape(B, S, RWKV_WIDTH) * g).astype(u.dtype)


def nsa_mixer(q, k_c, v_c, k_s, v_s, k_w, v_w, gate_logits, rel_bias,
              pos_k, w1_k, w2_k, pos_v, w1_v, w2_v):
    B, S, _ = q.shape
    G, Hg, dh = NSA_KV_HEADS, NSA_GROUP, NSA_HEAD_DIM
    qh = q.reshape(B, S, G, Hg, dh).transpose(0, 2, 3, 1, 4) * (dh ** -0.5)
    kv = lambda t: t.reshape(B, S, G, dh).transpose(0, 2, 1, 3)
    k_c, v_c, k_s, v_s, k_w, v_w = (kv(t) for t in (k_c, v_c, k_s, v_s, k_w, v_w))
    gates = jax.nn.sigmoid(gate_logits.reshape(B, S, G, Hg, N_NSA_BRANCHES)).transpose(0, 2, 3, 1, 4)

    n_cmp = (S - CMP_BLOCK) // CMP_STRIDE + 1
    cmp_start = np.arange(n_cmp) * CMP_STRIDE
    cmp_idx = cmp_start[:, None] + np.arange(CMP_BLOCK)[None, :]
    cmp_end = jnp.asarray(cmp_start + CMP_BLOCK - 1, jnp.int32)

    def compress(t, pos, w1, w2):
        blocks = (t[:, :, cmp_idx] + pos).reshape(B, G, n_cmp, CMP_BLOCK * dh)
        return jax.nn.gelu(blocks @ w1) @ w2

    kc_blk = compress(k_c, pos_k, w1_k, w2_k)
    vc_blk = compress(v_c, pos_v, w1_v, w2_v)

    n_slc = S // SLC_BLOCK
    n_sel = min(N_SELECT, n_slc)
    slc_start = np.arange(n_slc) * SLC_BLOCK
    overlap = jnp.asarray(((cmp_start[:, None] <= slc_start[None, :] + SLC_BLOCK - 1)
                           & (cmp_start[:, None] + CMP_BLOCK - 1 >= slc_start[None, :])).astype(np.float32))
    ks_blk = k_s.reshape(B, G, n_slc, SLC_BLOCK, dh)
    vs_blk = v_s.reshape(B, G, n_slc, SLC_BLOCK, dh)

    kw_pad = jnp.pad(k_w, ((0, 0), (0, 0), (WINDOW, 0), (0, 0)))
    vw_pad = jnp.pad(v_w, ((0, 0), (0, 0), (WINDOW, 0), (0, 0)))

    tbl_g = rel_bias.reshape(N_BUCKETS, G, Hg).transpose(1, 0, 2)
    b_idx = jnp.arange(B)[:, None, None, None]
    g_idx = jnp.arange(G)[None, :, None, None]

    def shared_bias(dist):
        return rel_bias[t5_bucket(dist)].reshape(dist.shape + (G, Hg)).transpose(2, 3, 0, 1)

    def block(i):
        q0 = i * NSA_Q_BLOCK
        t = q0 + jnp.arange(NSA_Q_BLOCK, dtype=jnp.int32)
        qb = lax.dynamic_slice_in_dim(qh, q0, NSA_Q_BLOCK, axis=3)
        gb = lax.dynamic_slice_in_dim(gates, q0, NSA_Q_BLOCK, axis=3)
        dist_c = t[:, None] - cmp_end[None, :]
        lc = jnp.einsum('bghqd,bgcd->bghqc', qb, kc_blk) + shared_bias(dist_c)
        pc = masked_softmax(lc, dist_c >= 0).astype(vc_blk.dtype)
        o_c = jnp.einsum('bghqc,bgcd->bghqd', pc, vc_blk)
        imp = jnp.einsum('bghqc,cn->bgqn', pc, overlap)
        blk = jnp.arange(n_slc)[None, :]
        cur = (t // SLC_BLOCK)[:, None]
        valid = blk <= cur
        forced = (blk == 0) | (blk == cur) | (blk == cur - 1)
        score = jnp.where(valid, imp + jnp.where(forced, FORCE_BONUS, 0.0), -1.0)
        top_val, top_idx = lax.top_k(score, n_sel)
        k_sel = ks_blk[b_idx, g_idx, top_idx].reshape(B, G, NSA_Q_BLOCK, n_sel * SLC_BLOCK, dh)
        v_sel = vs_blk[b_idx, g_idx, top_idx].reshape(B, G, NSA_Q_BLOCK, n_sel * SLC_BLOCK, dh)
        pos_s = (top_idx[..., None] * SLC_BLOCK + jnp.arange(SLC_BLOCK)).reshape(B, G, NSA_Q_BLOCK, n_sel * SLC_BLOCK)
        dist_s = t[None, None, :, None] - pos_s
        mask_s = jnp.repeat(top_val >= 0, SLC_BLOCK, axis=-1) & (dist_s >= 0)
        bias_s = tbl_g[g_idx, t5_bucket(dist_s)].transpose(0, 1, 4, 2, 3)
        ls = jnp.einsum('bghqd,bgqkd->bghqk', qb, k_sel) + bias_s
        ps = masked_softmax(ls, mask_s[:, :, None]).astype(v_sel.dtype)
        o_s = jnp.einsum('bghqk,bgqkd->bghqd', ps, v_sel)
        kwb = lax.dynamic_slice_in_dim(kw_pad, q0, NSA_Q_BLOCK + WINDOW, axis=2)
        vwb = lax.dynamic_slice_in_dim(vw_pad, q0, NSA_Q_BLOCK + WINDOW, axis=2)
        pos_w = q0 - WINDOW + jnp.arange(NSA_Q_BLOCK + WINDOW, dtype=jnp.int32)
        dist_w = t[:, None] - pos_w[None, :]
        mask_w = (dist_w >= 0) & (dist_w < WINDOW) & (pos_w[None, :] >= 0)
        lw = jnp.einsum('bghqd,bgkd->bghqk', qb, kwb) + shared_bias(dist_w)
        pw = masked_softmax(lw, mask_w).astype(vwb.dtype)
        o_w = jnp.einsum('bghqk,bgkd->bghqd', pw, vwb)
        return gb[..., 0:1] * o_c + gb[..., 1:2] * o_s + gb[..., 2:3] * o_w

    out = lax.map(block, jnp.arange(S // NSA_Q_BLOCK, dtype=jnp.int32))
    return out.transpose(1, 0, 4, 2, 3, 5).reshape(B, S, NSA_WIDTH)


def hier_moe(h, w_rg, b_rg, w_re, b_re, w_gate, w_up, w_down):
    B, S, _ = h.shape
    p_grp = jax.nn.softmax((h @ w_rg + b_rg).astype(jnp.float32), axis=-1)
    p_top, grp = lax.top_k(p_grp, 1)
    le = (h @ w_re + b_re).astype(jnp.float32).reshape(B, S, N_GROUPS, EXPERTS_PER_GROUP)
    le_g = jnp.sum(le * jax.nn.one_hot(grp[..., 0], N_GROUPS)[..., None], axis=2)
    e_val, e_idx = lax.top_k(le_g, TOP_K_EXPERTS)
    w_e = jax.nn.softmax(e_val, axis=-1) * p_top
    expert_id = grp * EXPERTS_PER_GROUP + e_idx
    combine = jnp.sum(jax.nn.one_hot(expert_id, N_EXPERTS) * w_e[..., None], axis=2).astype(h.dtype)
    y = jnp.zeros_like(h)
    for e in range(N_EXPERTS):
        hid = jax.nn.silu(h @ w_gate[e]) * (h @ w_up[e])
        y = y + combine[..., e:e + 1] * (hid @ w_down[e])
    return y


def setup_inputs(seed: int = 0) -> dict:
    key = jax.random.key(seed)
    keys = iter(jax.random.split(key, 48))
    L, D = DEPTH, D_MODEL

    def nrm(shape, scale):
        return jax.random.normal(next(keys), shape, jnp.float32) * scale

    def gain(shape):
        return 1.0 + nrm(shape, 0.1)

    branch_scale = jnp.concatenate([jnp.full((POOL_WIDTH,), POOL_WIDTH ** -0.5, jnp.float32),
                                    jnp.full((RWKV_WIDTH,), RWKV_WIDTH ** -0.5, jnp.float32),
                                    jnp.full((NSA_WIDTH,), NSA_WIDTH ** -0.5, jnp.float32)])
    return {
        'x': nrm((BATCH, SEQ, D), 1.0),
        'rel_bias': nrm((N_BUCKETS, NSA_HEADS), 0.5),
        'norm_mix': gain((L, D)),
        'w_in': nrm((L, D, IN_COLS), D ** -0.5),
        'pool_w': nrm((L, POOL_GROUPS, POOL_GW, POOL_GW), POOL_GW ** -0.5),
        'pool_scale': gain((L, POOL_WIDTH)),
        'rw_mu': jax.random.uniform(next(keys), (L, RWKV_COLS), jnp.float32),
        'rw_w0': nrm((L, RWKV_WIDTH), 0.5) + 0.5,
        'rw_w_up': nrm((L, DECAY_LORA, RWKV_WIDTH), DECAY_LORA ** -0.5),
        'rw_a0': nrm((L, RWKV_WIDTH), 0.5),
        'rw_a_up': nrm((L, AAA_LORA, RWKV_WIDTH), AAA_LORA ** -0.5),
        'rw_g_up': nrm((L, GATE_LORA, RWKV_WIDTH), GATE_LORA ** -0.5),
        'rw_k_k': gain((L, RWKV_WIDTH)),
        'rw_k_a': gain((L, RWKV_WIDTH)),
        'rw_r_k': nrm((L, RWKV_HEADS, RWKV_HEAD_DIM), 0.1),
        'rw_gn_w': gain((L, RWKV_WIDTH)),
        'rw_gn_b': nrm((L, RWKV_WIDTH), 0.02),
        'cmp_pos_k': nrm((L, CMP_BLOCK, NSA_HEAD_DIM), 0.5),
        'cmp_w1_k': nrm((L, CMP_BLOCK * NSA_HEAD_DIM, CMP_HIDDEN), (CMP_BLOCK * NSA_HEAD_DIM) ** -0.5),
        'cmp_w2_k': nrm((L, CMP_HIDDEN, NSA_HEAD_DIM), CMP_HIDDEN ** -0.5),
        'cmp_pos_v': nrm((L, CMP_BLOCK, NSA_HEAD_DIM), 0.5),
        'cmp_w1_v': nrm((L, CMP_BLOCK * NSA_HEAD_DIM, CMP_HIDDEN), (CMP_BLOCK * NSA_HEAD_DIM) ** -0.5),
        'cmp_w2_v': nrm((L, CMP_HIDDEN, NSA_HEAD_DIM), CMP_HIDDEN ** -0.5),
        'w_branch': nrm((L, BRANCH_ROWS, D), 1.0) * branch_scale[None, :, None],
        'w_merge': nrm((L, D, N_BRANCHES * D), D ** -0.5),
        'b_merge': nrm((L, N_BRANCHES * D), 0.1),
        'w_out': nrm((L, D, D), D ** -0.5),
        'norm_ffn': gain((L, D)),
        'w_router_grp': nrm((L, D, N_GROUPS), D ** -0.5),
        'b_router_grp': nrm((L, N_GROUPS), 0.01),
        'w_router_exp': nrm((L, D, N_EXPERTS), D ** -0.5),
        'b_router_exp': nrm((L, N_EXPERTS), 0.01),
        'w_exp_gate': nrm((L, N_EXPERTS, D, D_EXPERT), D ** -0.5),
        'w_exp_up': nrm((L, N_EXPERTS, D, D_EXPERT), D ** -0.5),
        'w_exp_down': nrm((L, N_EXPERTS, D_EXPERT, D), D_EXPERT ** -0.5),
        'norm_final': gain((D,)),
    }


def reference(x, rel_bias, norm_mix, w_in, pool_w, pool_scale, rw_mu, rw_w0, rw_w_up, rw_a0,
              rw_a_up, rw_g_up, rw_k_k, rw_k_a, rw_r_k, rw_gn_w, rw_gn_b, cmp_pos_k, cmp_w1_k,
              cmp_w2_k, cmp_pos_v, cmp_w1_v, cmp_w2_v, w_branch, w_merge, b_merge, w_out, norm_ffn,
              w_router_grp, b_router_grp, w_router_exp, b_router_exp, w_exp_gate, w_exp_up,
              w_exp_down, norm_final):
    B, S, D = x.shape
    for l in range(DEPTH):
        h = rms_norm(x, norm_mix[l])
        proj = h @ w_in[l]
        u_pool, u_rwkv, u_nsa = split_cols(proj, [POOL_COLS, RWKV_COLS, NSA_COLS])
        y_pool = pool_mixer(u_pool, pool_w[l], pool_scale[l])
        y_rwkv = rwkv7_mixer(u_rwkv, rw_mu[l], rw_w0[l], rw_w_up[l], rw_a0[l], rw_a_up[l], rw_g_up[l],
                             rw_k_k[l], rw_k_a[l], rw_r_k[l], rw_gn_w[l], rw_gn_b[l])
        q, k_c, v_c, k_s, v_s, k_w, v_w, g_nsa = split_cols(
            u_nsa, [NSA_WIDTH] + [NSA_KV_WIDTH] * (2 * N_NSA_BRANCHES) + [N_NSA_BRANCHES * NSA_HEADS])
        y_nsa = nsa_mixer(q, k_c, v_c, k_s, v_s, k_w, v_w, g_nsa, rel_bias,
                          cmp_pos_k[l], cmp_w1_k[l], cmp_w2_k[l], cmp_pos_v[l], cmp_w1_v[l], cmp_w2_v[l])
        wb = w_branch[l]
        br_pool = y_pool @ wb[:POOL_WIDTH]
        br_rwkv = y_rwkv @ wb[POOL_WIDTH:POOL_WIDTH + RWKV_WIDTH]
        br_nsa = y_nsa @ wb[POOL_WIDTH + RWKV_WIDTH:]
        gates = jax.nn.sigmoid(h @ w_merge[l] + b_merge[l]).reshape(B, S, N_BRANCHES, D)
        merged = gates[:, :, 0] * br_pool + gates[:, :, 1] * br_rwkv + gates[:, :, 2] * br_nsa
        x = x + merged @ w_out[l]
        h = rms_norm(x, norm_ffn[l])
        x = x + hier_moe(h, w_router_grp[l], b_router_grp[l], w_router_exp[l], b_router_exp[l],
                         w_exp_gate[l], w_exp_up[l], w_exp_down[l])
    return rms_norm(x, norm_final)
```

```python
import functools
import math

import jax
import jax.numpy as jnp
import numpy as np
from jax import lax
from jax.experimental import pallas as pl
from jax.experimental.pallas import tpu as pltpu

F32 = jnp.float32
BF16 = jnp.bfloat16
HI = lax.Precision.HIGHEST

D_MODEL = 1024
RMS_EPS = 1e-6
NEG = -1e30

POOL_WINDOWS = (2, 4, 8, 16)
POOL_WIDTH = 512
POOL_GW = 128
POOL_HALO = 16

RW_HEADS = 8
RW_DH = 64
RW_WIDTH = 512
DECAY_LORA, AAA_LORA, GATE_LORA = 32, 32, 96
RW_LORA = DECAY_LORA + AAA_LORA + GATE_LORA
RW_COLS = 3 * RW_WIDTH + RW_LORA
RW_GN_EPS = 64e-5
RW_CHUNK = 64

NSA_DH = 64
NSA_HEADS = 16
NSA_G = 4
NSA_HG = 4
NSA_KVW = NSA_G * NSA_DH
CMP_BLOCK, CMP_STRIDE, CMP_HIDDEN = 32, 16, 256
SLC_BLOCK = 64
N_SELECT = 8
WINDOW = 512
QB = 64
FORCE_BONUS = 1e3
N_BUCKETS, MAX_EXACT, MAX_DISTANCE = 32, 16, 128
NEAR_BLOCKS = 3
FAR_CHUNK_BLOCKS = 4
CMP_NEAR = 32

N_GROUPS, EPG, N_EXPERTS, D_EXPERT = 4, 8, 32, 256
MOE_TM = 256

C_POOL, C_R, C_K, C_V, C_LORA, C_Q, C_KV, C_GATE, P_COLS = 0, 512, 1024, 1536, 2048, 2304, 3328, 4864, 5120
LORA_PAD = 256
SRC_RW_END = POOL_WIDTH + RW_COLS

VMEM_LIMIT = 56 * 1024 * 1024


def _t5_bucket_np(dist):
    n = np.maximum(dist, 0)
    nf = np.maximum(n, 1).astype(np.float32)
    large = MAX_EXACT + (np.log(nf / MAX_EXACT) / math.log(MAX_DISTANCE / MAX_EXACT)
                         * (N_BUCKETS - MAX_EXACT)).astype(np.int32)
    large = np.minimum(large, N_BUCKETS - 1)
    return np.where(n < MAX_EXACT, n, large)


def _cp(sem, vmem=None):
    return pltpu.CompilerParams(dimension_semantics=sem, vmem_limit_bytes=vmem)


def _dot(a, b, precision=None):
    return jnp.dot(a, b, preferred_element_type=F32, precision=precision)


def _dot_nt(a, b, precision=None):
    return lax.dot_general(a, b, (((1,), (1,)), ((), ())), preferred_element_type=F32, precision=precision)


def _dot_tn(a, b, precision=None):
    return lax.dot_general(a, b, (((0,), (0,)), ((), ())), preferred_element_type=F32, precision=precision)


def _rms(x, g):
    return x * lax.rsqrt(jnp.mean(x * x, axis=-1, keepdims=True) + RMS_EPS) * g


def _norm_kernel(x_ref, g_ref, h_ref):
    h_ref[...] = _rms(x_ref[...], g_ref[...]).astype(h_ref.dtype)


def _norm(x, g, out_dtype, tm=512):
    n, d = x.shape
    return pl.pallas_call(
        _norm_kernel,
        out_shape=jax.ShapeDtypeStruct((n, d), out_dtype),
        grid=(n // tm,),
        in_specs=[pl.BlockSpec((tm, d), lambda i: (i, 0)), pl.BlockSpec((1, d), lambda i: (0, 0))],
        out_specs=pl.BlockSpec((tm, d), lambda i: (i, 0)),
        compiler_params=_cp(("parallel",)),
        name="rms_norm",
    )(x, g.reshape(1, d))


def _add_norm_kernel(x_ref, y1_ref, y2_ref, g_ref, xo_ref, h_ref):
    x = x_ref[...] + (y1_ref[...] + y2_ref[...])
    xo_ref[...] = x
    h_ref[...] = _rms(x, g_ref[...]).astype(h_ref.dtype)


def _add_norm(x, y1, y2, g, out_dtype, tm=512):
    n, d = x.shape
    row = pl.BlockSpec((tm, d), lambda i: (i, 0))
    return pl.pallas_call(
        _add_norm_kernel,
        out_shape=(jax.ShapeDtypeStruct((n, d), F32), jax.ShapeDtypeStruct((n, d), out_dtype)),
        grid=(n // tm,),
        in_specs=[row, row, row, pl.BlockSpec((1, d), lambda i: (0, 0))],
        out_specs=(row, row),
        compiler_params=_cp(("parallel",)),
        name="moe_combine_norm",
    )(x, y1, y2, g.reshape(1, d))


def _matmul_kernel(x_ref, w_ref, o_ref):
    o_ref[...] = _dot(x_ref[...], w_ref[...]).astype(o_ref.dtype)


def _matmul(x, w, tm=512, tn=1024):
    m, k = x.shape
    n = w.shape[1]
    return pl.pallas_call(
        _matmul_kernel,
        out_shape=jax.ShapeDtypeStruct((m, n), F32),
        grid=(n // tn, m // tm),
        in_specs=[pl.BlockSpec((tm, k), lambda j, i: (i, 0)), pl.BlockSpec((k, tn), lambda j, i: (0, j))],
        out_specs=pl.BlockSpec((tm, tn), lambda j, i: (i, j)),
        compiler_params=_cp(("parallel", "parallel"), VMEM_LIMIT),
        name="in_proj",
    )(x, w)


def _pool_kernel(u_ref, halo_ref, w_ref, scale_ref, o_ref, buf_ref, *, tile):
    i = pl.program_id(1)
    u = u_ref[...]
    buf_ref[POOL_HALO:, :] = u
    buf_ref[:POOL_HALO, :] = jnp.where(i > 0, halo_ref[...], 0.0)
    t = i * tile + lax.broadcasted_iota(jnp.int32, (tile, 1), 0)
    outs = []
    for gi, win in enumerate(POOL_WINDOWS):
        cols = slice(gi * POOL_GW, (gi + 1) * POOL_GW)
        s = u[:, cols]
        for j in range(1, win):
            s = s + buf_ref[POOL_HALO - j:POOL_HALO - j + tile, cols]
        cnt = jnp.minimum(t + 1, win).astype(F32)
        pooled = s / cnt - u[:, cols]
        outs.append(_dot(pooled.astype(BF16), w_ref[gi]))
    o_ref[...] = (jnp.concatenate(outs, axis=1) * scale_ref[...]).astype(o_ref.dtype)


def _pool(proj, w_grp, scale, batch, seq, tile=512):
    nt = seq // tile
    hb = tile // POOL_HALO
    return pl.pallas_call(
        functools.partial(_pool_kernel, tile=tile),
        out_shape=jax.ShapeDtypeStruct((batch * seq, POOL_WIDTH), BF16),
        grid=(batch, nt),
        in_specs=[
            pl.BlockSpec((tile, POOL_WIDTH), lambda b, i: (b * nt + i, 0)),
            pl.BlockSpec((POOL_HALO, POOL_WIDTH), lambda b, i: (jnp.maximum((b * nt + i) * hb - 1, 0), 0)),
            pl.BlockSpec((len(POOL_WINDOWS), POOL_GW, POOL_GW), lambda b, i: (0, 0, 0)),
            pl.BlockSpec((1, POOL_WIDTH), lambda b, i: (0, 0)),
        ],
        out_specs=pl.BlockSpec((tile, POOL_WIDTH), lambda b, i: (b * nt + i, 0)),
        scratch_shapes=[pltpu.VMEM((tile + POOL_HALO, POOL_WIDTH), F32)],
        compiler_params=_cp(("parallel", "parallel")),
        name="pool_mixer",
    )(proj, proj, w_grp.astype(BF16), scale.reshape(1, POOL_WIDTH))


def _token_shift(u, halo, mu, first):
    prev_row = jnp.where(first, 0.0, halo[7:8, :])
    rolled = pltpu.roll(u, 1, 0)
    row = lax.broadcasted_iota(jnp.int32, u.shape, 0)
    prev = jnp.where(row == 0, prev_row, rolled)
    return u + (prev - u) * mu


def _rwkv_chunk_kernel(r_ref, k_ref, v_ref, l_ref, rh_ref, kh_ref, vh_ref, lh_ref,
                       mur_ref, muk_ref, muv_ref, mul_ref, w0_ref, wup_ref, a0_ref, aup_ref, gup_ref,
                       kk_ref, ka_ref, rk_ref, qy_ref, mn_ref, g_ref, bonus_ref):
    first = pl.program_id(1) == 0
    c = RW_CHUNK
    r = _token_shift(r_ref[...], rh_ref[...], mur_ref[...], first)
    k = _token_shift(k_ref[...], kh_ref[...], muk_ref[...], first)
    v = _token_shift(v_ref[...], vh_ref[...], muv_ref[...], first)
    lo = _token_shift(l_ref[...], lh_ref[...], mul_ref[...], first)
    wd = lo[:, :DECAY_LORA]
    ad = lo[:, DECAY_LORA:DECAY_LORA + AAA_LORA]
    gd = lo[:, DECAY_LORA + AAA_LORA:RW_LORA]
    z = -(w0_ref[...] + _dot(jnp.tanh(wd), wup_ref[...], HI))
    w_log = -(jnp.maximum(z, 0.0) + jnp.log(1.0 + jnp.exp(-jnp.abs(z)))) - 0.5
    logw = -jnp.exp(w_log)
    a = jax.nn.sigmoid(a0_ref[...] + _dot(ad, aup_ref[...], HI))
    g_ref[0] = _dot(jax.nn.sigmoid(gd), gup_ref[...], HI)
    kkraw = k * kk_ref[...]
    k2 = k * (1.0 + (a - 1.0) * ka_ref[...])
    rkr = r * k2 * rk_ref[...]

    ti = lax.broadcasted_iota(jnp.int32, (c, c), 0)
    si = lax.broadcasted_iota(jnp.int32, (c, c), 1)
    incl = ti >= si
    strict = ti > si
    eye = ti == si
    ltri = incl.astype(F32)
    zeros = jnp.zeros((c, c), F32)
    bonus = []
    qy = []
    for h in range(RW_HEADS):
        sl = slice(h * RW_DH, (h + 1) * RW_DH)
        rh, kh, vh, ah, lw = r[:, sl], k2[:, sl], v[:, sl], a[:, sl], logw[:, sl]
        kkr = kkraw[:, sl]
        nrm = jnp.sqrt(jnp.sum(kkr * kkr, axis=1, keepdims=True))
        kkh = kkr / jnp.maximum(nrm, 1e-12)
        bonus.append(jnp.sum(rkr[:, sl], axis=1, keepdims=True) * vh)
        cum = _dot(ltri, lw, HI)
        cum_end = cum[c - 1:c, :]
        ginv = jnp.exp(-cum)
        gtail = jnp.exp(cum_end - cum)
        at = -kkh * jnp.exp(cum - lw)
        bt = kkh * ah * ginv
        kt = kh * ginv
        rt = rh * jnp.exp(cum)
        gram = _dot_nt(jnp.concatenate([at, rt], axis=0), jnp.concatenate([bt, kt], axis=0), HI)
        a_ab = jnp.where(strict, gram[:c, :c], 0.0)
        a_ak = jnp.where(strict, gram[:c, c:], 0.0)
        a_rb = jnp.where(incl, gram[c:, :c], 0.0)
        a_rk = jnp.where(incl, gram[c:, c:], 0.0)
        p = a_ab
        tinv = eye.astype(F32) + p
        for _ in range(int(math.log2(c)) - 1):
            p = _dot(p, p, HI)
            tinv = tinv + _dot(tinv, p, HI)
        av = _dot(a_ak, vh, HI)
        w12 = _dot(tinv, jnp.concatenate([at, av], axis=1), HI)
        zmat = jnp.concatenate([w12, jnp.concatenate([zeros, vh], axis=1)], axis=0)
        out1 = _dot(jnp.concatenate([a_rb, a_rk], axis=1), zmat, HI)
        out2 = _dot_tn(jnp.concatenate([kkh * ah * gtail, kh * gtail], axis=0), zmat, HI)
        qy.append(out1 + jnp.concatenate([rt, zeros], axis=1))
        diag = jnp.where(eye, jnp.exp(cum_end), 0.0)
        mn_ref[0, 0, h] = out2 + jnp.concatenate([diag, zeros], axis=1)
    qy_ref[0] = jnp.concatenate(qy, axis=1)
    bonus_ref[0] = jnp.concatenate(bonus, axis=1)


def _rwkv_scan_kernel(qy_ref, mn_ref, g_ref, bonus_ref, gnw_ref, gnb_ref, y_ref, st_ref, *, batch):
    @pl.when(pl.program_id(0) == 0)
    def _():
        st_ref[...] = jnp.zeros_like(st_ref)

    for b in range(batch):
        ys = []
        for h in range(RW_HEADS):
            qp = qy_ref[b, :, 2 * RW_DH * h:2 * RW_DH * h + RW_DH]
            y0 = qy_ref[b, :, 2 * RW_DH * h + RW_DH:2 * RW_DH * (h + 1)]
            mn = mn_ref[b, 0, h]
            st = st_ref[b * RW_HEADS + h]
            y = _dot(qp, st, HI) + y0
            st_ref[b * RW_HEADS + h] = _dot(mn[:, :RW_DH], st, HI) + mn[:, RW_DH:]
            mean = jnp.mean(y, axis=1, keepdims=True)
            var = jnp.mean(jnp.square(y - mean), axis=1, keepdims=True)
            ys.append((y - mean) * lax.rsqrt(var + RW_GN_EPS))
        yn = jnp.concatenate(ys, axis=1) * gnw_ref[...] + gnb_ref[...]
        y_ref[b] = ((yn + bonus_ref[b]) * g_ref[b]).astype(y_ref.dtype)


def _rwkv(proj, p, batch, seq):
    c = RW_CHUNK
    nc = seq // c
    hb = c // 8
    row512 = lambda col: pl.BlockSpec((c, RW_WIDTH), lambda b, i: (b * nc + i, col))
    halo512 = lambda col: pl.BlockSpec((8, RW_WIDTH), lambda b, i: (jnp.maximum((b * nc + i) * hb - 1, 0), col))
    const = lambda shape: pl.BlockSpec(shape, lambda b, i: (0,) * len(shape))
    vec = const((1, RW_WIDTH))
    out_row = lambda w: pl.BlockSpec((1, c, w), lambda b, i: (b, i, 0))
    qy, mn, g, bonus = pl.pallas_call(
        _rwkv_chunk_kernel,
        out_shape=(jax.ShapeDtypeStruct((batch, seq, 2 * RW_WIDTH), F32),
                   jax.ShapeDtypeStruct((batch, nc, RW_HEADS, RW_DH, 2 * RW_DH), F32),
                   jax.ShapeDtypeStruct((batch, seq, RW_WIDTH), F32),
                   jax.ShapeDtypeStruct((batch, seq, RW_WIDTH), F32)),
        grid=(batch, nc),
        in_specs=[row512(C_R // RW_WIDTH), row512(C_K // RW_WIDTH), row512(C_V // RW_WIDTH),
                  pl.BlockSpec((c, LORA_PAD), lambda b, i: (b * nc + i, C_LORA // LORA_PAD)),
                  halo512(C_R // RW_WIDTH), halo512(C_K // RW_WIDTH), halo512(C_V // RW_WIDTH),
                  pl.BlockSpec((8, LORA_PAD), lambda b, i: (jnp.maximum((b * nc + i) * hb - 1, 0), C_LORA // LORA_PAD)),
                  vec, vec, vec, const((1, LORA_PAD)),
                  vec, const((DECAY_LORA, RW_WIDTH)), vec, const((AAA_LORA, RW_WIDTH)), const((GATE_LORA, RW_WIDTH)),
                  vec, vec, vec],
        out_specs=(out_row(2 * RW_WIDTH),
                   pl.BlockSpec((1, 1, RW_HEADS, RW_DH, 2 * RW_DH), lambda b, i: (b, i, 0, 0, 0)),
                   out_row(RW_WIDTH), out_row(RW_WIDTH)),
        compiler_params=_cp(("parallel", "parallel"), VMEM_LIMIT),
        name="rwkv_chunk",
    )(proj, proj, proj, proj, proj, proj, proj, proj,
      p["mu_r"], p["mu_k"], p["mu_v"], p["mu_l"], p["w0"], p["w_up"], p["a0"], p["a_up"], p["g_up"],
      p["k_k"], p["k_a"], p["r_k"])

    full = lambda w: pl.BlockSpec((batch, c, w), lambda i: (0, i, 0))
    return pl.pallas_call(
        functools.partial(_rwkv_scan_kernel, batch=batch),
        out_shape=jax.ShapeDtypeStruct((batch, seq, RW_WIDTH), BF16),
        grid=(nc,),
        in_specs=[full(2 * RW_WIDTH),
                  pl.BlockSpec((batch, 1, RW_HEADS, RW_DH, 2 * RW_DH), lambda i: (0, i, 0, 0, 0)),
                  full(RW_WIDTH), full(RW_WIDTH),
                  pl.BlockSpec((1, RW_WIDTH), lambda i: (0, 0)), pl.BlockSpec((1, RW_WIDTH), lambda i: (0, 0))],
        out_specs=full(RW_WIDTH),
        scratch_shapes=[pltpu.VMEM((batch * RW_HEADS, RW_DH, RW_DH), F32)],
        compiler_params=_cp(("arbitrary",), VMEM_LIMIT),
        name="rwkv_scan",
    )(qy, mn, g, bonus, p["gn_w"], p["gn_b"]).reshape(batch * seq, RW_WIDTH)


def _gelu_tanh(x):
    return 0.5 * x * (1.0 + jnp.tanh(math.sqrt(2.0 / math.pi) * (x + 0.044715 * (x * x * x))))


def _compress_kernel(x_ref, pos_ref, w1_ref, w2_ref, o_ref):
    half = CMP_STRIDE * NSA_DH
    x = x_ref[0, 0, 0]
    w1 = w1_ref[0]
    posb = _dot(pos_ref[0], w1)[0:1, :]
    h1 = _dot(x, w1[:half])
    h2 = _dot(x, w1[half:])
    n = h2.shape[0]
    row = lax.broadcasted_iota(jnp.int32, h2.shape, 0)
    h2s = jnp.where(row < n - 1, pltpu.roll(h2, n - 1, 0), 0.0)
    hid = _gelu_tanh(h1 + h2s + posb)
    o_ref[0, 0, 0] = _dot(hid.astype(BF16), w2_ref[0]).astype(o_ref.dtype)


def _compress(xkv, pos, w1, w2, batch, seq):
    nr = seq // CMP_STRIDE
    wide = CMP_STRIDE * NSA_DH
    return pl.pallas_call(
        _compress_kernel,
        out_shape=jax.ShapeDtypeStruct((2, batch, NSA_G, nr, NSA_DH), BF16),
        grid=(2, batch, NSA_G),
        in_specs=[pl.BlockSpec((1, 1, 1, nr, wide), lambda t, b, g: (t, b, g, 0, 0)),
                  pl.BlockSpec((1, 8, 2 * wide), lambda t, b, g: (t, 0, 0)),
                  pl.BlockSpec((1, 2 * wide, CMP_HIDDEN), lambda t, b, g: (t, 0, 0)),
                  pl.BlockSpec((1, CMP_HIDDEN, NSA_DH), lambda t, b, g: (t, 0, 0))],
        out_specs=pl.BlockSpec((1, 1, 1, nr, NSA_DH), lambda t, b, g: (t, b, g, 0, 0)),
        compiler_params=_cp(("parallel", "parallel", "parallel")),
        name="nsa_compress",
    )(xkv, pos, w1, w2)


def _softmax_parts(s):
    m = jnp.max(s, axis=1, keepdims=True)
    e = jnp.exp(s - m)
    return m, e, jnp.sum(e, axis=1, keepdims=True)


def _nsa_kernel(q_ref, gate_ref, kc_ref, vc_ref, ks_ref, vs_ref, kw_ref, vw_ref,
                tblc_ref, tbln_ref, tblw_ref, cfar_ref, ov_ref, o_ref, *, ncmp):
    g = pl.program_id(1)
    i = pl.program_id(2)
    rows = NSA_HG * QB
    qt = q_ref[...]
    q = jnp.concatenate([qt[:, NSA_DH * h:NSA_DH * (h + 1)] for h in range(NSA_HG)], axis=0)
    qb = (q * (NSA_DH ** -0.5)).astype(BF16)
    cfar = cfar_ref[0]
    tile4 = lambda m: jnp.concatenate([m] * NSA_HG, axis=0)

    cp = lax.broadcasted_iota(jnp.int32, (CMP_NEAR, ncmp), 0)
    cc = lax.broadcasted_iota(jnp.int32, (CMP_NEAR, ncmp), 1)
    place = (cc == cp + (4 * i - CMP_NEAR // 2)).astype(F32)
    lc = _dot_nt(qb, kc_ref[0, 0]) + cfar + _dot(tblc_ref[0] - cfar, place, HI)
    tq = i * QB + lax.broadcasted_iota(jnp.int32, (rows, ncmp), 0) % QB
    cidx = lax.broadcasted_iota(jnp.int32, (rows, ncmp), 1)
    vis = tq - (cidx * CMP_STRIDE + CMP_BLOCK - 1) >= 0
    _, e, den = _softmax_parts(jnp.where(vis, lc, NEG))
    pc = jnp.where(vis, e / den, 0.0)
    o_c = _dot(pc.astype(BF16), vc_ref[0, 0])
    pcs = pc[0:QB] + pc[QB:2 * QB] + pc[2 * QB:3 * QB] + pc[3 * QB:4 * QB]
    imp = _dot(pcs, ov_ref[...], HI)

    nslc = imp.shape[1]
    nidx = lax.broadcasted_iota(jnp.int32, (QB, nslc), 1)
    valid = nidx <= i
    forced = (nidx == 0) | (nidx == i) | (nidx == i - 1)
    work = jnp.where(valid, imp + jnp.where(forced, FORCE_BONUS, 0.0), -1.0)
    sel = jnp.zeros((QB, nslc), F32)
    for _ in range(N_SELECT):
        m = jnp.max(work, axis=1, keepdims=True)
        first = jnp.min(jnp.where(work == m, nidx, nslc), axis=1, keepdims=True)
        pick = nidx == first
        sel = jnp.where(pick & (m >= 0.0), 1.0, sel)
        work = jnp.where(pick, -2.0, work)
    selb = sel.astype(BF16)

    def expand(first_block, width, limit):
        nn = lax.broadcasted_iota(jnp.int32, (nslc, width), 0)
        kb = first_block + lax.broadcasted_iota(jnp.int32, (nslc, width), 1) // SLC_BLOCK
        onehot = jnp.where((nn == kb) & (kb <= limit), 1.0, 0.0).astype(BF16)
        return tile4((_dot(selb, onehot) - 1.0) * (-NEG))

    near_w = NEAR_BLOCKS * SLC_BLOCK
    kn = ks_ref[0, 0, pl.ds(pl.multiple_of(i * QB, QB), near_w), :]
    vn = vs_ref[0, 0, pl.ds(pl.multiple_of(i * QB, QB), near_w), :]
    s = _dot_nt(qb, kn) + tbln_ref[0] + expand(i - (NEAR_BLOCKS - 1), near_w, i)
    m_s, e, l_s = _softmax_parts(s)
    acc = _dot(e.astype(BF16), vn)
    far_w = FAR_CHUNK_BLOCKS * SLC_BLOCK
    pad_s = (NEAR_BLOCKS - 1) * SLC_BLOCK

    def far_body(j, carry):
        m_s, l_s, acc = carry
        start = pl.multiple_of(pad_s + j * far_w, SLC_BLOCK)
        kf = ks_ref[0, 0, pl.ds(start, far_w), :]
        vf = vs_ref[0, 0, pl.ds(start, far_w), :]
        s = _dot_nt(qb, kf) + cfar + expand(j * FAR_CHUNK_BLOCKS, far_w, i - NEAR_BLOCKS)
        m_new = jnp.maximum(m_s, jnp.max(s, axis=1, keepdims=True))
        alpha = jnp.exp(m_s - m_new)
        e = jnp.exp(s - m_new)
        return m_new, alpha * l_s + jnp.sum(e, axis=1, keepdims=True), alpha * acc + _dot(e.astype(BF16), vf)

    n_far = jnp.maximum(i - (NEAR_BLOCKS - 1) + FAR_CHUNK_BLOCKS - 1, 0) // FAR_CHUNK_BLOCKS
    m_s, l_s, acc = lax.fori_loop(0, n_far, far_body, (m_s, l_s, acc))
    o_s = acc / l_s

    win_w = WINDOW + QB
    kw = kw_ref[0, 0, pl.ds(pl.multiple_of(i * QB, QB), win_w), :]
    vw = vw_ref[0, 0, pl.ds(pl.multiple_of(i * QB, QB), win_w), :]
    jj = lax.broadcasted_iota(jnp.int32, (rows, win_w), 1)
    s = _dot_nt(qb, kw) + tblw_ref[0] + jnp.where(jj >= WINDOW - i * QB, 0.0, NEG)
    _, e, den = _softmax_parts(s)
    o_w = _dot(e.astype(BF16), vw) / den

    gl = lax.broadcasted_iota(jnp.int32, (gate_ref.shape[1], 16), 0)
    gj = lax.broadcasted_iota(jnp.int32, (gate_ref.shape[1], 16), 1)
    pick_g = ((gl == g * (NSA_HG * 3) + gj) & (gj < NSA_HG * 3)).astype(F32)
    gates = _dot(jax.nn.sigmoid(gate_ref[...]), pick_g, HI)
    outs = []
    for h in range(NSA_HG):
        r0 = slice(h * QB, (h + 1) * QB)
        outs.append(gates[:, 3 * h:3 * h + 1] * o_c[r0] + gates[:, 3 * h + 1:3 * h + 2] * o_s[r0]
                    + gates[:, 3 * h + 2:3 * h + 3] * o_w[r0])
    o_ref[...] = jnp.concatenate(outs, axis=1).astype(o_ref.dtype)


def _bias_tables(rel_bias):
    def table(dist, keep):
        tbl = rel_bias[jnp.asarray(_t5_bucket_np(dist))]
        tbl = jnp.where(jnp.asarray(keep)[..., None], tbl, NEG)
        k = dist.shape[1]
        return tbl.transpose(2, 0, 1).reshape(NSA_G, NSA_HG * QB, k)

    qi = np.arange(QB)[:, None]
    dist_c = qi - CMP_STRIDE * (np.arange(CMP_NEAR)[None, :] - CMP_NEAR // 2) - (CMP_BLOCK - 1)
    tblc = table(dist_c, np.ones_like(dist_c, bool))
    jn = np.arange(NEAR_BLOCKS * SLC_BLOCK)[None, :]
    dist_n = (NEAR_BLOCKS - 1) * SLC_BLOCK + qi - jn
    tbln = table(dist_n, dist_n >= 0)
    jw = np.arange(WINDOW + QB)[None, :]
    dist_w = WINDOW + qi - jw
    tblw = table(dist_w, (dist_w >= 0) & (dist_w < WINDOW))
    cfar = jnp.repeat(rel_bias[N_BUCKETS - 1].reshape(NSA_G, NSA_HG), QB, axis=1).reshape(NSA_G, NSA_HG * QB, 1)
    return tblc, tbln, tblw, cfar


def _nsa(proj, kvc, ks, vs, kw, vw, tables, batch, seq):
    nq = seq // QB
    ncmp = seq // CMP_STRIDE
    nslc = seq // SLC_BLOCK
    tblc, tbln, tblw, cfar = tables
    cstart = np.arange(ncmp) * CMP_STRIDE
    sstart = np.arange(nslc) * SLC_BLOCK
    overlap = ((cstart[:, None] <= sstart[None, :] + SLC_BLOCK - 1)
               & (cstart[:, None] + CMP_BLOCK - 1 >= sstart[None, :])
               & (cstart[:, None] + CMP_BLOCK <= seq)).astype(np.float32)
    rows = NSA_HG * QB
    kv_spec = lambda n: pl.BlockSpec((1, 1, n, NSA_DH), lambda b, g, i: (b, g, 0, 0))
    tbl_spec = lambda k: pl.BlockSpec((1, rows, k), lambda b, g, i: (g, 0, 0))
    qcol = C_Q // (NSA_HG * NSA_DH)
    return pl.pallas_call(
        functools.partial(_nsa_kernel, ncmp=ncmp),
        out_shape=jax.ShapeDtypeStruct((batch * seq, NSA_HEADS * NSA_DH), BF16),
        grid=(batch, NSA_G, nq),
        in_specs=[pl.BlockSpec((QB, NSA_HG * NSA_DH), lambda b, g, i: (b * nq + i, qcol + g)),
                  pl.BlockSpec((QB, 128), lambda b, g, i: (b * nq + i, C_GATE // 128)),
                  pl.BlockSpec((1, 1, ncmp, NSA_DH), lambda b, g, i: (b, g, 0, 0)),
                  pl.BlockSpec((1, 1, ncmp, NSA_DH), lambda b, g, i: (b, g, 0, 0)),
                  kv_spec(ks.shape[2]), kv_spec(vs.shape[2]), kv_spec(kw.shape[2]), kv_spec(vw.shape[2]),
                  tbl_spec(CMP_NEAR), tbl_spec(NEAR_BLOCKS * SLC_BLOCK), tbl_spec(WINDOW + QB), tbl_spec(1),
                  pl.BlockSpec((ncmp, nslc), lambda b, g, i: (0, 0))],
        out_specs=pl.BlockSpec((QB, NSA_HG * NSA_DH), lambda b, g, i: (b * nq + i, g)),
        compiler_params=_cp(("parallel", "parallel", "arbitrary"), VMEM_LIMIT),
        name="nsa_attention",
    )(proj, proj, kvc[0], kvc[1], ks, vs, kw, vw, tblc, tbln, tblw, cfar, jnp.asarray(overlap))


def _merge_kernel(h_ref, yp_ref, yr_ref, yn_ref, x_ref, wbp_ref, wbr_ref, wbn_ref, wm_ref, bm_ref, wo_ref,
                  gn_ref, wr_ref, br_ref, xo_ref, h2_ref, route_ref):
    d = D_MODEL
    gl = jax.nn.sigmoid(_dot(h_ref[...], wm_ref[...]) + bm_ref[...])
    merged = (gl[:, :d] * _dot(yp_ref[...], wbp_ref[...]) + gl[:, d:2 * d] * _dot(yr_ref[...], wbr_ref[...])
              + gl[:, 2 * d:] * _dot(yn_ref[...], wbn_ref[...]))
    x = x_ref[...] + _dot(merged.astype(BF16), wo_ref[...])
    xo_ref[...] = x
    h2 = _rms(x, gn_ref[...])
    h2_ref[...] = h2.astype(h2_ref.dtype)

    logits = _dot(h2, wr_ref[...], HI) + br_ref[...]
    lane = lax.broadcasted_iota(jnp.int32, logits.shape, 1)
    big = logits.shape[1]
    lg = jnp.where(lane < N_GROUPS, logits, NEG)
    mg = jnp.max(lg, axis=1, keepdims=True)
    p_top = 1.0 / jnp.sum(jnp.exp(lg - mg), axis=1, keepdims=True)
    grp = jnp.min(jnp.where(lg == mg, lane, big), axis=1, keepdims=True)
    lo = N_GROUPS + EPG * grp
    le = jnp.where((lane >= lo) & (lane < lo + EPG), logits, NEG)
    e1 = jnp.max(le, axis=1, keepdims=True)
    i1 = jnp.min(jnp.where(le == e1, lane, big), axis=1, keepdims=True)
    le = jnp.where(lane == i1, NEG, le)
    e2 = jnp.max(le, axis=1, keepdims=True)
    i2 = jnp.min(jnp.where(le == e2, lane, big), axis=1, keepdims=True)
    t = jnp.exp(e2 - e1)
    w1 = p_top / (1.0 + t)
    w2 = p_top * t / (1.0 + t)
    route_ref[...] = jnp.where(lane == 0, (i1 - N_GROUPS).astype(F32),
                               jnp.where(lane == 1, (i2 - N_GROUPS).astype(F32),
                                         jnp.where(lane == 2, w1, jnp.where(lane == 3, w2, 0.0))))


def _merge(h, y_pool, y_rwkv, y_nsa, x, p, tm=256):
    n, d = x.shape
    row = lambda w: pl.BlockSpec((tm, w), lambda i: (i, 0))
    const = lambda a: pl.BlockSpec(a.shape, lambda i: (0, 0))
    ws = [p["wb_pool"], p["wb_rwkv"], p["wb_nsa"], p["w_merge"], p["b_merge"], p["w_out"],
          p["norm_ffn"], p["w_router"], p["b_router"]]
    return pl.pallas_call(
        _merge_kernel,
        out_shape=(jax.ShapeDtypeStruct((n, d), F32), jax.ShapeDtypeStruct((n, d), BF16),
                   jax.ShapeDtypeStruct((n, 128), F32)),
        grid=(n // tm,),
        in_specs=[row(d), row(POOL_WIDTH), row(RW_WIDTH), row(d), row(d)] + [const(w) for w in ws],
        out_specs=(row(d), row(d), row(128)),
        compiler_params=_cp(("parallel",), VMEM_LIMIT),
        name="merge_router",
    )(h, y_pool, y_rwkv, y_nsa, x, *ws)


def _expert_kernel(te_ref, nt_ref, x_ref, w_ref, wg_ref, wu_ref, wd_ref, o_ref):
    i = pl.program_id(0)

    @pl.when(i < nt_ref[0])
    def _():
        xb = x_ref[...]
        gate = _dot(xb, wg_ref[0])
        hid = gate * jax.nn.sigmoid(gate) * _dot(xb, wu_ref[0])
        o_ref[...] = w_ref[...] * _dot(hid.astype(BF16), wd_ref[0])

    @pl.when(i >= nt_ref[0])
    def _():
        o_ref[...] = jnp.zeros_like(o_ref)


def _experts(xs, roww, tile_expert, n_tiles, wg, wu, wd):
    r, d = xs.shape
    tm = MOE_TM
    return pl.pallas_call(
        _expert_kernel,
        out_shape=jax.ShapeDtypeStruct((r, d), F32),
        grid_spec=pltpu.PrefetchScalarGridSpec(
            num_scalar_prefetch=2,
            grid=(r // tm,),
            in_specs=[pl.BlockSpec((tm, d), lambda i, te, nt: (i, 0)),
                      pl.BlockSpec((tm, 1), lambda i, te, nt: (i, 0)),
                      pl.BlockSpec((1, d, D_EXPERT), lambda i, te, nt: (te[i], 0, 0)),
                      pl.BlockSpec((1, d, D_EXPERT), lambda i, te, nt: (te[i], 0, 0)),
                      pl.BlockSpec((1, D_EXPERT, d), lambda i, te, nt: (te[i], 0, 0))],
            out_specs=pl.BlockSpec((tm, d), lambda i, te, nt: (i, 0))),
        compiler_params=_cp(("arbitrary",), VMEM_LIMIT),
        name="moe_experts",
    )(tile_expert, n_tiles, xs, roww, wg, wu, wd)


def _moe(h2, route, wg, wu, wd):
    n = h2.shape[0]
    tm = MOE_TM
    r = 2 * n + N_EXPERTS * tm
    ids = route[:, 0:2].astype(jnp.int32).reshape(-1)
    wts = route[:, 2:4].reshape(-1)
    onehot = (ids[:, None] == jnp.arange(N_EXPERTS)[None, :]).astype(jnp.int32)
    rank = jnp.sum((jnp.cumsum(onehot, axis=0) - onehot) * onehot, axis=1)
    counts = jnp.sum(onehot, axis=0)
    tiles = (counts + tm - 1) // tm
    tile_end = jnp.cumsum(tiles)
    starts = (tile_end - tiles) * tm
    pos = starts[ids] + rank
    tok = jnp.arange(2 * n, dtype=jnp.int32) // 2
    rowtok = jnp.zeros((r,), jnp.int32).at[pos].set(tok)
    roww = jnp.zeros((r,), F32).at[pos].set(wts)
    n_tiles = tile_end[-1:].astype(jnp.int32)
    tile_expert = jnp.minimum(jnp.sum(tile_end[None, :] <= jnp.arange(r // tm)[:, None], axis=1),
                              N_EXPERTS - 1).astype(jnp.int32)
    ys = _experts(h2[rowtok], roww.reshape(r, 1), tile_expert, n_tiles, wg, wu, wd)
    return ys[pos[0::2]], ys[pos[1::2]]


def _layer_params(l, a):
    f = lambda t: t[l]
    row = lambda t: t[l].reshape(1, -1)
    w_in = a["w_in"][l]
    w_in_p = jnp.concatenate([w_in[:, :SRC_RW_END], jnp.zeros((D_MODEL, C_Q - SRC_RW_END), F32),
                              w_in[:, SRC_RW_END:],
                              jnp.zeros((D_MODEL, P_COLS - C_Q - (w_in.shape[1] - SRC_RW_END)), F32)], axis=1)
    mu = a["rw_mu"][l]
    wb = a["w_branch"][l].astype(BF16)
    w_router = jnp.zeros((D_MODEL, 128), F32)
    w_router = w_router.at[:, :N_GROUPS].set(a["w_router_grp"][l]).at[:, N_GROUPS:N_GROUPS + N_EXPERTS].set(
        a["w_router_exp"][l])
    b_router = jnp.zeros((1, 128), F32)
    b_router = b_router.at[0, :N_GROUPS].set(a["b_router_grp"][l]).at[0, N_GROUPS:N_GROUPS + N_EXPERTS].set(
        a["b_router_exp"][l])
    pos = jnp.stack([a["cmp_pos_k"][l].reshape(-1), a["cmp_pos_v"][l].reshape(-1)])
    return {
        "w_in": w_in_p.astype(BF16),
        "pool_w": f(a["pool_w"]), "pool_scale": f(a["pool_scale"]),
        "mu_r": mu[None, 0:RW_WIDTH], "mu_k": mu[None, RW_WIDTH:2 * RW_WIDTH],
        "mu_v": mu[None, 2 * RW_WIDTH:3 * RW_WIDTH],
        "mu_l": jnp.concatenate([mu[3 * RW_WIDTH:], jnp.zeros((LORA_PAD - RW_LORA,), F32)])[None],
        "w0": row(a["rw_w0"]), "w_up": f(a["rw_w_up"]), "a0": row(a["rw_a0"]), "a_up": f(a["rw_a_up"]),
        "g_up": f(a["rw_g_up"]), "k_k": row(a["rw_k_k"]), "k_a": row(a["rw_k_a"]), "r_k": row(a["rw_r_k"]),
        "gn_w": row(a["rw_gn_w"]), "gn_b": row(a["rw_gn_b"]),
        "cmp_pos": jnp.broadcast_to(pos[:, None, :], (2, 8, pos.shape[1])).astype(BF16),
        "cmp_w1": jnp.stack([a["cmp_w1_k"][l], a["cmp_w1_v"][l]]).astype(BF16),
        "cmp_w2": jnp.stack([a["cmp_w2_k"][l], a["cmp_w2_v"][l]]).astype(BF16),
        "wb_pool": wb[:POOL_WIDTH], "wb_rwkv": wb[POOL_WIDTH:POOL_WIDTH + RW_WIDTH],
        "wb_nsa": wb[POOL_WIDTH + RW_WIDTH:],
        "w_merge": a["w_merge"][l].astype(BF16), "b_merge": row(a["b_merge"]),
        "w_out": a["w_out"][l].astype(BF16), "norm_ffn": row(a["norm_ffn"]),
        "w_router": w_router, "b_router": b_router,
        "w_gate": a["w_exp_gate"][l].astype(BF16), "w_up_e": a["w_exp_up"][l].astype(BF16),
        "w_down": a["w_exp_down"][l].astype(BF16),
    }


def _mixers(proj, p, tables, batch, seq):
    y_pool = _pool(proj, p["pool_w"], p["pool_scale"], batch, seq)
    y_rwkv = _rwkv(proj, p, batch, seq)
    kv = proj[:, C_KV:C_GATE].reshape(batch, seq, 6, NSA_G, NSA_DH).transpose(2, 0, 3, 1, 4).astype(BF16)
    xkv = kv[0:2].reshape(2, batch, NSA_G, seq // CMP_STRIDE, CMP_STRIDE * NSA_DH)
    kvc = _compress(xkv, p["cmp_pos"], p["cmp_w1"], p["cmp_w2"], batch, seq)
    pad = lambda t, n: jnp.pad(t, ((0, 0), (0, 0), (n, 0), (0, 0)))
    pad_s = (NEAR_BLOCKS - 1) * SLC_BLOCK
    y_nsa = _nsa(proj, kvc, pad(kv[2], pad_s), pad(kv[3], pad_s), pad(kv[4], WINDOW), pad(kv[5], WINDOW),
                 tables, batch, seq)
    return y_pool, y_rwkv, y_nsa


def kernel(x, rel_bias, norm_mix, w_in, pool_w, pool_scale, rw_mu, rw_w0, rw_w_up, rw_a0, rw_a_up, rw_g_up, rw_k_k, rw_k_a, rw_r_k, rw_gn_w, rw_gn_b, cmp_pos_k, cmp_w1_k, cmp_w2_k, cmp_pos_v, cmp_w1_v, cmp_w2_v, w_branch, w_merge, b_merge, w_out, norm_ffn, w_router_grp, b_router_grp, w_router_exp, b_router_exp, w_exp_gate, w_exp_up, w_exp_down, norm_final):
    a = dict(w_in=w_in, pool_w=pool_w, pool_scale=pool_scale, rw_mu=rw_mu, rw_w0=rw_w0, rw_w_up=rw_w_up,
             rw_a0=rw_a0, rw_a_up=rw_a_up, rw_g_up=rw_g_up, rw_k_k=rw_k_k, rw_k_a=rw_k_a, rw_r_k=rw_r_k,
             rw_gn_w=rw_gn_w, rw_gn_b=rw_gn_b, cmp_pos_k=cmp_pos_k, cmp_w1_k=cmp_w1_k, cmp_w2_k=cmp_w2_k,
             cmp_pos_v=cmp_pos_v, cmp_w1_v=cmp_w1_v, cmp_w2_v=cmp_w2_v, w_branch=w_branch, w_merge=w_merge,
             b_merge=b_merge, w_out=w_out, norm_ffn=norm_ffn, w_router_grp=w_router_grp,
             b_router_grp=b_router_grp, w_router_exp=w_router_exp, b_router_exp=b_router_exp,
             w_exp_gate=w_exp_gate, w_exp_up=w_exp_up, w_exp_down=w_exp_down)
    batch, seq, d = x.shape
    depth = norm_mix.shape[0]
    tables = _bias_tables(rel_bias)
    xf = x.reshape(batch * seq, d)
    h = _norm(xf, norm_mix[0], BF16)
    for l in range(depth):
        p = _layer_params(l, a)
        proj = _matmul(h, p["w_in"])
        y_pool, y_rwkv, y_nsa = _mixers(proj, p, tables, batch, seq)
        xf, h2, route = _merge(h, y_pool, y_rwkv, y_nsa, xf, p)
        y1, y2 = _moe(h2, route, p["w_gate"], p["w_up_e"], p["w_down"])
        last = l == depth - 1
        g_next = norm_final if last else norm_mix[l + 1]
        xf, h = _add_norm(xf, y1, y2, g_next, F32 if last else BF16)
    return h.reshape(batch, seq, d)
```

```python
import functools
import math

import jax
import jax.numpy as jnp
import numpy as np
from jax import lax
from jax.experimental import pallas as pl
from jax.experimental.pallas import tpu as pltpu

F32 = jnp.float32
BF16 = jnp.bfloat16
HI = lax.Precision.HIGHEST

D_MODEL = 1024
RMS_EPS = 1e-6
NEG = -1e30

POOL_WINDOWS = (2, 4, 8, 16)
POOL_WIDTH = 512
POOL_GW = 128
POOL_HALO = 16

RW_HEADS = 8
RW_DH = 64
RW_WIDTH = 512
DECAY_LORA, AAA_LORA, GATE_LORA = 32, 32, 96
RW_LORA = DECAY_LORA + AAA_LORA + GATE_LORA
RW_COLS = 3 * RW_WIDTH + RW_LORA
RW_GN_EPS = 64e-5
RW_CHUNK = 64

NSA_DH = 64
NSA_HEADS = 16
NSA_G = 4
NSA_HG = 4
NSA_KVW = NSA_G * NSA_DH
CMP_BLOCK, CMP_STRIDE, CMP_HIDDEN = 32, 16, 256
SLC_BLOCK = 64
N_SELECT = 8
WINDOW = 512
QB = 64
FORCE_BONUS = 1e3
N_BUCKETS, MAX_EXACT, MAX_DISTANCE = 32, 16, 128
NEAR_BLOCKS = 3
FAR_CHUNK_BLOCKS = 8
CMP_NEAR = 32

N_GROUPS, EPG, N_EXPERTS, D_EXPERT = 4, 8, 32, 256
MOE_TM = 256

C_POOL, C_R, C_K, C_V, C_LORA, C_Q, C_KV, C_GATE, P_COLS = 0, 512, 1024, 1536, 2048, 2304, 3328, 4864, 5120
LORA_PAD = 256
SRC_RW_END = POOL_WIDTH + RW_COLS

VMEM_LIMIT = 56 * 1024 * 1024


def _t5_bucket_np(dist):
    n = np.maximum(dist, 0)
    nf = np.maximum(n, 1).astype(np.float32)
    large = MAX_EXACT + (np.log(nf / MAX_EXACT) / math.log(MAX_DISTANCE / MAX_EXACT)
                         * (N_BUCKETS - MAX_EXACT)).astype(np.int32)
    large = np.minimum(large, N_BUCKETS - 1)
    return np.where(n < MAX_EXACT, n, large)


def _cp(sem, vmem=None):
    return pltpu.CompilerParams(dimension_semantics=sem, vmem_limit_bytes=vmem)


def _dot(a, b, precision=None):
    return jnp.dot(a, b, preferred_element_type=F32, precision=precision)


def _dot_nt(a, b, precision=None):
    return lax.dot_general(a, b, (((1,), (1,)), ((), ())), preferred_element_type=F32, precision=precision)


def _dot_tn(a, b, precision=None):
    return lax.dot_general(a, b, (((0,), (0,)), ((), ())), preferred_element_type=F32, precision=precision)


def _bdot(a, b):
    return _dot(a.astype(BF16), b.astype(BF16))


def _bdot_nt(a, b):
    return _dot_nt(a.astype(BF16), b.astype(BF16))


def _bdot_tn(a, b):
    return _dot_tn(a.astype(BF16), b.astype(BF16))


def _split(a):
    hi = a.astype(BF16)
    return hi, (a - hi.astype(F32)).astype(BF16)


def _dot3(a, b):
    ah, al = _split(a)
    bh, bl = _split(b)
    return _dot(ah, bh) + (_dot(ah, bl) + _dot(al, bh))


def _rms(x, g):
    return x * lax.rsqrt(jnp.mean(x * x, axis=-1, keepdims=True) + RMS_EPS) * g


def _norm_kernel(x_ref, g_ref, h_ref):
    h_ref[...] = _rms(x_ref[...], g_ref[...]).astype(h_ref.dtype)


def _norm(x, g, out_dtype, tm=512):
    n, d = x.shape
    return pl.pallas_call(
        _norm_kernel,
        out_shape=jax.ShapeDtypeStruct((n, d), out_dtype),
        grid=(n // tm,),
        in_specs=[pl.BlockSpec((tm, d), lambda i: (i, 0)), pl.BlockSpec((1, d), lambda i: (0, 0))],
        out_specs=pl.BlockSpec((tm, d), lambda i: (i, 0)),
        compiler_params=_cp(("parallel",)),
        name="rms_norm",
    )(x, g.reshape(1, d))


def _add_norm_kernel(x_ref, y1_ref, y2_ref, g_ref, xo_ref, h_ref):
    x = x_ref[...] + (y1_ref[...] + y2_ref[...])
    xo_ref[...] = x
    h_ref[...] = _rms(x, g_ref[...]).astype(h_ref.dtype)


def _add_norm(x, y1, y2, g, out_dtype, tm=512):
    n, d = x.shape
    row = pl.BlockSpec((tm, d), lambda i: (i, 0))
    return pl.pallas_call(
        _add_norm_kernel,
        out_shape=(jax.ShapeDtypeStruct((n, d), F32), jax.ShapeDtypeStruct((n, d), out_dtype)),
        grid=(n // tm,),
        in_specs=[row, row, row, pl.BlockSpec((1, d), lambda i: (0, 0))],
        out_specs=(row, row),
        compiler_params=_cp(("parallel",)),
        name="moe_combine_norm",
    )(x, y1, y2, g.reshape(1, d))


def _matmul_kernel(x_ref, w_ref, o_ref):
    o_ref[...] = _dot(x_ref[...], w_ref[...]).astype(o_ref.dtype)


def _matmul(x, w, tm=512, tn=1024):
    m, k = x.shape
    n = w.shape[1]
    return pl.pallas_call(
        _matmul_kernel,
        out_shape=jax.ShapeDtypeStruct((m, n), F32),
        grid=(n // tn, m // tm),
        in_specs=[pl.BlockSpec((tm, k), lambda j, i: (i, 0)), pl.BlockSpec((k, tn), lambda j, i: (0, j))],
        out_specs=pl.BlockSpec((tm, tn), lambda j, i: (i, j)),
        compiler_params=_cp(("parallel", "parallel"), VMEM_LIMIT),
        name="in_proj",
    )(x, w)


def _pool_kernel(u_ref, halo_ref, w_ref, scale_ref, o_ref, buf_ref, *, tile):
    i = pl.program_id(1)
    u = u_ref[...]
    buf_ref[POOL_HALO:, :] = u
    buf_ref[:POOL_HALO, :] = jnp.where(i > 0, halo_ref[...], 0.0)
    t = i * tile + lax.broadcasted_iota(jnp.int32, (tile, 1), 0)
    outs = []
    for gi, win in enumerate(POOL_WINDOWS):
        cols = slice(gi * POOL_GW, (gi + 1) * POOL_GW)
        s = u[:, cols]
        for j in range(1, win):
            s = s + buf_ref[POOL_HALO - j:POOL_HALO - j + tile, cols]
        cnt = jnp.minimum(t + 1, win).astype(F32)
        pooled = s / cnt - u[:, cols]
        outs.append(_dot(pooled.astype(BF16), w_ref[gi]))
    o_ref[...] = (jnp.concatenate(outs, axis=1) * scale_ref[...]).astype(o_ref.dtype)


def _pool(proj, w_grp, scale, batch, seq, tile=512):
    nt = seq // tile
    hb = tile // POOL_HALO
    return pl.pallas_call(
        functools.partial(_pool_kernel, tile=tile),
        out_shape=jax.ShapeDtypeStruct((batch * seq, POOL_WIDTH), BF16),
        grid=(batch, nt),
        in_specs=[
            pl.BlockSpec((tile, POOL_WIDTH), lambda b, i: (b * nt + i, 0)),
            pl.BlockSpec((POOL_HALO, POOL_WIDTH), lambda b, i: (jnp.maximum((b * nt + i) * hb - 1, 0), 0)),
            pl.BlockSpec((len(POOL_WINDOWS), POOL_GW, POOL_GW), lambda b, i: (0, 0, 0)),
            pl.BlockSpec((1, POOL_WIDTH), lambda b, i: (0, 0)),
        ],
        out_specs=pl.BlockSpec((tile, POOL_WIDTH), lambda b, i: (b * nt + i, 0)),
        scratch_shapes=[pltpu.VMEM((tile + POOL_HALO, POOL_WIDTH), F32)],
        compiler_params=_cp(("parallel", "parallel")),
        name="pool_mixer",
    )(proj, proj, w_grp.astype(BF16), scale.reshape(1, POOL_WIDTH))


def _token_shift(u, halo, mu, first):
    prev_row = jnp.where(first, 0.0, halo[7:8, :])
    rolled = pltpu.roll(u, 1, 0)
    row = lax.broadcasted_iota(jnp.int32, u.shape, 0)
    prev = jnp.where(row == 0, prev_row, rolled)
    return u + (prev - u) * mu


def _rwkv_chunk_kernel(r_ref, k_ref, v_ref, l_ref, rh_ref, kh_ref, vh_ref, lh_ref,
                       mur_ref, muk_ref, muv_ref, mul_ref, w0_ref, wup_ref, a0_ref, aup_ref, gup_ref,
                       kk_ref, ka_ref, rk_ref, bd_ref, qy_ref, mn_ref, g_ref, bonus_ref):
    first = pl.program_id(1) == 0
    c = RW_CHUNK
    r = _token_shift(r_ref[...], rh_ref[...], mur_ref[...], first)
    k = _token_shift(k_ref[...], kh_ref[...], muk_ref[...], first)
    v = _token_shift(v_ref[...], vh_ref[...], muv_ref[...], first)
    lo = _token_shift(l_ref[...], lh_ref[...], mul_ref[...], first)
    wd = lo[:, :DECAY_LORA]
    ad = lo[:, DECAY_LORA:DECAY_LORA + AAA_LORA]
    gd = lo[:, DECAY_LORA + AAA_LORA:RW_LORA]
    z = -(w0_ref[...] + _dot(jnp.tanh(wd), wup_ref[...], HI))
    w_log = -(jnp.maximum(z, 0.0) + jnp.log(1.0 + jnp.exp(-jnp.abs(z)))) - 0.5
    logw = -jnp.exp(w_log)
    a = jax.nn.sigmoid(a0_ref[...] + _dot(ad, aup_ref[...], HI))
    g_ref[0] = _dot(jax.nn.sigmoid(gd), gup_ref[...], HI)
    kkraw = k * kk_ref[...]
    k2 = k * (1.0 + (a - 1.0) * ka_ref[...])
    rkr = r * k2 * rk_ref[...]

    cum_all = logw
    trow = lax.broadcasted_iota(jnp.int32, logw.shape, 0)
    step = 1
    while step < c:
        cum_all = cum_all + jnp.where(trow >= step, pltpu.roll(cum_all, step, 0), 0.0)
        step *= 2

    ti = lax.broadcasted_iota(jnp.int32, (c, c), 0)
    si = lax.broadcasted_iota(jnp.int32, (c, c), 1)
    incl = ti >= si
    strict = ti > si
    eye = ti == si
    zeros = jnp.zeros((c, c), F32)
    bd = bd_ref[...]

    def head_sum(t):
        hi, lo = _split(t)
        return _dot(hi, bd) + _dot(lo, bd)

    kk = kkraw / jnp.maximum(jnp.sqrt(head_sum(kkraw * kkraw)), 1e-12)
    bonus_ref[0] = head_sum(rkr) * v
    cum_end = cum_all[c - 1:c, :]
    ginv = jnp.exp(-cum_all)
    gtail = jnp.exp(cum_end - cum_all)
    gend = jnp.exp(cum_end)
    kka = kk * a
    at = -kk * jnp.exp(cum_all - logw)
    bt = kka * ginv
    kt = k2 * ginv
    rt = r * jnp.exp(cum_all)
    bhat = kka * gtail
    khat = k2 * gtail

    heads = [slice(h * RW_DH, (h + 1) * RW_DH) for h in range(RW_HEADS)]
    stack = lambda x, y, s: jnp.concatenate([x[:, s], y[:, s]], axis=0).astype(BF16)
    gram = [_dot_nt(stack(at, rt, s), stack(bt, kt, s)) for s in heads]
    a_ab = [jnp.where(strict, g[:c, :c], 0.0) for g in gram]
    a_ak = [jnp.where(strict, g[:c, c:], 0.0) for g in gram]
    incl2 = (lax.broadcasted_iota(jnp.int32, (c, 2 * c), 0)
             >= lax.broadcasted_iota(jnp.int32, (c, 2 * c), 1) % c)
    a_r = [jnp.where(incl2, g[c:, :], 0.0) for g in gram]
    p = a_ab
    tinv = [eye.astype(F32) + x for x in p]
    for _ in range(int(math.log2(c)) - 1):
        p = [_bdot(x, x) for x in p]
        tinv = [t + _bdot(t, x) for t, x in zip(tinv, p)]
    av = [_bdot(x, v[:, s]) for x, s in zip(a_ak, heads)]
    w12 = [_bdot(t, jnp.concatenate([at[:, s], x], axis=1)) for t, x, s in zip(tinv, av, heads)]
    zmat = [jnp.concatenate([w, jnp.concatenate([zeros, v[:, s]], axis=1)], axis=0).astype(BF16)
            for w, s in zip(w12, heads)]
    out1 = [_dot(x.astype(BF16), z) for x, z in zip(a_r, zmat)]
    out2 = [_dot_tn(stack(bhat, khat, s), z) for s, z in zip(heads, zmat)]
    qy_ref[0] = jnp.concatenate([o + jnp.concatenate([rt[:, s], zeros], axis=1) for o, s in zip(out1, heads)],
                                axis=1)
    for h, s in enumerate(heads):
        diag = jnp.where(eye, gend[:, s], 0.0)
        mn_ref[0, 0, h] = out2[h] + jnp.concatenate([diag, zeros], axis=1)


def _rwkv_scan_kernel(qy_ref, mn_ref, g_ref, bonus_ref, gnw_ref, gnb_ref, y_ref, st_ref, *, batch):
    @pl.when(pl.program_id(0) == 0)
    def _():
        st_ref[...] = jnp.zeros_like(st_ref)

    for b in range(batch):
        ys = []
        for h in range(RW_HEADS):
            qp = qy_ref[b, :, 2 * RW_DH * h:2 * RW_DH * h + RW_DH]
            y0 = qy_ref[b, :, 2 * RW_DH * h + RW_DH:2 * RW_DH * (h + 1)]
            mn = mn_ref[b, 0, h]
            st = st_ref[b * RW_HEADS + h]
            y = _bdot(qp, st) + y0
            st_ref[b * RW_HEADS + h] = _dot3(mn[:, :RW_DH], st) + mn[:, RW_DH:]
            mean = jnp.mean(y, axis=1, keepdims=True)
            var = jnp.mean(jnp.square(y - mean), axis=1, keepdims=True)
            ys.append((y - mean) * lax.rsqrt(var + RW_GN_EPS))
        yn = jnp.concatenate(ys, axis=1) * gnw_ref[...] + gnb_ref[...]
        y_ref[b] = ((yn + bonus_ref[b]) * g_ref[b]).astype(y_ref.dtype)


def _rwkv(proj, p, batch, seq):
    c = RW_CHUNK
    nc = seq // c
    hb = c // 8
    row512 = lambda col: pl.BlockSpec((c, RW_WIDTH), lambda b, i: (b * nc + i, col))
    halo512 = lambda col: pl.BlockSpec((8, RW_WIDTH), lambda b, i: (jnp.maximum((b * nc + i) * hb - 1, 0), col))
    const = lambda shape: pl.BlockSpec(shape, lambda b, i: (0,) * len(shape))
    vec = const((1, RW_WIDTH))
    out_row = lambda w: pl.BlockSpec((1, c, w), lambda b, i: (b, i, 0))
    qy, mn, g, bonus = pl.pallas_call(
        _rwkv_chunk_kernel,
        out_shape=(jax.ShapeDtypeStruct((batch, seq, 2 * RW_WIDTH), F32),
                   jax.ShapeDtypeStruct((batch, nc, RW_HEADS, RW_DH, 2 * RW_DH), F32),
                   jax.ShapeDtypeStruct((batch, seq, RW_WIDTH), F32),
                   jax.ShapeDtypeStruct((batch, seq, RW_WIDTH), F32)),
        grid=(batch, nc),
        in_specs=[row512(C_R // RW_WIDTH), row512(C_K // RW_WIDTH), row512(C_V // RW_WIDTH),
                  pl.BlockSpec((c, LORA_PAD), lambda b, i: (b * nc + i, C_LORA // LORA_PAD)),
                  halo512(C_R // RW_WIDTH), halo512(C_K // RW_WIDTH), halo512(C_V // RW_WIDTH),
                  pl.BlockSpec((8, LORA_PAD), lambda b, i: (jnp.maximum((b * nc + i) * hb - 1, 0), C_LORA // LORA_PAD)),
                  vec, vec, vec, const((1, LORA_PAD)),
                  vec, const((DECAY_LORA, RW_WIDTH)), vec, const((AAA_LORA, RW_WIDTH)), const((GATE_LORA, RW_WIDTH)),
                  vec, vec, vec, const((RW_WIDTH, RW_WIDTH))],
        out_specs=(out_row(2 * RW_WIDTH),
                   pl.BlockSpec((1, 1, RW_HEADS, RW_DH, 2 * RW_DH), lambda b, i: (b, i, 0, 0, 0)),
                   out_row(RW_WIDTH), out_row(RW_WIDTH)),
        compiler_params=_cp(("parallel", "parallel"), VMEM_LIMIT),
        name="rwkv_chunk",
    )(proj, proj, proj, proj, proj, proj, proj, proj,
      p["mu_r"], p["mu_k"], p["mu_v"], p["mu_l"], p["w0"], p["w_up"], p["a0"], p["a_up"], p["g_up"],
      p["k_k"], p["k_a"], p["r_k"],
      jnp.asarray(np.kron(np.eye(RW_HEADS), np.ones((RW_DH, RW_DH))), BF16))

    full = lambda w: pl.BlockSpec((batch, c, w), lambda i: (0, i, 0))
    return pl.pallas_call(
        functools.partial(_rwkv_scan_kernel, batch=batch),
        out_shape=jax.ShapeDtypeStruct((batch, seq, RW_WIDTH), BF16),
        grid=(nc,),
        in_specs=[full(2 * RW_WIDTH),
                  pl.BlockSpec((batch, 1, RW_HEADS, RW_DH, 2 * RW_DH), lambda i: (0, i, 0, 0, 0)),
                  full(RW_WIDTH), full(RW_WIDTH),
                  pl.BlockSpec((1, RW_WIDTH), lambda i: (0, 0)), pl.BlockSpec((1, RW_WIDTH), lambda i: (0, 0))],
        out_specs=full(RW_WIDTH),
        scratch_shapes=[pltpu.VMEM((batch * RW_HEADS, RW_DH, RW_DH), F32)],
        compiler_params=_cp(("arbitrary",), VMEM_LIMIT),
        name="rwkv_scan",
    )(qy, mn, g, bonus, p["gn_w"], p["gn_b"]).reshape(batch * seq, RW_WIDTH)


def _gelu_tanh(x):
    return 0.5 * x * (1.0 + jnp.tanh(math.sqrt(2.0 / math.pi) * (x + 0.044715 * (x * x * x))))


def _compress_kernel(x_ref, pos_ref, w1_ref, w2_ref, o_ref):
    half = CMP_STRIDE * NSA_DH
    x = x_ref[0, 0, 0]
    w1 = w1_ref[0]
    posb = _dot(pos_ref[0], w1)[0:1, :]
    h1 = _dot(x, w1[:half])
    h2 = _dot(x, w1[half:])
    n = h2.shape[0]
    row = lax.broadcasted_iota(jnp.int32, h2.shape, 0)
    h2s = jnp.where(row < n - 1, pltpu.roll(h2, n - 1, 0), 0.0)
    hid = _gelu_tanh(h1 + h2s + posb)
    o_ref[0, 0, 0] = _dot(hid.astype(BF16), w2_ref[0]).astype(o_ref.dtype)


def _compress(xkv, pos, w1, w2, batch, seq):
    nr = seq // CMP_STRIDE
    wide = CMP_STRIDE * NSA_DH
    return pl.pallas_call(
        _compress_kernel,
        out_shape=jax.ShapeDtypeStruct((2, batch, NSA_G, nr, NSA_DH), BF16),
        grid=(2, batch, NSA_G),
        in_specs=[pl.BlockSpec((1, 1, 1, nr, wide), lambda t, b, g: (t, b, g, 0, 0)),
                  pl.BlockSpec((1, 8, 2 * wide), lambda t, b, g: (t, 0, 0)),
                  pl.BlockSpec((1, 2 * wide, CMP_HIDDEN), lambda t, b, g: (t, 0, 0)),
                  pl.BlockSpec((1, CMP_HIDDEN, NSA_DH), lambda t, b, g: (t, 0, 0))],
        out_specs=pl.BlockSpec((1, 1, 1, nr, NSA_DH), lambda t, b, g: (t, b, g, 0, 0)),
        compiler_params=_cp(("parallel", "parallel", "parallel")),
        name="nsa_compress",
    )(xkv, pos, w1, w2)


def _softmax_parts(s):
    m = jnp.max(s, axis=1, keepdims=True)
    e = jnp.exp(s - m)
    return m, e, jnp.sum(e, axis=1, keepdims=True)


def _lane_tile_fold(x, op, init):
    for t in range(x.shape[1] // 128):
        init = op(init, x[:, 128 * t:128 * (t + 1)])
    return init


def _nsa_kernel(q_ref, gate_ref, kc_ref, vc_ref, ks_ref, vs_ref, kw_ref, vw_ref,
                tblc_ref, tbln_ref, tblw_ref, cfar_ref, ovt_ref, eblk_ref, o_ref, s_ref, madd_ref, *, ncmp):
    i = pl.program_id(2)
    rows = NSA_HG * QB
    qt = q_ref[...]
    q = jnp.concatenate([qt[:, NSA_DH * h:NSA_DH * (h + 1)] for h in range(NSA_HG)], axis=0)
    qb = (q * (NSA_DH ** -0.5)).astype(BF16)
    cfar = cfar_ref[0]
    per_head = lambda s, m: jnp.concatenate([s[h * QB:(h + 1) * QB] + m for h in range(NSA_HG)], axis=0)

    shift = (4 * i - CMP_NEAR // 2 + ncmp) % ncmp
    lc = _dot_nt(qb, kc_ref[0, 0]) + cfar + pltpu.roll(tblc_ref[0], shift, 1)
    tq = i * QB + lax.broadcasted_iota(jnp.int32, (rows, ncmp), 0) % QB
    cidx = lax.broadcasted_iota(jnp.int32, (rows, ncmp), 1)
    vis = tq - (cidx * CMP_STRIDE + CMP_BLOCK - 1) >= 0
    _, e, den = _softmax_parts(jnp.where(vis, lc, NEG))
    pc = jnp.where(vis, e / den, 0.0)
    o_c = _dot(pc.astype(BF16), vc_ref[0, 0])
    pcs = pc[0:QB] + pc[QB:2 * QB] + pc[2 * QB:3 * QB] + pc[3 * QB:4 * QB]
    pcs_hi, pcs_lo = _split(pcs)
    ovt = ovt_ref[...]
    imp = _dot_nt(ovt, pcs_hi) + _dot_nt(ovt, pcs_lo)

    nslc = imp.shape[0]
    nidx = lax.broadcasted_iota(jnp.int32, (nslc, QB), 0)
    valid = nidx <= i
    forced = (nidx == 0) | (nidx == i) | (nidx == i - 1)
    work = jnp.where(valid, imp + jnp.where(forced, FORCE_BONUS, 0.0), -1.0)
    sel = jnp.zeros((nslc, QB), F32)
    for _ in range(N_SELECT):
        m = jnp.max(work, axis=0, keepdims=True)
        first = jnp.min(jnp.where(work == m, nidx, nslc), axis=0, keepdims=True)
        pick = nidx == first
        sel = jnp.where(pick & (m >= 0.0), 1.0, sel)
        work = jnp.where(pick, -2.0, work)

    near_w = NEAR_BLOCKS * SLC_BLOCK
    nn = lax.broadcasted_iota(jnp.int32, (nslc, near_w), 0)
    kb = i - (NEAR_BLOCKS - 1) + lax.broadcasted_iota(jnp.int32, (nslc, near_w), 1) // SLC_BLOCK
    near_onehot = jnp.where(nn == kb, 1.0, 0.0).astype(BF16)
    near_mask = (_dot_tn(sel.astype(BF16), near_onehot) - 1.0) * (-NEG)
    kn = ks_ref[0, 0, pl.ds(pl.multiple_of(i * QB, QB), near_w), :]
    vn = vs_ref[0, 0, pl.ds(pl.multiple_of(i * QB, QB), near_w), :]
    s_near = per_head(_dot_nt(qb, kn) + tbln_ref[0], near_mask)
    m_near = jnp.max(s_near, axis=1, keepdims=True)

    far_w = FAR_CHUNK_BLOCKS * SLC_BLOCK
    pad_s = (NEAR_BLOCKS - 1) * SLC_BLOCK
    sel_far = jnp.where(nidx <= i - NEAR_BLOCKS, sel, 0.0).astype(BF16)
    madd_ref[...] = (_dot_tn(sel_far, eblk_ref[...]) - 1.0) * (-NEG)
    n_far = jnp.maximum(i - (NEAR_BLOCKS - 1) + FAR_CHUNK_BLOCKS - 1, 0) // FAR_CHUNK_BLOCKS

    def far_logits(j, mvec):
        start = pl.multiple_of(j * far_w, far_w)
        kf = ks_ref[0, 0, pl.ds(pl.multiple_of(pad_s + j * far_w, SLC_BLOCK), far_w), :]
        s = per_head(_dot_nt(qb, kf) + cfar, madd_ref[:, pl.ds(start, far_w)])
        s_ref[:, pl.ds(start, far_w)] = s
        return _lane_tile_fold(s, jnp.maximum, mvec)

    mvec = lax.fori_loop(0, n_far, far_logits, jnp.full((rows, 128), NEG, F32))
    m_s = jnp.maximum(m_near, jnp.max(mvec, axis=1, keepdims=True))

    e_near = jnp.exp(s_near - m_s)

    def far_values(j, carry):
        lvec, acc = carry
        start = pl.multiple_of(j * far_w, far_w)
        vf = vs_ref[0, 0, pl.ds(pl.multiple_of(pad_s + j * far_w, SLC_BLOCK), far_w), :]
        e = jnp.exp(s_ref[:, pl.ds(start, far_w)] - m_s)
        return _lane_tile_fold(e, jnp.add, lvec), acc + _dot(e.astype(BF16), vf)

    lvec, acc = lax.fori_loop(0, n_far, far_values,
                              (jnp.zeros((rows, 128), F32), _dot(e_near.astype(BF16), vn)))
    l_s = jnp.sum(e_near, axis=1, keepdims=True) + jnp.sum(lvec, axis=1, keepdims=True)
    o_s = acc / l_s

    win_w = WINDOW + QB
    kw = kw_ref[0, 0, pl.ds(pl.multiple_of(i * QB, QB), win_w), :]
    vw = vw_ref[0, 0, pl.ds(pl.multiple_of(i * QB, QB), win_w), :]
    jj = lax.broadcasted_iota(jnp.int32, (rows, win_w), 1)
    s = _dot_nt(qb, kw) + tblw_ref[0] + jnp.where(jj >= WINDOW - i * QB, 0.0, NEG)
    _, e, den = _softmax_parts(s)
    o_w = _dot(e.astype(BF16), vw) / den

    gates = jax.nn.sigmoid(gate_ref[0, 0])
    outs = []
    for h in range(NSA_HG):
        r0 = slice(h * QB, (h + 1) * QB)
        outs.append(gates[:, 3 * h:3 * h + 1] * o_c[r0] + gates[:, 3 * h + 1:3 * h + 2] * o_s[r0]
                    + gates[:, 3 * h + 2:3 * h + 3] * o_w[r0])
    o_ref[...] = jnp.concatenate(outs, axis=1).astype(o_ref.dtype)


def _bias_tables(rel_bias, ncmp):
    def table(dist, keep):
        tbl = rel_bias[jnp.asarray(_t5_bucket_np(dist))]
        tbl = jnp.where(jnp.asarray(keep)[..., None], tbl, NEG)
        k = dist.shape[1]
        return tbl.transpose(2, 0, 1).reshape(NSA_G, NSA_HG * QB, k)

    qi = np.arange(QB)[:, None]
    dist_c = qi - CMP_STRIDE * (np.arange(CMP_NEAR)[None, :] - CMP_NEAR // 2) - (CMP_BLOCK - 1)
    tblc = rel_bias[jnp.asarray(_t5_bucket_np(dist_c))] - rel_bias[N_BUCKETS - 1]
    tblc = jnp.where(jnp.asarray(dist_c >= 0)[..., None], tblc, 0.0)
    tblc = tblc.transpose(2, 0, 1).reshape(NSA_G, NSA_HG * QB, CMP_NEAR)
    tblc = jnp.pad(tblc, ((0, 0), (0, 0), (0, ncmp - CMP_NEAR)))
    jn = np.arange(NEAR_BLOCKS * SLC_BLOCK)[None, :]
    dist_n = (NEAR_BLOCKS - 1) * SLC_BLOCK + qi - jn
    tbln = table(dist_n, dist_n >= 0)
    jw = np.arange(WINDOW + QB)[None, :]
    dist_w = WINDOW + qi - jw
    tblw = table(dist_w, (dist_w >= 0) & (dist_w < WINDOW))
    cfar = jnp.repeat(rel_bias[N_BUCKETS - 1].reshape(NSA_G, NSA_HG), QB, axis=1).reshape(NSA_G, NSA_HG * QB, 1)
    return tblc, tbln, tblw, cfar


def _nsa(proj, gate_logits, kvc, ks, vs, kw, vw, tables, batch, seq):
    nq = seq // QB
    ncmp = seq // CMP_STRIDE
    nslc = seq // SLC_BLOCK
    tblc, tbln, tblw, cfar = tables
    cstart = np.arange(ncmp) * CMP_STRIDE
    sstart = np.arange(nslc) * SLC_BLOCK
    overlap_t = ((cstart[None, :] <= sstart[:, None] + SLC_BLOCK - 1)
                 & (cstart[None, :] + CMP_BLOCK - 1 >= sstart[:, None])
                 & (cstart[None, :] + CMP_BLOCK <= seq)).astype(np.float32)
    key_block = (np.arange(seq)[None, :] // SLC_BLOCK == np.arange(nslc)[:, None]).astype(np.float32)
    rows = NSA_HG * QB
    kv_spec = lambda n: pl.BlockSpec((1, 1, n, NSA_DH), lambda b, g, i: (b, g, 0, 0))
    tbl_spec = lambda k: pl.BlockSpec((1, rows, k), lambda b, g, i: (g, 0, 0))
    qcol = C_Q // (NSA_HG * NSA_DH)
    return pl.pallas_call(
        functools.partial(_nsa_kernel, ncmp=ncmp),
        out_shape=jax.ShapeDtypeStruct((batch * seq, NSA_HEADS * NSA_DH), BF16),
        grid=(batch, NSA_G, nq),
        in_specs=[pl.BlockSpec((QB, NSA_HG * NSA_DH), lambda b, g, i: (b * nq + i, qcol + g)),
                  pl.BlockSpec((1, 1, QB, 3 * NSA_HG), lambda b, g, i: (b, g, i, 0)),
                  pl.BlockSpec((1, 1, ncmp, NSA_DH), lambda b, g, i: (b, g, 0, 0)),
                  pl.BlockSpec((1, 1, ncmp, NSA_DH), lambda b, g, i: (b, g, 0, 0)),
                  kv_spec(ks.shape[2]), kv_spec(vs.shape[2]), kv_spec(kw.shape[2]), kv_spec(vw.shape[2]),
                  tbl_spec(ncmp), tbl_spec(NEAR_BLOCKS * SLC_BLOCK), tbl_spec(WINDOW + QB), tbl_spec(1),
                  pl.BlockSpec((nslc, ncmp), lambda b, g, i: (0, 0)),
                  pl.BlockSpec((nslc, seq), lambda b, g, i: (0, 0))],
        out_specs=pl.BlockSpec((QB, NSA_HG * NSA_DH), lambda b, g, i: (b * nq + i, g)),
        scratch_shapes=[pltpu.VMEM((rows, seq), F32), pltpu.VMEM((QB, seq), F32)],
        compiler_params=_cp(("parallel", "parallel", "arbitrary"), VMEM_LIMIT),
        name="nsa_attention",
    )(proj, gate_logits, kvc[0], kvc[1], ks, vs, kw, vw, tblc, tbln, tblw, cfar,
      jnp.asarray(overlap_t, BF16), jnp.asarray(key_block, BF16))


def _merge_kernel(h_ref, yp_ref, yr_ref, yn_ref, x_ref, wbp_ref, wbr_ref, wbn_ref, wm_ref, bm_ref, wo_ref,
                  gn_ref, wr_ref, br_ref, xo_ref, h2_ref, route_ref):
    d = D_MODEL
    gl = jax.nn.sigmoid(_dot(h_ref[...], wm_ref[...]) + bm_ref[...])
    merged = (gl[:, :d] * _dot(yp_ref[...], wbp_ref[...]) + gl[:, d:2 * d] * _dot(yr_ref[...], wbr_ref[...])
              + gl[:, 2 * d:] * _dot(yn_ref[...], wbn_ref[...]))
    x = x_ref[...] + _dot(merged.astype(BF16), wo_ref[...])
    xo_ref[...] = x
    h2 = _rms(x, gn_ref[...])
    h2_ref[...] = h2.astype(h2_ref.dtype)

    logits = _dot(h2, wr_ref[...], HI) + br_ref[...]
    lane = lax.broadcasted_iota(jnp.int32, logits.shape, 1)
    big = logits.shape[1]
    lg = jnp.where(lane < N_GROUPS, logits, NEG)
    mg = jnp.max(lg, axis=1, keepdims=True)
    p_top = 1.0 / jnp.sum(jnp.exp(lg - mg), axis=1, keepdims=True)
    grp = jnp.min(jnp.where(lg == mg, lane, big), axis=1, keepdims=True)
    lo = N_GROUPS + EPG * grp
    le = jnp.where((lane >= lo) & (lane < lo + EPG), logits, NEG)
    e1 = jnp.max(le, axis=1, keepdims=True)
    i1 = jnp.min(jnp.where(le == e1, lane, big), axis=1, keepdims=True)
    le = jnp.where(lane == i1, NEG, le)
    e2 = jnp.max(le, axis=1, keepdims=True)
    i2 = jnp.min(jnp.where(le == e2, lane, big), axis=1, keepdims=True)
    t = jnp.exp(e2 - e1)
    w1 = p_top / (1.0 + t)
    w2 = p_top * t / (1.0 + t)
    route_ref[...] = jnp.where(lane == 0, (i1 - N_GROUPS).astype(F32),
                               jnp.where(lane == 1, (i2 - N_GROUPS).astype(F32),
                                         jnp.where(lane == 2, w1, jnp.where(lane == 3, w2, 0.0))))


def _merge(h, y_pool, y_rwkv, y_nsa, x, p, tm=256):
    n, d = x.shape
    row = lambda w: pl.BlockSpec((tm, w), lambda i: (i, 0))
    const = lambda a: pl.BlockSpec(a.shape, lambda i: (0, 0))
    ws = [p["wb_pool"], p["wb_rwkv"], p["wb_nsa"], p["w_merge"], p["b_merge"], p["w_out"],
          p["norm_ffn"], p["w_router"], p["b_router"]]
    return pl.pallas_call(
        _merge_kernel,
        out_shape=(jax.ShapeDtypeStruct((n, d), F32), jax.ShapeDtypeStruct((n, d), F32),
                   jax.ShapeDtypeStruct((n, 128), F32)),
        grid=(n // tm,),
        in_specs=[row(d), row(POOL_WIDTH), row(RW_WIDTH), row(d), row(d)] + [const(w) for w in ws],
        out_specs=(row(d), row(d), row(128)),
        compiler_params=_cp(("parallel",), VMEM_LIMIT),
        name="merge_router",
    )(h, y_pool, y_rwkv, y_nsa, x, *ws)


def _expert_kernel(te_ref, nt_ref, x_ref, w_ref, wg_ref, wu_ref, wd_ref, o_ref):
    i = pl.program_id(0)

    @pl.when(i < nt_ref[0])
    def _():
        xb = x_ref[...].astype(BF16)
        gate = _dot(xb, wg_ref[0])
        hid = gate * jax.nn.sigmoid(gate) * _dot(xb, wu_ref[0])
        o_ref[...] = w_ref[...] * _dot(hid.astype(BF16), wd_ref[0])

    @pl.when(i >= nt_ref[0])
    def _():
        o_ref[...] = jnp.zeros_like(o_ref)


def _experts(xs, roww, tile_expert, n_tiles, wg, wu, wd):
    r, d = xs.shape
    tm = MOE_TM
    return pl.pallas_call(
        _expert_kernel,
        out_shape=jax.ShapeDtypeStruct((r, d), F32),
        grid_spec=pltpu.PrefetchScalarGridSpec(
            num_scalar_prefetch=2,
            grid=(r // tm,),
            in_specs=[pl.BlockSpec((tm, d), lambda i, te, nt: (i, 0)),
                      pl.BlockSpec((tm, 1), lambda i, te, nt: (i, 0)),
                      pl.BlockSpec((1, d, D_EXPERT), lambda i, te, nt: (te[i], 0, 0)),
                      pl.BlockSpec((1, d, D_EXPERT), lambda i, te, nt: (te[i], 0, 0)),
                      pl.BlockSpec((1, D_EXPERT, d), lambda i, te, nt: (te[i], 0, 0))],
            out_specs=pl.BlockSpec((tm, d), lambda i, te, nt: (i, 0))),
        compiler_params=_cp(("arbitrary",), VMEM_LIMIT),
        name="moe_experts",
    )(tile_expert, n_tiles, xs, roww, wg, wu, wd)


def _moe(h2, route, wg, wu, wd):
    n = h2.shape[0]
    tm = MOE_TM
    r = 2 * n + N_EXPERTS * tm
    ids = route[:, 0:2].astype(jnp.int32).reshape(-1)
    wts = route[:, 2:4].reshape(-1)
    onehot = (ids[:, None] == jnp.arange(N_EXPERTS)[None, :]).astype(jnp.int32)
    rank = jnp.sum((jnp.cumsum(onehot, axis=0) - onehot) * onehot, axis=1)
    counts = jnp.sum(onehot, axis=0)
    tiles = (counts + tm - 1) // tm
    tile_end = jnp.cumsum(tiles)
    starts = (tile_end - tiles) * tm
    pos = starts[ids] + rank
    tok = jnp.arange(2 * n, dtype=jnp.int32) // 2
    rowtok = jnp.zeros((r,), jnp.int32).at[pos].set(tok)
    roww = jnp.zeros((r,), F32).at[pos].set(wts)
    n_tiles = tile_end[-1:].astype(jnp.int32)
    tile_expert = jnp.minimum(jnp.sum(tile_end[None, :] <= jnp.arange(r // tm)[:, None], axis=1),
                              N_EXPERTS - 1).astype(jnp.int32)
    ys = _experts(h2[rowtok], roww.reshape(r, 1), tile_expert, n_tiles, wg, wu, wd)
    return ys[pos[0::2]], ys[pos[1::2]]


def _layer_params(l, a):
    f = lambda t: t[l]
    row = lambda t: t[l].reshape(1, -1)
    w_in = a["w_in"][l]
    w_in_p = jnp.concatenate([w_in[:, :SRC_RW_END], jnp.zeros((D_MODEL, C_Q - SRC_RW_END), F32),
                              w_in[:, SRC_RW_END:],
                              jnp.zeros((D_MODEL, P_COLS - C_Q - (w_in.shape[1] - SRC_RW_END)), F32)], axis=1)
    mu = a["rw_mu"][l]
    wb = a["w_branch"][l].astype(BF16)
    w_router = jnp.zeros((D_MODEL, 128), F32)
    w_router = w_router.at[:, :N_GROUPS].set(a["w_router_grp"][l]).at[:, N_GROUPS:N_GROUPS + N_EXPERTS].set(
        a["w_router_exp"][l])
    b_router = jnp.zeros((1, 128), F32)
    b_router = b_router.at[0, :N_GROUPS].set(a["b_router_grp"][l]).at[0, N_GROUPS:N_GROUPS + N_EXPERTS].set(
        a["b_router_exp"][l])
    pos = jnp.stack([a["cmp_pos_k"][l].reshape(-1), a["cmp_pos_v"][l].reshape(-1)])
    return {
        "w_in": w_in_p.astype(BF16),
        "pool_w": f(a["pool_w"]), "pool_scale": f(a["pool_scale"]),
        "mu_r": mu[None, 0:RW_WIDTH], "mu_k": mu[None, RW_WIDTH:2 * RW_WIDTH],
        "mu_v": mu[None, 2 * RW_WIDTH:3 * RW_WIDTH],
        "mu_l": jnp.concatenate([mu[3 * RW_WIDTH:], jnp.zeros((LORA_PAD - RW_LORA,), F32)])[None],
        "w0": row(a["rw_w0"]), "w_up": f(a["rw_w_up"]), "a0": row(a["rw_a0"]), "a_up": f(a["rw_a_up"]),
        "g_up": f(a["rw_g_up"]), "k_k": row(a["rw_k_k"]), "k_a": row(a["rw_k_a"]), "r_k": row(a["rw_r_k"]),
        "gn_w": row(a["rw_gn_w"]), "gn_b": row(a["rw_gn_b"]),
        "cmp_pos": jnp.broadcast_to(pos[:, None, :], (2, 8, pos.shape[1])).astype(BF16),
        "cmp_w1": jnp.stack([a["cmp_w1_k"][l], a["cmp_w1_v"][l]]).astype(BF16),
        "cmp_w2": jnp.stack([a["cmp_w2_k"][l], a["cmp_w2_v"][l]]).astype(BF16),
        "wb_pool": wb[:POOL_WIDTH], "wb_rwkv": wb[POOL_WIDTH:POOL_WIDTH + RW_WIDTH],
        "wb_nsa": wb[POOL_WIDTH + RW_WIDTH:],
        "w_merge": a["w_merge"][l].astype(BF16), "b_merge": row(a["b_merge"]),
        "w_out": a["w_out"][l].astype(BF16), "norm_ffn": row(a["norm_ffn"]),
        "w_router": w_router, "b_router": b_router,
        "w_gate": a["w_exp_gate"][l].astype(BF16), "w_up_e": a["w_exp_up"][l].astype(BF16),
        "w_down": a["w_exp_down"][l].astype(BF16),
    }


def _mixers(proj, p, tables, batch, seq):
    y_pool = _pool(proj, p["pool_w"], p["pool_scale"], batch, seq)
    y_rwkv = _rwkv(proj, p, batch, seq)
    kv = proj[:, C_KV:C_GATE].reshape(batch, seq, 6, NSA_G, NSA_DH).transpose(2, 0, 3, 1, 4).astype(BF16)
    xkv = kv[0:2].reshape(2, batch, NSA_G, seq // CMP_STRIDE, CMP_STRIDE * NSA_DH)
    kvc = _compress(xkv, p["cmp_pos"], p["cmp_w1"], p["cmp_w2"], batch, seq)
    pad = lambda t, n: jnp.pad(t, ((0, 0), (0, 0), (n, 0), (0, 0)))
    pad_s = (NEAR_BLOCKS - 1) * SLC_BLOCK
    gate_logits = proj[:, C_GATE:C_GATE + 3 * NSA_HEADS].reshape(batch, seq, NSA_G, 3 * NSA_HG).transpose(0, 2, 1, 3)
    y_nsa = _nsa(proj, gate_logits, kvc, pad(kv[2], pad_s), pad(kv[3], pad_s), pad(kv[4], WINDOW),
                 pad(kv[5], WINDOW), tables, batch, seq)
    return y_pool, y_rwkv, y_nsa


def kernel(x, rel_bias, norm_mix, w_in, pool_w, pool_scale, rw_mu, rw_w0, rw_w_up, rw_a0, rw_a_up, rw_g_up, rw_k_k, rw_k_a, rw_r_k, rw_gn_w, rw_gn_b, cmp_pos_k, cmp_w1_k, cmp_w2_k, cmp_pos_v, cmp_w1_v, cmp_w2_v, w_branch, w_merge, b_merge, w_out, norm_ffn, w_router_grp, b_router_grp, w_router_exp, b_router_exp, w_exp_gate, w_exp_up, w_exp_down, norm_final):
    a = dict(w_in=w_in, pool_w=pool_w, pool_scale=pool_scale, rw_mu=rw_mu, rw_w0=rw_w0, rw_w_up=rw_w_up,
             rw_a0=rw_a0, rw_a_up=rw_a_up, rw_g_up=rw_g_up, rw_k_k=rw_k_k, rw_k_a=rw_k_a, rw_r_k=rw_r_k,
             rw_gn_w=rw_gn_w, rw_gn_b=rw_gn_b, cmp_pos_k=cmp_pos_k, cmp_w1_k=cmp_w1_k, cmp_w2_k=cmp_w2_k,
             cmp_pos_v=cmp_pos_v, cmp_w1_v=cmp_w1_v, cmp_w2_v=cmp_w2_v, w_branch=w_branch, w_merge=w_merge,
             b_merge=b_merge, w_out=w_out, norm_ffn=norm_ffn, w_router_grp=w_router_grp,
             b_router_grp=b_router_grp, w_router_exp=w_router_exp, b_router_exp=b_router_exp,
             w_exp_gate=w_exp_gate, w_exp_up=w_exp_up, w_exp_down=w_exp_down)
    batch, seq, d = x.shape
    depth = norm_mix.shape[0]
    tables = _bias_tables(rel_bias, seq // CMP_STRIDE)
    xf = x.reshape(batch * seq, d)
    h = _norm(xf, norm_mix[0], BF16)
    for l in range(depth):
        p = _layer_params(l, a)
        proj = _matmul(h, p["w_in"])
        y_pool, y_rwkv, y_nsa = _mixers(proj, p, tables, batch, seq)
        xf, h2, route = _merge(h, y_pool, y_rwkv, y_nsa, xf, p)
        y1, y2 = _moe(h2, route, p["w_gate"], p["w_up_e"], p["w_down"])
        last = l == depth - 1
        g_next = norm_final if last else norm_mix[l + 1]
        xf, h = _add_norm(xf, y1, y2, g_next, F32 if last else BF16)
    return h.reshape(batch, seq, d)
```

```python
import functools
import math

import jax
import jax.numpy as jnp
import numpy as np
from jax import lax
from jax.experimental import pallas as pl
from jax.experimental.pallas import tpu as pltpu

F32 = jnp.float32
BF16 = jnp.bfloat16
HI = lax.Precision.HIGHEST

D_MODEL = 1024
RMS_EPS = 1e-6
NEG = -1e30

POOL_WINDOWS = (2, 4, 8, 16)
POOL_WIDTH = 512
POOL_GW = 128
POOL_HALO = 16

RW_HEADS = 8
RW_DH = 64
RW_WIDTH = 512
DECAY_LORA, AAA_LORA, GATE_LORA = 32, 32, 96
RW_LORA = DECAY_LORA + AAA_LORA + GATE_LORA
RW_COLS = 3 * RW_WIDTH + RW_LORA
RW_GN_EPS = 64e-5
RW_CHUNK = 64

NSA_DH = 64
NSA_HEADS = 16
NSA_G = 4
NSA_HG = 4
NSA_KVW = NSA_G * NSA_DH
CMP_BLOCK, CMP_STRIDE, CMP_HIDDEN = 32, 16, 256
SLC_BLOCK = 64
N_SELECT = 8
WINDOW = 512
QB = 64
FORCE_BONUS = 1e3
N_BUCKETS, MAX_EXACT, MAX_DISTANCE = 32, 16, 128
NEAR_BLOCKS = 3
FAR_CHUNK_BLOCKS = 8
CMP_NEAR = 32
CMP_AHEAD = (QB - CMP_BLOCK) // CMP_STRIDE + 1

N_GROUPS, EPG, N_EXPERTS, D_EXPERT = 4, 8, 32, 256
MOE_TM = 256

C_POOL, C_R, C_K, C_V, C_LORA, C_Q, C_KV, C_GATE, P_COLS = 0, 512, 1024, 1536, 2048, 2304, 3328, 4864, 5120
LORA_PAD = 256
SRC_RW_END = POOL_WIDTH + RW_COLS

VMEM_LIMIT = 56 * 1024 * 1024


def _t5_bucket_np(dist):
    n = np.maximum(dist, 0)
    nf = np.maximum(n, 1).astype(np.float32)
    large = MAX_EXACT + (np.log(nf / MAX_EXACT) / math.log(MAX_DISTANCE / MAX_EXACT)
                         * (N_BUCKETS - MAX_EXACT)).astype(np.int32)
    large = np.minimum(large, N_BUCKETS - 1)
    return np.where(n < MAX_EXACT, n, large)


def _cp(sem, vmem=None):
    return pltpu.CompilerParams(dimension_semantics=sem, vmem_limit_bytes=vmem)


def _dot(a, b, precision=None):
    return jnp.dot(a, b, preferred_element_type=F32, precision=precision)


def _dot_nt(a, b, precision=None):
    return lax.dot_general(a, b, (((1,), (1,)), ((), ())), preferred_element_type=F32, precision=precision)


def _dot_tn(a, b, precision=None):
    return lax.dot_general(a, b, (((0,), (0,)), ((), ())), preferred_element_type=F32, precision=precision)


def _bdot(a, b):
    return _dot(a.astype(BF16), b.astype(BF16))


def _bdot_nt(a, b):
    return _dot_nt(a.astype(BF16), b.astype(BF16))


def _bdot_tn(a, b):
    return _dot_tn(a.astype(BF16), b.astype(BF16))


def _split(a):
    hi = a.astype(BF16)
    return hi, (a - hi.astype(F32)).astype(BF16)


def _dot3(a, b):
    ah, al = _split(a)
    bh, bl = _split(b)
    return _dot(ah, bh) + (_dot(ah, bl) + _dot(al, bh))


def _rms(x, g):
    return x * lax.rsqrt(jnp.mean(x * x, axis=-1, keepdims=True) + RMS_EPS) * g


def _norm_kernel(x_ref, g_ref, h_ref):
    h_ref[...] = _rms(x_ref[...], g_ref[...]).astype(h_ref.dtype)


def _norm(x, g, out_dtype, tm=512):
    n, d = x.shape
    return pl.pallas_call(
        _norm_kernel,
        out_shape=jax.ShapeDtypeStruct((n, d), out_dtype),
        grid=(n // tm,),
        in_specs=[pl.BlockSpec((tm, d), lambda i: (i, 0)), pl.BlockSpec((1, d), lambda i: (0, 0))],
        out_specs=pl.BlockSpec((tm, d), lambda i: (i, 0)),
        compiler_params=_cp(("parallel",)),
        name="rms_norm",
    )(x, g.reshape(1, d))


def _add_norm_kernel(x_ref, y1_ref, y2_ref, g_ref, xo_ref, h_ref):
    x = x_ref[...] + (y1_ref[...] + y2_ref[...])
    xo_ref[...] = x
    h_ref[...] = _rms(x, g_ref[...]).astype(h_ref.dtype)


def _add_norm(x, y1, y2, g, out_dtype, tm=512):
    n, d = x.shape
    row = pl.BlockSpec((tm, d), lambda i: (i, 0))
    return pl.pallas_call(
        _add_norm_kernel,
        out_shape=(jax.ShapeDtypeStruct((n, d), F32), jax.ShapeDtypeStruct((n, d), out_dtype)),
        grid=(n // tm,),
        in_specs=[row, row, row, pl.BlockSpec((1, d), lambda i: (0, 0))],
        out_specs=(row, row),
        compiler_params=_cp(("parallel",)),
        name="moe_combine_norm",
    )(x, y1, y2, g.reshape(1, d))


def _matmul_kernel(x_ref, w_ref, o_ref):
    o_ref[...] = _dot(x_ref[...], w_ref[...]).astype(o_ref.dtype)


def _matmul(x, w, tm=512, tn=1024):
    m, k = x.shape
    n = w.shape[1]
    return pl.pallas_call(
        _matmul_kernel,
        out_shape=jax.ShapeDtypeStruct((m, n), F32),
        grid=(n // tn, m // tm),
        in_specs=[pl.BlockSpec((tm, k), lambda j, i: (i, 0)), pl.BlockSpec((k, tn), lambda j, i: (0, j))],
        out_specs=pl.BlockSpec((tm, tn), lambda j, i: (i, j)),
        compiler_params=_cp(("parallel", "parallel"), VMEM_LIMIT),
        name="in_proj",
    )(x, w)


def _pool_kernel(u_ref, halo_ref, w_ref, scale_ref, o_ref, buf_ref, *, tile):
    i = pl.program_id(1)
    u = u_ref[...]
    buf_ref[POOL_HALO:, :] = u
    buf_ref[:POOL_HALO, :] = jnp.where(i > 0, halo_ref[...], 0.0)
    t = i * tile + lax.broadcasted_iota(jnp.int32, (tile, 1), 0)
    outs = []
    for gi, win in enumerate(POOL_WINDOWS):
        cols = slice(gi * POOL_GW, (gi + 1) * POOL_GW)
        s = u[:, cols]
        for j in range(1, win):
            s = s + buf_ref[POOL_HALO - j:POOL_HALO - j + tile, cols]
        cnt = jnp.minimum(t + 1, win).astype(F32)
        pooled = s / cnt - u[:, cols]
        outs.append(_dot(pooled.astype(BF16), w_ref[gi]))
    o_ref[...] = (jnp.concatenate(outs, axis=1) * scale_ref[...]).astype(o_ref.dtype)


def _pool(proj, w_grp, scale, batch, seq, tile=512):
    nt = seq // tile
    hb = tile // POOL_HALO
    return pl.pallas_call(
        functools.partial(_pool_kernel, tile=tile),
        out_shape=jax.ShapeDtypeStruct((batch * seq, POOL_WIDTH), BF16),
        grid=(batch, nt),
        in_specs=[
            pl.BlockSpec((tile, POOL_WIDTH), lambda b, i: (b * nt + i, 0)),
            pl.BlockSpec((POOL_HALO, POOL_WIDTH), lambda b, i: (jnp.maximum((b * nt + i) * hb - 1, 0), 0)),
            pl.BlockSpec((len(POOL_WINDOWS), POOL_GW, POOL_GW), lambda b, i: (0, 0, 0)),
            pl.BlockSpec((1, POOL_WIDTH), lambda b, i: (0, 0)),
        ],
        out_specs=pl.BlockSpec((tile, POOL_WIDTH), lambda b, i: (b * nt + i, 0)),
        scratch_shapes=[pltpu.VMEM((tile + POOL_HALO, POOL_WIDTH), F32)],
        compiler_params=_cp(("parallel", "parallel")),
        name="pool_mixer",
    )(proj, proj, w_grp.astype(BF16), scale.reshape(1, POOL_WIDTH))


def _token_shift(u, halo, mu, first):
    prev_row = jnp.where(first, 0.0, halo[7:8, :])
    rolled = pltpu.roll(u, 1, 0)
    row = lax.broadcasted_iota(jnp.int32, u.shape, 0)
    prev = jnp.where(row == 0, prev_row, rolled)
    return u + (prev - u) * mu


def _rwkv_chunk_kernel(r_ref, k_ref, v_ref, l_ref, rh_ref, kh_ref, vh_ref, lh_ref,
                       mur_ref, muk_ref, muv_ref, mul_ref, w0_ref, wup_ref, a0_ref, aup_ref, gup_ref,
                       kk_ref, ka_ref, rk_ref, bd_ref, qy_ref, mn_ref, g_ref, bonus_ref):
    first = pl.program_id(1) == 0
    c = RW_CHUNK
    r = _token_shift(r_ref[...], rh_ref[...], mur_ref[...], first)
    k = _token_shift(k_ref[...], kh_ref[...], muk_ref[...], first)
    v = _token_shift(v_ref[...], vh_ref[...], muv_ref[...], first)
    lo = _token_shift(l_ref[...], lh_ref[...], mul_ref[...], first)
    wd = lo[:, :DECAY_LORA]
    ad = lo[:, DECAY_LORA:DECAY_LORA + AAA_LORA]
    gd = lo[:, DECAY_LORA + AAA_LORA:RW_LORA]
    z = -(w0_ref[...] + _dot(jnp.tanh(wd), wup_ref[...], HI))
    w_log = -(jnp.maximum(z, 0.0) + jnp.log(1.0 + jnp.exp(-jnp.abs(z)))) - 0.5
    logw = -jnp.exp(w_log)
    a = jax.nn.sigmoid(a0_ref[...] + _dot(ad, aup_ref[...], HI))
    g_ref[0] = _dot(jax.nn.sigmoid(gd), gup_ref[...], HI)
    kkraw = k * kk_ref[...]
    k2 = k * (1.0 + (a - 1.0) * ka_ref[...])
    rkr = r * k2 * rk_ref[...]

    cum_all = logw
    trow = lax.broadcasted_iota(jnp.int32, logw.shape, 0)
    step = 1
    while step < c:
        cum_all = cum_all + jnp.where(trow >= step, pltpu.roll(cum_all, step, 0), 0.0)
        step *= 2

    ti = lax.broadcasted_iota(jnp.int32, (c, c), 0)
    si = lax.broadcasted_iota(jnp.int32, (c, c), 1)
    incl = ti >= si
    strict = ti > si
    eye = ti == si
    zeros = jnp.zeros((c, c), F32)
    bd = bd_ref[...]

    def head_sum(t):
        hi, lo = _split(t)
        return _dot(hi, bd) + _dot(lo, bd)

    kk = kkraw / jnp.maximum(jnp.sqrt(head_sum(kkraw * kkraw)), 1e-12)
    bonus_ref[0] = head_sum(rkr) * v
    cum_end = cum_all[c - 1:c, :]
    ginv = jnp.exp(-cum_all)
    gtail = jnp.exp(cum_end - cum_all)
    gend = jnp.exp(cum_end)
    kka = kk * a
    at = -kk * jnp.exp(cum_all - logw)
    bt = kka * ginv
    kt = k2 * ginv
    rt = r * jnp.exp(cum_all)
    bhat = kka * gtail
    khat = k2 * gtail

    heads = [slice(h * RW_DH, (h + 1) * RW_DH) for h in range(RW_HEADS)]
    stack = lambda x, y, s: jnp.concatenate([x[:, s], y[:, s]], axis=0).astype(BF16)
    gram = [_dot_nt(stack(at, rt, s), stack(bt, kt, s)) for s in heads]
    a_ab = [jnp.where(strict, g[:c, :c], 0.0) for g in gram]
    a_ak = [jnp.where(strict, g[:c, c:], 0.0) for g in gram]
    incl2 = (lax.broadcasted_iota(jnp.int32, (c, 2 * c), 0)
             >= lax.broadcasted_iota(jnp.int32, (c, 2 * c), 1) % c)
    a_r = [jnp.where(incl2, g[c:, :], 0.0) for g in gram]
    p = a_ab
    tinv = [eye.astype(F32) + x for x in p]
    for _ in range(int(math.log2(c)) - 1):
        p = [_bdot(x, x) for x in p]
        tinv = [t + _bdot(t, x) for t, x in zip(tinv, p)]
    av = [_bdot(x, v[:, s]) for x, s in zip(a_ak, heads)]
    w12 = [_bdot(t, jnp.concatenate([at[:, s], x], axis=1)) for t, x, s in zip(tinv, av, heads)]
    zmat = [jnp.concatenate([w, jnp.concatenate([zeros, v[:, s]], axis=1)], axis=0).astype(BF16)
            for w, s in zip(w12, heads)]
    out1 = [_dot(x.astype(BF16), z) for x, z in zip(a_r, zmat)]
    out2 = [_dot_tn(stack(bhat, khat, s), z) for s, z in zip(heads, zmat)]
    qy_ref[0] = jnp.concatenate([o + jnp.concatenate([rt[:, s], zeros], axis=1) for o, s in zip(out1, heads)],
                                axis=1)
    for h, s in enumerate(heads):
        diag = jnp.where(eye, gend[:, s], 0.0)
        mn_ref[0, 0, h] = out2[h] + jnp.concatenate([diag, zeros], axis=1)


def _rwkv_scan_kernel(qy_ref, mn_ref, g_ref, bonus_ref, gnw_ref, gnb_ref, bd_ref, y_ref, st_ref, *, batch):
    @pl.when(pl.program_id(0) == 0)
    def _():
        st_ref[...] = jnp.zeros_like(st_ref)

    bd = bd_ref[...]

    def head_mean(t):
        hi, lo = _split(t)
        return (_dot(hi, bd) + _dot(lo, bd)) * (1.0 / RW_DH)

    pairs = [(b, h) for b in range(batch) for h in range(RW_HEADS)]
    sts = [st_ref[b * RW_HEADS + h] for b, h in pairs]
    ys = [_bdot(qy_ref[b, :, 2 * RW_DH * h:2 * RW_DH * h + RW_DH], st)
          + qy_ref[b, :, 2 * RW_DH * h + RW_DH:2 * RW_DH * (h + 1)] for (b, h), st in zip(pairs, sts)]
    for (b, h), st in zip(pairs, sts):
        mn = mn_ref[b, 0, h]
        st_ref[b * RW_HEADS + h] = _dot3(mn[:, :RW_DH], st) + mn[:, RW_DH:]
    for b in range(batch):
        y = jnp.concatenate(ys[b * RW_HEADS:(b + 1) * RW_HEADS], axis=1)
        dev = y - head_mean(y)
        yn = dev * lax.rsqrt(head_mean(dev * dev) + RW_GN_EPS) * gnw_ref[...] + gnb_ref[...]
        y_ref[b] = ((yn + bonus_ref[b]) * g_ref[b]).astype(y_ref.dtype)


def _rwkv(proj, p, batch, seq):
    head_ones = jnp.asarray(np.kron(np.eye(RW_HEADS), np.ones((RW_DH, RW_DH))), BF16)
    c = RW_CHUNK
    nc = seq // c
    hb = c // 8
    row512 = lambda col: pl.BlockSpec((c, RW_WIDTH), lambda b, i: (b * nc + i, col))
    halo512 = lambda col: pl.BlockSpec((8, RW_WIDTH), lambda b, i: (jnp.maximum((b * nc + i) * hb - 1, 0), col))
    const = lambda shape: pl.BlockSpec(shape, lambda b, i: (0,) * len(shape))
    vec = const((1, RW_WIDTH))
    out_row = lambda w: pl.BlockSpec((1, c, w), lambda b, i: (b, i, 0))
    qy, mn, g, bonus = pl.pallas_call(
        _rwkv_chunk_kernel,
        out_shape=(jax.ShapeDtypeStruct((batch, seq, 2 * RW_WIDTH), F32),
                   jax.ShapeDtypeStruct((batch, nc, RW_HEADS, RW_DH, 2 * RW_DH), F32),
                   jax.ShapeDtypeStruct((batch, seq, RW_WIDTH), F32),
                   jax.ShapeDtypeStruct((batch, seq, RW_WIDTH), F32)),
        grid=(batch, nc),
        in_specs=[row512(C_R // RW_WIDTH), row512(C_K // RW_WIDTH), row512(C_V // RW_WIDTH),
                  pl.BlockSpec((c, LORA_PAD), lambda b, i: (b * nc + i, C_LORA // LORA_PAD)),
                  halo512(C_R // RW_WIDTH), halo512(C_K // RW_WIDTH), halo512(C_V // RW_WIDTH),
                  pl.BlockSpec((8, LORA_PAD), lambda b, i: (jnp.maximum((b * nc + i) * hb - 1, 0), C_LORA // LORA_PAD)),
                  vec, vec, vec, const((1, LORA_PAD)),
                  vec, const((DECAY_LORA, RW_WIDTH)), vec, const((AAA_LORA, RW_WIDTH)), const((GATE_LORA, RW_WIDTH)),
                  vec, vec, vec, const((RW_WIDTH, RW_WIDTH))],
        out_specs=(out_row(2 * RW_WIDTH),
                   pl.BlockSpec((1, 1, RW_HEADS, RW_DH, 2 * RW_DH), lambda b, i: (b, i, 0, 0, 0)),
                   out_row(RW_WIDTH), out_row(RW_WIDTH)),
        compiler_params=_cp(("parallel", "parallel"), VMEM_LIMIT),
        name="rwkv_chunk",
    )(proj, proj, proj, proj, proj, proj, proj, proj,
      p["mu_r"], p["mu_k"], p["mu_v"], p["mu_l"], p["w0"], p["w_up"], p["a0"], p["a_up"], p["g_up"],
      p["k_k"], p["k_a"], p["r_k"], head_ones)

    full = lambda w: pl.BlockSpec((batch, c, w), lambda i: (0, i, 0))
    return pl.pallas_call(
        functools.partial(_rwkv_scan_kernel, batch=batch),
        out_shape=jax.ShapeDtypeStruct((batch, seq, RW_WIDTH), BF16),
        grid=(nc,),
        in_specs=[full(2 * RW_WIDTH),
                  pl.BlockSpec((batch, 1, RW_HEADS, RW_DH, 2 * RW_DH), lambda i: (0, i, 0, 0, 0)),
                  full(RW_WIDTH), full(RW_WIDTH),
                  pl.BlockSpec((1, RW_WIDTH), lambda i: (0, 0)), pl.BlockSpec((1, RW_WIDTH), lambda i: (0, 0)),
                  pl.BlockSpec((RW_WIDTH, RW_WIDTH), lambda i: (0, 0))],
        out_specs=full(RW_WIDTH),
        scratch_shapes=[pltpu.VMEM((batch * RW_HEADS, RW_DH, RW_DH), F32)],
        compiler_params=_cp(("arbitrary",), VMEM_LIMIT),
        name="rwkv_scan",
    )(qy, mn, g, bonus, p["gn_w"], p["gn_b"], head_ones).reshape(batch * seq, RW_WIDTH)


def _gelu_tanh(x):
    return 0.5 * x * (1.0 + jnp.tanh(math.sqrt(2.0 / math.pi) * (x + 0.044715 * (x * x * x))))


def _compress_kernel(x_ref, pos_ref, w1_ref, w2_ref, o_ref):
    half = CMP_STRIDE * NSA_DH
    x = x_ref[0, 0, 0]
    w1 = w1_ref[0]
    posb = _dot(pos_ref[0], w1)[0:1, :]
    h1 = _dot(x, w1[:half])
    h2 = _dot(x, w1[half:])
    n = h2.shape[0]
    row = lax.broadcasted_iota(jnp.int32, h2.shape, 0)
    h2s = jnp.where(row < n - 1, pltpu.roll(h2, n - 1, 0), 0.0)
    hid = _gelu_tanh(h1 + h2s + posb)
    o_ref[0, 0, 0] = _dot(hid.astype(BF16), w2_ref[0]).astype(o_ref.dtype)


def _compress(xkv, pos, w1, w2, batch, seq):
    nr = seq // CMP_STRIDE
    wide = CMP_STRIDE * NSA_DH
    return pl.pallas_call(
        _compress_kernel,
        out_shape=jax.ShapeDtypeStruct((2, batch, NSA_G, nr, NSA_DH), BF16),
        grid=(2, batch, NSA_G),
        in_specs=[pl.BlockSpec((1, 1, 1, nr, wide), lambda t, b, g: (t, b, g, 0, 0)),
                  pl.BlockSpec((1, 8, 2 * wide), lambda t, b, g: (t, 0, 0)),
                  pl.BlockSpec((1, 2 * wide, CMP_HIDDEN), lambda t, b, g: (t, 0, 0)),
                  pl.BlockSpec((1, CMP_HIDDEN, NSA_DH), lambda t, b, g: (t, 0, 0))],
        out_specs=pl.BlockSpec((1, 1, 1, nr, NSA_DH), lambda t, b, g: (t, b, g, 0, 0)),
        compiler_params=_cp(("parallel", "parallel", "parallel")),
        name="nsa_compress",
    )(xkv, pos, w1, w2)


def _softmax_parts(s):
    m = jnp.max(s, axis=1, keepdims=True)
    e = jnp.exp(s - m)
    return m, e, jnp.sum(e, axis=1, keepdims=True)


def _lane_tile_fold(x, op, init):
    for t in range(x.shape[1] // 128):
        init = op(init, x[:, 128 * t:128 * (t + 1)])
    return init


def _nsa_kernel(q_ref, gate_ref, kc_ref, vc_ref, ks_ref, vs_ref, kw_ref, vw_ref,
                tblc_ref, tbln_ref, tblw_ref, ovt_ref, o_ref, s_ref, *, ncmp):
    i = pl.program_id(2)
    rows = NSA_HG * QB
    qt = q_ref[...]
    qs = jnp.concatenate([qt[:, NSA_DH * h:NSA_DH * (h + 1)] for h in range(NSA_HG)], axis=0) * (NSA_DH ** -0.5)
    qb = qs.astype(BF16)

    shift = (4 * i - CMP_NEAR // 2 + ncmp) % ncmp
    lc = _dot_nt(qb, kc_ref[0, 0]) + pltpu.roll(tblc_ref[0], shift, 1)
    cidx = lax.broadcasted_iota(jnp.int32, (rows, ncmp), 1)
    _, e, den = _softmax_parts(jnp.where(cidx < (QB // CMP_STRIDE) * i + CMP_AHEAD, lc, NEG))
    tq = i * QB + lax.broadcasted_iota(jnp.int32, (rows, 1), 0) % QB
    pc = e * jnp.where(tq >= CMP_BLOCK - 1, 1.0 / den, 0.0)
    o_c = _dot(pc.astype(BF16), vc_ref[0, 0])
    pcs = pc[0:QB] + pc[QB:2 * QB] + pc[2 * QB:3 * QB] + pc[3 * QB:4 * QB]
    pcs_hi, pcs_lo = _split(pcs)
    ovt = ovt_ref[...]
    imp = _dot_nt(ovt, pcs_hi) + _dot_nt(ovt, pcs_lo)

    nslc = imp.shape[0]
    nidx = lax.broadcasted_iota(jnp.int32, (nslc, QB), 0)
    valid = nidx <= i
    forced = (nidx == 0) | (nidx == i) | (nidx == i - 1)
    work = jnp.where(valid, imp + jnp.where(forced, FORCE_BONUS, 0.0), -1.0)
    sel = jnp.zeros((nslc, QB), F32)
    for _ in range(N_SELECT):
        m = jnp.max(work, axis=0, keepdims=True)
        first = jnp.min(jnp.where(work == m, nidx, nslc), axis=0, keepdims=True)
        pick = nidx == first
        sel = jnp.where(pick & (m >= 0.0), 1.0, sel)
        work = jnp.where(pick, -2.0, work)

    ind_w = ks_ref.shape[3] - NSA_DH
    selq = sel.T
    if nslc < ind_w:
        selq = jnp.concatenate([selq, jnp.zeros((QB, ind_w - nslc), F32)], axis=1)
    bidx = lax.broadcasted_iota(jnp.int32, (QB, ind_w), 1)
    tile4 = lambda t: jnp.concatenate([t] * NSA_HG, axis=0)
    with_mask = lambda keep: jnp.concatenate([qs, tile4(jnp.where(keep, 0.0, NEG))], axis=1).astype(BF16)
    q_sel = with_mask(selq > 0.0)
    q_far = with_mask((selq > 0.0) & (bidx <= i - NEAR_BLOCKS))
    near_w = NEAR_BLOCKS * SLC_BLOCK
    kn = ks_ref[0, 0, pl.ds(pl.multiple_of(i * QB, QB), near_w), :]
    vn = vs_ref[0, 0, pl.ds(pl.multiple_of(i * QB, QB), near_w), :]
    s_near = _dot_nt(q_sel, kn) + tbln_ref[0]
    m_near = jnp.max(s_near, axis=1, keepdims=True)

    far_w = FAR_CHUNK_BLOCKS * SLC_BLOCK
    pad_s = (NEAR_BLOCKS - 1) * SLC_BLOCK
    n_far = jnp.maximum(i - (NEAR_BLOCKS - 1) + FAR_CHUNK_BLOCKS - 1, 0) // FAR_CHUNK_BLOCKS

    def far_logits(j, mvec):
        start = pl.multiple_of(j * far_w, far_w)
        kf = ks_ref[0, 0, pl.ds(pl.multiple_of(pad_s + j * far_w, SLC_BLOCK), far_w), :]
        s = _dot_nt(q_far, kf)
        s_ref[:, pl.ds(start, far_w)] = s
        return _lane_tile_fold(s, jnp.maximum, mvec)

    mvec = lax.fori_loop(0, n_far, far_logits, jnp.full((rows, 128), NEG, F32))
    m_s = jnp.maximum(m_near, jnp.max(mvec, axis=1, keepdims=True))

    e_near = jnp.exp(s_near - m_s)

    def far_values(j, carry):
        lvec, acc = carry
        start = pl.multiple_of(j * far_w, far_w)
        vf = vs_ref[0, 0, pl.ds(pl.multiple_of(pad_s + j * far_w, SLC_BLOCK), far_w), :]
        e = jnp.exp(s_ref[:, pl.ds(start, far_w)] - m_s)
        return _lane_tile_fold(e, jnp.add, lvec), acc + _dot(e.astype(BF16), vf)

    lvec, acc = lax.fori_loop(0, n_far, far_values,
                              (jnp.zeros((rows, 128), F32), _dot(e_near.astype(BF16), vn)))
    l_s = jnp.sum(e_near, axis=1, keepdims=True) + jnp.sum(lvec, axis=1, keepdims=True)
    o_s = acc / l_s

    win_w = WINDOW + QB
    kw = kw_ref[0, 0, pl.ds(pl.multiple_of(i * QB, QB), win_w), :]
    vw = vw_ref[0, 0, pl.ds(pl.multiple_of(i * QB, QB), win_w), :]
    flag = lax.broadcasted_iota(jnp.int32, (rows, kw_ref.shape[3] - NSA_DH), 1) == 0
    q_win = jnp.concatenate([qs, jnp.where(flag, NEG, 0.0)], axis=1).astype(BF16)
    s = _dot_nt(q_win, kw) + tblw_ref[0]
    _, e, den = _softmax_parts(s)
    o_w = _dot(e.astype(BF16), vw) / den

    gates = jax.nn.sigmoid(gate_ref[0, 0])
    outs = []
    for h in range(NSA_HG):
        r0 = slice(h * QB, (h + 1) * QB)
        outs.append(gates[:, 3 * h:3 * h + 1] * o_c[r0] + gates[:, 3 * h + 1:3 * h + 2] * o_s[r0]
                    + gates[:, 3 * h + 2:3 * h + 3] * o_w[r0])
    o_ref[...] = jnp.concatenate(outs, axis=1).astype(o_ref.dtype)


def _bias_tables(rel_bias, ncmp):
    def table(dist, keep, base):
        tbl = rel_bias[jnp.asarray(_t5_bucket_np(dist))] - base
        tbl = jnp.where(jnp.asarray(keep)[..., None], tbl, NEG)
        k = dist.shape[1]
        return tbl.transpose(2, 0, 1).reshape(NSA_G, NSA_HG * QB, k)

    far = rel_bias[N_BUCKETS - 1]
    qi = np.arange(QB)[:, None]
    dist_c = qi - CMP_STRIDE * (np.arange(CMP_NEAR)[None, :] - CMP_NEAR // 2) - (CMP_BLOCK - 1)
    tblc = table(dist_c, dist_c >= 0, far)
    tblc = tblc * jnp.asarray(np.arange(CMP_NEAR) < CMP_NEAR // 2 + CMP_AHEAD, F32)
    tblc = jnp.pad(tblc, ((0, 0), (0, 0), (0, ncmp - CMP_NEAR)))
    jn = np.arange(NEAR_BLOCKS * SLC_BLOCK)[None, :]
    dist_n = (NEAR_BLOCKS - 1) * SLC_BLOCK + qi - jn
    tbln = table(dist_n, dist_n >= 0, far)
    jw = np.arange(WINDOW + QB)[None, :]
    dist_w = WINDOW + qi - jw
    tblw = table(dist_w, (dist_w >= 0) & (dist_w < WINDOW), 0.0)
    return tblc, tbln, tblw


def _nsa(proj, gate_logits, kvc, ks, vs, kw, vw, tables, batch, seq):
    nq = seq // QB
    ncmp = seq // CMP_STRIDE
    nslc = seq // SLC_BLOCK
    tblc, tbln, tblw = tables
    cstart = np.arange(ncmp) * CMP_STRIDE
    sstart = np.arange(nslc) * SLC_BLOCK
    overlap_t = ((cstart[None, :] <= sstart[:, None] + SLC_BLOCK - 1)
                 & (cstart[None, :] + CMP_BLOCK - 1 >= sstart[:, None])
                 & (cstart[None, :] + CMP_BLOCK <= seq)).astype(np.float32)
    pad_s = ks.shape[2] - seq
    blk = (np.arange(seq)[:, None] // SLC_BLOCK == np.arange(NSA_DH)[None, :]).astype(np.float32)
    blk = np.concatenate([np.ones((pad_s, NSA_DH), np.float32), blk], axis=0)
    ks = jnp.concatenate([ks, jnp.broadcast_to(jnp.asarray(blk, BF16), ks.shape[:2] + blk.shape)], axis=-1)
    flag = np.zeros((kw.shape[2], NSA_DH), np.float32)
    flag[:kw.shape[2] - seq, 0] = 1.0
    kw = jnp.concatenate([kw, jnp.broadcast_to(jnp.asarray(flag, BF16), kw.shape[:2] + flag.shape)], axis=-1)
    rows = NSA_HG * QB
    kv_spec = lambda t: pl.BlockSpec((1, 1) + t.shape[2:], lambda b, g, i: (b, g, 0, 0))
    tbl_spec = lambda k: pl.BlockSpec((1, rows, k), lambda b, g, i: (g, 0, 0))
    qcol = C_Q // (NSA_HG * NSA_DH)
    return pl.pallas_call(
        functools.partial(_nsa_kernel, ncmp=ncmp),
        out_shape=jax.ShapeDtypeStruct((batch * seq, NSA_HEADS * NSA_DH), BF16),
        grid=(batch, NSA_G, nq),
        in_specs=[pl.BlockSpec((QB, NSA_HG * NSA_DH), lambda b, g, i: (b * nq + i, qcol + g)),
                  pl.BlockSpec((1, 1, QB, 3 * NSA_HG), lambda b, g, i: (b, g, i, 0)),
                  pl.BlockSpec((1, 1, ncmp, NSA_DH), lambda b, g, i: (b, g, 0, 0)),
                  pl.BlockSpec((1, 1, ncmp, NSA_DH), lambda b, g, i: (b, g, 0, 0)),
                  kv_spec(ks), kv_spec(vs), kv_spec(kw), kv_spec(vw),
                  tbl_spec(ncmp), tbl_spec(NEAR_BLOCKS * SLC_BLOCK), tbl_spec(WINDOW + QB),
                  pl.BlockSpec((nslc, ncmp), lambda b, g, i: (0, 0))],
        out_specs=pl.BlockSpec((QB, NSA_HG * NSA_DH), lambda b, g, i: (b * nq + i, g)),
        scratch_shapes=[pltpu.VMEM((rows, seq), F32)],
        compiler_params=_cp(("parallel", "parallel", "arbitrary"), VMEM_LIMIT),
        name="nsa_attention",
    )(proj, gate_logits, kvc[0], kvc[1], ks, vs, kw, vw, tblc, tbln, tblw, jnp.asarray(overlap_t, BF16))


def _merge_kernel(h_ref, yp_ref, yr_ref, yn_ref, x_ref, wbp_ref, wbr_ref, wbn_ref, wm_ref, bm_ref, wo_ref,
                  gn_ref, wr_ref, br_ref, xo_ref, h2_ref, route_ref):
    d = D_MODEL
    gl = jax.nn.sigmoid(_dot(h_ref[...], wm_ref[...]) + bm_ref[...])
    merged = (gl[:, :d] * _dot(yp_ref[...], wbp_ref[...]) + gl[:, d:2 * d] * _dot(yr_ref[...], wbr_ref[...])
              + gl[:, 2 * d:] * _dot(yn_ref[...], wbn_ref[...]))
    x = x_ref[...] + _dot(merged.astype(BF16), wo_ref[...])
    xo_ref[...] = x
    h2 = _rms(x, gn_ref[...])
    h2_ref[...] = h2.astype(h2_ref.dtype)

    logits = _dot(h2, wr_ref[...], HI) + br_ref[...]
    lane = lax.broadcasted_iota(jnp.int32, logits.shape, 1)
    big = logits.shape[1]
    lg = jnp.where(lane < N_GROUPS, logits, NEG)
    mg = jnp.max(lg, axis=1, keepdims=True)
    p_top = 1.0 / jnp.sum(jnp.exp(lg - mg), axis=1, keepdims=True)
    grp = jnp.min(jnp.where(lg == mg, lane, big), axis=1, keepdims=True)
    lo = N_GROUPS + EPG * grp
    le = jnp.where((lane >= lo) & (lane < lo + EPG), logits, NEG)
    e1 = jnp.max(le, axis=1, keepdims=True)
    i1 = jnp.min(jnp.where(le == e1, lane, big), axis=1, keepdims=True)
    le = jnp.where(lane == i1, NEG, le)
    e2 = jnp.max(le, axis=1, keepdims=True)
    i2 = jnp.min(jnp.where(le == e2, lane, big), axis=1, keepdims=True)
    t = jnp.exp(e2 - e1)
    w1 = p_top / (1.0 + t)
    w2 = p_top * t / (1.0 + t)
    route_ref[...] = jnp.where(lane == 0, (i1 - N_GROUPS).astype(F32),
                               jnp.where(lane == 1, (i2 - N_GROUPS).astype(F32),
                                         jnp.where(lane == 2, w1, jnp.where(lane == 3, w2, 0.0))))


def _merge(h, y_pool, y_rwkv, y_nsa, x, p, tm=256):
    n, d = x.shape
    row = lambda w: pl.BlockSpec((tm, w), lambda i: (i, 0))
    const = lambda a: pl.BlockSpec(a.shape, lambda i: (0, 0))
    ws = [p["wb_pool"], p["wb_rwkv"], p["wb_nsa"], p["w_merge"], p["b_merge"], p["w_out"],
          p["norm_ffn"], p["w_router"], p["b_router"]]
    return pl.pallas_call(
        _merge_kernel,
        out_shape=(jax.ShapeDtypeStruct((n, d), F32), jax.ShapeDtypeStruct((n, d), F32),
                   jax.ShapeDtypeStruct((n, 128), F32)),
        grid=(n // tm,),
        in_specs=[row(d), row(POOL_WIDTH), row(RW_WIDTH), row(d), row(d)] + [const(w) for w in ws],
        out_specs=(row(d), row(d), row(128)),
        compiler_params=_cp(("parallel",), VMEM_LIMIT),
        name="merge_router",
    )(h, y_pool, y_rwkv, y_nsa, x, *ws)


def _expert_kernel(te_ref, nt_ref, x_ref, w_ref, wg_ref, wu_ref, wd_ref, o_ref, wg_s, wu_s, wd_s):
    i = pl.program_id(0)

    @pl.when((i == 0) | (te_ref[i] != te_ref[jnp.maximum(i - 1, 0)]))
    def _():
        wg_s[...] = wg_ref[0].astype(BF16)
        wu_s[...] = wu_ref[0].astype(BF16)
        wd_s[...] = wd_ref[0].astype(BF16)

    @pl.when(i < nt_ref[0])
    def _():
        xb = x_ref[...].astype(BF16)
        gate = _dot(xb, wg_s[...])
        hid = gate * jax.nn.sigmoid(gate) * _dot(xb, wu_s[...])
        o_ref[...] = w_ref[...] * _dot(hid.astype(BF16), wd_s[...])

    @pl.when(i >= nt_ref[0])
    def _():
        o_ref[...] = jnp.zeros_like(o_ref)


def _experts(xs, roww, tile_expert, n_tiles, wg, wu, wd):
    r, d = xs.shape
    tm = MOE_TM
    return pl.pallas_call(
        _expert_kernel,
        out_shape=jax.ShapeDtypeStruct((r, d), F32),
        grid_spec=pltpu.PrefetchScalarGridSpec(
            num_scalar_prefetch=2,
            grid=(r // tm,),
            in_specs=[pl.BlockSpec((tm, d), lambda i, te, nt: (i, 0)),
                      pl.BlockSpec((tm, 1), lambda i, te, nt: (i, 0)),
                      pl.BlockSpec((1, d, D_EXPERT), lambda i, te, nt: (te[i], 0, 0)),
                      pl.BlockSpec((1, d, D_EXPERT), lambda i, te, nt: (te[i], 0, 0)),
                      pl.BlockSpec((1, D_EXPERT, d), lambda i, te, nt: (te[i], 0, 0))],
            out_specs=pl.BlockSpec((tm, d), lambda i, te, nt: (i, 0)),
            scratch_shapes=[pltpu.VMEM((d, D_EXPERT), BF16), pltpu.VMEM((d, D_EXPERT), BF16),
                            pltpu.VMEM((D_EXPERT, d), BF16)]),
        compiler_params=_cp(("arbitrary",), VMEM_LIMIT),
        name="moe_experts",
    )(tile_expert, n_tiles, xs, roww, wg, wu, wd)


def _moe(h2, route, wg, wu, wd):
    n = h2.shape[0]
    tm = MOE_TM
    r = 2 * n + N_EXPERTS * tm
    ids = route[:, 0:2].astype(jnp.int32).reshape(-1)
    wts = route[:, 2:4].reshape(-1)
    onehot = (ids[:, None] == jnp.arange(N_EXPERTS)[None, :]).astype(jnp.int32)
    rank = jnp.sum((jnp.cumsum(onehot, axis=0) - onehot) * onehot, axis=1)
    counts = jnp.sum(onehot, axis=0)
    tiles = (counts + tm - 1) // tm
    tile_end = jnp.cumsum(tiles)
    starts = (tile_end - tiles) * tm
    pos = starts[ids] + rank
    row_assign = jnp.full((r,), -1, jnp.int32).at[pos].set(jnp.arange(2 * n, dtype=jnp.int32))
    rowtok = jnp.maximum(row_assign, 0) // 2
    roww = jnp.where(row_assign >= 0, wts[jnp.maximum(row_assign, 0)], 0.0)
    n_tiles = tile_end[-1:].astype(jnp.int32)
    tile_expert = jnp.minimum(jnp.sum(tile_end[None, :] <= jnp.arange(r // tm)[:, None], axis=1),
                              N_EXPERTS - 1).astype(jnp.int32)
    ys = _experts(h2[rowtok], roww.reshape(r, 1), tile_expert, n_tiles, wg, wu, wd)
    return ys[pos[0::2]], ys[pos[1::2]]


def _layer_params(l, a):
    f = lambda t: t[l]
    row = lambda t: t[l].reshape(1, -1)
    w_in = a["w_in"][l]
    w_in_p = jnp.concatenate([w_in[:, :SRC_RW_END], jnp.zeros((D_MODEL, C_Q - SRC_RW_END), F32),
                              w_in[:, SRC_RW_END:],
                              jnp.zeros((D_MODEL, P_COLS - C_Q - (w_in.shape[1] - SRC_RW_END)), F32)], axis=1)
    mu = a["rw_mu"][l]
    wb = a["w_branch"][l].astype(BF16)
    w_router = jnp.zeros((D_MODEL, 128), F32)
    w_router = w_router.at[:, :N_GROUPS].set(a["w_router_grp"][l]).at[:, N_GROUPS:N_GROUPS + N_EXPERTS].set(
        a["w_router_exp"][l])
    b_router = jnp.zeros((1, 128), F32)
    b_router = b_router.at[0, :N_GROUPS].set(a["b_router_grp"][l]).at[0, N_GROUPS:N_GROUPS + N_EXPERTS].set(
        a["b_router_exp"][l])
    pos = jnp.stack([a["cmp_pos_k"][l].reshape(-1), a["cmp_pos_v"][l].reshape(-1)])
    return {
        "w_in": w_in_p.astype(BF16),
        "pool_w": f(a["pool_w"]), "pool_scale": f(a["pool_scale"]),
        "mu_r": mu[None, 0:RW_WIDTH], "mu_k": mu[None, RW_WIDTH:2 * RW_WIDTH],
        "mu_v": mu[None, 2 * RW_WIDTH:3 * RW_WIDTH],
        "mu_l": jnp.concatenate([mu[3 * RW_WIDTH:], jnp.zeros((LORA_PAD - RW_LORA,), F32)])[None],
        "w0": row(a["rw_w0"]), "w_up": f(a["rw_w_up"]), "a0": row(a["rw_a0"]), "a_up": f(a["rw_a_up"]),
        "g_up": f(a["rw_g_up"]), "k_k": row(a["rw_k_k"]), "k_a": row(a["rw_k_a"]), "r_k": row(a["rw_r_k"]),
        "gn_w": row(a["rw_gn_w"]), "gn_b": row(a["rw_gn_b"]),
        "cmp_pos": jnp.broadcast_to(pos[:, None, :], (2, 8, pos.shape[1])).astype(BF16),
        "cmp_w1": jnp.stack([a["cmp_w1_k"][l], a["cmp_w1_v"][l]]).astype(BF16),
        "cmp_w2": jnp.stack([a["cmp_w2_k"][l], a["cmp_w2_v"][l]]).astype(BF16),
        "wb_pool": wb[:POOL_WIDTH], "wb_rwkv": wb[POOL_WIDTH:POOL_WIDTH + RW_WIDTH],
        "wb_nsa": wb[POOL_WIDTH + RW_WIDTH:],
        "w_merge": a["w_merge"][l].astype(BF16), "b_merge": row(a["b_merge"]),
        "w_out": a["w_out"][l].astype(BF16), "norm_ffn": row(a["norm_ffn"]),
        "w_router": w_router, "b_router": b_router,
        "w_gate": a["w_exp_gate"][l], "w_up_e": a["w_exp_up"][l], "w_down": a["w_exp_down"][l],
    }


def _mixers(proj, p, tables, batch, seq):
    y_pool = _pool(proj, p["pool_w"], p["pool_scale"], batch, seq)
    y_rwkv = _rwkv(proj, p, batch, seq)
    kv = proj[:, C_KV:C_GATE].reshape(batch, seq, 6, NSA_G, NSA_DH).transpose(2, 0, 3, 1, 4).astype(BF16)
    xkv = kv[0:2].reshape(2, batch, NSA_G, seq // CMP_STRIDE, CMP_STRIDE * NSA_DH)
    kvc = _compress(xkv, p["cmp_pos"], p["cmp_w1"], p["cmp_w2"], batch, seq)
    pad = lambda t, n: jnp.pad(t, ((0, 0), (0, 0), (n, 0), (0, 0)))
    pad_s = (NEAR_BLOCKS - 1) * SLC_BLOCK
    gate_logits = proj[:, C_GATE:C_GATE + 3 * NSA_HEADS].reshape(batch, seq, NSA_G, 3 * NSA_HG).transpose(0, 2, 1, 3)
    y_nsa = _nsa(proj, gate_logits, kvc, pad(kv[2], pad_s), pad(kv[3], pad_s), pad(kv[4], WINDOW),
                 pad(kv[5], WINDOW), tables, batch, seq)
    return y_pool, y_rwkv, y_nsa


def kernel(x, rel_bias, norm_mix, w_in, pool_w, pool_scale, rw_mu, rw_w0, rw_w_up, rw_a0, rw_a_up, rw_g_up, rw_k_k, rw_k_a, rw_r_k, rw_gn_w, rw_gn_b, cmp_pos_k, cmp_w1_k, cmp_w2_k, cmp_pos_v, cmp_w1_v, cmp_w2_v, w_branch, w_merge, b_merge, w_out, norm_ffn, w_router_grp, b_router_grp, w_router_exp, b_router_exp, w_exp_gate, w_exp_up, w_exp_down, norm_final):
    a = dict(w_in=w_in, pool_w=pool_w, pool_scale=pool_scale, rw_mu=rw_mu, rw_w0=rw_w0, rw_w_up=rw_w_up,
             rw_a0=rw_a0, rw_a_up=rw_a_up, rw_g_up=rw_g_up, rw_k_k=rw_k_k, rw_k_a=rw_k_a, rw_r_k=rw_r_k,
             rw_gn_w=rw_gn_w, rw_gn_b=rw_gn_b, cmp_pos_k=cmp_pos_k, cmp_w1_k=cmp_w1_k, cmp_w2_k=cmp_w2_k,
             cmp_pos_v=cmp_pos_v, cmp_w1_v=cmp_w1_v, cmp_w2_v=cmp_w2_v, w_branch=w_branch, w_merge=w_merge,
             b_merge=b_merge, w_out=w_out, norm_ffn=norm_ffn, w_router_grp=w_router_grp,
             b_router_grp=b_router_grp, w_router_exp=w_router_exp, b_router_exp=b_router_exp,
             w_exp_gate=w_exp_gate, w_exp_up=w_exp_up, w_exp_down=w_exp_down)
    batch, seq, d = x.shape
    depth = norm_mix.shape[0]
    tables = _bias_tables(rel_bias, seq // CMP_STRIDE)
    xf = x.reshape(batch * seq, d)
    h = _norm(xf, norm_mix[0], BF16)
    for l in range(depth):
        p = _layer_params(l, a)
        proj = _matmul(h, p["w_in"])
        y_pool, y_rwkv, y_nsa = _mixers(proj, p, tables, batch, seq)
        xf, h2, route = _merge(h, y_pool, y_rwkv, y_nsa, xf, p)
        y1, y2 = _moe(h2, route, p["w_gate"], p["w_up_e"], p["w_down"])
        last = l == depth - 1
        g_next = norm_final if last else norm_mix[l + 1]
        xf, h = _add_norm(xf, y1, y2, g_next, F32 if last else BF16)
    return h.reshape(batch, seq, d)
```

```python
import functools
import math

import jax
import jax.numpy as jnp
import numpy as np
from jax import lax
from jax.experimental import pallas as pl
from jax.experimental.pallas import tpu as pltpu

F32 = jnp.float32
BF16 = jnp.bfloat16
HI = lax.Precision.HIGHEST

D_MODEL = 1024
RMS_EPS = 1e-6
NEG = -1e30

POOL_WINDOWS = (2, 4, 8, 16)
POOL_WIDTH = 512
POOL_GW = 128
POOL_HALO = 16

RW_HEADS = 8
RW_DH = 64
RW_WIDTH = 512
DECAY_LORA, AAA_LORA, GATE_LORA = 32, 32, 96
RW_LORA = DECAY_LORA + AAA_LORA + GATE_LORA
RW_COLS = 3 * RW_WIDTH + RW_LORA
RW_GN_EPS = 64e-5
RW_CHUNK = 64

NSA_DH = 64
NSA_HEADS = 16
NSA_G = 4
NSA_HG = 4
NSA_KVW = NSA_G * NSA_DH
CMP_BLOCK, CMP_STRIDE, CMP_HIDDEN = 32, 16, 256
SLC_BLOCK = 64
N_SELECT = 8
WINDOW = 512
QB = 64
NSA_NB = 4
FORCE_BONUS = 1e3
N_BUCKETS, MAX_EXACT, MAX_DISTANCE = 32, 16, 128
NEAR_BLOCKS = 3
FAR_CHUNK_BLOCKS = 8
CMP_NEAR = 32
CMP_AHEAD = (QB - CMP_BLOCK) // CMP_STRIDE + 1

N_GROUPS, EPG, N_EXPERTS, D_EXPERT = 4, 8, 32, 256
MOE_TM = 256

C_POOL, C_R, C_K, C_V, C_LORA, C_Q, C_KV, C_GATE, P_COLS = 0, 512, 1024, 1536, 2048, 2304, 3328, 4864, 5120
LORA_PAD = 256
SRC_RW_END = POOL_WIDTH + RW_COLS

VMEM_LIMIT = 56 * 1024 * 1024


def _t5_bucket_np(dist):
    n = np.maximum(dist, 0)
    nf = np.maximum(n, 1).astype(np.float32)
    large = MAX_EXACT + (np.log(nf / MAX_EXACT) / math.log(MAX_DISTANCE / MAX_EXACT)
                         * (N_BUCKETS - MAX_EXACT)).astype(np.int32)
    large = np.minimum(large, N_BUCKETS - 1)
    return np.where(n < MAX_EXACT, n, large)


def _cp(sem, vmem=None):
    return pltpu.CompilerParams(dimension_semantics=sem, vmem_limit_bytes=vmem)


def _dot(a, b, precision=None):
    return jnp.dot(a, b, preferred_element_type=F32, precision=precision)


def _dot_nt(a, b, precision=None):
    return lax.dot_general(a, b, (((1,), (1,)), ((), ())), preferred_element_type=F32, precision=precision)


def _dot_tn(a, b, precision=None):
    return lax.dot_general(a, b, (((0,), (0,)), ((), ())), preferred_element_type=F32, precision=precision)


def _bdot(a, b):
    return _dot(a.astype(BF16), b.astype(BF16))


def _bdot_nt(a, b):
    return _dot_nt(a.astype(BF16), b.astype(BF16))


def _bdot_tn(a, b):
    return _dot_tn(a.astype(BF16), b.astype(BF16))


def _split(a):
    hi = a.astype(BF16)
    return hi, (a - hi.astype(F32)).astype(BF16)


def _dot3(a, b):
    ah, al = _split(a)
    bh, bl = _split(b)
    return _dot(ah, bh) + (_dot(ah, bl) + _dot(al, bh))


def _rms(x, g):
    return x * lax.rsqrt(jnp.mean(x * x, axis=-1, keepdims=True) + RMS_EPS) * g


def _norm_kernel(x_ref, g_ref, h_ref):
    h_ref[...] = _rms(x_ref[...], g_ref[...]).astype(h_ref.dtype)


def _norm(x, g, out_dtype, tm=512):
    n, d = x.shape
    return pl.pallas_call(
        _norm_kernel,
        out_shape=jax.ShapeDtypeStruct((n, d), out_dtype),
        grid=(n // tm,),
        in_specs=[pl.BlockSpec((tm, d), lambda i: (i, 0)), pl.BlockSpec((1, d), lambda i: (0, 0))],
        out_specs=pl.BlockSpec((tm, d), lambda i: (i, 0)),
        compiler_params=_cp(("parallel",)),
        name="rms_norm",
    )(x, g.reshape(1, d))


def _add_norm_kernel(x_ref, y1_ref, y2_ref, g_ref, xo_ref, h_ref):
    x = x_ref[...] + (y1_ref[...] + y2_ref[...])
    xo_ref[...] = x
    h_ref[...] = _rms(x, g_ref[...]).astype(h_ref.dtype)


def _add_norm(x, y1, y2, g, out_dtype, tm=512):
    n, d = x.shape
    row = pl.BlockSpec((tm, d), lambda i: (i, 0))
    return pl.pallas_call(
        _add_norm_kernel,
        out_shape=(jax.ShapeDtypeStruct((n, d), F32), jax.ShapeDtypeStruct((n, d), out_dtype)),
        grid=(n // tm,),
        in_specs=[row, row, row, pl.BlockSpec((1, d), lambda i: (0, 0))],
        out_specs=(row, row),
        compiler_params=_cp(("parallel",)),
        name="moe_combine_norm",
    )(x, y1, y2, g.reshape(1, d))


def _matmul_kernel(x_ref, w_ref, o_ref):
    o_ref[...] = _dot(x_ref[...], w_ref[...]).astype(o_ref.dtype)


def _matmul(x, w, tm=512, tn=1024):
    m, k = x.shape
    n = w.shape[1]
    return pl.pallas_call(
        _matmul_kernel,
        out_shape=jax.ShapeDtypeStruct((m, n), F32),
        grid=(n // tn, m // tm),
        in_specs=[pl.BlockSpec((tm, k), lambda j, i: (i, 0)), pl.BlockSpec((k, tn), lambda j, i: (0, j))],
        out_specs=pl.BlockSpec((tm, tn), lambda j, i: (i, j)),
        compiler_params=_cp(("parallel", "parallel"), VMEM_LIMIT),
        name="in_proj",
    )(x, w)


def _pool_kernel(u_ref, halo_ref, w_ref, scale_ref, o_ref, buf_ref, *, tile):
    i = pl.program_id(1)
    u = u_ref[...]
    buf_ref[POOL_HALO:, :] = u
    buf_ref[:POOL_HALO, :] = jnp.where(i > 0, halo_ref[...], 0.0)
    t = i * tile + lax.broadcasted_iota(jnp.int32, (tile, 1), 0)
    outs = []
    for gi, win in enumerate(POOL_WINDOWS):
        cols = slice(gi * POOL_GW, (gi + 1) * POOL_GW)
        s = u[:, cols]
        for j in range(1, win):
            s = s + buf_ref[POOL_HALO - j:POOL_HALO - j + tile, cols]
        cnt = jnp.minimum(t + 1, win).astype(F32)
        pooled = s / cnt - u[:, cols]
        outs.append(_dot(pooled.astype(BF16), w_ref[gi]))
    o_ref[...] = (jnp.concatenate(outs, axis=1) * scale_ref[...]).astype(o_ref.dtype)


def _pool(proj, w_grp, scale, batch, seq, tile=512):
    nt = seq // tile
    hb = tile // POOL_HALO
    return pl.pallas_call(
        functools.partial(_pool_kernel, tile=tile),
        out_shape=jax.ShapeDtypeStruct((batch * seq, POOL_WIDTH), BF16),
        grid=(batch, nt),
        in_specs=[
            pl.BlockSpec((tile, POOL_WIDTH), lambda b, i: (b * nt + i, 0)),
            pl.BlockSpec((POOL_HALO, POOL_WIDTH), lambda b, i: (jnp.maximum((b * nt + i) * hb - 1, 0), 0)),
            pl.BlockSpec((len(POOL_WINDOWS), POOL_GW, POOL_GW), lambda b, i: (0, 0, 0)),
            pl.BlockSpec((1, POOL_WIDTH), lambda b, i: (0, 0)),
        ],
        out_specs=pl.BlockSpec((tile, POOL_WIDTH), lambda b, i: (b * nt + i, 0)),
        scratch_shapes=[pltpu.VMEM((tile + POOL_HALO, POOL_WIDTH), F32)],
        compiler_params=_cp(("parallel", "parallel")),
        name="pool_mixer",
    )(proj, proj, w_grp.astype(BF16), scale.reshape(1, POOL_WIDTH))


def _token_shift(u, halo, mu, first):
    prev_row = jnp.where(first, 0.0, halo[7:8, :])
    rolled = pltpu.roll(u, 1, 0)
    row = lax.broadcasted_iota(jnp.int32, u.shape, 0)
    prev = jnp.where(row == 0, prev_row, rolled)
    return u + (prev - u) * mu


def _rwkv_chunk_kernel(r_ref, k_ref, v_ref, l_ref, rh_ref, kh_ref, vh_ref, lh_ref,
                       mur_ref, muk_ref, muv_ref, mul_ref, w0_ref, wup_ref, a0_ref, aup_ref, gup_ref,
                       kk_ref, ka_ref, rk_ref, bd_ref, qy_ref, mn_ref, g_ref, bonus_ref):
    first = pl.program_id(1) == 0
    c = RW_CHUNK
    r = _token_shift(r_ref[...], rh_ref[...], mur_ref[...], first)
    k = _token_shift(k_ref[...], kh_ref[...], muk_ref[...], first)
    v = _token_shift(v_ref[...], vh_ref[...], muv_ref[...], first)
    lo = _token_shift(l_ref[...], lh_ref[...], mul_ref[...], first)
    wd = lo[:, :DECAY_LORA]
    ad = lo[:, DECAY_LORA:DECAY_LORA + AAA_LORA]
    gd = lo[:, DECAY_LORA + AAA_LORA:RW_LORA]
    z = -(w0_ref[...] + _dot(jnp.tanh(wd), wup_ref[...], HI))
    w_log = -(jnp.maximum(z, 0.0) + jnp.log(1.0 + jnp.exp(-jnp.abs(z)))) - 0.5
    logw = -jnp.exp(w_log)
    a = jax.nn.sigmoid(a0_ref[...] + _dot(ad, aup_ref[...], HI))
    g_ref[0] = _dot(jax.nn.sigmoid(gd), gup_ref[...], HI)
    kkraw = k * kk_ref[...]
    k2 = k * (1.0 + (a - 1.0) * ka_ref[...])
    rkr = r * k2 * rk_ref[...]

    cum_all = logw
    trow = lax.broadcasted_iota(jnp.int32, logw.shape, 0)
    step = 1
    while step < c:
        cum_all = cum_all + jnp.where(trow >= step, pltpu.roll(cum_all, step, 0), 0.0)
        step *= 2

    ti = lax.broadcasted_iota(jnp.int32, (c, c), 0)
    si = lax.broadcasted_iota(jnp.int32, (c, c), 1)
    incl = ti >= si
    strict = ti > si
    eye = ti == si
    zeros = jnp.zeros((c, c), F32)
    bd = bd_ref[...]

    def head_sum(t):
        hi, lo = _split(t)
        return _dot(hi, bd) + _dot(lo, bd)

    kk = kkraw / jnp.maximum(jnp.sqrt(head_sum(kkraw * kkraw)), 1e-12)
    bonus_ref[0] = head_sum(rkr) * v
    cum_end = cum_all[c - 1:c, :]
    ginv = jnp.exp(-cum_all)
    gtail = jnp.exp(cum_end - cum_all)
    gend = jnp.exp(cum_end)
    kka = kk * a
    at = -kk * jnp.exp(cum_all - logw)
    bt = kka * ginv
    kt = k2 * ginv
    rt = r * jnp.exp(cum_all)
    bhat = kka * gtail
    khat = k2 * gtail

    heads = [slice(h * RW_DH, (h + 1) * RW_DH) for h in range(RW_HEADS)]
    stack = lambda x, y, s: jnp.concatenate([x[:, s], y[:, s]], axis=0).astype(BF16)
    gram = [_dot_nt(stack(at, rt, s), stack(bt, kt, s)) for s in heads]
    a_ab = [jnp.where(strict, g[:c, :c], 0.0) for g in gram]
    a_ak = [jnp.where(strict, g[:c, c:], 0.0) for g in gram]
    incl2 = (lax.broadcasted_iota(jnp.int32, (c, 2 * c), 0)
             >= lax.broadcasted_iota(jnp.int32, (c, 2 * c), 1) % c)
    a_r = [jnp.where(incl2, g[c:, :], 0.0) for g in gram]
    p = a_ab
    tinv = [eye.astype(F32) + x for x in p]
    for _ in range(int(math.log2(c)) - 1):
        p = [_bdot(x, x) for x in p]
        tinv = [t + _bdot(t, x) for t, x in zip(tinv, p)]
    av = [_bdot(x, v[:, s]) for x, s in zip(a_ak, heads)]
    w12 = [_bdot(t, jnp.concatenate([at[:, s], x], axis=1)) for t, x, s in zip(tinv, av, heads)]
    zmat = [jnp.concatenate([w, jnp.concatenate([zeros, v[:, s]], axis=1)], axis=0).astype(BF16)
            for w, s in zip(w12, heads)]
    out1 = [_dot(x.astype(BF16), z) for x, z in zip(a_r, zmat)]
    out2 = [_dot_tn(stack(bhat, khat, s), z) for s, z in zip(heads, zmat)]
    qy_ref[0] = jnp.concatenate([o + jnp.concatenate([rt[:, s], zeros], axis=1) for o, s in zip(out1, heads)],
                                axis=1)
    for h, s in enumerate(heads):
        diag = jnp.where(eye, gend[:, s], 0.0)
        mn_ref[0, 0, h] = out2[h] + jnp.concatenate([diag, zeros], axis=1)


def _rwkv_scan_kernel(qy_ref, mn_ref, g_ref, bonus_ref, gnw_ref, gnb_ref, bd_ref, y_ref, st_ref, *, batch):
    @pl.when(pl.program_id(0) == 0)
    def _():
        st_ref[...] = jnp.zeros_like(st_ref)

    bd = bd_ref[...]

    def head_mean(t):
        hi, lo = _split(t)
        return (_dot(hi, bd) + _dot(lo, bd)) * (1.0 / RW_DH)

    pairs = [(b, h) for b in range(batch) for h in range(RW_HEADS)]
    sts = [st_ref[b * RW_HEADS + h] for b, h in pairs]
    ys = [_bdot(qy_ref[b, :, 2 * RW_DH * h:2 * RW_DH * h + RW_DH], st)
          + qy_ref[b, :, 2 * RW_DH * h + RW_DH:2 * RW_DH * (h + 1)] for (b, h), st in zip(pairs, sts)]
    for (b, h), st in zip(pairs, sts):
        mn = mn_ref[b, 0, h]
        st_ref[b * RW_HEADS + h] = _dot3(mn[:, :RW_DH], st) + mn[:, RW_DH:]
    for b in range(batch):
        y = jnp.concatenate(ys[b * RW_HEADS:(b + 1) * RW_HEADS], axis=1)
        dev = y - head_mean(y)
        yn = dev * lax.rsqrt(head_mean(dev * dev) + RW_GN_EPS) * gnw_ref[...] + gnb_ref[...]
        y_ref[b] = ((yn + bonus_ref[b]) * g_ref[b]).astype(y_ref.dtype)


def _rwkv(proj, p, batch, seq):
    head_ones = jnp.asarray(np.kron(np.eye(RW_HEADS), np.ones((RW_DH, RW_DH))), BF16)
    c = RW_CHUNK
    nc = seq // c
    hb = c // 8
    row512 = lambda col: pl.BlockSpec((c, RW_WIDTH), lambda b, i: (b * nc + i, col))
    halo512 = lambda col: pl.BlockSpec((8, RW_WIDTH), lambda b, i: (jnp.maximum((b * nc + i) * hb - 1, 0), col))
    const = lambda shape: pl.BlockSpec(shape, lambda b, i: (0,) * len(shape))
    vec = const((1, RW_WIDTH))
    out_row = lambda w: pl.BlockSpec((1, c, w), lambda b, i: (b, i, 0))
    qy, mn, g, bonus = pl.pallas_call(
        _rwkv_chunk_kernel,
        out_shape=(jax.ShapeDtypeStruct((batch, seq, 2 * RW_WIDTH), F32),
                   jax.ShapeDtypeStruct((batch, nc, RW_HEADS, RW_DH, 2 * RW_DH), F32),
                   jax.ShapeDtypeStruct((batch, seq, RW_WIDTH), F32),
                   jax.ShapeDtypeStruct((batch, seq, RW_WIDTH), F32)),
        grid=(batch, nc),
        in_specs=[row512(C_R // RW_WIDTH), row512(C_K // RW_WIDTH), row512(C_V // RW_WIDTH),
                  pl.BlockSpec((c, LORA_PAD), lambda b, i: (b * nc + i, C_LORA // LORA_PAD)),
                  halo512(C_R // RW_WIDTH), halo512(C_K // RW_WIDTH), halo512(C_V // RW_WIDTH),
                  pl.BlockSpec((8, LORA_PAD), lambda b, i: (jnp.maximum((b * nc + i) * hb - 1, 0), C_LORA // LORA_PAD)),
                  vec, vec, vec, const((1, LORA_PAD)),
                  vec, const((DECAY_LORA, RW_WIDTH)), vec, const((AAA_LORA, RW_WIDTH)), const((GATE_LORA, RW_WIDTH)),
                  vec, vec, vec, const((RW_WIDTH, RW_WIDTH))],
        out_specs=(out_row(2 * RW_WIDTH),
                   pl.BlockSpec((1, 1, RW_HEADS, RW_DH, 2 * RW_DH), lambda b, i: (b, i, 0, 0, 0)),
                   out_row(RW_WIDTH), out_row(RW_WIDTH)),
        compiler_params=_cp(("parallel", "parallel"), VMEM_LIMIT),
        name="rwkv_chunk",
    )(proj, proj, proj, proj, proj, proj, proj, proj,
      p["mu_r"], p["mu_k"], p["mu_v"], p["mu_l"], p["w0"], p["w_up"], p["a0"], p["a_up"], p["g_up"],
      p["k_k"], p["k_a"], p["r_k"], head_ones)

    full = lambda w: pl.BlockSpec((batch, c, w), lambda i: (0, i, 0))
    return pl.pallas_call(
        functools.partial(_rwkv_scan_kernel, batch=batch),
        out_shape=jax.ShapeDtypeStruct((batch, seq, RW_WIDTH), BF16),
        grid=(nc,),
        in_specs=[full(2 * RW_WIDTH),
                  pl.BlockSpec((batch, 1, RW_HEADS, RW_DH, 2 * RW_DH), lambda i: (0, i, 0, 0, 0)),
                  full(RW_WIDTH), full(RW_WIDTH),
                  pl.BlockSpec((1, RW_WIDTH), lambda i: (0, 0)), pl.BlockSpec((1, RW_WIDTH), lambda i: (0, 0)),
                  pl.BlockSpec((RW_WIDTH, RW_WIDTH), lambda i: (0, 0))],
        out_specs=full(RW_WIDTH),
        scratch_shapes=[pltpu.VMEM((batch * RW_HEADS, RW_DH, RW_DH), F32)],
        compiler_params=_cp(("arbitrary",), VMEM_LIMIT),
        name="rwkv_scan",
    )(qy, mn, g, bonus, p["gn_w"], p["gn_b"], head_ones).reshape(batch * seq, RW_WIDTH)


def _gelu_tanh(x):
    return 0.5 * x * (1.0 + jnp.tanh(math.sqrt(2.0 / math.pi) * (x + 0.044715 * (x * x * x))))


def _compress_kernel(x_ref, pos_ref, w1_ref, w2_ref, o_ref):
    half = CMP_STRIDE * NSA_DH
    x = x_ref[0, 0, 0]
    w1 = w1_ref[0]
    posb = _dot(pos_ref[0], w1)[0:1, :]
    h1 = _dot(x, w1[:half])
    h2 = _dot(x, w1[half:])
    n = h2.shape[0]
    row = lax.broadcasted_iota(jnp.int32, h2.shape, 0)
    h2s = jnp.where(row < n - 1, pltpu.roll(h2, n - 1, 0), 0.0)
    hid = _gelu_tanh(h1 + h2s + posb)
    o_ref[0, 0, 0] = _dot(hid.astype(BF16), w2_ref[0]).astype(o_ref.dtype)


def _compress(xkv, pos, w1, w2, batch, seq):
    nr = seq // CMP_STRIDE
    wide = CMP_STRIDE * NSA_DH
    return pl.pallas_call(
        _compress_kernel,
        out_shape=jax.ShapeDtypeStruct((2, batch, NSA_G, nr, NSA_DH), BF16),
        grid=(2, batch, NSA_G),
        in_specs=[pl.BlockSpec((1, 1, 1, nr, wide), lambda t, b, g: (t, b, g, 0, 0)),
                  pl.BlockSpec((1, 8, 2 * wide), lambda t, b, g: (t, 0, 0)),
                  pl.BlockSpec((1, 2 * wide, CMP_HIDDEN), lambda t, b, g: (t, 0, 0)),
                  pl.BlockSpec((1, CMP_HIDDEN, NSA_DH), lambda t, b, g: (t, 0, 0))],
        out_specs=pl.BlockSpec((1, 1, 1, nr, NSA_DH), lambda t, b, g: (t, b, g, 0, 0)),
        compiler_params=_cp(("parallel", "parallel", "parallel")),
        name="nsa_compress",
    )(xkv, pos, w1, w2)


def _softmax_parts(s):
    m = jnp.max(s, axis=1, keepdims=True)
    e = jnp.exp(s - m)
    return m, e, jnp.sum(e, axis=1, keepdims=True)


def _lane_tile_fold(x, op, init):
    for t in range(x.shape[1] // 128):
        init = op(init, x[:, 128 * t:128 * (t + 1)])
    return init


def _nsa_kernel(q_ref, gate_ref, kc_ref, vc_ref, ks_ref, vs_ref, kw_ref, vw_ref,
                tblc_ref, tbln_ref, tblw_ref, ovt_ref, o_ref, s_ref, *, ncmp):
    nb = NSA_NB
    blocks = [pl.program_id(2) * nb + u for u in range(nb)]
    rows = NSA_HG * QB
    each = lambda f, *ls: [f(*xs) for xs in zip(*ls)]
    qt = q_ref[...]
    qs = [jnp.concatenate([qt[u * QB:(u + 1) * QB, NSA_DH * h:NSA_DH * (h + 1)] for h in range(NSA_HG)], axis=0)
          * (NSA_DH ** -0.5) for u in range(nb)]
    qb = each(lambda x: x.astype(BF16), qs)
    rmax = lambda x: jnp.max(x, axis=1, keepdims=True)
    rsum = lambda x: jnp.sum(x, axis=1, keepdims=True)

    win_w = WINDOW + QB
    kw = [kw_ref[0, 0, pl.ds(pl.multiple_of(i * QB, QB), win_w), :] for i in blocks]
    vw = [vw_ref[0, 0, pl.ds(pl.multiple_of(i * QB, QB), win_w), :] for i in blocks]
    flag = lax.broadcasted_iota(jnp.int32, (rows, kw_ref.shape[3] - NSA_DH), 1) == 0
    q_win = each(lambda x: jnp.concatenate([x, jnp.where(flag, NEG, 0.0)], axis=1).astype(BF16), qs)
    tblw = tblw_ref[0]
    s_w = each(lambda x, k: _dot_nt(x, k) + tblw, q_win, kw)
    m_w = each(rmax, s_w)
    e_w = each(lambda s, m: jnp.exp(s - m), s_w, m_w)
    den_w = each(rsum, e_w)
    o_w = each(lambda e, v, d: _dot(e.astype(BF16), v) / d, e_w, vw, den_w)

    kc = kc_ref[0, 0]
    vc = vc_ref[0, 0]
    tblc = tblc_ref[0]
    cidx = lax.broadcasted_iota(jnp.int32, (rows, ncmp), 1)
    qrow = lax.broadcasted_iota(jnp.int32, (rows, 1), 0) % QB
    lc = [jnp.where(cidx < (QB // CMP_STRIDE) * i + CMP_AHEAD,
                    _dot_nt(x, kc) + pltpu.roll(tblc, (4 * i - CMP_NEAR // 2 + ncmp) % ncmp, 1), NEG)
          for x, i in zip(qb, blocks)]
    m_c = each(rmax, lc)
    e_c = each(lambda s, m: jnp.exp(s - m), lc, m_c)
    den_c = each(rsum, e_c)
    pc = [e * jnp.where(i * QB + qrow >= CMP_BLOCK - 1, 1.0 / d, 0.0) for e, d, i in zip(e_c, den_c, blocks)]
    o_c = each(lambda x: _dot(x.astype(BF16), vc), pc)
    pcs = each(lambda x: _split(x[0:QB] + x[QB:2 * QB] + x[2 * QB:3 * QB] + x[3 * QB:4 * QB]), pc)
    ovt = ovt_ref[...]
    imp = each(lambda x: _dot_nt(ovt, x[0]) + _dot_nt(ovt, x[1]), pcs)

    nslc = ovt.shape[0]
    nidx = lax.broadcasted_iota(jnp.int32, (nslc, QB), 0)
    work = [jnp.where(nidx <= i, x + jnp.where((nidx == 0) | (nidx == i) | (nidx == i - 1), FORCE_BONUS, 0.0), -1.0)
            for x, i in zip(imp, blocks)]
    sel = [jnp.zeros((nslc, QB), F32)] * nb
    for _ in range(N_SELECT):
        m = each(lambda w: jnp.max(w, axis=0, keepdims=True), work)
        first = each(lambda w, mm: jnp.min(jnp.where(w == mm, nidx, nslc), axis=0, keepdims=True), work, m)
        sel = each(lambda s, f, mm: jnp.where((nidx == f) & (mm >= 0.0), 1.0, s), sel, first, m)
        work = each(lambda w, f: jnp.where(nidx == f, -2.0, w), work, first)

    ind_w = ks_ref.shape[3] - NSA_DH
    selq = each(lambda x: x.T, sel)
    if nslc < ind_w:
        selq = each(lambda x: jnp.concatenate([x, jnp.zeros((QB, ind_w - nslc), F32)], axis=1), selq)
    bidx = lax.broadcasted_iota(jnp.int32, (QB, ind_w), 1)
    tile4 = lambda t: jnp.concatenate([t] * NSA_HG, axis=0)
    with_mask = lambda x, keep: jnp.concatenate([x, tile4(jnp.where(keep, 0.0, NEG))], axis=1).astype(BF16)
    q_sel = each(lambda x, s: with_mask(x, s > 0.0), qs, selq)
    q_far = jnp.concatenate([with_mask(x, (s > 0.0) & (bidx <= i - NEAR_BLOCKS))
                             for x, s, i in zip(qs, selq, blocks)], axis=0)
    near_w = NEAR_BLOCKS * SLC_BLOCK
    kn = [ks_ref[0, 0, pl.ds(pl.multiple_of(i * QB, QB), near_w), :] for i in blocks]
    vn = [vs_ref[0, 0, pl.ds(pl.multiple_of(i * QB, QB), near_w), :] for i in blocks]
    tbln = tbln_ref[0]
    s_near = each(lambda x, k: _dot_nt(x, k) + tbln, q_sel, kn)
    m_near = jnp.concatenate(each(rmax, s_near), axis=0)

    far_w = FAR_CHUNK_BLOCKS * SLC_BLOCK
    pad_s = (NEAR_BLOCKS - 1) * SLC_BLOCK
    n_far = jnp.maximum(blocks[-1] - (NEAR_BLOCKS - 1) + FAR_CHUNK_BLOCKS - 1, 0) // FAR_CHUNK_BLOCKS

    def far_logits(j, mvec):
        start = pl.multiple_of(j * far_w, far_w)
        kf = ks_ref[0, 0, pl.ds(pl.multiple_of(pad_s + j * far_w, SLC_BLOCK), far_w), :]
        s = _dot_nt(q_far, kf)
        s_ref[:, pl.ds(start, far_w)] = s
        return _lane_tile_fold(s, jnp.maximum, mvec)

    mvec = lax.fori_loop(0, n_far, far_logits, jnp.full((nb * rows, 128), NEG, F32))
    m_s = jnp.maximum(m_near, rmax(mvec))

    e_near = [jnp.exp(s - m_s[u * rows:(u + 1) * rows]) for u, s in enumerate(s_near)]
    acc0 = jnp.concatenate(each(lambda e, v: _dot(e.astype(BF16), v), e_near, vn), axis=0)

    def far_values(j, carry):
        lvec, acc = carry
        start = pl.multiple_of(j * far_w, far_w)
        vf = vs_ref[0, 0, pl.ds(pl.multiple_of(pad_s + j * far_w, SLC_BLOCK), far_w), :]
        e = jnp.exp(s_ref[:, pl.ds(start, far_w)] - m_s)
        return _lane_tile_fold(e, jnp.add, lvec), acc + _dot(e.astype(BF16), vf)

    lvec, acc = lax.fori_loop(0, n_far, far_values, (jnp.zeros((nb * rows, 128), F32), acc0))
    o_s = acc / (jnp.concatenate(each(rsum, e_near), axis=0) + rsum(lvec))

    for u in range(nb):
        gates = jax.nn.sigmoid(gate_ref[0, 0, u * QB:(u + 1) * QB, :])
        outs = []
        for h in range(NSA_HG):
            r0 = slice(h * QB, (h + 1) * QB)
            outs.append(gates[:, 3 * h:3 * h + 1] * o_c[u][r0]
                        + gates[:, 3 * h + 1:3 * h + 2] * o_s[u * rows + h * QB:u * rows + (h + 1) * QB]
                        + gates[:, 3 * h + 2:3 * h + 3] * o_w[u][r0])
        o_ref[u * QB:(u + 1) * QB, :] = jnp.concatenate(outs, axis=1).astype(o_ref.dtype)


def _bias_tables(rel_bias, ncmp):
    def table(dist, keep, base):
        tbl = rel_bias[jnp.asarray(_t5_bucket_np(dist))] - base
        tbl = jnp.where(jnp.asarray(keep)[..., None], tbl, NEG)
        k = dist.shape[1]
        return tbl.transpose(2, 0, 1).reshape(NSA_G, NSA_HG * QB, k)

    far = rel_bias[N_BUCKETS - 1]
    qi = np.arange(QB)[:, None]
    dist_c = qi - CMP_STRIDE * (np.arange(CMP_NEAR)[None, :] - CMP_NEAR // 2) - (CMP_BLOCK - 1)
    tblc = table(dist_c, dist_c >= 0, far)
    tblc = tblc * jnp.asarray(np.arange(CMP_NEAR) < CMP_NEAR // 2 + CMP_AHEAD, F32)
    tblc = jnp.pad(tblc, ((0, 0), (0, 0), (0, ncmp - CMP_NEAR)))
    jn = np.arange(NEAR_BLOCKS * SLC_BLOCK)[None, :]
    dist_n = (NEAR_BLOCKS - 1) * SLC_BLOCK + qi - jn
    tbln = table(dist_n, dist_n >= 0, far)
    jw = np.arange(WINDOW + QB)[None, :]
    dist_w = WINDOW + qi - jw
    tblw = table(dist_w, (dist_w >= 0) & (dist_w < WINDOW), 0.0)
    return tblc, tbln, tblw


def _nsa(proj, gate_logits, kvc, ks, vs, kw, vw, tables, batch, seq):
    nq = seq // QB
    ncmp = seq // CMP_STRIDE
    nslc = seq // SLC_BLOCK
    tblc, tbln, tblw = tables
    cstart = np.arange(ncmp) * CMP_STRIDE
    sstart = np.arange(nslc) * SLC_BLOCK
    overlap_t = ((cstart[None, :] <= sstart[:, None] + SLC_BLOCK - 1)
                 & (cstart[None, :] + CMP_BLOCK - 1 >= sstart[:, None])
                 & (cstart[None, :] + CMP_BLOCK <= seq)).astype(np.float32)
    pad_s = ks.shape[2] - seq
    blk = (np.arange(seq)[:, None] // SLC_BLOCK == np.arange(NSA_DH)[None, :]).astype(np.float32)
    blk = np.concatenate([np.ones((pad_s, NSA_DH), np.float32), blk], axis=0)
    ks = jnp.concatenate([ks, jnp.broadcast_to(jnp.asarray(blk, BF16), ks.shape[:2] + blk.shape)], axis=-1)
    flag = np.zeros((kw.shape[2], NSA_DH), np.float32)
    flag[:kw.shape[2] - seq, 0] = 1.0
    kw = jnp.concatenate([kw, jnp.broadcast_to(jnp.asarray(flag, BF16), kw.shape[:2] + flag.shape)], axis=-1)
    rows = NSA_HG * QB
    kv_spec = lambda t: pl.BlockSpec((1, 1) + t.shape[2:], lambda b, g, i: (b, g, 0, 0))
    tbl_spec = lambda k: pl.BlockSpec((1, rows, k), lambda b, g, i: (g, 0, 0))
    qcol = C_Q // (NSA_HG * NSA_DH)
    return pl.pallas_call(
        functools.partial(_nsa_kernel, ncmp=ncmp),
        out_shape=jax.ShapeDtypeStruct((batch * seq, NSA_HEADS * NSA_DH), BF16),
        grid=(batch, NSA_G, nq // NSA_NB),
        in_specs=[pl.BlockSpec((NSA_NB * QB, NSA_HG * NSA_DH), lambda b, g, i: (b * (nq // NSA_NB) + i, qcol + g)),
                  pl.BlockSpec((1, 1, NSA_NB * QB, 3 * NSA_HG), lambda b, g, i: (b, g, i, 0)),
                  pl.BlockSpec((1, 1, ncmp, NSA_DH), lambda b, g, i: (b, g, 0, 0)),
                  pl.BlockSpec((1, 1, ncmp, NSA_DH), lambda b, g, i: (b, g, 0, 0)),
                  kv_spec(ks), kv_spec(vs), kv_spec(kw), kv_spec(vw),
                  tbl_spec(ncmp), tbl_spec(NEAR_BLOCKS * SLC_BLOCK), tbl_spec(WINDOW + QB),
                  pl.BlockSpec((nslc, ncmp), lambda b, g, i: (0, 0))],
        out_specs=pl.BlockSpec((NSA_NB * QB, NSA_HG * NSA_DH), lambda b, g, i: (b * (nq // NSA_NB) + i, g)),
        scratch_shapes=[pltpu.VMEM((NSA_NB * rows, seq), F32)],
        compiler_params=_cp(("parallel", "parallel", "arbitrary"), VMEM_LIMIT),
        name="nsa_attention",
    )(proj, gate_logits, kvc[0], kvc[1], ks, vs, kw, vw, tblc, tbln, tblw, jnp.asarray(overlap_t, BF16))


def _merge_kernel(h_ref, yp_ref, yr_ref, yn_ref, x_ref, wbp_ref, wbr_ref, wbn_ref, wm_ref, bm_ref, wo_ref,
                  gn_ref, wr_ref, br_ref, xo_ref, h2_ref, route_ref):
    d = D_MODEL
    gl = jax.nn.sigmoid(_dot(h_ref[...], wm_ref[...]) + bm_ref[...])
    merged = (gl[:, :d] * _dot(yp_ref[...], wbp_ref[...]) + gl[:, d:2 * d] * _dot(yr_ref[...], wbr_ref[...])
              + gl[:, 2 * d:] * _dot(yn_ref[...], wbn_ref[...]))
    x = x_ref[...] + _dot(merged.astype(BF16), wo_ref[...])
    xo_ref[...] = x
    h2 = _rms(x, gn_ref[...])
    h2_ref[...] = h2.astype(h2_ref.dtype)

    logits = _dot(h2, wr_ref[...], HI) + br_ref[...]
    lane = lax.broadcasted_iota(jnp.int32, logits.shape, 1)
    big = logits.shape[1]
    lg = jnp.where(lane < N_GROUPS, logits, NEG)
    mg = jnp.max(lg, axis=1, keepdims=True)
    p_top = 1.0 / jnp.sum(jnp.exp(lg - mg), axis=1, keepdims=True)
    grp = jnp.min(jnp.where(lg == mg, lane, big), axis=1, keepdims=True)
    lo = N_GROUPS + EPG * grp
    le = jnp.where((lane >= lo) & (lane < lo + EPG), logits, NEG)
    e1 = jnp.max(le, axis=1, keepdims=True)
    i1 = jnp.min(jnp.where(le == e1, lane, big), axis=1, keepdims=True)
    le = jnp.where(lane == i1, NEG, le)
    e2 = jnp.max(le, axis=1, keepdims=True)
    i2 = jnp.min(jnp.where(le == e2, lane, big), axis=1, keepdims=True)
    t = jnp.exp(e2 - e1)
    w1 = p_top / (1.0 + t)
    w2 = p_top * t / (1.0 + t)
    route_ref[...] = jnp.where(lane == 0, (i1 - N_GROUPS).astype(F32),
                               jnp.where(lane == 1, (i2 - N_GROUPS).astype(F32),
                                         jnp.where(lane == 2, w1, jnp.where(lane == 3, w2, 0.0))))


def _merge(h, y_pool, y_rwkv, y_nsa, x, p, tm=256):
    n, d = x.shape
    row = lambda w: pl.BlockSpec((tm, w), lambda i: (i, 0))
    const = lambda a: pl.BlockSpec(a.shape, lambda i: (0, 0))
    ws = [p["wb_pool"], p["wb_rwkv"], p["wb_nsa"], p["w_merge"], p["b_merge"], p["w_out"],
          p["norm_ffn"], p["w_router"], p["b_router"]]
    return pl.pallas_call(
        _merge_kernel,
        out_shape=(jax.ShapeDtypeStruct((n, d), F32), jax.ShapeDtypeStruct((n, d), F32),
                   jax.ShapeDtypeStruct((n, 128), F32)),
        grid=(n // tm,),
        in_specs=[row(d), row(POOL_WIDTH), row(RW_WIDTH), row(d), row(d)] + [const(w) for w in ws],
        out_specs=(row(d), row(d), row(128)),
        compiler_params=_cp(("parallel",), VMEM_LIMIT),
        name="merge_router",
    )(h, y_pool, y_rwkv, y_nsa, x, *ws)


def _expert_kernel(te_ref, nt_ref, x_ref, w_ref, wg_ref, wu_ref, wd_ref, o_ref, wg_s, wu_s, wd_s):
    i = pl.program_id(0)

    @pl.when((i == 0) | (te_ref[i] != te_ref[jnp.maximum(i - 1, 0)]))
    def _():
        wg_s[...] = wg_ref[0].astype(BF16)
        wu_s[...] = wu_ref[0].astype(BF16)
        wd_s[...] = wd_ref[0].astype(BF16)

    @pl.when(i < nt_ref[0])
    def _():
        xb = x_ref[...].astype(BF16)
        gate = _dot(xb, wg_s[...])
        hid = gate * jax.nn.sigmoid(gate) * _dot(xb, wu_s[...])
        o_ref[...] = w_ref[...] * _dot(hid.astype(BF16), wd_s[...])

    @pl.when(i >= nt_ref[0])
    def _():
        o_ref[...] = jnp.zeros_like(o_ref)


def _experts(xs, roww, tile_expert, n_tiles, wg, wu, wd):
    r, d = xs.shape
    tm = MOE_TM
    return pl.pallas_call(
        _expert_kernel,
        out_shape=jax.ShapeDtypeStruct((r, d), F32),
        grid_spec=pltpu.PrefetchScalarGridSpec(
            num_scalar_prefetch=2,
            grid=(r // tm,),
            in_specs=[pl.BlockSpec((tm, d), lambda i, te, nt: (i, 0)),
                      pl.BlockSpec((tm, 1), lambda i, te, nt: (i, 0)),
                      pl.BlockSpec((1, d, D_EXPERT), lambda i, te, nt: (te[i], 0, 0)),
                      pl.BlockSpec((1, d, D_EXPERT), lambda i, te, nt: (te[i], 0, 0)),
                      pl.BlockSpec((1, D_EXPERT, d), lambda i, te, nt: (te[i], 0, 0))],
            out_specs=pl.BlockSpec((tm, d), lambda i, te, nt: (i, 0)),
            scratch_shapes=[pltpu.VMEM((d, D_EXPERT), BF16), pltpu.VMEM((d, D_EXPERT), BF16),
                            pltpu.VMEM((D_EXPERT, d), BF16)]),
        compiler_params=_cp(("arbitrary",), VMEM_LIMIT),
        name="moe_experts",
    )(tile_expert, n_tiles, xs, roww, wg, wu, wd)


def _moe(h2, route, wg, wu, wd):
    n = h2.shape[0]
    tm = MOE_TM
    r = 2 * n + N_EXPERTS * tm
    ids = route[:, 0:2].astype(jnp.int32).reshape(-1)
    wts = route[:, 2:4].reshape(-1)
    onehot = (ids[:, None] == jnp.arange(N_EXPERTS)[None, :]).astype(jnp.int32)
    rank = jnp.sum((jnp.cumsum(onehot, axis=0) - onehot) * onehot, axis=1)
    counts = jnp.sum(onehot, axis=0)
    tiles = (counts + tm - 1) // tm
    tile_end = jnp.cumsum(tiles)
    starts = (tile_end - tiles) * tm
    pos = starts[ids] + rank
    row_assign = jnp.full((r,), -1, jnp.int32).at[pos].set(jnp.arange(2 * n, dtype=jnp.int32))
    rowtok = jnp.maximum(row_assign, 0) // 2
    roww = jnp.where(row_assign >= 0, wts[jnp.maximum(row_assign, 0)], 0.0)
    n_tiles = tile_end[-1:].astype(jnp.int32)
    tile_expert = jnp.minimum(jnp.sum(tile_end[None, :] <= jnp.arange(r // tm)[:, None], axis=1),
                              N_EXPERTS - 1).astype(jnp.int32)
    ys = _experts(h2[rowtok], roww.reshape(r, 1), tile_expert, n_tiles, wg, wu, wd)
    return ys[pos[0::2]], ys[pos[1::2]]


def _layer_params(l, a):
    f = lambda t: t[l]
    row = lambda t: t[l].reshape(1, -1)
    w_in = a["w_in"][l]
    w_in_p = jnp.concatenate([w_in[:, :SRC_RW_END], jnp.zeros((D_MODEL, C_Q - SRC_RW_END), F32),
                              w_in[:, SRC_RW_END:],
                              jnp.zeros((D_MODEL, P_COLS - C_Q - (w_in.shape[1] - SRC_RW_END)), F32)], axis=1)
    mu = a["rw_mu"][l]
    wb = a["w_branch"][l].astype(BF16)
    w_router = jnp.zeros((D_MODEL, 128), F32)
    w_router = w_router.at[:, :N_GROUPS].set(a["w_router_grp"][l]).at[:, N_GROUPS:N_GROUPS + N_EXPERTS].set(
        a["w_router_exp"][l])
    b_router = jnp.zeros((1, 128), F32)
    b_router = b_router.at[0, :N_GROUPS].set(a["b_router_grp"][l]).at[0, N_GROUPS:N_GROUPS + N_EXPERTS].set(
        a["b_router_exp"][l])
    pos = jnp.stack([a["cmp_pos_k"][l].reshape(-1), a["cmp_pos_v"][l].reshape(-1)])
    return {
        "w_in": w_in_p.astype(BF16),
        "pool_w": f(a["pool_w"]), "pool_scale": f(a["pool_scale"]),
        "mu_r": mu[None, 0:RW_WIDTH], "mu_k": mu[None, RW_WIDTH:2 * RW_WIDTH],
        "mu_v": mu[None, 2 * RW_WIDTH:3 * RW_WIDTH],
        "mu_l": jnp.concatenate([mu[3 * RW_WIDTH:], jnp.zeros((LORA_PAD - RW_LORA,), F32)])[None],
        "w0": row(a["rw_w0"]), "w_up": f(a["rw_w_up"]), "a0": row(a["rw_a0"]), "a_up": f(a["rw_a_up"]),
        "g_up": f(a["rw_g_up"]), "k_k": row(a["rw_k_k"]), "k_a": row(a["rw_k_a"]), "r_k": row(a["rw_r_k"]),
        "gn_w": row(a["rw_gn_w"]), "gn_b": row(a["rw_gn_b"]),
        "cmp_pos": jnp.broadcast_to(pos[:, None, :], (2, 8, pos.shape[1])).astype(BF16),
        "cmp_w1": jnp.stack([a["cmp_w1_k"][l], a["cmp_w1_v"][l]]).astype(BF16),
        "cmp_w2": jnp.stack([a["cmp_w2_k"][l], a["cmp_w2_v"][l]]).astype(BF16),
        "wb_pool": wb[:POOL_WIDTH], "wb_rwkv": wb[POOL_WIDTH:POOL_WIDTH + RW_WIDTH],
        "wb_nsa": wb[POOL_WIDTH + RW_WIDTH:],
        "w_merge": a["w_merge"][l].astype(BF16), "b_merge": row(a["b_merge"]),
        "w_out": a["w_out"][l].astype(BF16), "norm_ffn": row(a["norm_ffn"]),
        "w_router": w_router, "b_router": b_router,
        "w_gate": a["w_exp_gate"][l], "w_up_e": a["w_exp_up"][l], "w_down": a["w_exp_down"][l],
    }


def _mixers(proj, p, tables, batch, seq):
    y_pool = _pool(proj, p["pool_w"], p["pool_scale"], batch, seq)
    y_rwkv = _rwkv(proj, p, batch, seq)
    kv = proj[:, C_KV:C_GATE].reshape(batch, seq, 6, NSA_G, NSA_DH).transpose(2, 0, 3, 1, 4).astype(BF16)
    xkv = kv[0:2].reshape(2, batch, NSA_G, seq // CMP_STRIDE, CMP_STRIDE * NSA_DH)
    kvc = _compress(xkv, p["cmp_pos"], p["cmp_w1"], p["cmp_w2"], batch, seq)
    pad = lambda t, n: jnp.pad(t, ((0, 0), (0, 0), (n, 0), (0, 0)))
    pad_s = (NEAR_BLOCKS - 1) * SLC_BLOCK
    gate_logits = proj[:, C_GATE:C_GATE + 3 * NSA_HEADS].reshape(batch, seq, NSA_G, 3 * NSA_HG).transpose(0, 2, 1, 3)
    y_nsa = _nsa(proj, gate_logits, kvc, pad(kv[2], pad_s), pad(kv[3], pad_s), pad(kv[4], WINDOW),
                 pad(kv[5], WINDOW), tables, batch, seq)
    return y_pool, y_rwkv, y_nsa


def kernel(x, rel_bias, norm_mix, w_in, pool_w, pool_scale, rw_mu, rw_w0, rw_w_up, rw_a0, rw_a_up, rw_g_up, rw_k_k, rw_k_a, rw_r_k, rw_gn_w, rw_gn_b, cmp_pos_k, cmp_w1_k, cmp_w2_k, cmp_pos_v, cmp_w1_v, cmp_w2_v, w_branch, w_merge, b_merge, w_out, norm_ffn, w_router_grp, b_router_grp, w_router_exp, b_router_exp, w_exp_gate, w_exp_up, w_exp_down, norm_final):
    a = dict(w_in=w_in, pool_w=pool_w, pool_scale=pool_scale, rw_mu=rw_mu, rw_w0=rw_w0, rw_w_up=rw_w_up,
             rw_a0=rw_a0, rw_a_up=rw_a_up, rw_g_up=rw_g_up, rw_k_k=rw_k_k, rw_k_a=rw_k_a, rw_r_k=rw_r_k,
             rw_gn_w=rw_gn_w, rw_gn_b=rw_gn_b, cmp_pos_k=cmp_pos_k, cmp_w1_k=cmp_w1_k, cmp_w2_k=cmp_w2_k,
             cmp_pos_v=cmp_pos_v, cmp_w1_v=cmp_w1_v, cmp_w2_v=cmp_w2_v, w_branch=w_branch, w_merge=w_merge,
             b_merge=b_merge, w_out=w_out, norm_ffn=norm_ffn, w_router_grp=w_router_grp,
             b_router_grp=b_router_grp, w_router_exp=w_router_exp, b_router_exp=b_router_exp,
             w_exp_gate=w_exp_gate, w_exp_up=w_exp_up, w_exp_down=w_exp_down)
    batch, seq, d = x.shape
    depth = norm_mix.shape[0]
    tables = _bias_tables(rel_bias, seq // CMP_STRIDE)
    xf = x.reshape(batch * seq, d)
    h = _norm(xf, norm_mix[0], BF16)
    for l in range(depth):
        p = _layer_params(l, a)
        proj = _matmul(h, p["w_in"])
        y_pool, y_rwkv, y_nsa = _mixers(proj, p, tables, batch, seq)
        xf, h2, route = _merge(h, y_pool, y_rwkv, y_nsa, xf, p)
        y1, y2 = _moe(h2, route, p["w_gate"], p["w_up_e"], p["w_down"])
        last = l == depth - 1
        g_next = norm_final if last else norm_mix[l + 1]
        xf, h = _add_norm(xf, y1, y2, g_next, F32 if last else BF16)
    return h.reshape(batch, seq, d)
```

```python
import functools
import math

import jax
import jax.numpy as jnp
import numpy as np
from jax import lax
from jax.experimental import pallas as pl
from jax.experimental.pallas import tpu as pltpu

F32 = jnp.float32
BF16 = jnp.bfloat16
HI = lax.Precision.HIGHEST

D_MODEL = 1024
RMS_EPS = 1e-6
NEG = -1e30

POOL_WINDOWS = (2, 4, 8, 16)
POOL_WIDTH = 512
POOL_GW = 128
POOL_HALO = 16

RW_HEADS = 8
RW_DH = 64
RW_WIDTH = 512
DECAY_LORA, AAA_LORA, GATE_LORA = 32, 32, 96
RW_LORA = DECAY_LORA + AAA_LORA + GATE_LORA
RW_COLS = 3 * RW_WIDTH + RW_LORA
RW_GN_EPS = 64e-5
RW_CHUNK = 64

NSA_DH = 64
NSA_HEADS = 16
NSA_G = 4
NSA_HG = 4
NSA_KVW = NSA_G * NSA_DH
CMP_BLOCK, CMP_STRIDE, CMP_HIDDEN = 32, 16, 256
SLC_BLOCK = 64
N_SELECT = 8
WINDOW = 512
QB = 64
NSA_NB = 4
FORCE_BONUS = 1e3
N_BUCKETS, MAX_EXACT, MAX_DISTANCE = 32, 16, 128
NEAR_BLOCKS = 3
FAR_CHUNK_BLOCKS = 8
CMP_NEAR = 32
CMP_AHEAD = (QB - CMP_BLOCK) // CMP_STRIDE + 1

N_GROUPS, EPG, N_EXPERTS, D_EXPERT = 4, 8, 32, 256
MOE_TM = 256

C_POOL, C_R, C_K, C_V, C_LORA, C_Q, C_GATE, P_COLS = 0, 512, 1024, 1536, 2048, 2304, 3328, 3584
SRC_Q, SRC_KV, SRC_GATE = 2208, 3232, 4768
N_KV = 6
LORA_PAD = 256
SRC_RW_END = POOL_WIDTH + RW_COLS

VMEM_LIMIT = 56 * 1024 * 1024


def _t5_bucket_np(dist):
    n = np.maximum(dist, 0)
    nf = np.maximum(n, 1).astype(np.float32)
    large = MAX_EXACT + (np.log(nf / MAX_EXACT) / math.log(MAX_DISTANCE / MAX_EXACT)
                         * (N_BUCKETS - MAX_EXACT)).astype(np.int32)
    large = np.minimum(large, N_BUCKETS - 1)
    return np.where(n < MAX_EXACT, n, large)


def _cp(sem, vmem=None):
    return pltpu.CompilerParams(dimension_semantics=sem, vmem_limit_bytes=vmem)


def _dot(a, b, precision=None):
    return jnp.dot(a, b, preferred_element_type=F32, precision=precision)


def _dot_nt(a, b, precision=None):
    return lax.dot_general(a, b, (((1,), (1,)), ((), ())), preferred_element_type=F32, precision=precision)


def _dot_tn(a, b, precision=None):
    return lax.dot_general(a, b, (((0,), (0,)), ((), ())), preferred_element_type=F32, precision=precision)


def _bdot(a, b):
    return _dot(a.astype(BF16), b.astype(BF16))


def _bdot_nt(a, b):
    return _dot_nt(a.astype(BF16), b.astype(BF16))


def _bdot_tn(a, b):
    return _dot_tn(a.astype(BF16), b.astype(BF16))


def _split(a):
    hi = a.astype(BF16)
    return hi, (a - hi.astype(F32)).astype(BF16)


def _dot3(a, b):
    ah, al = _split(a)
    bh, bl = _split(b)
    return _dot(ah, bh) + (_dot(ah, bl) + _dot(al, bh))


def _rms(x, g):
    return x * lax.rsqrt(jnp.mean(x * x, axis=-1, keepdims=True) + RMS_EPS) * g


def _norm_kernel(x_ref, g_ref, h_ref):
    h_ref[...] = _rms(x_ref[...], g_ref[...]).astype(h_ref.dtype)


def _norm(x, g, out_dtype, tm=512):
    n, d = x.shape
    return pl.pallas_call(
        _norm_kernel,
        out_shape=jax.ShapeDtypeStruct((n, d), out_dtype),
        grid=(n // tm,),
        in_specs=[pl.BlockSpec((tm, d), lambda i: (i, 0)), pl.BlockSpec((1, d), lambda i: (0, 0))],
        out_specs=pl.BlockSpec((tm, d), lambda i: (i, 0)),
        compiler_params=_cp(("parallel",)),
        name="rms_norm",
    )(x, g.reshape(1, d))


def _add_norm_kernel(x_ref, y1_ref, y2_ref, g_ref, xo_ref, h_ref):
    x = x_ref[...] + (y1_ref[...] + y2_ref[...])
    xo_ref[...] = x
    h_ref[...] = _rms(x, g_ref[...]).astype(h_ref.dtype)


def _add_norm(x, y1, y2, g, out_dtype, tm=512):
    n, d = x.shape
    row = pl.BlockSpec((tm, d), lambda i: (i, 0))
    return pl.pallas_call(
        _add_norm_kernel,
        out_shape=(jax.ShapeDtypeStruct((n, d), F32), jax.ShapeDtypeStruct((n, d), out_dtype)),
        grid=(n // tm,),
        in_specs=[row, row, row, pl.BlockSpec((1, d), lambda i: (0, 0))],
        out_specs=(row, row),
        compiler_params=_cp(("parallel",)),
        name="moe_combine_norm",
    )(x, y1, y2, g.reshape(1, d))


def _matmul_kernel(x_ref, w_ref, o_ref):
    o_ref[...] = _dot(x_ref[...], w_ref[...]).astype(o_ref.dtype)


def _matmul(x, w, tm=512, tn=P_COLS // 2):
    m, k = x.shape
    n = w.shape[1]
    return pl.pallas_call(
        _matmul_kernel,
        out_shape=jax.ShapeDtypeStruct((m, n), F32),
        grid=(n // tn, m // tm),
        in_specs=[pl.BlockSpec((tm, k), lambda j, i: (i, 0)), pl.BlockSpec((k, tn), lambda j, i: (0, j))],
        out_specs=pl.BlockSpec((tm, tn), lambda j, i: (i, j)),
        compiler_params=_cp(("parallel", "parallel"), VMEM_LIMIT),
        name="in_proj",
    )(x, w)


def _kv_proj_kernel(x_ref, w_ref, o_ref):
    res = _dot(x_ref[...], w_ref[...])
    for t in range(N_KV):
        for g in range(NSA_G):
            c0 = (t * NSA_G + g) * NSA_DH
            o_ref[t, 0, g] = res[:, c0:c0 + NSA_DH].astype(o_ref.dtype)


def _kv_proj(x, w, batch, seq, tm=512):
    k = x.shape[1]
    nt = seq // tm
    return pl.pallas_call(
        _kv_proj_kernel,
        out_shape=jax.ShapeDtypeStruct((N_KV, batch, NSA_G, seq, NSA_DH), BF16),
        grid=(batch, nt),
        in_specs=[pl.BlockSpec((tm, k), lambda b, i: (b * nt + i, 0)),
                  pl.BlockSpec(w.shape, lambda b, i: (0, 0))],
        out_specs=pl.BlockSpec((N_KV, 1, NSA_G, tm, NSA_DH), lambda b, i: (0, b, 0, i, 0)),
        compiler_params=_cp(("parallel", "parallel"), VMEM_LIMIT),
        name="kv_proj",
    )(x, w)


def _pool_kernel(u_ref, halo_ref, w_ref, scale_ref, o_ref, buf_ref, *, tile):
    i = pl.program_id(1)
    u = u_ref[...]
    buf_ref[POOL_HALO:, :] = u
    buf_ref[:POOL_HALO, :] = jnp.where(i > 0, halo_ref[...], 0.0)
    t = i * tile + lax.broadcasted_iota(jnp.int32, (tile, 1), 0)
    outs = []
    for gi, win in enumerate(POOL_WINDOWS):
        cols = slice(gi * POOL_GW, (gi + 1) * POOL_GW)
        s = u[:, cols]
        for j in range(1, win):
            s = s + buf_ref[POOL_HALO - j:POOL_HALO - j + tile, cols]
        cnt = jnp.minimum(t + 1, win).astype(F32)
        pooled = s / cnt - u[:, cols]
        outs.append(_dot(pooled.astype(BF16), w_ref[gi]))
    o_ref[...] = (jnp.concatenate(outs, axis=1) * scale_ref[...]).astype(o_ref.dtype)


def _pool(proj, w_grp, scale, batch, seq, tile=512):
    nt = seq // tile
    hb = tile // POOL_HALO
    return pl.pallas_call(
        functools.partial(_pool_kernel, tile=tile),
        out_shape=jax.ShapeDtypeStruct((batch * seq, POOL_WIDTH), BF16),
        grid=(batch, nt),
        in_specs=[
            pl.BlockSpec((tile, POOL_WIDTH), lambda b, i: (b * nt + i, 0)),
            pl.BlockSpec((POOL_HALO, POOL_WIDTH), lambda b, i: (jnp.maximum((b * nt + i) * hb - 1, 0), 0)),
            pl.BlockSpec((len(POOL_WINDOWS), POOL_GW, POOL_GW), lambda b, i: (0, 0, 0)),
            pl.BlockSpec((1, POOL_WIDTH), lambda b, i: (0, 0)),
        ],
        out_specs=pl.BlockSpec((tile, POOL_WIDTH), lambda b, i: (b * nt + i, 0)),
        scratch_shapes=[pltpu.VMEM((tile + POOL_HALO, POOL_WIDTH), F32)],
        compiler_params=_cp(("parallel", "parallel")),
        name="pool_mixer",
    )(proj, proj, w_grp.astype(BF16), scale.reshape(1, POOL_WIDTH))


def _token_shift(u, halo, mu, first):
    prev_row = jnp.where(first, 0.0, halo[7:8, :])
    rolled = pltpu.roll(u, 1, 0)
    row = lax.broadcasted_iota(jnp.int32, u.shape, 0)
    prev = jnp.where(row == 0, prev_row, rolled)
    return u + (prev - u) * mu


def _rwkv_chunk_kernel(r_ref, k_ref, v_ref, l_ref, rh_ref, kh_ref, vh_ref, lh_ref,
                       mur_ref, muk_ref, muv_ref, mul_ref, w0_ref, wup_ref, a0_ref, aup_ref, gup_ref,
                       kk_ref, ka_ref, rk_ref, bd_ref, qy_ref, mn_ref, g_ref, bonus_ref):
    first = pl.program_id(1) == 0
    c = RW_CHUNK
    r = _token_shift(r_ref[...], rh_ref[...], mur_ref[...], first)
    k = _token_shift(k_ref[...], kh_ref[...], muk_ref[...], first)
    v = _token_shift(v_ref[...], vh_ref[...], muv_ref[...], first)
    lo = _token_shift(l_ref[...], lh_ref[...], mul_ref[...], first)
    wd = lo[:, :DECAY_LORA]
    ad = lo[:, DECAY_LORA:DECAY_LORA + AAA_LORA]
    gd = lo[:, DECAY_LORA + AAA_LORA:RW_LORA]
    z = -(w0_ref[...] + _dot(jnp.tanh(wd), wup_ref[...], HI))
    w_log = -(jnp.maximum(z, 0.0) + jnp.log(1.0 + jnp.exp(-jnp.abs(z)))) - 0.5
    logw = -jnp.exp(w_log)
    a = jax.nn.sigmoid(a0_ref[...] + _dot(ad, aup_ref[...], HI))
    g_ref[0] = _dot(jax.nn.sigmoid(gd), gup_ref[...], HI)
    kkraw = k * kk_ref[...]
    k2 = k * (1.0 + (a - 1.0) * ka_ref[...])
    rkr = r * k2 * rk_ref[...]

    cum_all = logw
    trow = lax.broadcasted_iota(jnp.int32, logw.shape, 0)
    step = 1
    while step < c:
        cum_all = cum_all + jnp.where(trow >= step, pltpu.roll(cum_all, step, 0), 0.0)
        step *= 2

    ti = lax.broadcasted_iota(jnp.int32, (c, c), 0)
    si = lax.broadcasted_iota(jnp.int32, (c, c), 1)
    incl = ti >= si
    strict = ti > si
    eye = ti == si
    zeros = jnp.zeros((c, c), F32)
    bd = bd_ref[...]

    def head_sum(t):
        hi, lo = _split(t)
        return _dot(hi, bd) + _dot(lo, bd)

    kk = kkraw / jnp.maximum(jnp.sqrt(head_sum(kkraw * kkraw)), 1e-12)
    bonus_ref[0] = head_sum(rkr) * v
    cum_end = cum_all[c - 1:c, :]
    ginv = jnp.exp(-cum_all)
    gtail = jnp.exp(cum_end - cum_all)
    gend = jnp.exp(cum_end)
    kka = kk * a
    at = -kk * jnp.exp(cum_all - logw)
    bt = kka * ginv
    kt = k2 * ginv
    rt = r * jnp.exp(cum_all)
    bhat = kka * gtail
    khat = k2 * gtail

    heads = [slice(h * RW_DH, (h + 1) * RW_DH) for h in range(RW_HEADS)]
    stack = lambda x, y, s: jnp.concatenate([x[:, s], y[:, s]], axis=0).astype(BF16)
    gram = [_dot_nt(stack(at, rt, s), stack(bt, kt, s)) for s in heads]
    a_ab = [jnp.where(strict, g[:c, :c], 0.0) for g in gram]
    a_ak = [jnp.where(strict, g[:c, c:], 0.0) for g in gram]
    incl2 = (lax.broadcasted_iota(jnp.int32, (c, 2 * c), 0)
             >= lax.broadcasted_iota(jnp.int32, (c, 2 * c), 1) % c)
    a_r = [jnp.where(incl2, g[c:, :], 0.0) for g in gram]
    p = a_ab
    tinv = [eye.astype(F32) + x for x in p]
    for _ in range(int(math.log2(c)) - 1):
        p = [_bdot(x, x) for x in p]
        tinv = [t + _bdot(t, x) for t, x in zip(tinv, p)]
    av = [_bdot(x, v[:, s]) for x, s in zip(a_ak, heads)]
    w12 = [_bdot(t, jnp.concatenate([at[:, s], x], axis=1)) for t, x, s in zip(tinv, av, heads)]
    zmat = [jnp.concatenate([w, jnp.concatenate([zeros, v[:, s]], axis=1)], axis=0).astype(BF16)
            for w, s in zip(w12, heads)]
    out1 = [_dot(x.astype(BF16), z) for x, z in zip(a_r, zmat)]
    out2 = [_dot_tn(stack(bhat, khat, s), z) for s, z in zip(heads, zmat)]
    qy_ref[0] = jnp.concatenate([o + jnp.concatenate([rt[:, s], zeros], axis=1) for o, s in zip(out1, heads)],
                                axis=1)
    for h, s in enumerate(heads):
        diag = jnp.where(eye, gend[:, s], 0.0)
        mn_ref[0, 0, h] = out2[h] + jnp.concatenate([diag, zeros], axis=1)


def _rwkv_scan_kernel(qy_ref, mn_ref, g_ref, bonus_ref, gnw_ref, gnb_ref, bd_ref, y_ref, st_ref, *, batch):
    @pl.when(pl.program_id(0) == 0)
    def _():
        st_ref[...] = jnp.zeros_like(st_ref)

    bd = bd_ref[...]

    def head_mean(t):
        hi, lo = _split(t)
        return (_dot(hi, bd) + _dot(lo, bd)) * (1.0 / RW_DH)

    pairs = [(b, h) for b in range(batch) for h in range(RW_HEADS)]
    sts = [st_ref[b * RW_HEADS + h] for b, h in pairs]
    ys = [_bdot(qy_ref[b, :, 2 * RW_DH * h:2 * RW_DH * h + RW_DH], st)
          + qy_ref[b, :, 2 * RW_DH * h + RW_DH:2 * RW_DH * (h + 1)] for (b, h), st in zip(pairs, sts)]
    for (b, h), st in zip(pairs, sts):
        mn = mn_ref[b, 0, h]
        st_ref[b * RW_HEADS + h] = _dot3(mn[:, :RW_DH], st) + mn[:, RW_DH:]
    for b in range(batch):
        y = jnp.concatenate(ys[b * RW_HEADS:(b + 1) * RW_HEADS], axis=1)
        dev = y - head_mean(y)
        yn = dev * lax.rsqrt(head_mean(dev * dev) + RW_GN_EPS) * gnw_ref[...] + gnb_ref[...]
        y_ref[b] = ((yn + bonus_ref[b]) * g_ref[b]).astype(y_ref.dtype)


def _rwkv(proj, p, batch, seq):
    head_ones = jnp.asarray(np.kron(np.eye(RW_HEADS), np.ones((RW_DH, RW_DH))), BF16)
    c = RW_CHUNK
    nc = seq // c
    hb = c // 8
    row512 = lambda col: pl.BlockSpec((c, RW_WIDTH), lambda b, i: (b * nc + i, col))
    halo512 = lambda col: pl.BlockSpec((8, RW_WIDTH), lambda b, i: (jnp.maximum((b * nc + i) * hb - 1, 0), col))
    const = lambda shape: pl.BlockSpec(shape, lambda b, i: (0,) * len(shape))
    vec = const((1, RW_WIDTH))
    out_row = lambda w: pl.BlockSpec((1, c, w), lambda b, i: (b, i, 0))
    qy, mn, g, bonus = pl.pallas_call(
        _rwkv_chunk_kernel,
        out_shape=(jax.ShapeDtypeStruct((batch, seq, 2 * RW_WIDTH), F32),
                   jax.ShapeDtypeStruct((batch, nc, RW_HEADS, RW_DH, 2 * RW_DH), F32),
                   jax.ShapeDtypeStruct((batch, seq, RW_WIDTH), F32),
                   jax.ShapeDtypeStruct((batch, seq, RW_WIDTH), F32)),
        grid=(batch, nc),
        in_specs=[row512(C_R // RW_WIDTH), row512(C_K // RW_WIDTH), row512(C_V // RW_WIDTH),
                  pl.BlockSpec((c, LORA_PAD), lambda b, i: (b * nc + i, C_LORA // LORA_PAD)),
                  halo512(C_R // RW_WIDTH), halo512(C_K // RW_WIDTH), halo512(C_V // RW_WIDTH),
                  pl.BlockSpec((8, LORA_PAD), lambda b, i: (jnp.maximum((b * nc + i) * hb - 1, 0), C_LORA // LORA_PAD)),
                  vec, vec, vec, const((1, LORA_PAD)),
                  vec, const((DECAY_LORA, RW_WIDTH)), vec, const((AAA_LORA, RW_WIDTH)), const((GATE_LORA, RW_WIDTH)),
                  vec, vec, vec, const((RW_WIDTH, RW_WIDTH))],
        out_specs=(out_row(2 * RW_WIDTH),
                   pl.BlockSpec((1, 1, RW_HEADS, RW_DH, 2 * RW_DH), lambda b, i: (b, i, 0, 0, 0)),
                   out_row(RW_WIDTH), out_row(RW_WIDTH)),
        compiler_params=_cp(("parallel", "parallel"), VMEM_LIMIT),
        name="rwkv_chunk",
    )(proj, proj, proj, proj, proj, proj, proj, proj,
      p["mu_r"], p["mu_k"], p["mu_v"], p["mu_l"], p["w0"], p["w_up"], p["a0"], p["a_up"], p["g_up"],
      p["k_k"], p["k_a"], p["r_k"], head_ones)

    full = lambda w: pl.BlockSpec((batch, c, w), lambda i: (0, i, 0))
    return pl.pallas_call(
        functools.partial(_rwkv_scan_kernel, batch=batch),
        out_shape=jax.ShapeDtypeStruct((batch, seq, RW_WIDTH), BF16),
        grid=(nc,),
        in_specs=[full(2 * RW_WIDTH),
                  pl.BlockSpec((batch, 1, RW_HEADS, RW_DH, 2 * RW_DH), lambda i: (0, i, 0, 0, 0)),
                  full(RW_WIDTH), full(RW_WIDTH),
                  pl.BlockSpec((1, RW_WIDTH), lambda i: (0, 0)), pl.BlockSpec((1, RW_WIDTH), lambda i: (0, 0)),
                  pl.BlockSpec((RW_WIDTH, RW_WIDTH), lambda i: (0, 0))],
        out_specs=full(RW_WIDTH),
        scratch_shapes=[pltpu.VMEM((batch * RW_HEADS, RW_DH, RW_DH), F32)],
        compiler_params=_cp(("arbitrary",), VMEM_LIMIT),
        name="rwkv_scan",
    )(qy, mn, g, bonus, p["gn_w"], p["gn_b"], head_ones).reshape(batch * seq, RW_WIDTH)


def _gelu_tanh(x):
    return 0.5 * x * (1.0 + jnp.tanh(math.sqrt(2.0 / math.pi) * (x + 0.044715 * (x * x * x))))


def _compress_kernel(x_ref, pos_ref, w1_ref, w2_ref, o_ref):
    half = CMP_STRIDE * NSA_DH
    x = x_ref[0, 0, 0]
    w1 = w1_ref[0]
    posb = _dot(pos_ref[0], w1)[0:1, :]
    h1 = _dot(x, w1[:half])
    h2 = _dot(x, w1[half:])
    n = h2.shape[0]
    row = lax.broadcasted_iota(jnp.int32, h2.shape, 0)
    h2s = jnp.where(row < n - 1, pltpu.roll(h2, n - 1, 0), 0.0)
    hid = _gelu_tanh(h1 + h2s + posb)
    o_ref[0, 0, 0] = _dot(hid.astype(BF16), w2_ref[0]).astype(o_ref.dtype)


def _compress(xkv, pos, w1, w2, batch, seq):
    nr = seq // CMP_STRIDE
    wide = CMP_STRIDE * NSA_DH
    return pl.pallas_call(
        _compress_kernel,
        out_shape=jax.ShapeDtypeStruct((2, batch, NSA_G, nr, NSA_DH), BF16),
        grid=(2, batch, NSA_G),
        in_specs=[pl.BlockSpec((1, 1, 1, nr, wide), lambda t, b, g: (t, b, g, 0, 0)),
                  pl.BlockSpec((1, 8, 2 * wide), lambda t, b, g: (t, 0, 0)),
                  pl.BlockSpec((1, 2 * wide, CMP_HIDDEN), lambda t, b, g: (t, 0, 0)),
                  pl.BlockSpec((1, CMP_HIDDEN, NSA_DH), lambda t, b, g: (t, 0, 0))],
        out_specs=pl.BlockSpec((1, 1, 1, nr, NSA_DH), lambda t, b, g: (t, b, g, 0, 0)),
        compiler_params=_cp(("parallel", "parallel", "parallel")),
        name="nsa_compress",
    )(xkv, pos, w1, w2)


def _softmax_parts(s):
    m = jnp.max(s, axis=1, keepdims=True)
    e = jnp.exp(s - m)
    return m, e, jnp.sum(e, axis=1, keepdims=True)


def _lane_tile_fold(x, op, init):
    for t in range(x.shape[1] // 128):
        init = op(init, x[:, 128 * t:128 * (t + 1)])
    return init


def _nsa_kernel(q_ref, gate_ref, kc_ref, vc_ref, ks_ref, vs_ref, kw_ref, vw_ref,
                tblc_ref, tbln_ref, tblw_ref, ovt_ref, o_ref, s_ref, *, ncmp):
    nb = NSA_NB
    blocks = [pl.program_id(2) * nb + u for u in range(nb)]
    rows = NSA_HG * QB
    each = lambda f, *ls: [f(*xs) for xs in zip(*ls)]
    qt = q_ref[...]
    qs = [jnp.concatenate([qt[u * QB:(u + 1) * QB, NSA_DH * h:NSA_DH * (h + 1)] for h in range(NSA_HG)], axis=0)
          * (NSA_DH ** -0.5) for u in range(nb)]
    qb = each(lambda x: x.astype(BF16), qs)
    rmax = lambda x: jnp.max(x, axis=1, keepdims=True)
    rsum = lambda x: jnp.sum(x, axis=1, keepdims=True)

    win_w = WINDOW + QB
    kw = [kw_ref[0, 0, pl.ds(pl.multiple_of(i * QB, QB), win_w), :] for i in blocks]
    vw = [vw_ref[0, 0, pl.ds(pl.multiple_of(i * QB, QB), win_w), :] for i in blocks]
    flag = lax.broadcasted_iota(jnp.int32, (rows, kw_ref.shape[3] - NSA_DH), 1) == 0
    q_win = each(lambda x: jnp.concatenate([x, jnp.where(flag, NEG, 0.0)], axis=1).astype(BF16), qs)
    tblw = tblw_ref[0]
    s_w = each(lambda x, k: _dot_nt(x, k) + tblw, q_win, kw)
    m_w = each(rmax, s_w)
    e_w = each(lambda s, m: jnp.exp(s - m), s_w, m_w)
    den_w = each(rsum, e_w)
    o_w = each(lambda e, v, d: _dot(e.astype(BF16), v) / d, e_w, vw, den_w)

    kc = kc_ref[0, 0]
    vc = vc_ref[0, 0]
    tblc = tblc_ref[0]
    cidx = lax.broadcasted_iota(jnp.int32, (rows, ncmp), 1)
    qrow = lax.broadcasted_iota(jnp.int32, (rows, 1), 0) % QB
    lc = [jnp.where(cidx < (QB // CMP_STRIDE) * i + CMP_AHEAD,
                    _dot_nt(x, kc) + pltpu.roll(tblc, (4 * i - CMP_NEAR // 2 + ncmp) % ncmp, 1), NEG)
          for x, i in zip(qb, blocks)]
    m_c = each(rmax, lc)
    e_c = each(lambda s, m: jnp.exp(s - m), lc, m_c)
    den_c = each(rsum, e_c)
    pc = [e * jnp.where(i * QB + qrow >= CMP_BLOCK - 1, 1.0 / d, 0.0) for e, d, i in zip(e_c, den_c, blocks)]
    o_c = each(lambda x: _dot(x.astype(BF16), vc), pc)
    pcs = each(lambda x: _split(x[0:QB] + x[QB:2 * QB] + x[2 * QB:3 * QB] + x[3 * QB:4 * QB]), pc)
    ovt = ovt_ref[...]
    imp = each(lambda x: _dot_nt(ovt, x[0]) + _dot_nt(ovt, x[1]), pcs)

    nslc = ovt.shape[0]
    nidx = lax.broadcasted_iota(jnp.int32, (nslc, QB), 0)
    work = [jnp.where(nidx <= i, x + jnp.where((nidx == 0) | (nidx == i) | (nidx == i - 1), FORCE_BONUS, 0.0), -1.0)
            for x, i in zip(imp, blocks)]
    sel = [jnp.zeros((nslc, QB), F32)] * nb
    for _ in range(N_SELECT):
        m = each(lambda w: jnp.max(w, axis=0, keepdims=True), work)
        first = each(lambda w, mm: jnp.min(jnp.where(w == mm, nidx, nslc), axis=0, keepdims=True), work, m)
        sel = each(lambda s, f, mm: jnp.where((nidx == f) & (mm >= 0.0), 1.0, s), sel, first, m)
        work = each(lambda w, f: jnp.where(nidx == f, -2.0, w), work, first)

    ind_w = ks_ref.shape[3] - NSA_DH
    selq = each(lambda x: x.T, sel)
    if nslc < ind_w:
        selq = each(lambda x: jnp.concatenate([x, jnp.zeros((QB, ind_w - nslc), F32)], axis=1), selq)
    bidx = lax.broadcasted_iota(jnp.int32, (QB, ind_w), 1)
    tile4 = lambda t: jnp.concatenate([t] * NSA_HG, axis=0)
    with_mask = lambda x, keep: jnp.concatenate([x, tile4(jnp.where(keep, 0.0, NEG))], axis=1).astype(BF16)
    q_sel = each(lambda x, s: with_mask(x, s > 0.0), qs, selq)
    q_far = jnp.concatenate([with_mask(x, (s > 0.0) & (bidx <= i - NEAR_BLOCKS))
                             for x, s, i in zip(qs, selq, blocks)], axis=0)
    near_w = NEAR_BLOCKS * SLC_BLOCK
    kn = [ks_ref[0, 0, pl.ds(pl.multiple_of(i * QB, QB), near_w), :] for i in blocks]
    vn = [vs_ref[0, 0, pl.ds(pl.multiple_of(i * QB, QB), near_w), :] for i in blocks]
    tbln = tbln_ref[0]
    s_near = each(lambda x, k: _dot_nt(x, k) + tbln, q_sel, kn)
    m_near = jnp.concatenate(each(rmax, s_near), axis=0)

    far_w = FAR_CHUNK_BLOCKS * SLC_BLOCK
    pad_s = (NEAR_BLOCKS - 1) * SLC_BLOCK
    n_far = jnp.maximum(blocks[-1] - (NEAR_BLOCKS - 1) + FAR_CHUNK_BLOCKS - 1, 0) // FAR_CHUNK_BLOCKS

    def far_logits(j, mvec):
        start = pl.multiple_of(j * far_w, far_w)
        kf = ks_ref[0, 0, pl.ds(pl.multiple_of(pad_s + j * far_w, SLC_BLOCK), far_w), :]
        s = _dot_nt(q_far, kf)
        s_ref[:, pl.ds(start, far_w)] = s
        return _lane_tile_fold(s, jnp.maximum, mvec)

    mvec = lax.fori_loop(0, n_far, far_logits, jnp.full((nb * rows, 128), NEG, F32))
    m_s = jnp.maximum(m_near, rmax(mvec))

    e_near = [jnp.exp(s - m_s[u * rows:(u + 1) * rows]) for u, s in enumerate(s_near)]
    acc0 = jnp.concatenate(each(lambda e, v: _dot(e.astype(BF16), v), e_near, vn), axis=0)

    def far_values(j, carry):
        lvec, acc = carry
        start = pl.multiple_of(j * far_w, far_w)
        vf = vs_ref[0, 0, pl.ds(pl.multiple_of(pad_s + j * far_w, SLC_BLOCK), far_w), :]
        e = jnp.exp(s_ref[:, pl.ds(start, far_w)] - m_s)
        return _lane_tile_fold(e, jnp.add, lvec), acc + _dot(e.astype(BF16), vf)

    lvec, acc = lax.fori_loop(0, n_far, far_values, (jnp.zeros((nb * rows, 128), F32), acc0))
    o_s = acc / (jnp.concatenate(each(rsum, e_near), axis=0) + rsum(lvec))

    for u in range(nb):
        gates = jax.nn.sigmoid(gate_ref[0, 0, u * QB:(u + 1) * QB, :])
        outs = []
        for h in range(NSA_HG):
            r0 = slice(h * QB, (h + 1) * QB)
            outs.append(gates[:, 3 * h:3 * h + 1] * o_c[u][r0]
                        + gates[:, 3 * h + 1:3 * h + 2] * o_s[u * rows + h * QB:u * rows + (h + 1) * QB]
                        + gates[:, 3 * h + 2:3 * h + 3] * o_w[u][r0])
        o_ref[u * QB:(u + 1) * QB, :] = jnp.concatenate(outs, axis=1).astype(o_ref.dtype)


def _bias_tables(rel_bias, ncmp):
    def table(dist, keep, base):
        tbl = rel_bias[jnp.asarray(_t5_bucket_np(dist))] - base
        tbl = jnp.where(jnp.asarray(keep)[..., None], tbl, NEG)
        k = dist.shape[1]
        return tbl.transpose(2, 0, 1).reshape(NSA_G, NSA_HG * QB, k)

    far = rel_bias[N_BUCKETS - 1]
    qi = np.arange(QB)[:, None]
    dist_c = qi - CMP_STRIDE * (np.arange(CMP_NEAR)[None, :] - CMP_NEAR // 2) - (CMP_BLOCK - 1)
    tblc = table(dist_c, dist_c >= 0, far)
    tblc = tblc * jnp.asarray(np.arange(CMP_NEAR) < CMP_NEAR // 2 + CMP_AHEAD, F32)
    tblc = jnp.pad(tblc, ((0, 0), (0, 0), (0, ncmp - CMP_NEAR)))
    jn = np.arange(NEAR_BLOCKS * SLC_BLOCK)[None, :]
    dist_n = (NEAR_BLOCKS - 1) * SLC_BLOCK + qi - jn
    tbln = table(dist_n, dist_n >= 0, far)
    jw = np.arange(WINDOW + QB)[None, :]
    dist_w = WINDOW + qi - jw
    tblw = table(dist_w, (dist_w >= 0) & (dist_w < WINDOW), 0.0)
    return tblc, tbln, tblw


def _nsa(proj, gate_logits, kvc, ks, vs, kw, vw, tables, batch, seq):
    nq = seq // QB
    ncmp = seq // CMP_STRIDE
    nslc = seq // SLC_BLOCK
    tblc, tbln, tblw = tables
    cstart = np.arange(ncmp) * CMP_STRIDE
    sstart = np.arange(nslc) * SLC_BLOCK
    overlap_t = ((cstart[None, :] <= sstart[:, None] + SLC_BLOCK - 1)
                 & (cstart[None, :] + CMP_BLOCK - 1 >= sstart[:, None])
                 & (cstart[None, :] + CMP_BLOCK <= seq)).astype(np.float32)
    pad_s = ks.shape[2] - seq
    blk = (np.arange(seq)[:, None] // SLC_BLOCK == np.arange(NSA_DH)[None, :]).astype(np.float32)
    blk = np.concatenate([np.ones((pad_s, NSA_DH), np.float32), blk], axis=0)
    ks = jnp.concatenate([ks, jnp.broadcast_to(jnp.asarray(blk, BF16), ks.shape[:2] + blk.shape)], axis=-1)
    flag = np.zeros((kw.shape[2], NSA_DH), np.float32)
    flag[:kw.shape[2] - seq, 0] = 1.0
    kw = jnp.concatenate([kw, jnp.broadcast_to(jnp.asarray(flag, BF16), kw.shape[:2] + flag.shape)], axis=-1)
    rows = NSA_HG * QB
    kv_spec = lambda t: pl.BlockSpec((1, 1) + t.shape[2:], lambda b, g, i: (b, g, 0, 0))
    tbl_spec = lambda k: pl.BlockSpec((1, rows, k), lambda b, g, i: (g, 0, 0))
    qcol = C_Q // (NSA_HG * NSA_DH)
    return pl.pallas_call(
        functools.partial(_nsa_kernel, ncmp=ncmp),
        out_shape=jax.ShapeDtypeStruct((batch * seq, NSA_HEADS * NSA_DH), BF16),
        grid=(batch, NSA_G, nq // NSA_NB),
        in_specs=[pl.BlockSpec((NSA_NB * QB, NSA_HG * NSA_DH), lambda b, g, i: (b * (nq // NSA_NB) + i, qcol + g)),
                  pl.BlockSpec((1, 1, NSA_NB * QB, 3 * NSA_HG), lambda b, g, i: (b, g, i, 0)),
                  pl.BlockSpec((1, 1, ncmp, NSA_DH), lambda b, g, i: (b, g, 0, 0)),
                  pl.BlockSpec((1, 1, ncmp, NSA_DH), lambda b, g, i: (b, g, 0, 0)),
                  kv_spec(ks), kv_spec(vs), kv_spec(kw), kv_spec(vw),
                  tbl_spec(ncmp), tbl_spec(NEAR_BLOCKS * SLC_BLOCK), tbl_spec(WINDOW + QB),
                  pl.BlockSpec((nslc, ncmp), lambda b, g, i: (0, 0))],
        out_specs=pl.BlockSpec((NSA_NB * QB, NSA_HG * NSA_DH), lambda b, g, i: (b * (nq // NSA_NB) + i, g)),
        scratch_shapes=[pltpu.VMEM((NSA_NB * rows, seq), F32)],
        compiler_params=_cp(("parallel", "parallel", "arbitrary"), VMEM_LIMIT),
        name="nsa_attention",
    )(proj, gate_logits, kvc[0], kvc[1], ks, vs, kw, vw, tblc, tbln, tblw, jnp.asarray(overlap_t, BF16))


def _merge_kernel(h_ref, yp_ref, yr_ref, yn_ref, x_ref, wbp_ref, wbr_ref, wbn_ref, wm_ref, bm_ref, wo_ref,
                  gn_ref, wr_ref, br_ref, xo_ref, h2_ref, route_ref):
    d = D_MODEL
    gl = jax.nn.sigmoid(_dot(h_ref[...], wm_ref[...]) + bm_ref[...])
    merged = (gl[:, :d] * _dot(yp_ref[...], wbp_ref[...]) + gl[:, d:2 * d] * _dot(yr_ref[...], wbr_ref[...])
              + gl[:, 2 * d:] * _dot(yn_ref[...], wbn_ref[...]))
    x = x_ref[...] + _dot(merged.astype(BF16), wo_ref[...])
    xo_ref[...] = x
    h2 = _rms(x, gn_ref[...])
    h2_ref[...] = h2.astype(h2_ref.dtype)

    logits = _dot(h2, wr_ref[...], HI) + br_ref[...]
    lane = lax.broadcasted_iota(jnp.int32, logits.shape, 1)
    big = logits.shape[1]
    lg = jnp.where(lane < N_GROUPS, logits, NEG)
    mg = jnp.max(lg, axis=1, keepdims=True)
    p_top = 1.0 / jnp.sum(jnp.exp(lg - mg), axis=1, keepdims=True)
    grp = jnp.min(jnp.where(lg == mg, lane, big), axis=1, keepdims=True)
    lo = N_GROUPS + EPG * grp
    le = jnp.where((lane >= lo) & (lane < lo + EPG), logits, NEG)
    e1 = jnp.max(le, axis=1, keepdims=True)
    i1 = jnp.min(jnp.where(le == e1, lane, big), axis=1, keepdims=True)
    le = jnp.where(lane == i1, NEG, le)
    e2 = jnp.max(le, axis=1, keepdims=True)
    i2 = jnp.min(jnp.where(le == e2, lane, big), axis=1, keepdims=True)
    t = jnp.exp(e2 - e1)
    w1 = p_top / (1.0 + t)
    w2 = p_top * t / (1.0 + t)
    route_ref[...] = jnp.where(lane == 0, (i1 - N_GROUPS).astype(F32),
                               jnp.where(lane == 1, (i2 - N_GROUPS).astype(F32),
                                         jnp.where(lane == 2, w1, jnp.where(lane == 3, w2, 0.0))))


def _merge(h, y_pool, y_rwkv, y_nsa, x, p, tm=256):
    n, d = x.shape
    row = lambda w: pl.BlockSpec((tm, w), lambda i: (i, 0))
    const = lambda a: pl.BlockSpec(a.shape, lambda i: (0, 0))
    ws = [p["wb_pool"], p["wb_rwkv"], p["wb_nsa"], p["w_merge"], p["b_merge"], p["w_out"],
          p["norm_ffn"], p["w_router"], p["b_router"]]
    return pl.pallas_call(
        _merge_kernel,
        out_shape=(jax.ShapeDtypeStruct((n, d), F32), jax.ShapeDtypeStruct((n, d), F32),
                   jax.ShapeDtypeStruct((n, 128), F32)),
        grid=(n // tm,),
        in_specs=[row(d), row(POOL_WIDTH), row(RW_WIDTH), row(d), row(d)] + [const(w) for w in ws],
        out_specs=(row(d), row(d), row(128)),
        compiler_params=_cp(("parallel",), VMEM_LIMIT),
        name="merge_router",
    )(h, y_pool, y_rwkv, y_nsa, x, *ws)


def _expert_kernel(te_ref, nt_ref, tok_ref, h_hbm, w_ref, wg_ref, wu_ref, wd_ref, o_ref,
                   xbuf, sem, wg_s, wu_s, wd_s):
    i = pl.program_id(0)
    tm = MOE_TM
    n_tiles = nt_ref[0]

    def row_copy(tile, slot, r):
        return pltpu.make_async_copy(h_hbm.at[pl.ds(tok_ref[tile * tm + r], 1), :],
                                     xbuf.at[slot, pl.ds(r, 1), :], sem.at[slot])

    def tile_wait(slot):
        pltpu.make_async_copy(h_hbm.at[pl.ds(0, tm), :], xbuf.at[slot], sem.at[slot]).wait()

    @pl.when((i == 0) & (n_tiles > 0))
    def _():
        def body(r, carry):
            row_copy(0, 0, r).start()
            return carry
        lax.fori_loop(0, tm, body, 0, unroll=8)

    @pl.when((i == 0) | (te_ref[i] != te_ref[jnp.maximum(i - 1, 0)]))
    def _():
        wg_s[...] = wg_ref[0].astype(BF16)
        wu_s[...] = wu_ref[0].astype(BF16)
        wd_s[...] = wd_ref[0].astype(BF16)

    @pl.when(i < n_tiles)
    def _():
        slot = i % 2
        quarter = tm // 4

        def fetch_next(part):
            for r in range(part * quarter, (part + 1) * quarter):
                row_copy(i + 1, 1 - slot, r).start()

        tile_wait(slot)
        xb = xbuf[slot].astype(BF16)
        fetch_next(0)
        gate = _dot(xb, wg_s[...])
        fetch_next(1)
        up = _dot(xb, wu_s[...])
        fetch_next(2)
        hid = (gate * jax.nn.sigmoid(gate) * up).astype(BF16)
        fetch_next(3)
        o_ref[...] = w_ref[...] * _dot(hid, wd_s[...])

    @pl.when((i == n_tiles) & (n_tiles > 0))
    def _():
        tile_wait(i % 2)

    @pl.when(i >= nt_ref[0])
    def _():
        o_ref[...] = jnp.zeros_like(o_ref)


def _experts(h2, rowtok, roww, tile_expert, n_tiles, wg, wu, wd):
    d = h2.shape[1]
    r = rowtok.shape[0]
    tm = MOE_TM
    return pl.pallas_call(
        _expert_kernel,
        out_shape=jax.ShapeDtypeStruct((r, d), F32),
        grid_spec=pltpu.PrefetchScalarGridSpec(
            num_scalar_prefetch=3,
            grid=(r // tm,),
            in_specs=[pl.BlockSpec(memory_space=pl.ANY),
                      pl.BlockSpec((tm, 1), lambda i, te, nt, tok: (i, 0)),
                      pl.BlockSpec((1, d, D_EXPERT), lambda i, te, nt, tok: (te[i], 0, 0)),
                      pl.BlockSpec((1, d, D_EXPERT), lambda i, te, nt, tok: (te[i], 0, 0)),
                      pl.BlockSpec((1, D_EXPERT, d), lambda i, te, nt, tok: (te[i], 0, 0))],
            out_specs=pl.BlockSpec((tm, d), lambda i, te, nt, tok: (i, 0)),
            scratch_shapes=[pltpu.VMEM((2, tm, d), F32), pltpu.SemaphoreType.DMA((2,)),
                            pltpu.VMEM((d, D_EXPERT), BF16), pltpu.VMEM((d, D_EXPERT), BF16),
                            pltpu.VMEM((D_EXPERT, d), BF16)]),
        compiler_params=_cp(("arbitrary",), VMEM_LIMIT),
        name="moe_experts",
    )(tile_expert, n_tiles, rowtok, h2, roww, wg, wu, wd)


def _moe(h2, route, wg, wu, wd):
    n = h2.shape[0]
    tm = MOE_TM
    r = 2 * n + (N_EXPERTS + 1) * tm
    ids = route[:, 0:2].astype(jnp.int32).reshape(-1)
    wts = route[:, 2:4].reshape(-1)
    onehot = (ids[:, None] == jnp.arange(N_EXPERTS)[None, :]).astype(jnp.int32)
    rank = jnp.sum((jnp.cumsum(onehot, axis=0) - onehot) * onehot, axis=1)
    counts = jnp.sum(onehot, axis=0)
    tiles = (counts + tm - 1) // tm
    tile_end = jnp.cumsum(tiles)
    starts = (tile_end - tiles) * tm
    pos = starts[ids] + rank
    row_assign = jnp.full((r,), -1, jnp.int32).at[pos].set(jnp.arange(2 * n, dtype=jnp.int32))
    rowtok = jnp.maximum(row_assign, 0) // 2
    roww = jnp.where(row_assign >= 0, wts[jnp.maximum(row_assign, 0)], 0.0)
    n_tiles = tile_end[-1:].astype(jnp.int32)
    tile_expert = jnp.minimum(jnp.sum(tile_end[None, :] <= jnp.arange(r // tm)[:, None], axis=1),
                              N_EXPERTS - 1).astype(jnp.int32)
    ys = _experts(h2, rowtok, roww.reshape(r, 1), tile_expert, n_tiles, wg, wu, wd)
    return ys[pos[0::2]], ys[pos[1::2]]


def _layer_params(l, a):
    f = lambda t: t[l]
    row = lambda t: t[l].reshape(1, -1)
    w_in = a["w_in"][l]
    w_in_p = jnp.concatenate([w_in[:, :SRC_RW_END], jnp.zeros((D_MODEL, C_Q - SRC_RW_END), F32),
                              w_in[:, SRC_Q:SRC_KV], w_in[:, SRC_GATE:],
                              jnp.zeros((D_MODEL, P_COLS - C_GATE - (w_in.shape[1] - SRC_GATE)), F32)], axis=1)
    mu = a["rw_mu"][l]
    wb = a["w_branch"][l].astype(BF16)
    w_router = jnp.zeros((D_MODEL, 128), F32)
    w_router = w_router.at[:, :N_GROUPS].set(a["w_router_grp"][l]).at[:, N_GROUPS:N_GROUPS + N_EXPERTS].set(
        a["w_router_exp"][l])
    b_router = jnp.zeros((1, 128), F32)
    b_router = b_router.at[0, :N_GROUPS].set(a["b_router_grp"][l]).at[0, N_GROUPS:N_GROUPS + N_EXPERTS].set(
        a["b_router_exp"][l])
    pos = jnp.stack([a["cmp_pos_k"][l].reshape(-1), a["cmp_pos_v"][l].reshape(-1)])
    return {
        "w_in": w_in_p.astype(BF16), "w_kv": w_in[:, SRC_KV:SRC_GATE].astype(BF16),
        "pool_w": f(a["pool_w"]), "pool_scale": f(a["pool_scale"]),
        "mu_r": mu[None, 0:RW_WIDTH], "mu_k": mu[None, RW_WIDTH:2 * RW_WIDTH],
        "mu_v": mu[None, 2 * RW_WIDTH:3 * RW_WIDTH],
        "mu_l": jnp.concatenate([mu[3 * RW_WIDTH:], jnp.zeros((LORA_PAD - RW_LORA,), F32)])[None],
        "w0": row(a["rw_w0"]), "w_up": f(a["rw_w_up"]), "a0": row(a["rw_a0"]), "a_up": f(a["rw_a_up"]),
        "g_up": f(a["rw_g_up"]), "k_k": row(a["rw_k_k"]), "k_a": row(a["rw_k_a"]), "r_k": row(a["rw_r_k"]),
        "gn_w": row(a["rw_gn_w"]), "gn_b": row(a["rw_gn_b"]),
        "cmp_pos": jnp.broadcast_to(pos[:, None, :], (2, 8, pos.shape[1])).astype(BF16),
        "cmp_w1": jnp.stack([a["cmp_w1_k"][l], a["cmp_w1_v"][l]]).astype(BF16),
        "cmp_w2": jnp.stack([a["cmp_w2_k"][l], a["cmp_w2_v"][l]]).astype(BF16),
        "wb_pool": wb[:POOL_WIDTH], "wb_rwkv": wb[POOL_WIDTH:POOL_WIDTH + RW_WIDTH],
        "wb_nsa": wb[POOL_WIDTH + RW_WIDTH:],
        "w_merge": a["w_merge"][l].astype(BF16), "b_merge": row(a["b_merge"]),
        "w_out": a["w_out"][l].astype(BF16), "norm_ffn": row(a["norm_ffn"]),
        "w_router": w_router, "b_router": b_router,
        "w_gate": a["w_exp_gate"][l], "w_up_e": a["w_exp_up"][l], "w_down": a["w_exp_down"][l],
    }


def _mixers(proj, kv, p, tables, batch, seq):
    y_pool = _pool(proj, p["pool_w"], p["pool_scale"], batch, seq)
    y_rwkv = _rwkv(proj, p, batch, seq)
    xkv = kv[0:2].reshape(2, batch, NSA_G, seq // CMP_STRIDE, CMP_STRIDE * NSA_DH)
    kvc = _compress(xkv, p["cmp_pos"], p["cmp_w1"], p["cmp_w2"], batch, seq)
    pad = lambda t, n: jnp.pad(t, ((0, 0), (0, 0), (n, 0), (0, 0)))
    pad_s = (NEAR_BLOCKS - 1) * SLC_BLOCK
    gate_logits = proj[:, C_GATE:C_GATE + 3 * NSA_HEADS].reshape(batch, seq, NSA_G, 3 * NSA_HG).transpose(0, 2, 1, 3)
    y_nsa = _nsa(proj, gate_logits, kvc, pad(kv[2], pad_s), pad(kv[3], pad_s), pad(kv[4], WINDOW),
                 pad(kv[5], WINDOW), tables, batch, seq)
    return y_pool, y_rwkv, y_nsa


def kernel(x, rel_bias, norm_mix, w_in, pool_w, pool_scale, rw_mu, rw_w0, rw_w_up, rw_a0, rw_a_up, rw_g_up, rw_k_k, rw_k_a, rw_r_k, rw_gn_w, rw_gn_b, cmp_pos_k, cmp_w1_k, cmp_w2_k, cmp_pos_v, cmp_w1_v, cmp_w2_v, w_branch, w_merge, b_merge, w_out, norm_ffn, w_router_grp, b_router_grp, w_router_exp, b_router_exp, w_exp_gate, w_exp_up, w_exp_down, norm_final):
    a = dict(w_in=w_in, pool_w=pool_w, pool_scale=pool_scale, rw_mu=rw_mu, rw_w0=rw_w0, rw_w_up=rw_w_up,
             rw_a0=rw_a0, rw_a_up=rw_a_up, rw_g_up=rw_g_up, rw_k_k=rw_k_k, rw_k_a=rw_k_a, rw_r_k=rw_r_k,
             rw_gn_w=rw_gn_w, rw_gn_b=rw_gn_b, cmp_pos_k=cmp_pos_k, cmp_w1_k=cmp_w1_k, cmp_w2_k=cmp_w2_k,
             cmp_pos_v=cmp_pos_v, cmp_w1_v=cmp_w1_v, cmp_w2_v=cmp_w2_v, w_branch=w_branch, w_merge=w_merge,
             b_merge=b_merge, w_out=w_out, norm_ffn=norm_ffn, w_router_grp=w_router_grp,
             b_router_grp=b_router_grp, w_router_exp=w_router_exp, b_router_exp=b_router_exp,
             w_exp_gate=w_exp_gate, w_exp_up=w_exp_up, w_exp_down=w_exp_down)
    batch, seq, d = x.shape
    depth = norm_mix.shape[0]
    tables = _bias_tables(rel_bias, seq // CMP_STRIDE)
    xf = x.reshape(batch * seq, d)
    h = _norm(xf, norm_mix[0], BF16)
    for l in range(depth):
        p = _layer_params(l, a)
        proj = _matmul(h, p["w_in"])
        kv = _kv_proj(h, p["w_kv"], batch, seq)
        y_pool, y_rwkv, y_nsa = _mixers(proj, kv, p, tables, batch, seq)
        xf, h2, route = _merge(h, y_pool, y_rwkv, y_nsa, xf, p)
        y1, y2 = _moe(h2, route, p["w_gate"], p["w_up_e"], p["w_down"])
        last = l == depth - 1
        g_next = norm_final if last else norm_mix[l + 1]
        xf, h = _add_norm(xf, y1, y2, g_next, F32 if last else BF16)
    return h.reshape(batch, seq, d)
```

```python
import functools
import math

import jax
import jax.numpy as jnp
import numpy as np
from jax import lax
from jax.experimental import pallas as pl
from jax.experimental.pallas import tpu as pltpu

F32 = jnp.float32
BF16 = jnp.bfloat16
HI = lax.Precision.HIGHEST

D_MODEL = 1024
RMS_EPS = 1e-6
NEG = -1e30
LOG2E = math.log2(math.e)

POOL_WINDOWS = (2, 4, 8, 16)
POOL_WIDTH = 512
POOL_GW = 128
POOL_HALO = 16

RW_HEADS = 8
RW_DH = 64
RW_WIDTH = 512
DECAY_LORA, AAA_LORA, GATE_LORA = 32, 32, 96
RW_LORA = DECAY_LORA + AAA_LORA + GATE_LORA
RW_COLS = 3 * RW_WIDTH + RW_LORA
RW_GN_EPS = 64e-5
RW_CHUNK = 64
RW_NB = 2

NSA_DH = 64
NSA_HEADS = 16
NSA_G = 4
NSA_HG = 4
NSA_KVW = NSA_G * NSA_DH
CMP_BLOCK, CMP_STRIDE, CMP_HIDDEN = 32, 16, 256
SLC_BLOCK = 64
N_SELECT = 8
WINDOW = 512
QB = 64
NSA_NB = 4
FORCE_BONUS = 1e3
N_BUCKETS, MAX_EXACT, MAX_DISTANCE = 32, 16, 128
NEAR_BLOCKS = 3
FAR_CHUNK_BLOCKS = 8
CMP_NEAR = 32
CMP_AHEAD = (QB - CMP_BLOCK) // CMP_STRIDE + 1

N_GROUPS, EPG, N_EXPERTS, D_EXPERT = 4, 8, 32, 256
MOE_TM = 256

C_POOL, C_R, C_K, C_V, C_LORA, C_Q, C_GATE, P_COLS = 0, 512, 1024, 1536, 2048, 2304, 3328, 3584
SRC_Q, SRC_KV, SRC_GATE = 2208, 3232, 4768
N_KV = 6
LORA_PAD = 256
SRC_RW_END = POOL_WIDTH + RW_COLS

VMEM_LIMIT = 56 * 1024 * 1024


def _t5_bucket_np(dist):
    n = np.maximum(dist, 0)
    nf = np.maximum(n, 1).astype(np.float32)
    large = MAX_EXACT + (np.log(nf / MAX_EXACT) / math.log(MAX_DISTANCE / MAX_EXACT)
                         * (N_BUCKETS - MAX_EXACT)).astype(np.int32)
    large = np.minimum(large, N_BUCKETS - 1)
    return np.where(n < MAX_EXACT, n, large)


def _cp(sem, vmem=None):
    return pltpu.CompilerParams(dimension_semantics=sem, vmem_limit_bytes=vmem)


def _dot(a, b, precision=None):
    return jnp.dot(a, b, preferred_element_type=F32, precision=precision)


def _dot_nt(a, b, precision=None):
    return lax.dot_general(a, b, (((1,), (1,)), ((), ())), preferred_element_type=F32, precision=precision)


def _dot_tn(a, b, precision=None):
    return lax.dot_general(a, b, (((0,), (0,)), ((), ())), preferred_element_type=F32, precision=precision)


def _bdot(a, b):
    return _dot(a.astype(BF16), b.astype(BF16))


def _bdot_nt(a, b):
    return _dot_nt(a.astype(BF16), b.astype(BF16))


def _bdot_tn(a, b):
    return _dot_tn(a.astype(BF16), b.astype(BF16))


def _split(a):
    hi = a.astype(BF16)
    return hi, (a - hi.astype(F32)).astype(BF16)


def _dot3(a, b):
    ah, al = _split(a)
    bh, bl = _split(b)
    return _dot(ah, bh) + (_dot(ah, bl) + _dot(al, bh))


def _rms(x, g):
    return x * lax.rsqrt(jnp.mean(x * x, axis=-1, keepdims=True) + RMS_EPS) * g


def _norm_kernel(x_ref, g_ref, h_ref):
    h_ref[...] = _rms(x_ref[...], g_ref[...]).astype(h_ref.dtype)


def _norm(x, g, out_dtype, tm=512):
    n, d = x.shape
    return pl.pallas_call(
        _norm_kernel,
        out_shape=jax.ShapeDtypeStruct((n, d), out_dtype),
        grid=(n // tm,),
        in_specs=[pl.BlockSpec((tm, d), lambda i: (i, 0)), pl.BlockSpec((1, d), lambda i: (0, 0))],
        out_specs=pl.BlockSpec((tm, d), lambda i: (i, 0)),
        compiler_params=_cp(("parallel",)),
        name="rms_norm",
    )(x, g.reshape(1, d))


def _add_norm_kernel(x_ref, y1_ref, y2_ref, g_ref, xo_ref, h_ref):
    x = x_ref[...] + (y1_ref[...] + y2_ref[...])
    xo_ref[...] = x
    h_ref[...] = _rms(x, g_ref[...]).astype(h_ref.dtype)


def _add_norm(x, y1, y2, g, out_dtype, tm=512):
    n, d = x.shape
    row = pl.BlockSpec((tm, d), lambda i: (i, 0))
    return pl.pallas_call(
        _add_norm_kernel,
        out_shape=(jax.ShapeDtypeStruct((n, d), F32), jax.ShapeDtypeStruct((n, d), out_dtype)),
        grid=(n // tm,),
        in_specs=[row, row, row, pl.BlockSpec((1, d), lambda i: (0, 0))],
        out_specs=(row, row),
        compiler_params=_cp(("parallel",)),
        name="moe_combine_norm",
    )(x, y1, y2, g.reshape(1, d))


def _matmul_kernel(x_ref, w_ref, o_ref):
    o_ref[...] = _dot(x_ref[...], w_ref[...]).astype(o_ref.dtype)


def _matmul(x, w, tm=512, tn=P_COLS // 2):
    m, k = x.shape
    n = w.shape[1]
    return pl.pallas_call(
        _matmul_kernel,
        out_shape=jax.ShapeDtypeStruct((m, n), F32),
        grid=(n // tn, m // tm),
        in_specs=[pl.BlockSpec((tm, k), lambda j, i: (i, 0)), pl.BlockSpec((k, tn), lambda j, i: (0, j))],
        out_specs=pl.BlockSpec((tm, tn), lambda j, i: (i, j)),
        compiler_params=_cp(("parallel", "parallel"), VMEM_LIMIT),
        name="in_proj",
    )(x, w)


def _kv_proj_kernel(x_ref, w_ref, o_ref):
    res = _dot(x_ref[...], w_ref[...])
    for t in range(N_KV):
        for g in range(NSA_G):
            c0 = (t * NSA_G + g) * NSA_DH
            o_ref[t, 0, g] = res[:, c0:c0 + NSA_DH].astype(o_ref.dtype)


def _kv_proj(x, w, batch, seq, tm=512):
    k = x.shape[1]
    nt = seq // tm
    return pl.pallas_call(
        _kv_proj_kernel,
        out_shape=jax.ShapeDtypeStruct((N_KV, batch, NSA_G, seq, NSA_DH), BF16),
        grid=(batch, nt),
        in_specs=[pl.BlockSpec((tm, k), lambda b, i: (b * nt + i, 0)),
                  pl.BlockSpec(w.shape, lambda b, i: (0, 0))],
        out_specs=pl.BlockSpec((N_KV, 1, NSA_G, tm, NSA_DH), lambda b, i: (0, b, 0, i, 0)),
        compiler_params=_cp(("parallel", "parallel"), VMEM_LIMIT),
        name="kv_proj",
    )(x, w)


def _pool_kernel(u_ref, halo_ref, w_ref, scale_ref, o_ref, buf_ref, *, tile):
    i = pl.program_id(1)
    u = u_ref[...]
    buf_ref[POOL_HALO:, :] = u
    buf_ref[:POOL_HALO, :] = jnp.where(i > 0, halo_ref[...], 0.0)
    t = i * tile + lax.broadcasted_iota(jnp.int32, (tile, 1), 0)
    outs = []
    for gi, win in enumerate(POOL_WINDOWS):
        cols = slice(gi * POOL_GW, (gi + 1) * POOL_GW)
        s = u[:, cols]
        for j in range(1, win):
            s = s + buf_ref[POOL_HALO - j:POOL_HALO - j + tile, cols]
        cnt = jnp.minimum(t + 1, win).astype(F32)
        pooled = s / cnt - u[:, cols]
        outs.append(_dot(pooled.astype(BF16), w_ref[gi]))
    o_ref[...] = (jnp.concatenate(outs, axis=1) * scale_ref[...]).astype(o_ref.dtype)


def _pool(proj, w_grp, scale, batch, seq, tile=512):
    nt = seq // tile
    hb = tile // POOL_HALO
    return pl.pallas_call(
        functools.partial(_pool_kernel, tile=tile),
        out_shape=jax.ShapeDtypeStruct((batch * seq, POOL_WIDTH), BF16),
        grid=(batch, nt),
        in_specs=[
            pl.BlockSpec((tile, POOL_WIDTH), lambda b, i: (b * nt + i, 0)),
            pl.BlockSpec((POOL_HALO, POOL_WIDTH), lambda b, i: (jnp.maximum((b * nt + i) * hb - 1, 0), 0)),
            pl.BlockSpec((len(POOL_WINDOWS), POOL_GW, POOL_GW), lambda b, i: (0, 0, 0)),
            pl.BlockSpec((1, POOL_WIDTH), lambda b, i: (0, 0)),
        ],
        out_specs=pl.BlockSpec((tile, POOL_WIDTH), lambda b, i: (b * nt + i, 0)),
        scratch_shapes=[pltpu.VMEM((tile + POOL_HALO, POOL_WIDTH), F32)],
        compiler_params=_cp(("parallel", "parallel")),
        name="pool_mixer",
    )(proj, proj, w_grp.astype(BF16), scale.reshape(1, POOL_WIDTH))


def _token_shift(u, halo, mu, first):
    prev_row = jnp.where(first, 0.0, halo[7:8, :])
    rolled = pltpu.roll(u, 1, 0)
    row = lax.broadcasted_iota(jnp.int32, u.shape, 0)
    prev = jnp.where(row == 0, prev_row, rolled)
    return u + (prev - u) * mu


def _rwkv_chunk_kernel(r_ref, k_ref, v_ref, l_ref, rh_ref, kh_ref, vh_ref, lh_ref,
                       mur_ref, muk_ref, muv_ref, mul_ref, w0_ref, wup_ref, a0_ref, aup_ref, gup_ref,
                       kk_ref, ka_ref, rk_ref, bd_ref, qy_ref, mn_ref, g_ref, bonus_ref):
    first = pl.program_id(1) == 0
    c = RW_CHUNK
    r = _token_shift(r_ref[...], rh_ref[...], mur_ref[...], first)
    k = _token_shift(k_ref[...], kh_ref[...], muk_ref[...], first)
    v = _token_shift(v_ref[...], vh_ref[...], muv_ref[...], first)
    lo = _token_shift(l_ref[...], lh_ref[...], mul_ref[...], first)
    wd = lo[:, :DECAY_LORA]
    ad = lo[:, DECAY_LORA:DECAY_LORA + AAA_LORA]
    gd = lo[:, DECAY_LORA + AAA_LORA:RW_LORA]
    z = -(w0_ref[...] + _dot(jnp.tanh(wd), wup_ref[...], HI))
    w_log = -(jnp.maximum(z, 0.0) + jnp.log(1.0 + jnp.exp(-jnp.abs(z)))) - 0.5
    logw = -jnp.exp(w_log)
    a = jax.nn.sigmoid(a0_ref[...] + _dot(ad, aup_ref[...], HI))
    g_ref[0] = _dot(jax.nn.sigmoid(gd), gup_ref[...], HI)
    kkraw = k * kk_ref[...]
    k2 = k * (1.0 + (a - 1.0) * ka_ref[...])
    rkr = r * k2 * rk_ref[...]

    cum_all = logw
    trow = lax.broadcasted_iota(jnp.int32, logw.shape, 0) % c
    step = 1
    while step < c:
        cum_all = cum_all + jnp.where(trow >= step, pltpu.roll(cum_all, step, 0), 0.0)
        step *= 2

    ti = lax.broadcasted_iota(jnp.int32, (c, c), 0)
    si = lax.broadcasted_iota(jnp.int32, (c, c), 1)
    incl = ti >= si
    strict = ti > si
    eye = ti == si
    zeros = jnp.zeros((c, c), F32)
    bd = bd_ref[...]

    def head_sum(t):
        hi, lo = _split(t)
        return _dot(hi, bd) + _dot(lo, bd)

    kk = kkraw / jnp.maximum(jnp.sqrt(head_sum(kkraw * kkraw)), 1e-12)
    bonus_ref[0] = head_sum(rkr) * v
    nchunk = logw.shape[0] // c
    ends = [cum_all[(j + 1) * c - 1:(j + 1) * c, :] for j in range(nchunk)]
    cum_end = jnp.concatenate([jnp.broadcast_to(e, (c, e.shape[1])) for e in ends], axis=0)
    ginv = jnp.exp(-cum_all)
    gtail = jnp.exp(cum_end - cum_all)
    gend = jnp.exp(cum_end)
    kka = kk * a
    at = -kk * jnp.exp(cum_all - logw)
    bt = kka * ginv
    kt = k2 * ginv
    rt = r * jnp.exp(cum_all)
    bhat = kka * gtail
    khat = k2 * gtail

    heads = [(slice(j * c, (j + 1) * c), slice(h * RW_DH, (h + 1) * RW_DH))
             for j in range(nchunk) for h in range(RW_HEADS)]
    stack = lambda x, y, s: jnp.concatenate([x[s], y[s]], axis=0).astype(BF16)
    gram = [_dot_nt(stack(at, rt, s), stack(bt, kt, s)) for s in heads]
    a_ab = [jnp.where(strict, g[:c, :c], 0.0) for g in gram]
    a_ak = [jnp.where(strict, g[:c, c:], 0.0) for g in gram]
    incl2 = (lax.broadcasted_iota(jnp.int32, (c, 2 * c), 0)
             >= lax.broadcasted_iota(jnp.int32, (c, 2 * c), 1) % c)
    a_r = [jnp.where(incl2, g[c:, :], 0.0) for g in gram]
    p = a_ab
    tinv = [eye.astype(F32) + x for x in p]
    for _ in range(int(math.log2(c)) - 1):
        p = [_bdot(x, x) for x in p]
        tinv = [t + _bdot(t, x) for t, x in zip(tinv, p)]
    av = [_bdot(x, v[s]) for x, s in zip(a_ak, heads)]
    w12 = [_bdot(t, jnp.concatenate([at[s], x], axis=1)) for t, x, s in zip(tinv, av, heads)]
    zmat = [jnp.concatenate([w, jnp.concatenate([zeros, v[s]], axis=1)], axis=0).astype(BF16)
            for w, s in zip(w12, heads)]
    out1 = [_dot(x.astype(BF16), z) for x, z in zip(a_r, zmat)]
    out2 = [_dot_tn(stack(bhat, khat, s), z) for s, z in zip(heads, zmat)]
    qy = [o + jnp.concatenate([rt[s], zeros], axis=1) for o, s in zip(out1, heads)]
    for j in range(nchunk):
        qy_ref[0, j * c:(j + 1) * c, :] = jnp.concatenate(qy[j * RW_HEADS:(j + 1) * RW_HEADS], axis=1)
        for h in range(RW_HEADS):
            idx = j * RW_HEADS + h
            diag = jnp.where(eye, gend[heads[idx]], 0.0)
            mn_ref[0, j, h] = out2[idx] + jnp.concatenate([diag, zeros], axis=1)


def _rwkv_scan_kernel(qy_ref, mn_ref, g_ref, bonus_ref, gnw_ref, gnb_ref, bd_ref, y_ref, st_ref, *, batch):
    @pl.when(pl.program_id(0) == 0)
    def _():
        st_ref[...] = jnp.zeros_like(st_ref)

    bd = bd_ref[...]

    def head_mean(t):
        hi, lo = _split(t)
        return (_dot(hi, bd) + _dot(lo, bd)) * (1.0 / RW_DH)

    pairs = [(b, h) for b in range(batch) for h in range(RW_HEADS)]
    sts = [st_ref[b * RW_HEADS + h] for b, h in pairs]
    ys = [_bdot(qy_ref[b, :, 2 * RW_DH * h:2 * RW_DH * h + RW_DH], st)
          + qy_ref[b, :, 2 * RW_DH * h + RW_DH:2 * RW_DH * (h + 1)] for (b, h), st in zip(pairs, sts)]
    for (b, h), st in zip(pairs, sts):
        mn = mn_ref[b, 0, h]
        st_ref[b * RW_HEADS + h] = _dot3(mn[:, :RW_DH], st) + mn[:, RW_DH:]
    for b in range(batch):
        y = jnp.concatenate(ys[b * RW_HEADS:(b + 1) * RW_HEADS], axis=1)
        dev = y - head_mean(y)
        yn = dev * lax.rsqrt(head_mean(dev * dev) + RW_GN_EPS) * gnw_ref[...] + gnb_ref[...]
        y_ref[b] = ((yn + bonus_ref[b]) * g_ref[b]).astype(y_ref.dtype)


def _rwkv(proj, p, batch, seq):
    head_ones = jnp.asarray(np.kron(np.eye(RW_HEADS), np.ones((RW_DH, RW_DH))), BF16)
    c = RW_CHUNK
    nc = seq // c
    tb = RW_NB * c
    nt = seq // tb
    hb = tb // 8
    row512 = lambda col: pl.BlockSpec((tb, RW_WIDTH), lambda b, i: (b * nt + i, col))
    halo512 = lambda col: pl.BlockSpec((8, RW_WIDTH), lambda b, i: (jnp.maximum((b * nt + i) * hb - 1, 0), col))
    const = lambda shape: pl.BlockSpec(shape, lambda b, i: (0,) * len(shape))
    vec = const((1, RW_WIDTH))
    out_row = lambda w: pl.BlockSpec((1, tb, w), lambda b, i: (b, i, 0))
    qy, mn, g, bonus = pl.pallas_call(
        _rwkv_chunk_kernel,
        out_shape=(jax.ShapeDtypeStruct((batch, seq, 2 * RW_WIDTH), F32),
                   jax.ShapeDtypeStruct((batch, nc, RW_HEADS, RW_DH, 2 * RW_DH), F32),
                   jax.ShapeDtypeStruct((batch, seq, RW_WIDTH), F32),
                   jax.ShapeDtypeStruct((batch, seq, RW_WIDTH), F32)),
        grid=(batch, nt),
        in_specs=[row512(C_R // RW_WIDTH), row512(C_K // RW_WIDTH), row512(C_V // RW_WIDTH),
                  pl.BlockSpec((tb, LORA_PAD), lambda b, i: (b * nt + i, C_LORA // LORA_PAD)),
                  halo512(C_R // RW_WIDTH), halo512(C_K // RW_WIDTH), halo512(C_V // RW_WIDTH),
                  pl.BlockSpec((8, LORA_PAD), lambda b, i: (jnp.maximum((b * nt + i) * hb - 1, 0), C_LORA // LORA_PAD)),
                  vec, vec, vec, const((1, LORA_PAD)),
                  vec, const((DECAY_LORA, RW_WIDTH)), vec, const((AAA_LORA, RW_WIDTH)), const((GATE_LORA, RW_WIDTH)),
                  vec, vec, vec, const((RW_WIDTH, RW_WIDTH))],
        out_specs=(out_row(2 * RW_WIDTH),
                   pl.BlockSpec((1, RW_NB, RW_HEADS, RW_DH, 2 * RW_DH), lambda b, i: (b, i, 0, 0, 0)),
                   out_row(RW_WIDTH), out_row(RW_WIDTH)),
        compiler_params=_cp(("parallel", "parallel"), VMEM_LIMIT),
        name="rwkv_chunk",
    )(proj, proj, proj, proj, proj, proj, proj, proj,
      p["mu_r"], p["mu_k"], p["mu_v"], p["mu_l"], p["w0"], p["w_up"], p["a0"], p["a_up"], p["g_up"],
      p["k_k"], p["k_a"], p["r_k"], head_ones)

    full = lambda w: pl.BlockSpec((batch, c, w), lambda i: (0, i, 0))
    return pl.pallas_call(
        functools.partial(_rwkv_scan_kernel, batch=batch),
        out_shape=jax.ShapeDtypeStruct((batch, seq, RW_WIDTH), BF16),
        grid=(nc,),
        in_specs=[full(2 * RW_WIDTH),
                  pl.BlockSpec((batch, 1, RW_HEADS, RW_DH, 2 * RW_DH), lambda i: (0, i, 0, 0, 0)),
                  full(RW_WIDTH), full(RW_WIDTH),
                  pl.BlockSpec((1, RW_WIDTH), lambda i: (0, 0)), pl.BlockSpec((1, RW_WIDTH), lambda i: (0, 0)),
                  pl.BlockSpec((RW_WIDTH, RW_WIDTH), lambda i: (0, 0))],
        out_specs=full(RW_WIDTH),
        scratch_shapes=[pltpu.VMEM((batch * RW_HEADS, RW_DH, RW_DH), F32)],
        compiler_params=_cp(("arbitrary",), VMEM_LIMIT),
        name="rwkv_scan",
    )(qy, mn, g, bonus, p["gn_w"], p["gn_b"], head_ones).reshape(batch * seq, RW_WIDTH)


def _gelu_tanh(x):
    return 0.5 * x * (1.0 + jnp.tanh(math.sqrt(2.0 / math.pi) * (x + 0.044715 * (x * x * x))))


def _compress_kernel(x_ref, pos_ref, w1_ref, w2_ref, o_ref):
    half = CMP_STRIDE * NSA_DH
    x = x_ref[0, 0, 0]
    w1 = w1_ref[0]
    posb = _dot(pos_ref[0], w1)[0:1, :]
    h1 = _dot(x, w1[:half])
    h2 = _dot(x, w1[half:])
    n = h2.shape[0]
    row = lax.broadcasted_iota(jnp.int32, h2.shape, 0)
    h2s = jnp.where(row < n - 1, pltpu.roll(h2, n - 1, 0), 0.0)
    hid = _gelu_tanh(h1 + h2s + posb)
    o_ref[0, 0, 0] = _dot(hid.astype(BF16), w2_ref[0]).astype(o_ref.dtype)


def _compress(xkv, pos, w1, w2, batch, seq):
    nr = seq // CMP_STRIDE
    wide = CMP_STRIDE * NSA_DH
    return pl.pallas_call(
        _compress_kernel,
        out_shape=jax.ShapeDtypeStruct((2, batch, NSA_G, nr, NSA_DH), BF16),
        grid=(2, batch, NSA_G),
        in_specs=[pl.BlockSpec((1, 1, 1, nr, wide), lambda t, b, g: (t, b, g, 0, 0)),
                  pl.BlockSpec((1, 8, 2 * wide), lambda t, b, g: (t, 0, 0)),
                  pl.BlockSpec((1, 2 * wide, CMP_HIDDEN), lambda t, b, g: (t, 0, 0)),
                  pl.BlockSpec((1, CMP_HIDDEN, NSA_DH), lambda t, b, g: (t, 0, 0))],
        out_specs=pl.BlockSpec((1, 1, 1, nr, NSA_DH), lambda t, b, g: (t, b, g, 0, 0)),
        compiler_params=_cp(("parallel", "parallel", "parallel")),
        name="nsa_compress",
    )(xkv, pos, w1, w2)


def _softmax_parts(s):
    m = jnp.max(s, axis=1, keepdims=True)
    e = jnp.exp(s - m)
    return m, e, jnp.sum(e, axis=1, keepdims=True)


def _lane_tile_fold(x, op, init):
    for t in range(x.shape[1] // 128):
        init = op(init, x[:, 128 * t:128 * (t + 1)])
    return init


def _nsa_kernel(q_ref, gate_ref, kc_ref, vc_ref, ks_ref, vs_ref, kw_ref, vw_ref,
                tblc_ref, tbln_ref, tblw_ref, ovt_ref, gexp_ref, o_ref, s_ref, *, ncmp):
    nb = NSA_NB
    blocks = [pl.program_id(2) * nb + u for u in range(nb)]
    rows = NSA_HG * QB
    each = lambda f, *ls: [f(*xs) for xs in zip(*ls)]
    qt = q_ref[...]
    qs = [jnp.concatenate([qt[u * QB:(u + 1) * QB, NSA_DH * h:NSA_DH * (h + 1)] for h in range(NSA_HG)], axis=0)
          * (NSA_DH ** -0.5 * LOG2E) for u in range(nb)]
    qb = each(lambda x: x.astype(BF16), qs)
    rmax = lambda x: jnp.max(x, axis=1, keepdims=True)
    rsum = lambda x: jnp.sum(x, axis=1, keepdims=True)

    win_w = WINDOW + QB
    kw = [kw_ref[0, 0, pl.ds(pl.multiple_of(i * QB, QB), win_w), :] for i in blocks]
    vw = [vw_ref[0, 0, pl.ds(pl.multiple_of(i * QB, QB), win_w), :] for i in blocks]
    flag = lax.broadcasted_iota(jnp.int32, (rows, kw_ref.shape[3] - NSA_DH), 1) == 0
    q_win = each(lambda x: jnp.concatenate([x, jnp.where(flag, NEG, 0.0)], axis=1).astype(BF16), qs)
    tblw = tblw_ref[0]
    s_w = each(lambda x, k: _dot_nt(x, k) + tblw, q_win, kw)
    m_w = each(rmax, s_w)
    e_w = each(lambda s, m: jnp.exp2(s - m), s_w, m_w)
    den_w = each(rsum, e_w)
    o_w = each(lambda e, v, d: _dot(e.astype(BF16), v) / d, e_w, vw, den_w)

    kc = kc_ref[0, 0]
    vc = vc_ref[0, 0]
    tblc = tblc_ref[0]
    cidx = lax.broadcasted_iota(jnp.int32, (rows, ncmp), 1)
    qrow = lax.broadcasted_iota(jnp.int32, (rows, 1), 0) % QB
    lc = [jnp.where(cidx < (QB // CMP_STRIDE) * i + CMP_AHEAD,
                    _dot_nt(x, kc) + pltpu.roll(tblc, (4 * i - CMP_NEAR // 2 + ncmp) % ncmp, 1), NEG)
          for x, i in zip(qb, blocks)]
    m_c = each(rmax, lc)
    e_c = each(lambda s, m: jnp.exp2(s - m), lc, m_c)
    den_c = each(rsum, e_c)
    pc = [e * jnp.where(i * QB + qrow >= CMP_BLOCK - 1, 1.0 / d, 0.0) for e, d, i in zip(e_c, den_c, blocks)]
    o_c = each(lambda x: _dot(x.astype(BF16), vc), pc)
    pcs_hi, pcs_lo = _split(jnp.concatenate(
        each(lambda x: x[0:QB] + x[QB:2 * QB] + x[2 * QB:3 * QB] + x[3 * QB:4 * QB], pc), axis=0))
    ovt = ovt_ref[...]
    imp = _dot_nt(ovt, pcs_hi) + _dot_nt(ovt, pcs_lo)

    nslc = ovt.shape[0]
    nidx = lax.broadcasted_iota(jnp.int32, (nslc, nb * QB), 0)
    cur = blocks[0] + lax.broadcasted_iota(jnp.int32, (nslc, nb * QB), 1) // QB
    forced = (nidx == 0) | (nidx == cur) | (nidx == cur - 1)
    work = jnp.where(nidx <= cur, imp + jnp.where(forced, FORCE_BONUS, 0.0), -1.0)
    sel_all = jnp.zeros((nslc, nb * QB), F32)
    for _ in range(N_SELECT):
        m = jnp.max(work, axis=0, keepdims=True)
        first = jnp.min(jnp.where(work == m, nidx, nslc), axis=0, keepdims=True)
        pick = nidx == first
        sel_all = jnp.where(pick & (m >= 0.0), 1.0, sel_all)
        work = jnp.where(pick, -2.0, work)
    sel_t = sel_all.T

    ind_w = ks_ref.shape[3] - NSA_DH
    selq = [sel_t[u * QB:(u + 1) * QB] for u in range(nb)]
    if nslc < ind_w:
        selq = each(lambda x: jnp.concatenate([x, jnp.zeros((QB, ind_w - nslc), F32)], axis=1), selq)
    bidx = lax.broadcasted_iota(jnp.int32, (QB, ind_w), 1)
    tile4 = lambda t: jnp.concatenate([t] * NSA_HG, axis=0)
    with_mask = lambda x, keep: jnp.concatenate([x, tile4(jnp.where(keep, 0.0, NEG))], axis=1).astype(BF16)
    q_sel = each(lambda x, s: with_mask(x, s > 0.0), qs, selq)
    q_far = jnp.concatenate([with_mask(x, (s > 0.0) & (bidx <= i - NEAR_BLOCKS))
                             for x, s, i in zip(qs, selq, blocks)], axis=0)
    near_w = NEAR_BLOCKS * SLC_BLOCK
    kn = [ks_ref[0, 0, pl.ds(pl.multiple_of(i * QB, QB), near_w), :] for i in blocks]
    vn = [vs_ref[0, 0, pl.ds(pl.multiple_of(i * QB, QB), near_w), :] for i in blocks]
    tbln = tbln_ref[0]
    s_near = each(lambda x, k: _dot_nt(x, k) + tbln, q_sel, kn)
    m_near = jnp.concatenate(each(rmax, s_near), axis=0)

    far_w = FAR_CHUNK_BLOCKS * SLC_BLOCK
    pad_s = (NEAR_BLOCKS - 1) * SLC_BLOCK
    n_far = jnp.maximum(blocks[-1] - (NEAR_BLOCKS - 1) + FAR_CHUNK_BLOCKS - 1, 0) // FAR_CHUNK_BLOCKS

    def far_logits(j, mvec):
        start = pl.multiple_of(j * far_w, far_w)
        kf = ks_ref[0, 0, pl.ds(pl.multiple_of(pad_s + j * far_w, SLC_BLOCK), far_w), :]
        s = _dot_nt(q_far, kf)
        s_ref[:, pl.ds(start, far_w)] = s
        return _lane_tile_fold(s, jnp.maximum, mvec)

    mvec = lax.fori_loop(0, n_far, far_logits, jnp.full((nb * rows, 128), NEG, F32))
    m_s = jnp.maximum(m_near, rmax(mvec))

    e_near = [jnp.exp2(s - m_s[u * rows:(u + 1) * rows]) for u, s in enumerate(s_near)]
    acc0 = jnp.concatenate(each(lambda e, v: _dot(e.astype(BF16), v), e_near, vn), axis=0)

    def far_values(j, carry):
        lvec, acc = carry
        start = pl.multiple_of(j * far_w, far_w)
        vf = vs_ref[0, 0, pl.ds(pl.multiple_of(pad_s + j * far_w, SLC_BLOCK), far_w), :]
        e = jnp.exp2(s_ref[:, pl.ds(start, far_w)] - m_s)
        return _lane_tile_fold(e, jnp.add, lvec), acc + _dot(e.astype(BF16), vf)

    lvec, acc = lax.fori_loop(0, n_far, far_values, (jnp.zeros((nb * rows, 128), F32), acc0))
    o_s = acc / (jnp.concatenate(each(rsum, e_near), axis=0) + rsum(lvec))

    g_hi, g_lo = _split(jax.nn.sigmoid(gate_ref[0, 0]))
    gexp = gexp_ref[...]
    ge = _dot(g_hi, gexp) + _dot(g_lo, gexp)
    for u in range(nb):
        outs = []
        for h in range(NSA_HG):
            r0 = slice(h * QB, (h + 1) * QB)
            gt = lambda br: ge[u * QB:(u + 1) * QB, (3 * h + br) * 128:(3 * h + br) * 128 + NSA_DH]
            outs.append(gt(0) * o_c[u][r0] + gt(1) * o_s[u * rows + h * QB:u * rows + (h + 1) * QB]
                        + gt(2) * o_w[u][r0])
        o_ref[u * QB:(u + 1) * QB, :] = jnp.concatenate(outs, axis=1).astype(o_ref.dtype)


def _bias_tables(rel_bias, ncmp):
    def table(dist, keep, base):
        onehot = jnp.asarray(np.eye(N_BUCKETS, dtype=np.float32)[_t5_bucket_np(dist)])
        tbl = jnp.einsum("qkb,bh->qkh", onehot, rel_bias, precision=HI)
        tbl = (tbl - base) * LOG2E
        tbl = jnp.where(jnp.asarray(keep)[..., None], tbl, NEG)
        k = dist.shape[1]
        return tbl.transpose(2, 0, 1).reshape(NSA_G, NSA_HG * QB, k)

    far = rel_bias[N_BUCKETS - 1]
    qi = np.arange(QB)[:, None]
    dist_c = qi - CMP_STRIDE * (np.arange(CMP_NEAR)[None, :] - CMP_NEAR // 2) - (CMP_BLOCK - 1)
    tblc = table(dist_c, dist_c >= 0, far)
    tblc = tblc * jnp.asarray(np.arange(CMP_NEAR) < CMP_NEAR // 2 + CMP_AHEAD, F32)
    tblc = jnp.pad(tblc, ((0, 0), (0, 0), (0, ncmp - CMP_NEAR)))
    jn = np.arange(NEAR_BLOCKS * SLC_BLOCK)[None, :]
    dist_n = (NEAR_BLOCKS - 1) * SLC_BLOCK + qi - jn
    tbln = table(dist_n, dist_n >= 0, far)
    jw = np.arange(WINDOW + QB)[None, :]
    dist_w = WINDOW + qi - jw
    tblw = table(dist_w, (dist_w >= 0) & (dist_w < WINDOW), 0.0)
    return tblc, tbln, tblw


def _nsa(proj, gate_logits, kvc, ks, vs, kw, vw, tables, batch, seq):
    nq = seq // QB
    ncmp = seq // CMP_STRIDE
    nslc = seq // SLC_BLOCK
    tblc, tbln, tblw = tables
    cstart = np.arange(ncmp) * CMP_STRIDE
    sstart = np.arange(nslc) * SLC_BLOCK
    overlap_t = ((cstart[None, :] <= sstart[:, None] + SLC_BLOCK - 1)
                 & (cstart[None, :] + CMP_BLOCK - 1 >= sstart[:, None])
                 & (cstart[None, :] + CMP_BLOCK <= seq)).astype(np.float32)
    pad_s = ks.shape[2] - seq
    blk = (np.arange(seq)[:, None] // SLC_BLOCK == np.arange(NSA_DH)[None, :]).astype(np.float32)
    blk = np.concatenate([np.ones((pad_s, NSA_DH), np.float32), blk], axis=0)
    ks = jnp.concatenate([ks, jnp.broadcast_to(jnp.asarray(blk, BF16), ks.shape[:2] + blk.shape)], axis=-1)
    flag = np.zeros((kw.shape[2], NSA_DH), np.float32)
    flag[:kw.shape[2] - seq, 0] = 1.0
    kw = jnp.concatenate([kw, jnp.broadcast_to(jnp.asarray(flag, BF16), kw.shape[:2] + flag.shape)], axis=-1)
    rows = NSA_HG * QB
    n_gate = 3 * NSA_HG
    gate_spread = (np.arange(n_gate * 128)[None, :] // 128 == np.arange(n_gate)[:, None]) \
        & (np.arange(n_gate * 128)[None, :] % 128 < NSA_DH)
    kv_spec = lambda t: pl.BlockSpec((1, 1) + t.shape[2:], lambda b, g, i: (b, g, 0, 0))
    tbl_spec = lambda k: pl.BlockSpec((1, rows, k), lambda b, g, i: (g, 0, 0))
    qcol = C_Q // (NSA_HG * NSA_DH)
    return pl.pallas_call(
        functools.partial(_nsa_kernel, ncmp=ncmp),
        out_shape=jax.ShapeDtypeStruct((batch * seq, NSA_HEADS * NSA_DH), BF16),
        grid=(batch, NSA_G, nq // NSA_NB),
        in_specs=[pl.BlockSpec((NSA_NB * QB, NSA_HG * NSA_DH), lambda b, g, i: (b * (nq // NSA_NB) + i, qcol + g)),
                  pl.BlockSpec((1, 1, NSA_NB * QB, 3 * NSA_HG), lambda b, g, i: (b, g, i, 0)),
                  pl.BlockSpec((1, 1, ncmp, NSA_DH), lambda b, g, i: (b, g, 0, 0)),
                  pl.BlockSpec((1, 1, ncmp, NSA_DH), lambda b, g, i: (b, g, 0, 0)),
                  kv_spec(ks), kv_spec(vs), kv_spec(kw), kv_spec(vw),
                  tbl_spec(ncmp), tbl_spec(NEAR_BLOCKS * SLC_BLOCK), tbl_spec(WINDOW + QB),
                  pl.BlockSpec((nslc, ncmp), lambda b, g, i: (0, 0)),
                  pl.BlockSpec(gate_spread.shape, lambda b, g, i: (0, 0))],
        out_specs=pl.BlockSpec((NSA_NB * QB, NSA_HG * NSA_DH), lambda b, g, i: (b * (nq // NSA_NB) + i, g)),
        scratch_shapes=[pltpu.VMEM((NSA_NB * rows, seq), F32)],
        compiler_params=_cp(("parallel", "parallel", "arbitrary"), VMEM_LIMIT),
        name="nsa_attention",
    )(proj, gate_logits, kvc[0], kvc[1], ks, vs, kw, vw, tblc, tbln, tblw, jnp.asarray(overlap_t, BF16),
      jnp.asarray(gate_spread, BF16))


def _merge_kernel(h_ref, yp_ref, yr_ref, yn_ref, x_ref, wbp_ref, wbr_ref, wbn_ref, wm_ref, bm_ref, wo_ref,
                  gn_ref, wr_ref, br_ref, xo_ref, h2_ref, route_ref):
    d = D_MODEL
    gl = jax.nn.sigmoid(_dot(h_ref[...], wm_ref[...]) + bm_ref[...])
    merged = (gl[:, :d] * _dot(yp_ref[...], wbp_ref[...]) + gl[:, d:2 * d] * _dot(yr_ref[...], wbr_ref[...])
              + gl[:, 2 * d:] * _dot(yn_ref[...], wbn_ref[...]))
    x = x_ref[...] + _dot(merged.astype(BF16), wo_ref[...])
    xo_ref[...] = x
    h2 = _rms(x, gn_ref[...])
    h2_ref[...] = h2.astype(h2_ref.dtype)

    h2_hi, h2_lo = _split(h2)
    nl = br_ref.shape[1]
    both = _dot(h2_hi, wr_ref[...])
    logits = both[:, :nl] + (both[:, nl:] + _dot(h2_lo, wr_ref[:, :nl])) + br_ref[...]
    lane = lax.broadcasted_iota(jnp.int32, logits.shape, 1)
    big = logits.shape[1]
    lg = jnp.where(lane < N_GROUPS, logits, NEG)
    mg = jnp.max(lg, axis=1, keepdims=True)
    p_top = 1.0 / jnp.sum(jnp.exp(lg - mg), axis=1, keepdims=True)
    grp = jnp.min(jnp.where(lg == mg, lane, big), axis=1, keepdims=True)
    lo = N_GROUPS + EPG * grp
    le = jnp.where((lane >= lo) & (lane < lo + EPG), logits, NEG)
    e1 = jnp.max(le, axis=1, keepdims=True)
    i1 = jnp.min(jnp.where(le == e1, lane, big), axis=1, keepdims=True)
    le = jnp.where(lane == i1, NEG, le)
    e2 = jnp.max(le, axis=1, keepdims=True)
    i2 = jnp.min(jnp.where(le == e2, lane, big), axis=1, keepdims=True)
    t = jnp.exp(e2 - e1)
    w1 = p_top / (1.0 + t)
    w2 = p_top * t / (1.0 + t)
    route_ref[...] = jnp.where(lane == 0, (i1 - N_GROUPS).astype(F32),
                               jnp.where(lane == 1, (i2 - N_GROUPS).astype(F32),
                                         jnp.where(lane == 2, w1, jnp.where(lane == 3, w2, 0.0))))


def _merge(h, y_pool, y_rwkv, y_nsa, x, p, tm=256):
    n, d = x.shape
    row = lambda w: pl.BlockSpec((tm, w), lambda i: (i, 0))
    const = lambda a: pl.BlockSpec(a.shape, lambda i: (0, 0))
    ws = [p["wb_pool"], p["wb_rwkv"], p["wb_nsa"], p["w_merge"], p["b_merge"], p["w_out"],
          p["norm_ffn"], p["w_router"], p["b_router"]]
    return pl.pallas_call(
        _merge_kernel,
        out_shape=(jax.ShapeDtypeStruct((n, d), F32), jax.ShapeDtypeStruct((n, d), F32),
                   jax.ShapeDtypeStruct((n, 128), F32)),
        grid=(n // tm,),
        in_specs=[row(d), row(POOL_WIDTH), row(RW_WIDTH), row(d), row(d)] + [const(w) for w in ws],
        out_specs=(row(d), row(d), row(128)),
        compiler_params=_cp(("parallel",), VMEM_LIMIT),
        name="merge_router",
    )(h, y_pool, y_rwkv, y_nsa, x, *ws)


def _expert_kernel(te_ref, nt_ref, tok_ref, h_hbm, w_ref, wg_ref, wu_ref, wd_ref, o_ref,
                   xbuf, sem, wg_s, wu_s, wd_s):
    i = pl.program_id(0)
    tm = MOE_TM
    n_tiles = nt_ref[0]

    def row_copy(tile, slot, r):
        return pltpu.make_async_copy(h_hbm.at[pl.ds(tok_ref[tile * tm + r], 1), :],
                                     xbuf.at[slot, pl.ds(r, 1), :], sem.at[slot])

    def tile_wait(slot):
        pltpu.make_async_copy(h_hbm.at[pl.ds(0, tm), :], xbuf.at[slot], sem.at[slot]).wait()

    @pl.when((i == 0) & (n_tiles > 0))
    def _():
        def body(r, carry):
            row_copy(0, 0, r).start()
            return carry
        lax.fori_loop(0, tm, body, 0, unroll=8)

    @pl.when((i == 0) | (te_ref[i] != te_ref[jnp.maximum(i - 1, 0)]))
    def _():
        wg_s[...] = wg_ref[0].astype(BF16)
        wu_s[...] = wu_ref[0].astype(BF16)
        wd_s[...] = wd_ref[0].astype(BF16)

    @pl.when(i < n_tiles)
    def _():
        slot = i % 2
        quarter = tm // 4

        def fetch_next(part):
            for r in range(part * quarter, (part + 1) * quarter):
                row_copy(i + 1, 1 - slot, r).start()

        tile_wait(slot)
        xb = xbuf[slot].astype(BF16)
        fetch_next(0)
        gate = _dot(xb, wg_s[...])
        fetch_next(1)
        up = _dot(xb, wu_s[...])
        fetch_next(2)
        hid = (gate * jax.nn.sigmoid(gate) * up).astype(BF16)
        fetch_next(3)
        o_ref[...] = w_ref[...] * _dot(hid, wd_s[...])

    @pl.when((i == n_tiles) & (n_tiles > 0))
    def _():
        tile_wait(i % 2)

    @pl.when(i >= nt_ref[0])
    def _():
        o_ref[...] = jnp.zeros_like(o_ref)


def _experts(h2, rowtok, roww, tile_expert, n_tiles, wg, wu, wd):
    d = h2.shape[1]
    r = rowtok.shape[0]
    tm = MOE_TM
    return pl.pallas_call(
        _expert_kernel,
        out_shape=jax.ShapeDtypeStruct((r, d), F32),
        grid_spec=pltpu.PrefetchScalarGridSpec(
            num_scalar_prefetch=3,
            grid=(r // tm,),
            in_specs=[pl.BlockSpec(memory_space=pl.ANY),
                      pl.BlockSpec((tm, 1), lambda i, te, nt, tok: (i, 0)),
                      pl.BlockSpec((1, d, D_EXPERT), lambda i, te, nt, tok: (te[i], 0, 0)),
                      pl.BlockSpec((1, d, D_EXPERT), lambda i, te, nt, tok: (te[i], 0, 0)),
                      pl.BlockSpec((1, D_EXPERT, d), lambda i, te, nt, tok: (te[i], 0, 0))],
            out_specs=pl.BlockSpec((tm, d), lambda i, te, nt, tok: (i, 0)),
            scratch_shapes=[pltpu.VMEM((2, tm, d), F32), pltpu.SemaphoreType.DMA((2,)),
                            pltpu.VMEM((d, D_EXPERT), BF16), pltpu.VMEM((d, D_EXPERT), BF16),
                            pltpu.VMEM((D_EXPERT, d), BF16)]),
        compiler_params=_cp(("arbitrary",), VMEM_LIMIT),
        name="moe_experts",
    )(tile_expert, n_tiles, rowtok, h2, roww, wg, wu, wd)


def _moe(h2, route, wg, wu, wd):
    n = h2.shape[0]
    tm = MOE_TM
    r = 2 * n + (N_EXPERTS + 1) * tm
    ids = route[:, 0:2].astype(jnp.int32).reshape(-1)
    wts = route[:, 2:4].reshape(-1)
    onehot = (ids[:, None] == jnp.arange(N_EXPERTS)[None, :]).astype(jnp.int32)
    rank = jnp.sum((jnp.cumsum(onehot, axis=0) - onehot) * onehot, axis=1)
    counts = jnp.sum(onehot, axis=0)
    tiles = (counts + tm - 1) // tm
    tile_end = jnp.cumsum(tiles)
    starts = (tile_end - tiles) * tm
    pos = starts[ids] + rank
    row_assign = jnp.full((r,), -1, jnp.int32).at[pos].set(jnp.arange(2 * n, dtype=jnp.int32))
    rowtok = jnp.maximum(row_assign, 0) // 2
    roww = jnp.where(row_assign >= 0, wts[jnp.maximum(row_assign, 0)], 0.0)
    n_tiles = tile_end[-1:].astype(jnp.int32)
    tile_expert = jnp.minimum(jnp.sum(tile_end[None, :] <= jnp.arange(r // tm)[:, None], axis=1),
                              N_EXPERTS - 1).astype(jnp.int32)
    ys = _experts(h2, rowtok, roww.reshape(r, 1), tile_expert, n_tiles, wg, wu, wd)
    return ys[pos[0::2]], ys[pos[1::2]]


def _layer_params(l, a):
    f = lambda t: t[l]
    row = lambda t: t[l].reshape(1, -1)
    w_in = a["w_in"][l]
    w_in_p = jnp.concatenate([w_in[:, :SRC_RW_END], jnp.zeros((D_MODEL, C_Q - SRC_RW_END), F32),
                              w_in[:, SRC_Q:SRC_KV], w_in[:, SRC_GATE:],
                              jnp.zeros((D_MODEL, P_COLS - C_GATE - (w_in.shape[1] - SRC_GATE)), F32)], axis=1)
    mu = a["rw_mu"][l]
    wb = a["w_branch"][l].astype(BF16)
    w_router = jnp.zeros((D_MODEL, 128), F32)
    w_router = w_router.at[:, :N_GROUPS].set(a["w_router_grp"][l]).at[:, N_GROUPS:N_GROUPS + N_EXPERTS].set(
        a["w_router_exp"][l])
    b_router = jnp.zeros((1, 128), F32)
    b_router = b_router.at[0, :N_GROUPS].set(a["b_router_grp"][l]).at[0, N_GROUPS:N_GROUPS + N_EXPERTS].set(
        a["b_router_exp"][l])
    pos = jnp.stack([a["cmp_pos_k"][l].reshape(-1), a["cmp_pos_v"][l].reshape(-1)])
    return {
        "w_in": w_in_p.astype(BF16), "w_kv": w_in[:, SRC_KV:SRC_GATE].astype(BF16),
        "pool_w": f(a["pool_w"]), "pool_scale": f(a["pool_scale"]),
        "mu_r": mu[None, 0:RW_WIDTH], "mu_k": mu[None, RW_WIDTH:2 * RW_WIDTH],
        "mu_v": mu[None, 2 * RW_WIDTH:3 * RW_WIDTH],
        "mu_l": jnp.concatenate([mu[3 * RW_WIDTH:], jnp.zeros((LORA_PAD - RW_LORA,), F32)])[None],
        "w0": row(a["rw_w0"]), "w_up": f(a["rw_w_up"]), "a0": row(a["rw_a0"]), "a_up": f(a["rw_a_up"]),
        "g_up": f(a["rw_g_up"]), "k_k": row(a["rw_k_k"]), "k_a": row(a["rw_k_a"]), "r_k": row(a["rw_r_k"]),
        "gn_w": row(a["rw_gn_w"]), "gn_b": row(a["rw_gn_b"]),
        "cmp_pos": jnp.broadcast_to(pos[:, None, :], (2, 8, pos.shape[1])).astype(BF16),
        "cmp_w1": jnp.stack([a["cmp_w1_k"][l], a["cmp_w1_v"][l]]).astype(BF16),
        "cmp_w2": jnp.stack([a["cmp_w2_k"][l], a["cmp_w2_v"][l]]).astype(BF16),
        "wb_pool": wb[:POOL_WIDTH], "wb_rwkv": wb[POOL_WIDTH:POOL_WIDTH + RW_WIDTH],
        "wb_nsa": wb[POOL_WIDTH + RW_WIDTH:],
        "w_merge": a["w_merge"][l].astype(BF16), "b_merge": row(a["b_merge"]),
        "w_out": a["w_out"][l].astype(BF16), "norm_ffn": row(a["norm_ffn"]),
        "w_router": jnp.concatenate(_split(w_router), axis=1), "b_router": b_router,
        "w_gate": a["w_exp_gate"][l], "w_up_e": a["w_exp_up"][l], "w_down": a["w_exp_down"][l],
    }


def _mixers(proj, kv, p, tables, batch, seq):
    y_pool = _pool(proj, p["pool_w"], p["pool_scale"], batch, seq)
    y_rwkv = _rwkv(proj, p, batch, seq)
    xkv = kv[0:2].reshape(2, batch, NSA_G, seq // CMP_STRIDE, CMP_STRIDE * NSA_DH)
    kvc = _compress(xkv, p["cmp_pos"], p["cmp_w1"], p["cmp_w2"], batch, seq)
    pad = lambda t, n: jnp.pad(t, ((0, 0), (0, 0), (n, 0), (0, 0)))
    pad_s = (NEAR_BLOCKS - 1) * SLC_BLOCK
    gate_logits = proj[:, C_GATE:C_GATE + 3 * NSA_HEADS].reshape(batch, seq, NSA_G, 3 * NSA_HG).transpose(0, 2, 1, 3)
    y_nsa = _nsa(proj, gate_logits, kvc, pad(kv[2], pad_s), pad(kv[3], pad_s), pad(kv[4], WINDOW),
                 pad(kv[5], WINDOW), tables, batch, seq)
    return y_pool, y_rwkv, y_nsa


def kernel(x, rel_bias, norm_mix, w_in, pool_w, pool_scale, rw_mu, rw_w0, rw_w_up, rw_a0, rw_a_up, rw_g_up, rw_k_k, rw_k_a, rw_r_k, rw_gn_w, rw_gn_b, cmp_pos_k, cmp_w1_k, cmp_w2_k, cmp_pos_v, cmp_w1_v, cmp_w2_v, w_branch, w_merge, b_merge, w_out, norm_ffn, w_router_grp, b_router_grp, w_router_exp, b_router_exp, w_exp_gate, w_exp_up, w_exp_down, norm_final):
    a = dict(w_in=w_in, pool_w=pool_w, pool_scale=pool_scale, rw_mu=rw_mu, rw_w0=rw_w0, rw_w_up=rw_w_up,
             rw_a0=rw_a0, rw_a_up=rw_a_up, rw_g_up=rw_g_up, rw_k_k=rw_k_k, rw_k_a=rw_k_a, rw_r_k=rw_r_k,
             rw_gn_w=rw_gn_w, rw_gn_b=rw_gn_b, cmp_pos_k=cmp_pos_k, cmp_w1_k=cmp_w1_k, cmp_w2_k=cmp_w2_k,
             cmp_pos_v=cmp_pos_v, cmp_w1_v=cmp_w1_v, cmp_w2_v=cmp_w2_v, w_branch=w_branch, w_merge=w_merge,
             b_merge=b_merge, w_out=w_out, norm_ffn=norm_ffn, w_router_grp=w_router_grp,
             b_router_grp=b_router_grp, w_router_exp=w_router_exp, b_router_exp=b_router_exp,
             w_exp_gate=w_exp_gate, w_exp_up=w_exp_up, w_exp_down=w_exp_down)
    batch, seq, d = x.shape
    depth = norm_mix.shape[0]
    tables = _bias_tables(rel_bias, seq // CMP_STRIDE)
    xf = x.reshape(batch * seq, d)
    h = _norm(xf, norm_mix[0], BF16)
    for l in range(depth):
        p = _layer_params(l, a)
        proj = _matmul(h, p["w_in"])
        kv = _kv_proj(h, p["w_kv"], batch, seq)
        y_pool, y_rwkv, y_nsa = _mixers(proj, kv, p, tables, batch, seq)
        xf, h2, route = _merge(h, y_pool, y_rwkv, y_nsa, xf, p)
        y1, y2 = _moe(h2, route, p["w_gate"], p["w_up_e"], p["w_down"])
        last = l == depth - 1
        g_next = norm_final if last else norm_mix[l + 1]
        xf, h = _add_norm(xf, y1, y2, g_next, F32 if last else BF16)
    return h.reshape(batch, seq, d)
```

```python
import functools
import math

import jax
import jax.numpy as jnp
import numpy as np
from jax import lax
from jax.experimental import pallas as pl
from jax.experimental.pallas import tpu as pltpu

F32 = jnp.float32
BF16 = jnp.bfloat16
HI = lax.Precision.HIGHEST

D_MODEL = 1024
RMS_EPS = 1e-6
NEG = -1e30
LOG2E = math.log2(math.e)

POOL_WINDOWS = (2, 4, 8, 16)
POOL_WIDTH = 512
POOL_GW = 128
POOL_HALO = 16

RW_HEADS = 8
RW_DH = 64
RW_WIDTH = 512
DECAY_LORA, AAA_LORA, GATE_LORA = 32, 32, 96
RW_LORA = DECAY_LORA + AAA_LORA + GATE_LORA
RW_COLS = 3 * RW_WIDTH + RW_LORA
RW_GN_EPS = 64e-5
RW_CHUNK = 64
RW_NB = 2

NSA_DH = 64
NSA_HEADS = 16
NSA_G = 4
NSA_HG = 4
NSA_KVW = NSA_G * NSA_DH
CMP_BLOCK, CMP_STRIDE, CMP_HIDDEN = 32, 16, 256
SLC_BLOCK = 64
N_SELECT = 8
WINDOW = 512
QB = 64
NSA_NB = 8
FORCE_BONUS = 1e3
N_BUCKETS, MAX_EXACT, MAX_DISTANCE = 32, 16, 128
NEAR_BLOCKS = 3
FAR_CHUNK_BLOCKS = 8
CMP_NEAR = 32
CMP_AHEAD = (QB - CMP_BLOCK) // CMP_STRIDE + 1

N_GROUPS, EPG, N_EXPERTS, D_EXPERT = 4, 8, 32, 256
MOE_TM = 256

C_POOL, C_R, C_K, C_V, C_LORA, C_Q, C_GATE, P_COLS = 0, 512, 1024, 1536, 2048, 2304, 3328, 3584
SRC_Q, SRC_KV, SRC_GATE = 2208, 3232, 4768
N_KV = 6
LORA_PAD = 256
SRC_RW_END = POOL_WIDTH + RW_COLS

VMEM_LIMIT = 56 * 1024 * 1024


def _t5_bucket_np(dist):
    n = np.maximum(dist, 0)
    nf = np.maximum(n, 1).astype(np.float32)
    large = MAX_EXACT + (np.log(nf / MAX_EXACT) / math.log(MAX_DISTANCE / MAX_EXACT)
                         * (N_BUCKETS - MAX_EXACT)).astype(np.int32)
    large = np.minimum(large, N_BUCKETS - 1)
    return np.where(n < MAX_EXACT, n, large)


def _cp(sem, vmem=None):
    return pltpu.CompilerParams(dimension_semantics=sem, vmem_limit_bytes=vmem)


def _dot(a, b, precision=None):
    return jnp.dot(a, b, preferred_element_type=F32, precision=precision)


def _dot_nt(a, b, precision=None):
    return lax.dot_general(a, b, (((1,), (1,)), ((), ())), preferred_element_type=F32, precision=precision)


def _dot_tn(a, b, precision=None):
    return lax.dot_general(a, b, (((0,), (0,)), ((), ())), preferred_element_type=F32, precision=precision)


def _bdot(a, b):
    return _dot(a.astype(BF16), b.astype(BF16))


def _bdot_nt(a, b):
    return _dot_nt(a.astype(BF16), b.astype(BF16))


def _bdot_tn(a, b):
    return _dot_tn(a.astype(BF16), b.astype(BF16))


def _split(a):
    hi = a.astype(BF16)
    return hi, (a - hi.astype(F32)).astype(BF16)


def _dot3(a, b):
    ah, al = _split(a)
    bh, bl = _split(b)
    return _dot(ah, bh) + (_dot(ah, bl) + _dot(al, bh))


def _rms(x, g):
    return x * lax.rsqrt(jnp.mean(x * x, axis=-1, keepdims=True) + RMS_EPS) * g


def _norm_kernel(x_ref, g_ref, h_ref):
    h_ref[...] = _rms(x_ref[...], g_ref[...]).astype(h_ref.dtype)


def _norm(x, g, out_dtype, tm=512):
    n, d = x.shape
    return pl.pallas_call(
        _norm_kernel,
        out_shape=jax.ShapeDtypeStruct((n, d), out_dtype),
        grid=(n // tm,),
        in_specs=[pl.BlockSpec((tm, d), lambda i: (i, 0)), pl.BlockSpec((1, d), lambda i: (0, 0))],
        out_specs=pl.BlockSpec((tm, d), lambda i: (i, 0)),
        compiler_params=_cp(("parallel",)),
        name="rms_norm",
    )(x, g.reshape(1, d))


def _add_norm_kernel(x_ref, y1_ref, y2_ref, g_ref, xo_ref, h_ref):
    x = x_ref[...] + (y1_ref[...] + y2_ref[...])
    xo_ref[...] = x
    h_ref[...] = _rms(x, g_ref[...]).astype(h_ref.dtype)


def _add_norm(x, y1, y2, g, out_dtype, tm=512):
    n, d = x.shape
    row = pl.BlockSpec((tm, d), lambda i: (i, 0))
    return pl.pallas_call(
        _add_norm_kernel,
        out_shape=(jax.ShapeDtypeStruct((n, d), F32), jax.ShapeDtypeStruct((n, d), out_dtype)),
        grid=(n // tm,),
        in_specs=[row, row, row, pl.BlockSpec((1, d), lambda i: (0, 0))],
        out_specs=(row, row),
        compiler_params=_cp(("parallel",)),
        name="moe_combine_norm",
    )(x, y1, y2, g.reshape(1, d))


def _matmul_kernel(x_ref, w_ref, o_ref):
    o_ref[...] = _dot(x_ref[...], w_ref[...]).astype(o_ref.dtype)


def _matmul(x, w, tm=512, tn=P_COLS // 2):
    m, k = x.shape
    n = w.shape[1]
    return pl.pallas_call(
        _matmul_kernel,
        out_shape=jax.ShapeDtypeStruct((m, n), F32),
        grid=(n // tn, m // tm),
        in_specs=[pl.BlockSpec((tm, k), lambda j, i: (i, 0)), pl.BlockSpec((k, tn), lambda j, i: (0, j))],
        out_specs=pl.BlockSpec((tm, tn), lambda j, i: (i, j)),
        compiler_params=_cp(("parallel", "parallel"), VMEM_LIMIT),
        name="in_proj",
    )(x, w)


def _kv_proj_kernel(x_ref, w_ref, o_ref):
    res = _dot(x_ref[...], w_ref[...])
    for t in range(N_KV):
        for g in range(NSA_G):
            c0 = (t * NSA_G + g) * NSA_DH
            o_ref[t, 0, g] = res[:, c0:c0 + NSA_DH].astype(o_ref.dtype)


def _kv_proj(x, w, batch, seq, tm=512):
    k = x.shape[1]
    nt = seq // tm
    return pl.pallas_call(
        _kv_proj_kernel,
        out_shape=jax.ShapeDtypeStruct((N_KV, batch, NSA_G, seq, NSA_DH), BF16),
        grid=(batch, nt),
        in_specs=[pl.BlockSpec((tm, k), lambda b, i: (b * nt + i, 0)),
                  pl.BlockSpec(w.shape, lambda b, i: (0, 0))],
        out_specs=pl.BlockSpec((N_KV, 1, NSA_G, tm, NSA_DH), lambda b, i: (0, b, 0, i, 0)),
        compiler_params=_cp(("parallel", "parallel"), VMEM_LIMIT),
        name="kv_proj",
    )(x, w)


def _pool_kernel(u_ref, halo_ref, w_ref, scale_ref, o_ref, buf_ref, *, tile):
    i = pl.program_id(1)
    u = u_ref[...]
    buf_ref[POOL_HALO:, :] = u
    buf_ref[:POOL_HALO, :] = jnp.where(i > 0, halo_ref[...], 0.0)
    t = i * tile + lax.broadcasted_iota(jnp.int32, (tile, 1), 0)
    outs = []
    for gi, win in enumerate(POOL_WINDOWS):
        cols = slice(gi * POOL_GW, (gi + 1) * POOL_GW)
        s = u[:, cols]
        for j in range(1, win):
            s = s + buf_ref[POOL_HALO - j:POOL_HALO - j + tile, cols]
        cnt = jnp.minimum(t + 1, win).astype(F32)
        pooled = s / cnt - u[:, cols]
        outs.append(_dot(pooled.astype(BF16), w_ref[gi]))
    o_ref[...] = (jnp.concatenate(outs, axis=1) * scale_ref[...]).astype(o_ref.dtype)


def _pool(proj, w_grp, scale, batch, seq, tile=512):
    nt = seq // tile
    hb = tile // POOL_HALO
    return pl.pallas_call(
        functools.partial(_pool_kernel, tile=tile),
        out_shape=jax.ShapeDtypeStruct((batch * seq, POOL_WIDTH), BF16),
        grid=(batch, nt),
        in_specs=[
            pl.BlockSpec((tile, POOL_WIDTH), lambda b, i: (b * nt + i, 0)),
            pl.BlockSpec((POOL_HALO, POOL_WIDTH), lambda b, i: (jnp.maximum((b * nt + i) * hb - 1, 0), 0)),
            pl.BlockSpec((len(POOL_WINDOWS), POOL_GW, POOL_GW), lambda b, i: (0, 0, 0)),
            pl.BlockSpec((1, POOL_WIDTH), lambda b, i: (0, 0)),
        ],
        out_specs=pl.BlockSpec((tile, POOL_WIDTH), lambda b, i: (b * nt + i, 0)),
        scratch_shapes=[pltpu.VMEM((tile + POOL_HALO, POOL_WIDTH), F32)],
        compiler_params=_cp(("parallel", "parallel")),
        name="pool_mixer",
    )(proj, proj, w_grp.astype(BF16), scale.reshape(1, POOL_WIDTH))


def _token_shift(u, halo, mu, first):
    prev_row = jnp.where(first, 0.0, halo[7:8, :])
    rolled = pltpu.roll(u, 1, 0)
    row = lax.broadcasted_iota(jnp.int32, u.shape, 0)
    prev = jnp.where(row == 0, prev_row, rolled)
    return u + (prev - u) * mu


def _rwkv_chunk_kernel(r_ref, k_ref, v_ref, l_ref, rh_ref, kh_ref, vh_ref, lh_ref,
                       mur_ref, muk_ref, muv_ref, mul_ref, w0_ref, wup_ref, a0_ref, aup_ref, gup_ref,
                       kk_ref, ka_ref, rk_ref, bd_ref, qy_ref, mn_ref, g_ref, bonus_ref):
    first = pl.program_id(1) == 0
    c = RW_CHUNK
    r = _token_shift(r_ref[...], rh_ref[...], mur_ref[...], first)
    k = _token_shift(k_ref[...], kh_ref[...], muk_ref[...], first)
    v = _token_shift(v_ref[...], vh_ref[...], muv_ref[...], first)
    lo = _token_shift(l_ref[...], lh_ref[...], mul_ref[...], first)
    wd = lo[:, :DECAY_LORA]
    ad = lo[:, DECAY_LORA:DECAY_LORA + AAA_LORA]
    gd = lo[:, DECAY_LORA + AAA_LORA:RW_LORA]
    z = -(w0_ref[...] + _dot(jnp.tanh(wd), wup_ref[...], HI))
    w_log = -(jnp.maximum(z, 0.0) + jnp.log(1.0 + jnp.exp(-jnp.abs(z)))) - 0.5
    logw = -jnp.exp(w_log)
    a = jax.nn.sigmoid(a0_ref[...] + _dot(ad, aup_ref[...], HI))
    g_ref[0] = _dot(jax.nn.sigmoid(gd), gup_ref[...], HI)
    kkraw = k * kk_ref[...]
    k2 = k * (1.0 + (a - 1.0) * ka_ref[...])
    rkr = r * k2 * rk_ref[...]

    cum_all = logw
    trow = lax.broadcasted_iota(jnp.int32, logw.shape, 0) % c
    step = 1
    while step < c:
        cum_all = cum_all + jnp.where(trow >= step, pltpu.roll(cum_all, step, 0), 0.0)
        step *= 2

    ti = lax.broadcasted_iota(jnp.int32, (c, c), 0)
    si = lax.broadcasted_iota(jnp.int32, (c, c), 1)
    incl = ti >= si
    strict = ti > si
    eye = ti == si
    zeros = jnp.zeros((c, c), F32)
    bd = bd_ref[...]

    def head_sum(t):
        hi, lo = _split(t)
        return _dot(hi, bd) + _dot(lo, bd)

    kk = kkraw / jnp.maximum(jnp.sqrt(head_sum(kkraw * kkraw)), 1e-12)
    bonus_ref[0] = head_sum(rkr) * v
    nchunk = logw.shape[0] // c
    ends = [cum_all[(j + 1) * c - 1:(j + 1) * c, :] for j in range(nchunk)]
    cum_end = jnp.concatenate([jnp.broadcast_to(e, (c, e.shape[1])) for e in ends], axis=0)
    ginv = jnp.exp(-cum_all)
    gtail = jnp.exp(cum_end - cum_all)
    gend = jnp.exp(cum_end)
    kka = kk * a
    at = -kk * jnp.exp(cum_all - logw)
    bt = kka * ginv
    kt = k2 * ginv
    rt = r * jnp.exp(cum_all)
    bhat = kka * gtail
    khat = k2 * gtail

    heads = [(slice(j * c, (j + 1) * c), slice(h * RW_DH, (h + 1) * RW_DH))
             for j in range(nchunk) for h in range(RW_HEADS)]
    stack = lambda x, y, s: jnp.concatenate([x[s], y[s]], axis=0).astype(BF16)
    gram = [_dot_nt(stack(at, rt, s), stack(bt, kt, s)) for s in heads]
    a_ab = [jnp.where(strict, g[:c, :c], 0.0) for g in gram]
    a_ak = [jnp.where(strict, g[:c, c:], 0.0) for g in gram]
    incl2 = (lax.broadcasted_iota(jnp.int32, (c, 2 * c), 0)
             >= lax.broadcasted_iota(jnp.int32, (c, 2 * c), 1) % c)
    a_r = [jnp.where(incl2, g[c:, :], 0.0) for g in gram]
    p = a_ab
    tinv = [eye.astype(F32) + x for x in p]
    for _ in range(int(math.log2(c)) - 1):
        p = [_bdot(x, x) for x in p]
        tinv = [t + _bdot(t, x) for t, x in zip(tinv, p)]
    av = [_bdot(x, v[s]) for x, s in zip(a_ak, heads)]
    w12 = [_bdot(t, jnp.concatenate([at[s], x], axis=1)) for t, x, s in zip(tinv, av, heads)]
    zmat = [jnp.concatenate([w, jnp.concatenate([zeros, v[s]], axis=1)], axis=0).astype(BF16)
            for w, s in zip(w12, heads)]
    out1 = [_dot(x.astype(BF16), z) for x, z in zip(a_r, zmat)]
    out2 = [_dot_tn(stack(bhat, khat, s), z) for s, z in zip(heads, zmat)]
    qy = [o + jnp.concatenate([rt[s], zeros], axis=1) for o, s in zip(out1, heads)]
    for j in range(nchunk):
        qy_ref[0, j * c:(j + 1) * c, :] = jnp.concatenate(qy[j * RW_HEADS:(j + 1) * RW_HEADS], axis=1)
        for h in range(RW_HEADS):
            idx = j * RW_HEADS + h
            diag = jnp.where(eye, gend[heads[idx]], 0.0)
            mn_ref[0, j, h] = out2[idx] + jnp.concatenate([diag, zeros], axis=1)


def _rwkv_scan_kernel(qy_ref, mn_ref, g_ref, bonus_ref, gnw_ref, gnb_ref, bd_ref, y_ref, st_ref, *, batch):
    @pl.when(pl.program_id(0) == 0)
    def _():
        st_ref[...] = jnp.zeros_like(st_ref)

    bd = bd_ref[...]

    def head_mean(t):
        hi, lo = _split(t)
        return (_dot(hi, bd) + _dot(lo, bd)) * (1.0 / RW_DH)

    pairs = [(b, h) for b in range(batch) for h in range(RW_HEADS)]
    sts = [st_ref[b * RW_HEADS + h] for b, h in pairs]
    ys = [_bdot(qy_ref[b, :, 2 * RW_DH * h:2 * RW_DH * h + RW_DH], st)
          + qy_ref[b, :, 2 * RW_DH * h + RW_DH:2 * RW_DH * (h + 1)] for (b, h), st in zip(pairs, sts)]
    for (b, h), st in zip(pairs, sts):
        mn = mn_ref[b, 0, h]
        st_ref[b * RW_HEADS + h] = _dot3(mn[:, :RW_DH], st) + mn[:, RW_DH:]
    for b in range(batch):
        y = jnp.concatenate(ys[b * RW_HEADS:(b + 1) * RW_HEADS], axis=1)
        dev = y - head_mean(y)
        yn = dev * lax.rsqrt(head_mean(dev * dev) + RW_GN_EPS) * gnw_ref[...] + gnb_ref[...]
        y_ref[b] = ((yn + bonus_ref[b]) * g_ref[b]).astype(y_ref.dtype)


def _rwkv(proj, p, batch, seq):
    head_ones = jnp.asarray(np.kron(np.eye(RW_HEADS), np.ones((RW_DH, RW_DH))), BF16)
    c = RW_CHUNK
    nc = seq // c
    tb = RW_NB * c
    nt = seq // tb
    hb = tb // 8
    row512 = lambda col: pl.BlockSpec((tb, RW_WIDTH), lambda b, i: (b * nt + i, col))
    halo512 = lambda col: pl.BlockSpec((8, RW_WIDTH), lambda b, i: (jnp.maximum((b * nt + i) * hb - 1, 0), col))
    const = lambda shape: pl.BlockSpec(shape, lambda b, i: (0,) * len(shape))
    vec = const((1, RW_WIDTH))
    out_row = lambda w: pl.BlockSpec((1, tb, w), lambda b, i: (b, i, 0))
    qy, mn, g, bonus = pl.pallas_call(
        _rwkv_chunk_kernel,
        out_shape=(jax.ShapeDtypeStruct((batch, seq, 2 * RW_WIDTH), F32),
                   jax.ShapeDtypeStruct((batch, nc, RW_HEADS, RW_DH, 2 * RW_DH), F32),
                   jax.ShapeDtypeStruct((batch, seq, RW_WIDTH), F32),
                   jax.ShapeDtypeStruct((batch, seq, RW_WIDTH), F32)),
        grid=(batch, nt),
        in_specs=[row512(C_R // RW_WIDTH), row512(C_K // RW_WIDTH), row512(C_V // RW_WIDTH),
                  pl.BlockSpec((tb, LORA_PAD), lambda b, i: (b * nt + i, C_LORA // LORA_PAD)),
                  halo512(C_R // RW_WIDTH), halo512(C_K // RW_WIDTH), halo512(C_V // RW_WIDTH),
                  pl.BlockSpec((8, LORA_PAD), lambda b, i: (jnp.maximum((b * nt + i) * hb - 1, 0), C_LORA // LORA_PAD)),
                  vec, vec, vec, const((1, LORA_PAD)),
                  vec, const((DECAY_LORA, RW_WIDTH)), vec, const((AAA_LORA, RW_WIDTH)), const((GATE_LORA, RW_WIDTH)),
                  vec, vec, vec, const((RW_WIDTH, RW_WIDTH))],
        out_specs=(out_row(2 * RW_WIDTH),
                   pl.BlockSpec((1, RW_NB, RW_HEADS, RW_DH, 2 * RW_DH), lambda b, i: (b, i, 0, 0, 0)),
                   out_row(RW_WIDTH), out_row(RW_WIDTH)),
        compiler_params=_cp(("parallel", "parallel"), VMEM_LIMIT),
        name="rwkv_chunk",
    )(proj, proj, proj, proj, proj, proj, proj, proj,
      p["mu_r"], p["mu_k"], p["mu_v"], p["mu_l"], p["w0"], p["w_up"], p["a0"], p["a_up"], p["g_up"],
      p["k_k"], p["k_a"], p["r_k"], head_ones)

    full = lambda w: pl.BlockSpec((batch, c, w), lambda i: (0, i, 0))
    return pl.pallas_call(
        functools.partial(_rwkv_scan_kernel, batch=batch),
        out_shape=jax.ShapeDtypeStruct((batch, seq, RW_WIDTH), BF16),
        grid=(nc,),
        in_specs=[full(2 * RW_WIDTH),
                  pl.BlockSpec((batch, 1, RW_HEADS, RW_DH, 2 * RW_DH), lambda i: (0, i, 0, 0, 0)),
                  full(RW_WIDTH), full(RW_WIDTH),
                  pl.BlockSpec((1, RW_WIDTH), lambda i: (0, 0)), pl.BlockSpec((1, RW_WIDTH), lambda i: (0, 0)),
                  pl.BlockSpec((RW_WIDTH, RW_WIDTH), lambda i: (0, 0))],
        out_specs=full(RW_WIDTH),
        scratch_shapes=[pltpu.VMEM((batch * RW_HEADS, RW_DH, RW_DH), F32)],
        compiler_params=_cp(("arbitrary",), VMEM_LIMIT),
        name="rwkv_scan",
    )(qy, mn, g, bonus, p["gn_w"], p["gn_b"], head_ones).reshape(batch * seq, RW_WIDTH)


def _gelu_tanh(x):
    return 0.5 * x * (1.0 + jnp.tanh(math.sqrt(2.0 / math.pi) * (x + 0.044715 * (x * x * x))))


def _compress_kernel(x_ref, pos_ref, w1_ref, w2_ref, o_ref):
    half = CMP_STRIDE * NSA_DH
    x = x_ref[0, 0, 0]
    w1 = w1_ref[0]
    posb = _dot(pos_ref[0], w1)[0:1, :]
    h1 = _dot(x, w1[:half])
    h2 = _dot(x, w1[half:])
    n = h2.shape[0]
    row = lax.broadcasted_iota(jnp.int32, h2.shape, 0)
    h2s = jnp.where(row < n - 1, pltpu.roll(h2, n - 1, 0), 0.0)
    hid = _gelu_tanh(h1 + h2s + posb)
    o_ref[0, 0, 0] = _dot(hid.astype(BF16), w2_ref[0]).astype(o_ref.dtype)


def _compress(xkv, pos, w1, w2, batch, seq):
    nr = seq // CMP_STRIDE
    wide = CMP_STRIDE * NSA_DH
    return pl.pallas_call(
        _compress_kernel,
        out_shape=jax.ShapeDtypeStruct((2, batch, NSA_G, nr, NSA_DH), BF16),
        grid=(2, batch, NSA_G),
        in_specs=[pl.BlockSpec((1, 1, 1, nr, wide), lambda t, b, g: (t, b, g, 0, 0)),
                  pl.BlockSpec((1, 8, 2 * wide), lambda t, b, g: (t, 0, 0)),
                  pl.BlockSpec((1, 2 * wide, CMP_HIDDEN), lambda t, b, g: (t, 0, 0)),
                  pl.BlockSpec((1, CMP_HIDDEN, NSA_DH), lambda t, b, g: (t, 0, 0))],
        out_specs=pl.BlockSpec((1, 1, 1, nr, NSA_DH), lambda t, b, g: (t, b, g, 0, 0)),
        compiler_params=_cp(("parallel", "parallel", "parallel")),
        name="nsa_compress",
    )(xkv, pos, w1, w2)


def _softmax_parts(s):
    m = jnp.max(s, axis=1, keepdims=True)
    e = jnp.exp(s - m)
    return m, e, jnp.sum(e, axis=1, keepdims=True)


def _lane_tile_fold(x, op, init):
    for t in range(x.shape[1] // 128):
        init = op(init, x[:, 128 * t:128 * (t + 1)])
    return init


def _nsa_kernel(q_ref, gate_ref, kc_ref, vc_ref, ks_ref, vs_ref, kw_ref, vw_ref,
                tblc_ref, tbln_ref, tblw_ref, ovt_ref, gexp_ref, o_ref, s_ref, *, ncmp):
    nb = NSA_NB
    blocks = [pl.program_id(2) * nb + u for u in range(nb)]
    rows = NSA_HG * QB
    each = lambda f, *ls: [f(*xs) for xs in zip(*ls)]
    qt = q_ref[...]
    qs = [jnp.concatenate([qt[u * QB:(u + 1) * QB, NSA_DH * h:NSA_DH * (h + 1)] for h in range(NSA_HG)], axis=0)
          * (NSA_DH ** -0.5 * LOG2E) for u in range(nb)]
    qb = each(lambda x: x.astype(BF16), qs)
    rmax = lambda x: jnp.max(x, axis=1, keepdims=True)
    rsum = lambda x: jnp.sum(x, axis=1, keepdims=True)

    win_w = WINDOW + QB
    kw = [kw_ref[0, 0, pl.ds(pl.multiple_of(i * QB, QB), win_w), :] for i in blocks]
    vw = [vw_ref[0, 0, pl.ds(pl.multiple_of(i * QB, QB), win_w), :] for i in blocks]
    flag = lax.broadcasted_iota(jnp.int32, (rows, kw_ref.shape[3] - NSA_DH), 1) == 0
    q_win = each(lambda x: jnp.concatenate([x, jnp.where(flag, NEG, 0.0)], axis=1).astype(BF16), qs)
    tblw = tblw_ref[0]
    s_w = each(lambda x, k: _dot_nt(x, k) + tblw, q_win, kw)
    m_w = each(rmax, s_w)
    e_w = each(lambda s, m: jnp.exp2(s - m), s_w, m_w)
    pv = lambda x: x[:, :NSA_DH] / x[:, NSA_DH:NSA_DH + 1]
    o_w = each(lambda e, v: pv(_dot(e.astype(BF16), v)), e_w, vw)

    kc = kc_ref[0, 0]
    vc = vc_ref[0, 0]
    tblc = tblc_ref[0]
    cidx = lax.broadcasted_iota(jnp.int32, (rows, ncmp), 1)
    qrow = lax.broadcasted_iota(jnp.int32, (rows, 1), 0) % QB
    lc = [jnp.where(cidx < (QB // CMP_STRIDE) * i + CMP_AHEAD,
                    _dot_nt(x, kc) + pltpu.roll(tblc, (4 * i - CMP_NEAR // 2 + ncmp) % ncmp, 1), NEG)
          for x, i in zip(qb, blocks)]
    m_c = each(rmax, lc)
    e_c = each(lambda s, m: jnp.exp2(s - m), lc, m_c)
    den_c = each(rsum, e_c)
    pc = [e * jnp.where(i * QB + qrow >= CMP_BLOCK - 1, 1.0 / d, 0.0) for e, d, i in zip(e_c, den_c, blocks)]
    o_c = each(lambda x: _dot(x.astype(BF16), vc), pc)
    pcs_hi, pcs_lo = _split(jnp.concatenate(
        each(lambda x: x[0:QB] + x[QB:2 * QB] + x[2 * QB:3 * QB] + x[3 * QB:4 * QB], pc), axis=0))
    ovt = ovt_ref[...]
    imp = _dot_nt(ovt, pcs_hi) + _dot_nt(ovt, pcs_lo)

    nslc = ovt.shape[0]
    nidx = lax.broadcasted_iota(jnp.int32, (nslc, nb * QB), 0)
    cur = blocks[0] + lax.broadcasted_iota(jnp.int32, (nslc, nb * QB), 1) // QB
    forced = (nidx == 0) | (nidx == cur) | (nidx == cur - 1)
    work = jnp.where(nidx <= cur, imp + jnp.where(forced, FORCE_BONUS, 0.0), -1.0)
    sel_all = jnp.zeros((nslc, nb * QB), F32)
    for _ in range(N_SELECT):
        m = jnp.max(work, axis=0, keepdims=True)
        first = jnp.min(jnp.where(work == m, nidx, nslc), axis=0, keepdims=True)
        pick = nidx == first
        sel_all = jnp.where(pick & (m >= 0.0), 1.0, sel_all)
        work = jnp.where(pick, -2.0, work)
    sel_t = sel_all.T

    ind_w = ks_ref.shape[3] - NSA_DH
    selq = [sel_t[u * QB:(u + 1) * QB] for u in range(nb)]
    if nslc < ind_w:
        selq = each(lambda x: jnp.concatenate([x, jnp.zeros((QB, ind_w - nslc), F32)], axis=1), selq)
    bidx = lax.broadcasted_iota(jnp.int32, (QB, ind_w), 1)
    tile4 = lambda t: jnp.concatenate([t] * NSA_HG, axis=0)
    with_mask = lambda x, keep: jnp.concatenate([x, tile4(jnp.where(keep, 0.0, NEG))], axis=1).astype(BF16)
    q_sel = each(lambda x, s: with_mask(x, s > 0.0), qs, selq)
    q_far = jnp.concatenate([with_mask(x, (s > 0.0) & (bidx <= i - NEAR_BLOCKS))
                             for x, s, i in zip(qs, selq, blocks)], axis=0)
    near_w = NEAR_BLOCKS * SLC_BLOCK
    kn = [ks_ref[0, 0, pl.ds(pl.multiple_of(i * QB, QB), near_w), :] for i in blocks]
    vn = [vs_ref[0, 0, pl.ds(pl.multiple_of(i * QB, QB), near_w), :] for i in blocks]
    tbln = tbln_ref[0]
    s_near = each(lambda x, k: _dot_nt(x, k) + tbln, q_sel, kn)
    m_near = jnp.concatenate(each(rmax, s_near), axis=0)

    far_w = FAR_CHUNK_BLOCKS * SLC_BLOCK
    pad_s = (NEAR_BLOCKS - 1) * SLC_BLOCK
    n_far = jnp.maximum(blocks[-1] - (NEAR_BLOCKS - 1) + FAR_CHUNK_BLOCKS - 1, 0) // FAR_CHUNK_BLOCKS

    def far_logits(j, mvec):
        start = pl.multiple_of(j * far_w, far_w)
        kf = ks_ref[0, 0, pl.ds(pl.multiple_of(pad_s + j * far_w, SLC_BLOCK), far_w), :]
        s = _dot_nt(q_far, kf)
        s_ref[:, pl.ds(start, far_w)] = s
        return _lane_tile_fold(s, jnp.maximum, mvec)

    mvec = lax.fori_loop(0, n_far, far_logits, jnp.full((nb * rows, 128), NEG, F32))
    m_s = jnp.maximum(m_near, rmax(mvec))

    e_near = [jnp.exp2(s - m_s[u * rows:(u + 1) * rows]) for u, s in enumerate(s_near)]
    acc0 = jnp.concatenate(each(lambda e, v: _dot(e.astype(BF16), v), e_near, vn), axis=0)

    def far_values(j, acc):
        start = pl.multiple_of(j * far_w, far_w)
        vf = vs_ref[0, 0, pl.ds(pl.multiple_of(pad_s + j * far_w, SLC_BLOCK), far_w), :]
        e = jnp.exp2(s_ref[:, pl.ds(start, far_w)] - m_s)
        return acc + _dot(e.astype(BF16), vf)

    o_s = pv(lax.fori_loop(0, n_far, far_values, acc0))

    g_hi, g_lo = _split(jax.nn.sigmoid(gate_ref[0, 0]))
    gexp = gexp_ref[...]
    ge = _dot(g_hi, gexp) + _dot(g_lo, gexp)
    for u in range(nb):
        outs = []
        for h in range(NSA_HG):
            r0 = slice(h * QB, (h + 1) * QB)
            gt = lambda br: ge[u * QB:(u + 1) * QB, (3 * h + br) * 128:(3 * h + br) * 128 + NSA_DH]
            outs.append(gt(0) * o_c[u][r0] + gt(1) * o_s[u * rows + h * QB:u * rows + (h + 1) * QB]
                        + gt(2) * o_w[u][r0])
        o_ref[u * QB:(u + 1) * QB, :] = jnp.concatenate(outs, axis=1).astype(o_ref.dtype)


def _bias_tables(rel_bias, ncmp):
    def table(dist, keep, base):
        onehot = jnp.asarray(np.eye(N_BUCKETS, dtype=np.float32)[_t5_bucket_np(dist)])
        tbl = jnp.einsum("qkb,bh->qkh", onehot, rel_bias, precision=HI)
        tbl = (tbl - base) * LOG2E
        tbl = jnp.where(jnp.asarray(keep)[..., None], tbl, NEG)
        k = dist.shape[1]
        return tbl.transpose(2, 0, 1).reshape(NSA_G, NSA_HG * QB, k)

    far = rel_bias[N_BUCKETS - 1]
    qi = np.arange(QB)[:, None]
    dist_c = qi - CMP_STRIDE * (np.arange(CMP_NEAR)[None, :] - CMP_NEAR // 2) - (CMP_BLOCK - 1)
    tblc = table(dist_c, dist_c >= 0, far)
    tblc = tblc * jnp.asarray(np.arange(CMP_NEAR) < CMP_NEAR // 2 + CMP_AHEAD, F32)
    tblc = jnp.pad(tblc, ((0, 0), (0, 0), (0, ncmp - CMP_NEAR)))
    jn = np.arange(NEAR_BLOCKS * SLC_BLOCK)[None, :]
    dist_n = (NEAR_BLOCKS - 1) * SLC_BLOCK + qi - jn
    tbln = table(dist_n, dist_n >= 0, far)
    jw = np.arange(WINDOW + QB)[None, :]
    dist_w = WINDOW + qi - jw
    tblw = table(dist_w, (dist_w >= 0) & (dist_w < WINDOW), 0.0)
    return tblc, tbln, tblw


def _nsa(proj, gate_logits, kvc, ks, vs, kw, vw, tables, batch, seq):
    nq = seq // QB
    ncmp = seq // CMP_STRIDE
    nslc = seq // SLC_BLOCK
    tblc, tbln, tblw = tables
    cstart = np.arange(ncmp) * CMP_STRIDE
    sstart = np.arange(nslc) * SLC_BLOCK
    overlap_t = ((cstart[None, :] <= sstart[:, None] + SLC_BLOCK - 1)
                 & (cstart[None, :] + CMP_BLOCK - 1 >= sstart[:, None])
                 & (cstart[None, :] + CMP_BLOCK <= seq)).astype(np.float32)
    pad_s = ks.shape[2] - seq
    blk = (np.arange(seq)[:, None] // SLC_BLOCK == np.arange(NSA_DH)[None, :]).astype(np.float32)
    blk = np.concatenate([np.ones((pad_s, NSA_DH), np.float32), blk], axis=0)
    ks = jnp.concatenate([ks, jnp.broadcast_to(jnp.asarray(blk, BF16), ks.shape[:2] + blk.shape)], axis=-1)
    flag = np.zeros((kw.shape[2], NSA_DH), np.float32)
    flag[:kw.shape[2] - seq, 0] = 1.0
    kw = jnp.concatenate([kw, jnp.broadcast_to(jnp.asarray(flag, BF16), kw.shape[:2] + flag.shape)], axis=-1)
    ones_col = np.zeros((1, NSA_DH), np.float32)
    ones_col[0, 0] = 1.0
    with_ones = lambda t: jnp.concatenate(
        [t, jnp.broadcast_to(jnp.asarray(ones_col, BF16), t.shape[:3] + (NSA_DH,))], axis=-1)
    vs, vw = with_ones(vs), with_ones(vw)
    rows = NSA_HG * QB
    n_gate = 3 * NSA_HG
    gate_spread = (np.arange(n_gate * 128)[None, :] // 128 == np.arange(n_gate)[:, None]) \
        & (np.arange(n_gate * 128)[None, :] % 128 < NSA_DH)
    kv_spec = lambda t: pl.BlockSpec((1, 1) + t.shape[2:], lambda b, g, i: (b, g, 0, 0))
    tbl_spec = lambda k: pl.BlockSpec((1, rows, k), lambda b, g, i: (g, 0, 0))
    qcol = C_Q // (NSA_HG * NSA_DH)
    return pl.pallas_call(
        functools.partial(_nsa_kernel, ncmp=ncmp),
        out_shape=jax.ShapeDtypeStruct((batch * seq, NSA_HEADS * NSA_DH), BF16),
        grid=(batch, NSA_G, nq // NSA_NB),
        in_specs=[pl.BlockSpec((NSA_NB * QB, NSA_HG * NSA_DH), lambda b, g, i: (b * (nq // NSA_NB) + i, qcol + g)),
                  pl.BlockSpec((1, 1, NSA_NB * QB, 3 * NSA_HG), lambda b, g, i: (b, g, i, 0)),
                  pl.BlockSpec((1, 1, ncmp, NSA_DH), lambda b, g, i: (b, g, 0, 0)),
                  pl.BlockSpec((1, 1, ncmp, NSA_DH), lambda b, g, i: (b, g, 0, 0)),
                  kv_spec(ks), kv_spec(vs), kv_spec(kw), kv_spec(vw),
                  tbl_spec(ncmp), tbl_spec(NEAR_BLOCKS * SLC_BLOCK), tbl_spec(WINDOW + QB),
                  pl.BlockSpec((nslc, ncmp), lambda b, g, i: (0, 0)),
                  pl.BlockSpec(gate_spread.shape, lambda b, g, i: (0, 0))],
        out_specs=pl.BlockSpec((NSA_NB * QB, NSA_HG * NSA_DH), lambda b, g, i: (b * (nq // NSA_NB) + i, g)),
        scratch_shapes=[pltpu.VMEM((NSA_NB * rows, seq), F32)],
        compiler_params=_cp(("parallel", "parallel", "arbitrary"), VMEM_LIMIT),
        name="nsa_attention",
    )(proj, gate_logits, kvc[0], kvc[1], ks, vs, kw, vw, tblc, tbln, tblw, jnp.asarray(overlap_t, BF16),
      jnp.asarray(gate_spread, BF16))


def _merge_kernel(h_ref, yp_ref, yr_ref, yn_ref, x_ref, wbp_ref, wbr_ref, wbn_ref, wm_ref, bm_ref, wo_ref,
                  gn_ref, wr_ref, br_ref, xo_ref, h2_ref, route_ref):
    d = D_MODEL
    gl = jax.nn.sigmoid(_dot(h_ref[...], wm_ref[...]) + bm_ref[...])
    merged = (gl[:, :d] * _dot(yp_ref[...], wbp_ref[...]) + gl[:, d:2 * d] * _dot(yr_ref[...], wbr_ref[...])
              + gl[:, 2 * d:] * _dot(yn_ref[...], wbn_ref[...]))
    x = x_ref[...] + _dot(merged.astype(BF16), wo_ref[...])
    xo_ref[...] = x
    h2 = _rms(x, gn_ref[...])
    h2_ref[...] = h2.astype(h2_ref.dtype)

    h2_hi, h2_lo = _split(h2)
    nl = br_ref.shape[1]
    both = _dot(h2_hi, wr_ref[...])
    logits = both[:, :nl] + (both[:, nl:] + _dot(h2_lo, wr_ref[:, :nl])) + br_ref[...]
    lane = lax.broadcasted_iota(jnp.int32, logits.shape, 1)
    big = logits.shape[1]
    lg = jnp.where(lane < N_GROUPS, logits, NEG)
    mg = jnp.max(lg, axis=1, keepdims=True)
    p_top = 1.0 / jnp.sum(jnp.exp(lg - mg), axis=1, keepdims=True)
    grp = jnp.min(jnp.where(lg == mg, lane, big), axis=1, keepdims=True)
    lo = N_GROUPS + EPG * grp
    le = jnp.where((lane >= lo) & (lane < lo + EPG), logits, NEG)
    e1 = jnp.max(le, axis=1, keepdims=True)
    i1 = jnp.min(jnp.where(le == e1, lane, big), axis=1, keepdims=True)
    le = jnp.where(lane == i1, NEG, le)
    e2 = jnp.max(le, axis=1, keepdims=True)
    i2 = jnp.min(jnp.where(le == e2, lane, big), axis=1, keepdims=True)
    t = jnp.exp(e2 - e1)
    w1 = p_top / (1.0 + t)
    w2 = p_top * t / (1.0 + t)
    route_ref[...] = jnp.where(lane == 0, (i1 - N_GROUPS).astype(F32),
                               jnp.where(lane == 1, (i2 - N_GROUPS).astype(F32),
                                         jnp.where(lane == 2, w1, jnp.where(lane == 3, w2, 0.0))))


def _merge(h, y_pool, y_rwkv, y_nsa, x, p, tm=256):
    n, d = x.shape
    row = lambda w: pl.BlockSpec((tm, w), lambda i: (i, 0))
    const = lambda a: pl.BlockSpec(a.shape, lambda i: (0, 0))
    ws = [p["wb_pool"], p["wb_rwkv"], p["wb_nsa"], p["w_merge"], p["b_merge"], p["w_out"],
          p["norm_ffn"], p["w_router"], p["b_router"]]
    return pl.pallas_call(
        _merge_kernel,
        out_shape=(jax.ShapeDtypeStruct((n, d), F32), jax.ShapeDtypeStruct((n, d), F32),
                   jax.ShapeDtypeStruct((n, 128), F32)),
        grid=(n // tm,),
        in_specs=[row(d), row(POOL_WIDTH), row(RW_WIDTH), row(d), row(d)] + [const(w) for w in ws],
        out_specs=(row(d), row(d), row(128)),
        compiler_params=_cp(("parallel",), VMEM_LIMIT),
        name="merge_router",
    )(h, y_pool, y_rwkv, y_nsa, x, *ws)


def _expert_kernel(te_ref, nt_ref, tok_ref, h_hbm, w_ref, wg_ref, wu_ref, wd_ref, o_ref,
                   xbuf, sem, wg_s, wu_s, wd_s):
    i = pl.program_id(0)
    tm = MOE_TM
    n_tiles = nt_ref[0]

    def row_copy(tile, slot, r):
        return pltpu.make_async_copy(h_hbm.at[pl.ds(tok_ref[tile * tm + r], 1), :],
                                     xbuf.at[slot, pl.ds(r, 1), :], sem.at[slot])

    def tile_wait(slot):
        pltpu.make_async_copy(h_hbm.at[pl.ds(0, tm), :], xbuf.at[slot], sem.at[slot]).wait()

    @pl.when((i == 0) & (n_tiles > 0))
    def _():
        def body(r, carry):
            row_copy(0, 0, r).start()
            return carry
        lax.fori_loop(0, tm, body, 0, unroll=8)

    @pl.when((i == 0) | (te_ref[i] != te_ref[jnp.maximum(i - 1, 0)]))
    def _():
        wg_s[...] = wg_ref[0].astype(BF16)
        wu_s[...] = wu_ref[0].astype(BF16)
        wd_s[...] = wd_ref[0].astype(BF16)

    @pl.when(i < n_tiles)
    def _():
        slot = i % 2
        quarter = tm // 4

        def fetch_next(part):
            for r in range(part * quarter, (part + 1) * quarter):
                row_copy(i + 1, 1 - slot, r).start()

        tile_wait(slot)
        xb = xbuf[slot].astype(BF16)
        fetch_next(0)
        gate = _dot(xb, wg_s[...])
        fetch_next(1)
        up = _dot(xb, wu_s[...])
        fetch_next(2)
        hid = (gate * jax.nn.sigmoid(gate) * up).astype(BF16)
        fetch_next(3)
        o_ref[...] = w_ref[...] * _dot(hid, wd_s[...])

    @pl.when((i == n_tiles) & (n_tiles > 0))
    def _():
        tile_wait(i % 2)

    @pl.when(i >= nt_ref[0])
    def _():
        o_ref[...] = jnp.zeros_like(o_ref)


def _experts(h2, rowtok, roww, tile_expert, n_tiles, wg, wu, wd):
    d = h2.shape[1]
    r = rowtok.shape[0]
    tm = MOE_TM
    return pl.pallas_call(
        _expert_kernel,
        out_shape=jax.ShapeDtypeStruct((r, d), F32),
        grid_spec=pltpu.PrefetchScalarGridSpec(
            num_scalar_prefetch=3,
            grid=(r // tm,),
            in_specs=[pl.BlockSpec(memory_space=pl.ANY),
                      pl.BlockSpec((tm, 1), lambda i, te, nt, tok: (i, 0)),
                      pl.BlockSpec((1, d, D_EXPERT), lambda i, te, nt, tok: (te[i], 0, 0)),
                      pl.BlockSpec((1, d, D_EXPERT), lambda i, te, nt, tok: (te[i], 0, 0)),
                      pl.BlockSpec((1, D_EXPERT, d), lambda i, te, nt, tok: (te[i], 0, 0))],
            out_specs=pl.BlockSpec((tm, d), lambda i, te, nt, tok: (i, 0)),
            scratch_shapes=[pltpu.VMEM((2, tm, d), F32), pltpu.SemaphoreType.DMA((2,)),
                            pltpu.VMEM((d, D_EXPERT), BF16), pltpu.VMEM((d, D_EXPERT), BF16),
                            pltpu.VMEM((D_EXPERT, d), BF16)]),
        compiler_params=_cp(("arbitrary",), VMEM_LIMIT),
        name="moe_experts",
    )(tile_expert, n_tiles, rowtok, h2, roww, wg, wu, wd)


def _moe(h2, route, wg, wu, wd):
    n = h2.shape[0]
    tm = MOE_TM
    r = 2 * n + (N_EXPERTS + 1) * tm
    ids = route[:, 0:2].astype(jnp.int32).reshape(-1)
    wts = route[:, 2:4].reshape(-1)
    onehot = (ids[:, None] == jnp.arange(N_EXPERTS)[None, :]).astype(jnp.int32)
    rank = jnp.sum((jnp.cumsum(onehot, axis=0) - onehot) * onehot, axis=1)
    counts = jnp.sum(onehot, axis=0)
    tiles = (counts + tm - 1) // tm
    tile_end = jnp.cumsum(tiles)
    starts = (tile_end - tiles) * tm
    pos = starts[ids] + rank
    row_assign = jnp.full((r,), -1, jnp.int32).at[pos].set(jnp.arange(2 * n, dtype=jnp.int32))
    rowtok = jnp.maximum(row_assign, 0) // 2
    roww = jnp.where(row_assign >= 0, wts[jnp.maximum(row_assign, 0)], 0.0)
    n_tiles = tile_end[-1:].astype(jnp.int32)
    tile_expert = jnp.minimum(jnp.sum(tile_end[None, :] <= jnp.arange(r // tm)[:, None], axis=1),
                              N_EXPERTS - 1).astype(jnp.int32)
    ys = _experts(h2, rowtok, roww.reshape(r, 1), tile_expert, n_tiles, wg, wu, wd)
    return ys[pos[0::2]], ys[pos[1::2]]


def _layer_params(l, a):
    f = lambda t: t[l]
    row = lambda t: t[l].reshape(1, -1)
    w_in = a["w_in"][l]
    w_in_p = jnp.concatenate([w_in[:, :SRC_RW_END], jnp.zeros((D_MODEL, C_Q - SRC_RW_END), F32),
                              w_in[:, SRC_Q:SRC_KV], w_in[:, SRC_GATE:],
                              jnp.zeros((D_MODEL, P_COLS - C_GATE - (w_in.shape[1] - SRC_GATE)), F32)], axis=1)
    mu = a["rw_mu"][l]
    wb = a["w_branch"][l].astype(BF16)
    w_router = jnp.zeros((D_MODEL, 128), F32)
    w_router = w_router.at[:, :N_GROUPS].set(a["w_router_grp"][l]).at[:, N_GROUPS:N_GROUPS + N_EXPERTS].set(
        a["w_router_exp"][l])
    b_router = jnp.zeros((1, 128), F32)
    b_router = b_router.at[0, :N_GROUPS].set(a["b_router_grp"][l]).at[0, N_GROUPS:N_GROUPS + N_EXPERTS].set(
        a["b_router_exp"][l])
    pos = jnp.stack([a["cmp_pos_k"][l].reshape(-1), a["cmp_pos_v"][l].reshape(-1)])
    return {
        "w_in": w_in_p.astype(BF16), "w_kv": w_in[:, SRC_KV:SRC_GATE].astype(BF16),
        "pool_w": f(a["pool_w"]), "pool_scale": f(a["pool_scale"]),
        "mu_r": mu[None, 0:RW_WIDTH], "mu_k": mu[None, RW_WIDTH:2 * RW_WIDTH],
        "mu_v": mu[None, 2 * RW_WIDTH:3 * RW_WIDTH],
        "mu_l": jnp.concatenate([mu[3 * RW_WIDTH:], jnp.zeros((LORA_PAD - RW_LORA,), F32)])[None],
        "w0": row(a["rw_w0"]), "w_up": f(a["rw_w_up"]), "a0": row(a["rw_a0"]), "a_up": f(a["rw_a_up"]),
        "g_up": f(a["rw_g_up"]), "k_k": row(a["rw_k_k"]), "k_a": row(a["rw_k_a"]), "r_k": row(a["rw_r_k"]),
        "gn_w": row(a["rw_gn_w"]), "gn_b": row(a["rw_gn_b"]),
        "cmp_pos": jnp.broadcast_to(pos[:, None, :], (2, 8, pos.shape[1])).astype(BF16),
        "cmp_w1": jnp.stack([a["cmp_w1_k"][l], a["cmp_w1_v"][l]]).astype(BF16),
        "cmp_w2": jnp.stack([a["cmp_w2_k"][l], a["cmp_w2_v"][l]]).astype(BF16),
        "wb_pool": wb[:POOL_WIDTH], "wb_rwkv": wb[POOL_WIDTH:POOL_WIDTH + RW_WIDTH],
        "wb_nsa": wb[POOL_WIDTH + RW_WIDTH:],
        "w_merge": a["w_merge"][l].astype(BF16), "b_merge": row(a["b_merge"]),
        "w_out": a["w_out"][l].astype(BF16), "norm_ffn": row(a["norm_ffn"]),
        "w_router": jnp.concatenate(_split(w_router), axis=1), "b_router": b_router,
        "w_gate": a["w_exp_gate"][l], "w_up_e": a["w_exp_up"][l], "w_down": a["w_exp_down"][l],
    }


def _mixers(proj, kv, p, tables, batch, seq):
    y_pool = _pool(proj, p["pool_w"], p["pool_scale"], batch, seq)
    y_rwkv = _rwkv(proj, p, batch, seq)
    xkv = kv[0:2].reshape(2, batch, NSA_G, seq // CMP_STRIDE, CMP_STRIDE * NSA_DH)
    kvc = _compress(xkv, p["cmp_pos"], p["cmp_w1"], p["cmp_w2"], batch, seq)
    pad = lambda t, n: jnp.pad(t, ((0, 0), (0, 0), (n, 0), (0, 0)))
    pad_s = (NEAR_BLOCKS - 1) * SLC_BLOCK
    gate_logits = proj[:, C_GATE:C_GATE + 3 * NSA_HEADS].reshape(batch, seq, NSA_G, 3 * NSA_HG).transpose(0, 2, 1, 3)
    y_nsa = _nsa(proj, gate_logits, kvc, pad(kv[2], pad_s), pad(kv[3], pad_s), pad(kv[4], WINDOW),
                 pad(kv[5], WINDOW), tables, batch, seq)
    return y_pool, y_rwkv, y_nsa


def kernel(x, rel_bias, norm_mix, w_in, pool_w, pool_scale, rw_mu, rw_w0, rw_w_up, rw_a0, rw_a_up, rw_g_up, rw_k_k, rw_k_a, rw_r_k, rw_gn_w, rw_gn_b, cmp_pos_k, cmp_w1_k, cmp_w2_k, cmp_pos_v, cmp_w1_v, cmp_w2_v, w_branch, w_merge, b_merge, w_out, norm_ffn, w_router_grp, b_router_grp, w_router_exp, b_router_exp, w_exp_gate, w_exp_up, w_exp_down, norm_final):
    a = dict(w_in=w_in, pool_w=pool_w, pool_scale=pool_scale, rw_mu=rw_mu, rw_w0=rw_w0, rw_w_up=rw_w_up,
             rw_a0=rw_a0, rw_a_up=rw_a_up, rw_g_up=rw_g_up, rw_k_k=rw_k_k, rw_k_a=rw_k_a, rw_r_k=rw_r_k,
             rw_gn_w=rw_gn_w, rw_gn_b=rw_gn_b, cmp_pos_k=cmp_pos_k, cmp_w1_k=cmp_w1_k, cmp_w2_k=cmp_w2_k,
             cmp_pos_v=cmp_pos_v, cmp_w1_v=cmp_w1_v, cmp_w2_v=cmp_w2_v, w_branch=w_branch, w_merge=w_merge,
             b_merge=b_merge, w_out=w_out, norm_ffn=norm_ffn, w_router_grp=w_router_grp,
             b_router_grp=b_router_grp, w_router_exp=w_router_exp, b_router_exp=b_router_exp,
             w_exp_gate=w_exp_gate, w_exp_up=w_exp_up, w_exp_down=w_exp_down)
    batch, seq, d = x.shape
    depth = norm_mix.shape[0]
    tables = _bias_tables(rel_bias, seq // CMP_STRIDE)
    xf = x.reshape(batch * seq, d)
    h = _norm(xf, norm_mix[0], BF16)
    for l in range(depth):
        p = _layer_params(l, a)
        proj = _matmul(h, p["w_in"])
        kv = _kv_proj(h, p["w_kv"], batch, seq)
        y_pool, y_rwkv, y_nsa = _mixers(proj, kv, p, tables, batch, seq)
        xf, h2, route = _merge(h, y_pool, y_rwkv, y_nsa, xf, p)
        y1, y2 = _moe(h2, route, p["w_gate"], p["w_up_e"], p["w_down"])
        last = l == depth - 1
        g_next = norm_final if last else norm_mix[l + 1]
        xf, h = _add_norm(xf, y1, y2, g_next, F32 if last else BF16)
    return h.reshape(batch, seq, d)
```

```python
import functools
import math

import jax
import jax.numpy as jnp
import numpy as np
from jax import lax
from jax.experimental import pallas as pl
from jax.experimental.pallas import tpu as pltpu

F32 = jnp.float32
BF16 = jnp.bfloat16
HI = lax.Precision.HIGHEST

D_MODEL = 1024
RMS_EPS = 1e-6
NEG = -1e30
LOG2E = math.log2(math.e)

POOL_WINDOWS = (2, 4, 8, 16)
POOL_WIDTH = 512
POOL_GW = 128
POOL_HALO = 16

RW_HEADS = 8
RW_DH = 64
RW_WIDTH = 512
DECAY_LORA, AAA_LORA, GATE_LORA = 32, 32, 96
RW_LORA = DECAY_LORA + AAA_LORA + GATE_LORA
RW_COLS = 3 * RW_WIDTH + RW_LORA
RW_GN_EPS = 64e-5
RW_CHUNK = 64
RW_NB = 2

NSA_DH = 64
NSA_HEADS = 16
NSA_G = 4
NSA_HG = 4
NSA_KVW = NSA_G * NSA_DH
CMP_BLOCK, CMP_STRIDE, CMP_HIDDEN = 32, 16, 256
SLC_BLOCK = 64
N_SELECT = 8
WINDOW = 512
QB = 64
NSA_NB = 8
FORCE_BONUS = 1e3
N_BUCKETS, MAX_EXACT, MAX_DISTANCE = 32, 16, 128
NEAR_BLOCKS = 3
FAR_CHUNK_BLOCKS = 8
CMP_NEAR = 32
CMP_AHEAD = (QB - CMP_BLOCK) // CMP_STRIDE + 1

N_GROUPS, EPG, N_EXPERTS, D_EXPERT = 4, 8, 32, 256
MOE_TM = 256

C_POOL, C_R, C_K, C_V, C_LORA, C_Q, C_GATE, P_COLS = 0, 512, 1024, 1536, 2048, 2304, 3328, 3584
SRC_Q, SRC_KV, SRC_GATE = 2208, 3232, 4768
N_KV = 6
LORA_PAD = 256
SRC_RW_END = POOL_WIDTH + RW_COLS

VMEM_LIMIT = 56 * 1024 * 1024


def _t5_bucket_np(dist):
    n = np.maximum(dist, 0)
    nf = np.maximum(n, 1).astype(np.float32)
    large = MAX_EXACT + (np.log(nf / MAX_EXACT) / math.log(MAX_DISTANCE / MAX_EXACT)
                         * (N_BUCKETS - MAX_EXACT)).astype(np.int32)
    large = np.minimum(large, N_BUCKETS - 1)
    return np.where(n < MAX_EXACT, n, large)


def _cp(sem, vmem=None):
    return pltpu.CompilerParams(dimension_semantics=sem, vmem_limit_bytes=vmem)


def _dot(a, b, precision=None):
    return jnp.dot(a, b, preferred_element_type=F32, precision=precision)


def _dot_nt(a, b, precision=None):
    return lax.dot_general(a, b, (((1,), (1,)), ((), ())), preferred_element_type=F32, precision=precision)


def _dot_tn(a, b, precision=None):
    return lax.dot_general(a, b, (((0,), (0,)), ((), ())), preferred_element_type=F32, precision=precision)


def _bdot(a, b):
    return _dot(a.astype(BF16), b.astype(BF16))


def _bdot_nt(a, b):
    return _dot_nt(a.astype(BF16), b.astype(BF16))


def _bdot_tn(a, b):
    return _dot_tn(a.astype(BF16), b.astype(BF16))


def _split(a):
    hi = a.astype(BF16)
    return hi, (a - hi.astype(F32)).astype(BF16)


def _dot3(a, b):
    ah, al = _split(a)
    bh, bl = _split(b)
    return _dot(ah, bh) + (_dot(ah, bl) + _dot(al, bh))


def _rms(x, g):
    return x * lax.rsqrt(jnp.mean(x * x, axis=-1, keepdims=True) + RMS_EPS) * g


def _norm_kernel(x_ref, g_ref, h_ref):
    h_ref[...] = _rms(x_ref[...], g_ref[...]).astype(h_ref.dtype)


def _norm(x, g, out_dtype, tm=512):
    n, d = x.shape
    return pl.pallas_call(
        _norm_kernel,
        out_shape=jax.ShapeDtypeStruct((n, d), out_dtype),
        grid=(n // tm,),
        in_specs=[pl.BlockSpec((tm, d), lambda i: (i, 0)), pl.BlockSpec((1, d), lambda i: (0, 0))],
        out_specs=pl.BlockSpec((tm, d), lambda i: (i, 0)),
        compiler_params=_cp(("parallel",)),
        name="rms_norm",
    )(x, g.reshape(1, d))


def _add_norm_kernel(x_ref, y1_ref, y2_ref, g_ref, xo_ref, h_ref):
    x = x_ref[...] + (y1_ref[...] + y2_ref[...])
    xo_ref[...] = x
    h_ref[...] = _rms(x, g_ref[...]).astype(h_ref.dtype)


def _add_norm(x, y1, y2, g, out_dtype, tm=512):
    n, d = x.shape
    row = pl.BlockSpec((tm, d), lambda i: (i, 0))
    return pl.pallas_call(
        _add_norm_kernel,
        out_shape=(jax.ShapeDtypeStruct((n, d), F32), jax.ShapeDtypeStruct((n, d), out_dtype)),
        grid=(n // tm,),
        in_specs=[row, row, row, pl.BlockSpec((1, d), lambda i: (0, 0))],
        out_specs=(row, row),
        compiler_params=_cp(("parallel",)),
        name="moe_combine_norm",
    )(x, y1, y2, g.reshape(1, d))


def _matmul_kernel(x_ref, w_ref, o_ref):
    o_ref[...] = _dot(x_ref[...], w_ref[...]).astype(o_ref.dtype)


def _matmul(x, w, tm=512, tn=P_COLS // 2):
    m, k = x.shape
    n = w.shape[1]
    return pl.pallas_call(
        _matmul_kernel,
        out_shape=jax.ShapeDtypeStruct((m, n), F32),
        grid=(n // tn, m // tm),
        in_specs=[pl.BlockSpec((tm, k), lambda j, i: (i, 0)), pl.BlockSpec((k, tn), lambda j, i: (0, j))],
        out_specs=pl.BlockSpec((tm, tn), lambda j, i: (i, j)),
        compiler_params=_cp(("parallel", "parallel"), VMEM_LIMIT),
        name="in_proj",
    )(x, w)


def _kv_proj_kernel(x_ref, w_ref, o_ref):
    res = _dot(x_ref[...], w_ref[...])
    for t in range(N_KV):
        for g in range(NSA_G):
            c0 = (t * NSA_G + g) * NSA_DH
            o_ref[t, 0, g] = res[:, c0:c0 + NSA_DH].astype(o_ref.dtype)


def _kv_proj(x, w, batch, seq, tm=512):
    k = x.shape[1]
    nt = seq // tm
    return pl.pallas_call(
        _kv_proj_kernel,
        out_shape=jax.ShapeDtypeStruct((N_KV, batch, NSA_G, seq, NSA_DH), BF16),
        grid=(batch, nt),
        in_specs=[pl.BlockSpec((tm, k), lambda b, i: (b * nt + i, 0)),
                  pl.BlockSpec(w.shape, lambda b, i: (0, 0))],
        out_specs=pl.BlockSpec((N_KV, 1, NSA_G, tm, NSA_DH), lambda b, i: (0, b, 0, i, 0)),
        compiler_params=_cp(("parallel", "parallel"), VMEM_LIMIT),
        name="kv_proj",
    )(x, w)


def _pool_kernel(u_ref, halo_ref, w_ref, scale_ref, o_ref, buf_ref, *, tile):
    i = pl.program_id(1)
    u = u_ref[...]
    buf_ref[POOL_HALO:, :] = u
    buf_ref[:POOL_HALO, :] = jnp.where(i > 0, halo_ref[...], 0.0)
    t = i * tile + lax.broadcasted_iota(jnp.int32, (tile, 1), 0)
    outs = []
    for gi, win in enumerate(POOL_WINDOWS):
        cols = slice(gi * POOL_GW, (gi + 1) * POOL_GW)
        s = u[:, cols]
        for j in range(1, win):
            s = s + buf_ref[POOL_HALO - j:POOL_HALO - j + tile, cols]
        cnt = jnp.minimum(t + 1, win).astype(F32)
        pooled = s / cnt - u[:, cols]
        outs.append(_dot(pooled.astype(BF16), w_ref[gi]))
    o_ref[...] = (jnp.concatenate(outs, axis=1) * scale_ref[...]).astype(o_ref.dtype)


def _pool(proj, w_grp, scale, batch, seq, tile=512):
    nt = seq // tile
    hb = tile // POOL_HALO
    return pl.pallas_call(
        functools.partial(_pool_kernel, tile=tile),
        out_shape=jax.ShapeDtypeStruct((batch * seq, POOL_WIDTH), BF16),
        grid=(batch, nt),
        in_specs=[
            pl.BlockSpec((tile, POOL_WIDTH), lambda b, i: (b * nt + i, 0)),
            pl.BlockSpec((POOL_HALO, POOL_WIDTH), lambda b, i: (jnp.maximum((b * nt + i) * hb - 1, 0), 0)),
            pl.BlockSpec((len(POOL_WINDOWS), POOL_GW, POOL_GW), lambda b, i: (0, 0, 0)),
            pl.BlockSpec((1, POOL_WIDTH), lambda b, i: (0, 0)),
        ],
        out_specs=pl.BlockSpec((tile, POOL_WIDTH), lambda b, i: (b * nt + i, 0)),
        scratch_shapes=[pltpu.VMEM((tile + POOL_HALO, POOL_WIDTH), F32)],
        compiler_params=_cp(("parallel", "parallel")),
        name="pool_mixer",
    )(proj, proj, w_grp.astype(BF16), scale.reshape(1, POOL_WIDTH))


def _token_shift(u, halo, mu, first):
    prev_row = jnp.where(first, 0.0, halo[7:8, :])
    rolled = pltpu.roll(u, 1, 0)
    row = lax.broadcasted_iota(jnp.int32, u.shape, 0)
    prev = jnp.where(row == 0, prev_row, rolled)
    return u + (prev - u) * mu


def _rwkv_chunk_kernel(r_ref, k_ref, v_ref, l_ref, rh_ref, kh_ref, vh_ref, lh_ref,
                       mur_ref, muk_ref, muv_ref, mul_ref, w0_ref, wup_ref, a0_ref, aup_ref, gup_ref,
                       kk_ref, ka_ref, rk_ref, bd_ref, qy_ref, mn_ref, g_ref, bonus_ref):
    first = pl.program_id(1) == 0
    c = RW_CHUNK
    r = _token_shift(r_ref[...], rh_ref[...], mur_ref[...], first)
    k = _token_shift(k_ref[...], kh_ref[...], muk_ref[...], first)
    v = _token_shift(v_ref[...], vh_ref[...], muv_ref[...], first)
    lo = _token_shift(l_ref[...], lh_ref[...], mul_ref[...], first)
    wd = lo[:, :DECAY_LORA]
    ad = lo[:, DECAY_LORA:DECAY_LORA + AAA_LORA]
    gd = lo[:, DECAY_LORA + AAA_LORA:RW_LORA]
    z = -(w0_ref[...] + _dot(jnp.tanh(wd), wup_ref[...], HI))
    w_log = -(jnp.maximum(z, 0.0) + jnp.log(1.0 + jnp.exp(-jnp.abs(z)))) - 0.5
    logw = -jnp.exp(w_log)
    a = jax.nn.sigmoid(a0_ref[...] + _dot(ad, aup_ref[...], HI))
    g_ref[0] = _dot(jax.nn.sigmoid(gd), gup_ref[...], HI)
    kkraw = k * kk_ref[...]
    k2 = k * (1.0 + (a - 1.0) * ka_ref[...])
    rkr = r * k2 * rk_ref[...]

    cum_all = logw
    trow = lax.broadcasted_iota(jnp.int32, logw.shape, 0) % c
    step = 1
    while step < c:
        cum_all = cum_all + jnp.where(trow >= step, pltpu.roll(cum_all, step, 0), 0.0)
        step *= 2

    ti = lax.broadcasted_iota(jnp.int32, (c, c), 0)
    si = lax.broadcasted_iota(jnp.int32, (c, c), 1)
    incl = ti >= si
    strict = ti > si
    eye = ti == si
    zeros = jnp.zeros((c, c), F32)
    bd = bd_ref[...]

    def head_sum(t):
        hi, lo = _split(t)
        return _dot(hi, bd) + _dot(lo, bd)

    kk = kkraw / jnp.maximum(jnp.sqrt(head_sum(kkraw * kkraw)), 1e-12)
    bonus_ref[0] = head_sum(rkr) * v
    nchunk = logw.shape[0] // c
    ends = [cum_all[(j + 1) * c - 1:(j + 1) * c, :] for j in range(nchunk)]
    cum_end = jnp.concatenate([jnp.broadcast_to(e, (c, e.shape[1])) for e in ends], axis=0)
    ginv = jnp.exp(-cum_all)
    gtail = jnp.exp(cum_end - cum_all)
    gend = jnp.exp(cum_end)
    kka = kk * a
    at = -kk * jnp.exp(cum_all - logw)
    bt = kka * ginv
    kt = k2 * ginv
    rt = r * jnp.exp(cum_all)
    bhat = kka * gtail
    khat = k2 * gtail

    heads = [(slice(j * c, (j + 1) * c), slice(h * RW_DH, (h + 1) * RW_DH))
             for j in range(nchunk) for h in range(RW_HEADS)]
    stack = lambda x, y, s: jnp.concatenate([x[s], y[s]], axis=0).astype(BF16)
    gram = [_dot_nt(stack(at, rt, s), stack(bt, kt, s)) for s in heads]
    a_ab = [jnp.where(strict, g[:c, :c], 0.0) for g in gram]
    a_ak = [jnp.where(strict, g[:c, c:], 0.0) for g in gram]
    incl2 = (lax.broadcasted_iota(jnp.int32, (c, 2 * c), 0)
             >= lax.broadcasted_iota(jnp.int32, (c, 2 * c), 1) % c)
    a_r = [jnp.where(incl2, g[c:, :], 0.0) for g in gram]
    p = a_ab
    tinv = [eye.astype(F32) + x for x in p]
    for _ in range(int(math.log2(c)) - 1):
        p = [_bdot(x, x) for x in p]
        tinv = [t + _bdot(t, x) for t, x in zip(tinv, p)]
    av = [_bdot(x, v[s]) for x, s in zip(a_ak, heads)]
    w12 = [_bdot(t, jnp.concatenate([at[s], x], axis=1)) for t, x, s in zip(tinv, av, heads)]
    zmat = [jnp.concatenate([w, jnp.concatenate([zeros, v[s]], axis=1)], axis=0).astype(BF16)
            for w, s in zip(w12, heads)]
    out1 = [_dot(x.astype(BF16), z) for x, z in zip(a_r, zmat)]
    out2 = [_dot_tn(stack(bhat, khat, s), z) for s, z in zip(heads, zmat)]
    qy = [o + jnp.concatenate([rt[s], zeros], axis=1) for o, s in zip(out1, heads)]
    for j in range(nchunk):
        qy_ref[0, j * c:(j + 1) * c, :] = jnp.concatenate(qy[j * RW_HEADS:(j + 1) * RW_HEADS], axis=1)
        for h in range(RW_HEADS):
            idx = j * RW_HEADS + h
            diag = jnp.where(eye, gend[heads[idx]], 0.0)
            mn_ref[0, j, h] = out2[idx] + jnp.concatenate([diag, zeros], axis=1)


def _rwkv_scan_kernel(qy_ref, mn_ref, g_ref, bonus_ref, gnw_ref, gnb_ref, bd_ref, y_ref, st_ref, *, batch):
    @pl.when(pl.program_id(0) == 0)
    def _():
        st_ref[...] = jnp.zeros_like(st_ref)

    bd = bd_ref[...]

    def head_mean(t):
        hi, lo = _split(t)
        return (_dot(hi, bd) + _dot(lo, bd)) * (1.0 / RW_DH)

    pairs = [(b, h) for b in range(batch) for h in range(RW_HEADS)]
    sts = [st_ref[b * RW_HEADS + h] for b, h in pairs]
    ys = [_bdot(qy_ref[b, :, 2 * RW_DH * h:2 * RW_DH * h + RW_DH], st)
          + qy_ref[b, :, 2 * RW_DH * h + RW_DH:2 * RW_DH * (h + 1)] for (b, h), st in zip(pairs, sts)]
    for (b, h), st in zip(pairs, sts):
        mn = mn_ref[b, 0, h]
        st_ref[b * RW_HEADS + h] = _dot3(mn[:, :RW_DH], st) + mn[:, RW_DH:]
    for b in range(batch):
        y = jnp.concatenate(ys[b * RW_HEADS:(b + 1) * RW_HEADS], axis=1)
        dev = y - head_mean(y)
        yn = dev * lax.rsqrt(head_mean(dev * dev) + RW_GN_EPS) * gnw_ref[...] + gnb_ref[...]
        y_ref[b] = ((yn + bonus_ref[b]) * g_ref[b]).astype(y_ref.dtype)


def _rwkv(proj, p, batch, seq):
    head_ones = jnp.asarray(np.kron(np.eye(RW_HEADS), np.ones((RW_DH, RW_DH))), BF16)
    c = RW_CHUNK
    nc = seq // c
    tb = RW_NB * c
    nt = seq // tb
    hb = tb // 8
    row512 = lambda col: pl.BlockSpec((tb, RW_WIDTH), lambda b, i: (b * nt + i, col))
    halo512 = lambda col: pl.BlockSpec((8, RW_WIDTH), lambda b, i: (jnp.maximum((b * nt + i) * hb - 1, 0), col))
    const = lambda shape: pl.BlockSpec(shape, lambda b, i: (0,) * len(shape))
    vec = const((1, RW_WIDTH))
    out_row = lambda w: pl.BlockSpec((1, tb, w), lambda b, i: (b, i, 0))
    qy, mn, g, bonus = pl.pallas_call(
        _rwkv_chunk_kernel,
        out_shape=(jax.ShapeDtypeStruct((batch, seq, 2 * RW_WIDTH), F32),
                   jax.ShapeDtypeStruct((batch, nc, RW_HEADS, RW_DH, 2 * RW_DH), F32),
                   jax.ShapeDtypeStruct((batch, seq, RW_WIDTH), F32),
                   jax.ShapeDtypeStruct((batch, seq, RW_WIDTH), F32)),
        grid=(batch, nt),
        in_specs=[row512(C_R // RW_WIDTH), row512(C_K // RW_WIDTH), row512(C_V // RW_WIDTH),
                  pl.BlockSpec((tb, LORA_PAD), lambda b, i: (b * nt + i, C_LORA // LORA_PAD)),
                  halo512(C_R // RW_WIDTH), halo512(C_K // RW_WIDTH), halo512(C_V // RW_WIDTH),
                  pl.BlockSpec((8, LORA_PAD), lambda b, i: (jnp.maximum((b * nt + i) * hb - 1, 0), C_LORA // LORA_PAD)),
                  vec, vec, vec, const((1, LORA_PAD)),
                  vec, const((DECAY_LORA, RW_WIDTH)), vec, const((AAA_LORA, RW_WIDTH)), const((GATE_LORA, RW_WIDTH)),
                  vec, vec, vec, const((RW_WIDTH, RW_WIDTH))],
        out_specs=(out_row(2 * RW_WIDTH),
                   pl.BlockSpec((1, RW_NB, RW_HEADS, RW_DH, 2 * RW_DH), lambda b, i: (b, i, 0, 0, 0)),
                   out_row(RW_WIDTH), out_row(RW_WIDTH)),
        compiler_params=_cp(("parallel", "parallel"), VMEM_LIMIT),
        name="rwkv_chunk",
    )(proj, proj, proj, proj, proj, proj, proj, proj,
      p["mu_r"], p["mu_k"], p["mu_v"], p["mu_l"], p["w0"], p["w_up"], p["a0"], p["a_up"], p["g_up"],
      p["k_k"], p["k_a"], p["r_k"], head_ones)

    full = lambda w: pl.BlockSpec((batch, c, w), lambda i: (0, i, 0))
    return pl.pallas_call(
        functools.partial(_rwkv_scan_kernel, batch=batch),
        out_shape=jax.ShapeDtypeStruct((batch, seq, RW_WIDTH), BF16),
        grid=(nc,),
        in_specs=[full(2 * RW_WIDTH),
                  pl.BlockSpec((batch, 1, RW_HEADS, RW_DH, 2 * RW_DH), lambda i: (0, i, 0, 0, 0)),
                  full(RW_WIDTH), full(RW_WIDTH),
                  pl.BlockSpec((1, RW_WIDTH), lambda i: (0, 0)), pl.BlockSpec((1, RW_WIDTH), lambda i: (0, 0)),
                  pl.BlockSpec((RW_WIDTH, RW_WIDTH), lambda i: (0, 0))],
        out_specs=full(RW_WIDTH),
        scratch_shapes=[pltpu.VMEM((batch * RW_HEADS, RW_DH, RW_DH), F32)],
        compiler_params=_cp(("arbitrary",), VMEM_LIMIT),
        name="rwkv_scan",
    )(qy, mn, g, bonus, p["gn_w"], p["gn_b"], head_ones).reshape(batch * seq, RW_WIDTH)


def _gelu_tanh(x):
    return 0.5 * x * (1.0 + jnp.tanh(math.sqrt(2.0 / math.pi) * (x + 0.044715 * (x * x * x))))


def _compress_kernel(x_ref, pos_ref, w1_ref, w2_ref, o_ref):
    half = CMP_STRIDE * NSA_DH
    x = x_ref[0, 0, 0]
    w1 = w1_ref[0]
    posb = _dot(pos_ref[0], w1)[0:1, :]
    h1 = _dot(x, w1[:half])
    h2 = _dot(x, w1[half:])
    n = h2.shape[0]
    row = lax.broadcasted_iota(jnp.int32, h2.shape, 0)
    h2s = jnp.where(row < n - 1, pltpu.roll(h2, n - 1, 0), 0.0)
    hid = _gelu_tanh(h1 + h2s + posb)
    o_ref[0, 0, 0] = _dot(hid.astype(BF16), w2_ref[0]).astype(o_ref.dtype)


def _compress(xkv, pos, w1, w2, batch, seq):
    nr = seq // CMP_STRIDE
    wide = CMP_STRIDE * NSA_DH
    return pl.pallas_call(
        _compress_kernel,
        out_shape=jax.ShapeDtypeStruct((2, batch, NSA_G, nr, NSA_DH), BF16),
        grid=(2, batch, NSA_G),
        in_specs=[pl.BlockSpec((1, 1, 1, nr, wide), lambda t, b, g: (t, b, g, 0, 0)),
                  pl.BlockSpec((1, 8, 2 * wide), lambda t, b, g: (t, 0, 0)),
                  pl.BlockSpec((1, 2 * wide, CMP_HIDDEN), lambda t, b, g: (t, 0, 0)),
                  pl.BlockSpec((1, CMP_HIDDEN, NSA_DH), lambda t, b, g: (t, 0, 0))],
        out_specs=pl.BlockSpec((1, 1, 1, nr, NSA_DH), lambda t, b, g: (t, b, g, 0, 0)),
        compiler_params=_cp(("parallel", "parallel", "parallel")),
        name="nsa_compress",
    )(xkv, pos, w1, w2)


def _softmax_parts(s):
    m = jnp.max(s, axis=1, keepdims=True)
    e = jnp.exp(s - m)
    return m, e, jnp.sum(e, axis=1, keepdims=True)


def _lane_tile_fold(x, op, init):
    for t in range(x.shape[1] // 128):
        init = op(init, x[:, 128 * t:128 * (t + 1)])
    return init


def _nsa_kernel(q_ref, gate_ref, kc_ref, vc_ref, ks_ref, vs_ref, kw_ref, vw_ref,
                tblc_ref, tbln_ref, tblw_ref, ovt_ref, gexp_ref, o_ref, s_ref, *, ncmp):
    nb = NSA_NB
    blocks = [pl.program_id(2) * nb + u for u in range(nb)]
    rows = NSA_HG * QB
    each = lambda f, *ls: [f(*xs) for xs in zip(*ls)]
    qt = q_ref[...]
    qs = [jnp.concatenate([qt[u * QB:(u + 1) * QB, NSA_DH * h:NSA_DH * (h + 1)] for h in range(NSA_HG)], axis=0)
          * (NSA_DH ** -0.5 * LOG2E) for u in range(nb)]
    qb = each(lambda x: x.astype(BF16), qs)
    rmax = lambda x: jnp.max(x, axis=1, keepdims=True)
    rsum = lambda x: jnp.sum(x, axis=1, keepdims=True)

    win_w = WINDOW + QB
    kw = [kw_ref[0, 0, pl.ds(pl.multiple_of(i * QB, QB), win_w), :] for i in blocks]
    vw = [vw_ref[0, 0, pl.ds(pl.multiple_of(i * QB, QB), win_w), :] for i in blocks]
    flag = lax.broadcasted_iota(jnp.int32, (rows, kw_ref.shape[3] - NSA_DH), 1) == 0
    q_win = each(lambda x: jnp.concatenate([x, jnp.where(flag, NEG, 0.0)], axis=1).astype(BF16), qs)
    tblw = tblw_ref[0]
    s_w = each(lambda x, k: _dot_nt(x, k) + tblw, q_win, kw)
    m_w = each(rmax, s_w)
    e_w = each(lambda s, m: jnp.exp2(s - m), s_w, m_w)
    pv = lambda x: x[:, :NSA_DH] / x[:, NSA_DH:NSA_DH + 1]
    o_w = each(lambda e, v: pv(_dot(e.astype(BF16), v)), e_w, vw)

    kc = kc_ref[0, 0]
    vc = vc_ref[0, 0]
    tblc = tblc_ref[0]
    cidx = lax.broadcasted_iota(jnp.int32, (rows, ncmp), 1)
    qrow = lax.broadcasted_iota(jnp.int32, (rows, 1), 0) % QB
    lc = [jnp.where(cidx < (QB // CMP_STRIDE) * i + CMP_AHEAD,
                    _dot_nt(x, kc) + pltpu.roll(tblc, (4 * i - CMP_NEAR // 2 + ncmp) % ncmp, 1), NEG)
          for x, i in zip(qb, blocks)]
    m_c = each(rmax, lc)
    e_c = each(lambda s, m: jnp.exp2(s - m), lc, m_c)
    den_c = each(rsum, e_c)
    pc = [e * jnp.where(i * QB + qrow >= CMP_BLOCK - 1, 1.0 / d, 0.0) for e, d, i in zip(e_c, den_c, blocks)]
    o_c = each(lambda x: _dot(x.astype(BF16), vc), pc)
    pcs_hi, pcs_lo = _split(jnp.concatenate(
        each(lambda x: x[0:QB] + x[QB:2 * QB] + x[2 * QB:3 * QB] + x[3 * QB:4 * QB], pc), axis=0))
    ovt = ovt_ref[...]
    imp = _dot_nt(ovt, pcs_hi) + _dot_nt(ovt, pcs_lo)

    nslc = ovt.shape[0]
    nidx = lax.broadcasted_iota(jnp.int32, (nslc, nb * QB), 0)
    cur = blocks[0] + lax.broadcasted_iota(jnp.int32, (nslc, nb * QB), 1) // QB
    forced = (nidx == 0) | (nidx == cur) | (nidx == cur - 1)
    work = jnp.where(nidx <= cur, imp + jnp.where(forced, FORCE_BONUS, 0.0), -1.0)
    sel_all = jnp.zeros((nslc, nb * QB), F32)
    for _ in range(N_SELECT):
        m = jnp.max(work, axis=0, keepdims=True)
        first = jnp.min(jnp.where(work == m, nidx, nslc), axis=0, keepdims=True)
        pick = nidx == first
        sel_all = jnp.where(pick & (m >= 0.0), 1.0, sel_all)
        work = jnp.where(pick, -2.0, work)
    sel_t = sel_all.T

    ind_w = ks_ref.shape[3] - NSA_DH
    selq = [sel_t[u * QB:(u + 1) * QB] for u in range(nb)]
    if nslc < ind_w:
        selq = each(lambda x: jnp.concatenate([x, jnp.zeros((QB, ind_w - nslc), F32)], axis=1), selq)
    bidx = lax.broadcasted_iota(jnp.int32, (QB, ind_w), 1)
    tile4 = lambda t: jnp.concatenate([t] * NSA_HG, axis=0)
    with_mask = lambda x, keep: jnp.concatenate([x, tile4(jnp.where(keep, 0.0, NEG))], axis=1).astype(BF16)
    q_sel = each(lambda x, s: with_mask(x, s > 0.0), qs, selq)
    q_far = jnp.concatenate([with_mask(x, (s > 0.0) & (bidx <= i - NEAR_BLOCKS))
                             for x, s, i in zip(qs, selq, blocks)], axis=0)
    near_w = NEAR_BLOCKS * SLC_BLOCK
    kn = [ks_ref[0, 0, pl.ds(pl.multiple_of(i * QB, QB), near_w), :] for i in blocks]
    vn = [vs_ref[0, 0, pl.ds(pl.multiple_of(i * QB, QB), near_w), :] for i in blocks]
    tbln = tbln_ref[0]
    s_near = each(lambda x, k: _dot_nt(x, k) + tbln, q_sel, kn)
    m_near = jnp.concatenate(each(rmax, s_near), axis=0)

    far_w = FAR_CHUNK_BLOCKS * SLC_BLOCK
    pad_s = (NEAR_BLOCKS - 1) * SLC_BLOCK
    n_far = jnp.maximum(blocks[-1] - (NEAR_BLOCKS - 1) + FAR_CHUNK_BLOCKS - 1, 0) // FAR_CHUNK_BLOCKS

    def far_logits(j, mvec):
        start = pl.multiple_of(j * far_w, far_w)
        kf = ks_ref[0, 0, pl.ds(pl.multiple_of(pad_s + j * far_w, SLC_BLOCK), far_w), :]
        s = _dot_nt(q_far, kf)
        s_ref[:, pl.ds(start, far_w)] = s
        return _lane_tile_fold(s, jnp.maximum, mvec)

    mvec = lax.fori_loop(0, n_far, far_logits, jnp.full((nb * rows, 128), NEG, F32))
    m_s = jnp.maximum(m_near, rmax(mvec))

    e_near = [jnp.exp2(s - m_s[u * rows:(u + 1) * rows]) for u, s in enumerate(s_near)]
    acc0 = jnp.concatenate(each(lambda e, v: _dot(e.astype(BF16), v), e_near, vn), axis=0)

    def far_values(j, acc):
        start = pl.multiple_of(j * far_w, far_w)
        vf = vs_ref[0, 0, pl.ds(pl.multiple_of(pad_s + j * far_w, SLC_BLOCK), far_w), :]
        e = jnp.exp2(s_ref[:, pl.ds(start, far_w)] - m_s)
        return acc + _dot(e.astype(BF16), vf)

    o_s = pv(lax.fori_loop(0, n_far, far_values, acc0))

    g_hi, g_lo = _split(jax.nn.sigmoid(gate_ref[0, 0]))
    gexp = gexp_ref[...]
    ge = _dot(g_hi, gexp) + _dot(g_lo, gexp)
    for u in range(nb):
        outs = []
        for h in range(NSA_HG):
            r0 = slice(h * QB, (h + 1) * QB)
            gt = lambda br: ge[u * QB:(u + 1) * QB, (3 * h + br) * 128:(3 * h + br) * 128 + NSA_DH]
            outs.append(gt(0) * o_c[u][r0] + gt(1) * o_s[u * rows + h * QB:u * rows + (h + 1) * QB]
                        + gt(2) * o_w[u][r0])
        o_ref[u * QB:(u + 1) * QB, :] = jnp.concatenate(outs, axis=1).astype(o_ref.dtype)


def _bias_tables(rel_bias, ncmp):
    def table(dist, keep, base):
        onehot = jnp.asarray(np.eye(N_BUCKETS, dtype=np.float32)[_t5_bucket_np(dist)])
        tbl = jnp.einsum("qkb,bh->qkh", onehot, rel_bias, precision=HI)
        tbl = (tbl - base) * LOG2E
        tbl = jnp.where(jnp.asarray(keep)[..., None], tbl, NEG)
        k = dist.shape[1]
        return tbl.transpose(2, 0, 1).reshape(NSA_G, NSA_HG * QB, k)

    far = rel_bias[N_BUCKETS - 1]
    qi = np.arange(QB)[:, None]
    dist_c = qi - CMP_STRIDE * (np.arange(CMP_NEAR)[None, :] - CMP_NEAR // 2) - (CMP_BLOCK - 1)
    tblc = table(dist_c, dist_c >= 0, far)
    tblc = tblc * jnp.asarray(np.arange(CMP_NEAR) < CMP_NEAR // 2 + CMP_AHEAD, F32)
    tblc = jnp.pad(tblc, ((0, 0), (0, 0), (0, ncmp - CMP_NEAR)))
    jn = np.arange(NEAR_BLOCKS * SLC_BLOCK)[None, :]
    dist_n = (NEAR_BLOCKS - 1) * SLC_BLOCK + qi - jn
    tbln = table(dist_n, dist_n >= 0, far)
    jw = np.arange(WINDOW + QB)[None, :]
    dist_w = WINDOW + qi - jw
    tblw = table(dist_w, (dist_w >= 0) & (dist_w < WINDOW), 0.0)
    return tblc, tbln, tblw


def _nsa(proj, gate_logits, kvc, ks, vs, kw, vw, tables, batch, seq):
    nq = seq // QB
    ncmp = seq // CMP_STRIDE
    nslc = seq // SLC_BLOCK
    tblc, tbln, tblw = tables
    cstart = np.arange(ncmp) * CMP_STRIDE
    sstart = np.arange(nslc) * SLC_BLOCK
    overlap_t = ((cstart[None, :] <= sstart[:, None] + SLC_BLOCK - 1)
                 & (cstart[None, :] + CMP_BLOCK - 1 >= sstart[:, None])
                 & (cstart[None, :] + CMP_BLOCK <= seq)).astype(np.float32)
    pad_s = ks.shape[2] - seq
    blk = (np.arange(seq)[:, None] // SLC_BLOCK == np.arange(NSA_DH)[None, :]).astype(np.float32)
    blk = np.concatenate([np.ones((pad_s, NSA_DH), np.float32), blk], axis=0)
    ks = jnp.concatenate([ks, jnp.broadcast_to(jnp.asarray(blk, BF16), ks.shape[:2] + blk.shape)], axis=-1)
    flag = np.zeros((kw.shape[2], NSA_DH), np.float32)
    flag[:kw.shape[2] - seq, 0] = 1.0
    kw = jnp.concatenate([kw, jnp.broadcast_to(jnp.asarray(flag, BF16), kw.shape[:2] + flag.shape)], axis=-1)
    ones_col = np.zeros((1, NSA_DH), np.float32)
    ones_col[0, 0] = 1.0
    with_ones = lambda t: jnp.concatenate(
        [t, jnp.broadcast_to(jnp.asarray(ones_col, BF16), t.shape[:3] + (NSA_DH,))], axis=-1)
    vs, vw = with_ones(vs), with_ones(vw)
    rows = NSA_HG * QB
    n_gate = 3 * NSA_HG
    gate_spread = (np.arange(n_gate * 128)[None, :] // 128 == np.arange(n_gate)[:, None]) \
        & (np.arange(n_gate * 128)[None, :] % 128 < NSA_DH)
    kv_spec = lambda t: pl.BlockSpec((1, 1) + t.shape[2:], lambda b, g, i: (b, g, 0, 0))
    tbl_spec = lambda k: pl.BlockSpec((1, rows, k), lambda b, g, i: (g, 0, 0))
    qcol = C_Q // (NSA_HG * NSA_DH)
    return pl.pallas_call(
        functools.partial(_nsa_kernel, ncmp=ncmp),
        out_shape=jax.ShapeDtypeStruct((batch * seq, NSA_HEADS * NSA_DH), BF16),
        grid=(batch, NSA_G, nq // NSA_NB),
        in_specs=[pl.BlockSpec((NSA_NB * QB, NSA_HG * NSA_DH), lambda b, g, i: (b * (nq // NSA_NB) + i, qcol + g)),
                  pl.BlockSpec((1, 1, NSA_NB * QB, 3 * NSA_HG), lambda b, g, i: (b, g, i, 0)),
                  pl.BlockSpec((1, 1, ncmp, NSA_DH), lambda b, g, i: (b, g, 0, 0)),
                  pl.BlockSpec((1, 1, ncmp, NSA_DH), lambda b, g, i: (b, g, 0, 0)),
                  kv_spec(ks), kv_spec(vs), kv_spec(kw), kv_spec(vw),
                  tbl_spec(ncmp), tbl_spec(NEAR_BLOCKS * SLC_BLOCK), tbl_spec(WINDOW + QB),
                  pl.BlockSpec((nslc, ncmp), lambda b, g, i: (0, 0)),
                  pl.BlockSpec(gate_spread.shape, lambda b, g, i: (0, 0))],
        out_specs=pl.BlockSpec((NSA_NB * QB, NSA_HG * NSA_DH), lambda b, g, i: (b * (nq // NSA_NB) + i, g)),
        scratch_shapes=[pltpu.VMEM((NSA_NB * rows, seq), F32)],
        compiler_params=_cp(("parallel", "parallel", "arbitrary"), VMEM_LIMIT),
        name="nsa_attention",
    )(proj, gate_logits, kvc[0], kvc[1], ks, vs, kw, vw, tblc, tbln, tblw, jnp.asarray(overlap_t, BF16),
      jnp.asarray(gate_spread, BF16))


def _merge_kernel(h_ref, yp_ref, yr_ref, yn_ref, x_ref, wbp_ref, wbr_ref, wbn_ref, wm_ref, bm_ref, wo_ref,
                  gn_ref, wr_ref, br_ref, xo_ref, h2_ref, route_ref):
    d = D_MODEL
    gl = jax.nn.sigmoid(_dot(h_ref[...], wm_ref[...]) + bm_ref[...])
    merged = (gl[:, :d] * _dot(yp_ref[...], wbp_ref[...]) + gl[:, d:2 * d] * _dot(yr_ref[...], wbr_ref[...])
              + gl[:, 2 * d:] * _dot(yn_ref[...], wbn_ref[...]))
    x = x_ref[...] + _dot(merged.astype(BF16), wo_ref[...])
    xo_ref[...] = x
    h2 = _rms(x, gn_ref[...])
    for s in range(h2_ref.shape[1]):
        h2_ref[:, s, :] = h2[:, s * 128:(s + 1) * 128]

    h2_hi, h2_lo = _split(h2)
    nl = br_ref.shape[1]
    both = _dot(h2_hi, wr_ref[...])
    logits = both[:, :nl] + (both[:, nl:] + _dot(h2_lo, wr_ref[:, :nl])) + br_ref[...]
    lane = lax.broadcasted_iota(jnp.int32, logits.shape, 1)
    big = logits.shape[1]
    lg = jnp.where(lane < N_GROUPS, logits, NEG)
    mg = jnp.max(lg, axis=1, keepdims=True)
    p_top = 1.0 / jnp.sum(jnp.exp(lg - mg), axis=1, keepdims=True)
    grp = jnp.min(jnp.where(lg == mg, lane, big), axis=1, keepdims=True)
    lo = N_GROUPS + EPG * grp
    le = jnp.where((lane >= lo) & (lane < lo + EPG), logits, NEG)
    e1 = jnp.max(le, axis=1, keepdims=True)
    i1 = jnp.min(jnp.where(le == e1, lane, big), axis=1, keepdims=True)
    le = jnp.where(lane == i1, NEG, le)
    e2 = jnp.max(le, axis=1, keepdims=True)
    i2 = jnp.min(jnp.where(le == e2, lane, big), axis=1, keepdims=True)
    t = jnp.exp(e2 - e1)
    w1 = p_top / (1.0 + t)
    w2 = p_top * t / (1.0 + t)
    route_ref[...] = jnp.where(lane == 0, (i1 - N_GROUPS).astype(F32),
                               jnp.where(lane == 1, (i2 - N_GROUPS).astype(F32),
                                         jnp.where(lane == 2, w1, jnp.where(lane == 3, w2, 0.0))))


def _merge(h, y_pool, y_rwkv, y_nsa, x, p, tm=256):
    n, d = x.shape
    row = lambda w: pl.BlockSpec((tm, w), lambda i: (i, 0))
    const = lambda a: pl.BlockSpec(a.shape, lambda i: (0, 0))
    ws = [p["wb_pool"], p["wb_rwkv"], p["wb_nsa"], p["w_merge"], p["b_merge"], p["w_out"],
          p["norm_ffn"], p["w_router"], p["b_router"]]
    return pl.pallas_call(
        _merge_kernel,
        out_shape=(jax.ShapeDtypeStruct((n, d), F32), jax.ShapeDtypeStruct((n, d // 128, 128), F32),
                   jax.ShapeDtypeStruct((n, 128), F32)),
        grid=(n // tm,),
        in_specs=[row(d), row(POOL_WIDTH), row(RW_WIDTH), row(d), row(d)] + [const(w) for w in ws],
        out_specs=(row(d), pl.BlockSpec((tm, d // 128, 128), lambda i: (i, 0, 0)), row(128)),
        compiler_params=_cp(("parallel",), VMEM_LIMIT),
        name="merge_router",
    )(h, y_pool, y_rwkv, y_nsa, x, *ws)


def _expert_kernel(te_ref, nt_ref, tok_ref, h_hbm, w_ref, wg_ref, wu_ref, wd_ref, o_ref,
                   xbuf, sem, wg_s, wu_s, wd_s):
    i = pl.program_id(0)
    tm = MOE_TM
    n_tiles = nt_ref[0]

    def row_copy(tile, slot, r):
        return pltpu.make_async_copy(h_hbm.at[tok_ref[tile * tm + r]], xbuf.at[slot, r], sem.at[slot])

    def tile_wait(slot):
        pltpu.make_async_copy(h_hbm.at[pl.ds(0, tm)], xbuf.at[slot], sem.at[slot]).wait()

    @pl.when((i == 0) & (n_tiles > 0))
    def _():
        def body(r, carry):
            row_copy(0, 0, r).start()
            return carry
        lax.fori_loop(0, tm, body, 0, unroll=8)

    @pl.when((i == 0) | (te_ref[i] != te_ref[jnp.maximum(i - 1, 0)]))
    def _():
        wg_s[...] = wg_ref[0].astype(BF16)
        wu_s[...] = wu_ref[0].astype(BF16)
        wd_s[...] = wd_ref[0].astype(BF16)

    @pl.when(i < n_tiles)
    def _():
        slot = i % 2
        quarter = tm // 4

        def fetch_next(part):
            for r in range(part * quarter, (part + 1) * quarter):
                row_copy(i + 1, 1 - slot, r).start()

        tile_wait(slot)
        xb = jnp.concatenate([xbuf[slot, :, s, :] for s in range(xbuf.shape[2])], axis=1).astype(BF16)
        fetch_next(0)
        gate = _dot(xb, wg_s[...])
        fetch_next(1)
        up = _dot(xb, wu_s[...])
        fetch_next(2)
        hid = (gate * jax.nn.sigmoid(gate) * up).astype(BF16)
        fetch_next(3)
        o_ref[...] = w_ref[...] * _dot(hid, wd_s[...])

    @pl.when((i == n_tiles) & (n_tiles > 0))
    def _():
        tile_wait(i % 2)

    @pl.when(i >= nt_ref[0])
    def _():
        o_ref[...] = jnp.zeros_like(o_ref)


def _experts(h2, rowtok, roww, tile_expert, n_tiles, wg, wu, wd):
    d = h2.shape[1] * h2.shape[2]
    r = rowtok.shape[0]
    tm = MOE_TM
    return pl.pallas_call(
        _expert_kernel,
        out_shape=jax.ShapeDtypeStruct((r, d), F32),
        grid_spec=pltpu.PrefetchScalarGridSpec(
            num_scalar_prefetch=3,
            grid=(r // tm,),
            in_specs=[pl.BlockSpec(memory_space=pl.ANY),
                      pl.BlockSpec((tm, 1), lambda i, te, nt, tok: (i, 0)),
                      pl.BlockSpec((1, d, D_EXPERT), lambda i, te, nt, tok: (te[i], 0, 0)),
                      pl.BlockSpec((1, d, D_EXPERT), lambda i, te, nt, tok: (te[i], 0, 0)),
                      pl.BlockSpec((1, D_EXPERT, d), lambda i, te, nt, tok: (te[i], 0, 0))],
            out_specs=pl.BlockSpec((tm, d), lambda i, te, nt, tok: (i, 0)),
            scratch_shapes=[pltpu.VMEM((2, tm) + h2.shape[1:], F32), pltpu.SemaphoreType.DMA((2,)),
                            pltpu.VMEM((d, D_EXPERT), BF16), pltpu.VMEM((d, D_EXPERT), BF16),
                            pltpu.VMEM((D_EXPERT, d), BF16)]),
        compiler_params=_cp(("arbitrary",), VMEM_LIMIT),
        name="moe_experts",
    )(tile_expert, n_tiles, rowtok, h2, roww, wg, wu, wd)


def _moe(h2, route, wg, wu, wd):
    n = h2.shape[0]
    tm = MOE_TM
    r = 2 * n + (N_EXPERTS + 1) * tm
    ids = route[:, 0:2].astype(jnp.int32).reshape(-1)
    wts = route[:, 2:4].reshape(-1)
    onehot = (ids[:, None] == jnp.arange(N_EXPERTS)[None, :]).astype(jnp.int32)
    rank = jnp.sum((jnp.cumsum(onehot, axis=0) - onehot) * onehot, axis=1)
    counts = jnp.sum(onehot, axis=0)
    tiles = (counts + tm - 1) // tm
    tile_end = jnp.cumsum(tiles)
    starts = (tile_end - tiles) * tm
    pos = starts[ids] + rank
    row_assign = jnp.full((r,), -1, jnp.int32).at[pos].set(jnp.arange(2 * n, dtype=jnp.int32))
    rowtok = jnp.maximum(row_assign, 0) // 2
    roww = jnp.where(row_assign >= 0, wts[jnp.maximum(row_assign, 0)], 0.0)
    n_tiles = tile_end[-1:].astype(jnp.int32)
    tile_expert = jnp.minimum(jnp.sum(tile_end[None, :] <= jnp.arange(r // tm)[:, None], axis=1),
                              N_EXPERTS - 1).astype(jnp.int32)
    ys = _experts(h2, rowtok, roww.reshape(r, 1), tile_expert, n_tiles, wg, wu, wd)
    return ys[pos[0::2]], ys[pos[1::2]]


def _layer_params(l, a):
    f = lambda t: t[l]
    row = lambda t: t[l].reshape(1, -1)
    w_in = a["w_in"][l]
    w_in_p = jnp.concatenate([w_in[:, :SRC_RW_END], jnp.zeros((D_MODEL, C_Q - SRC_RW_END), F32),
                              w_in[:, SRC_Q:SRC_KV], w_in[:, SRC_GATE:],
                              jnp.zeros((D_MODEL, P_COLS - C_GATE - (w_in.shape[1] - SRC_GATE)), F32)], axis=1)
    mu = a["rw_mu"][l]
    wb = a["w_branch"][l].astype(BF16)
    w_router = jnp.zeros((D_MODEL, 128), F32)
    w_router = w_router.at[:, :N_GROUPS].set(a["w_router_grp"][l]).at[:, N_GROUPS:N_GROUPS + N_EXPERTS].set(
        a["w_router_exp"][l])
    b_router = jnp.zeros((1, 128), F32)
    b_router = b_router.at[0, :N_GROUPS].set(a["b_router_grp"][l]).at[0, N_GROUPS:N_GROUPS + N_EXPERTS].set(
        a["b_router_exp"][l])
    pos = jnp.stack([a["cmp_pos_k"][l].reshape(-1), a["cmp_pos_v"][l].reshape(-1)])
    return {
        "w_in": w_in_p.astype(BF16), "w_kv": w_in[:, SRC_KV:SRC_GATE].astype(BF16),
        "pool_w": f(a["pool_w"]), "pool_scale": f(a["pool_scale"]),
        "mu_r": mu[None, 0:RW_WIDTH], "mu_k": mu[None, RW_WIDTH:2 * RW_WIDTH],
        "mu_v": mu[None, 2 * RW_WIDTH:3 * RW_WIDTH],
        "mu_l": jnp.concatenate([mu[3 * RW_WIDTH:], jnp.zeros((LORA_PAD - RW_LORA,), F32)])[None],
        "w0": row(a["rw_w0"]), "w_up": f(a["rw_w_up"]), "a0": row(a["rw_a0"]), "a_up": f(a["rw_a_up"]),
        "g_up": f(a["rw_g_up"]), "k_k": row(a["rw_k_k"]), "k_a": row(a["rw_k_a"]), "r_k": row(a["rw_r_k"]),
        "gn_w": row(a["rw_gn_w"]), "gn_b": row(a["rw_gn_b"]),
        "cmp_pos": jnp.broadcast_to(pos[:, None, :], (2, 8, pos.shape[1])).astype(BF16),
        "cmp_w1": jnp.stack([a["cmp_w1_k"][l], a["cmp_w1_v"][l]]).astype(BF16),
        "cmp_w2": jnp.stack([a["cmp_w2_k"][l], a["cmp_w2_v"][l]]).astype(BF16),
        "wb_pool": wb[:POOL_WIDTH], "wb_rwkv": wb[POOL_WIDTH:POOL_WIDTH + RW_WIDTH],
        "wb_nsa": wb[POOL_WIDTH + RW_WIDTH:],
        "w_merge": a["w_merge"][l].astype(BF16), "b_merge": row(a["b_merge"]),
        "w_out": a["w_out"][l].astype(BF16), "norm_ffn": row(a["norm_ffn"]),
        "w_router": jnp.concatenate(_split(w_router), axis=1), "b_router": b_router,
        "w_gate": a["w_exp_gate"][l], "w_up_e": a["w_exp_up"][l], "w_down": a["w_exp_down"][l],
    }


def _mixers(proj, kv, p, tables, batch, seq):
    y_pool = _pool(proj, p["pool_w"], p["pool_scale"], batch, seq)
    y_rwkv = _rwkv(proj, p, batch, seq)
    xkv = kv[0:2].reshape(2, batch, NSA_G, seq // CMP_STRIDE, CMP_STRIDE * NSA_DH)
    kvc = _compress(xkv, p["cmp_pos"], p["cmp_w1"], p["cmp_w2"], batch, seq)
    pad = lambda t, n: jnp.pad(t, ((0, 0), (0, 0), (n, 0), (0, 0)))
    pad_s = (NEAR_BLOCKS - 1) * SLC_BLOCK
    gate_logits = proj[:, C_GATE:C_GATE + 3 * NSA_HEADS].reshape(batch, seq, NSA_G, 3 * NSA_HG).transpose(0, 2, 1, 3)
    y_nsa = _nsa(proj, gate_logits, kvc, pad(kv[2], pad_s), pad(kv[3], pad_s), pad(kv[4], WINDOW),
                 pad(kv[5], WINDOW), tables, batch, seq)
    return y_pool, y_rwkv, y_nsa


def kernel(x, rel_bias, norm_mix, w_in, pool_w, pool_scale, rw_mu, rw_w0, rw_w_up, rw_a0, rw_a_up, rw_g_up, rw_k_k, rw_k_a, rw_r_k, rw_gn_w, rw_gn_b, cmp_pos_k, cmp_w1_k, cmp_w2_k, cmp_pos_v, cmp_w1_v, cmp_w2_v, w_branch, w_merge, b_merge, w_out, norm_ffn, w_router_grp, b_router_grp, w_router_exp, b_router_exp, w_exp_gate, w_exp_up, w_exp_down, norm_final):
    a = dict(w_in=w_in, pool_w=pool_w, pool_scale=pool_scale, rw_mu=rw_mu, rw_w0=rw_w0, rw_w_up=rw_w_up,
             rw_a0=rw_a0, rw_a_up=rw_a_up, rw_g_up=rw_g_up, rw_k_k=rw_k_k, rw_k_a=rw_k_a, rw_r_k=rw_r_k,
             rw_gn_w=rw_gn_w, rw_gn_b=rw_gn_b, cmp_pos_k=cmp_pos_k, cmp_w1_k=cmp_w1_k, cmp_w2_k=cmp_w2_k,
             cmp_pos_v=cmp_pos_v, cmp_w1_v=cmp_w1_v, cmp_w2_v=cmp_w2_v, w_branch=w_branch, w_merge=w_merge,
             b_merge=b_merge, w_out=w_out, norm_ffn=norm_ffn, w_router_grp=w_router_grp,
             b_router_grp=b_router_grp, w_router_exp=w_router_exp, b_router_exp=b_router_exp,
             w_exp_gate=w_exp_gate, w_exp_up=w_exp_up, w_exp_down=w_exp_down)
    batch, seq, d = x.shape
    depth = norm_mix.shape[0]
    tables = _bias_tables(rel_bias, seq // CMP_STRIDE)
    xf = x.reshape(batch * seq, d)
    h = _norm(xf, norm_mix[0], BF16)
    for l in range(depth):
        p = _layer_params(l, a)
        proj = _matmul(h, p["w_in"])
        kv = _kv_proj(h, p["w_kv"], batch, seq)
        y_pool, y_rwkv, y_nsa = _mixers(proj, kv, p, tables, batch, seq)
        xf, h2, route = _merge(h, y_pool, y_rwkv, y_nsa, xf, p)
        y1, y2 = _moe(h2, route, p["w_gate"], p["w_up_e"], p["w_down"])
        last = l == depth - 1
        g_next = norm_final if last else norm_mix[l + 1]
        xf, h = _add_norm(xf, y1, y2, g_next, F32 if last else BF16)
    return h.reshape(batch, seq, d)
```

```python
import functools
import math

import jax
import jax.numpy as jnp
import numpy as np
from jax import lax
from jax.experimental import pallas as pl
from jax.experimental.pallas import tpu as pltpu

F32 = jnp.float32
BF16 = jnp.bfloat16
HI = lax.Precision.HIGHEST

D_MODEL = 1024
RMS_EPS = 1e-6
NEG = -1e30
LOG2E = math.log2(math.e)

POOL_WINDOWS = (2, 4, 8, 16)
POOL_WIDTH = 512
POOL_GW = 128
POOL_HALO = 16

RW_HEADS = 8
RW_DH = 64
RW_WIDTH = 512
DECAY_LORA, AAA_LORA, GATE_LORA = 32, 32, 96
RW_LORA = DECAY_LORA + AAA_LORA + GATE_LORA
RW_COLS = 3 * RW_WIDTH + RW_LORA
RW_GN_EPS = 64e-5
RW_CHUNK = 64
RW_NB = 2

NSA_DH = 64
NSA_HEADS = 16
NSA_G = 4
NSA_HG = 4
NSA_KVW = NSA_G * NSA_DH
CMP_BLOCK, CMP_STRIDE, CMP_HIDDEN = 32, 16, 256
SLC_BLOCK = 64
N_SELECT = 8
WINDOW = 512
KV_PAD = WINDOW
QB = 64
NSA_NB = 8
FORCE_BONUS = 1e3
N_BUCKETS, MAX_EXACT, MAX_DISTANCE = 32, 16, 128
NEAR_BLOCKS = 3
FAR_CHUNK_BLOCKS = 8
CMP_NEAR = 32
CMP_AHEAD = (QB - CMP_BLOCK) // CMP_STRIDE + 1

N_GROUPS, EPG, N_EXPERTS, D_EXPERT = 4, 8, 32, 256
MOE_TM = 256

C_POOL, C_R, C_K, C_V, C_LORA, C_Q, C_GATE, P_COLS = 0, 512, 1024, 1536, 2048, 2304, 3328, 3584
SRC_Q, SRC_KV, SRC_GATE = 2208, 3232, 4768
N_KV = 6
LORA_PAD = 256
SRC_RW_END = POOL_WIDTH + RW_COLS

VMEM_LIMIT = 56 * 1024 * 1024


def _t5_bucket_np(dist):
    n = np.maximum(dist, 0)
    nf = np.maximum(n, 1).astype(np.float32)
    large = MAX_EXACT + (np.log(nf / MAX_EXACT) / math.log(MAX_DISTANCE / MAX_EXACT)
                         * (N_BUCKETS - MAX_EXACT)).astype(np.int32)
    large = np.minimum(large, N_BUCKETS - 1)
    return np.where(n < MAX_EXACT, n, large)


def _cp(sem, vmem=None):
    return pltpu.CompilerParams(dimension_semantics=sem, vmem_limit_bytes=vmem)


def _dot(a, b, precision=None):
    return jnp.dot(a, b, preferred_element_type=F32, precision=precision)


def _dot_nt(a, b, precision=None):
    return lax.dot_general(a, b, (((1,), (1,)), ((), ())), preferred_element_type=F32, precision=precision)


def _dot_tn(a, b, precision=None):
    return lax.dot_general(a, b, (((0,), (0,)), ((), ())), preferred_element_type=F32, precision=precision)


def _bdot(a, b):
    return _dot(a.astype(BF16), b.astype(BF16))


def _bdot_nt(a, b):
    return _dot_nt(a.astype(BF16), b.astype(BF16))


def _bdot_tn(a, b):
    return _dot_tn(a.astype(BF16), b.astype(BF16))


def _split(a):
    hi = a.astype(BF16)
    return hi, (a - hi.astype(F32)).astype(BF16)


def _dot3(a, b):
    ah, al = _split(a)
    bh, bl = _split(b)
    return _dot(ah, bh) + (_dot(ah, bl) + _dot(al, bh))


def _rms(x, g):
    return x * lax.rsqrt(jnp.mean(x * x, axis=-1, keepdims=True) + RMS_EPS) * g


def _norm_kernel(x_ref, g_ref, h_ref):
    h_ref[...] = _rms(x_ref[...], g_ref[...]).astype(h_ref.dtype)


def _norm(x, g, out_dtype, tm=512):
    n, d = x.shape
    return pl.pallas_call(
        _norm_kernel,
        out_shape=jax.ShapeDtypeStruct((n, d), out_dtype),
        grid=(n // tm,),
        in_specs=[pl.BlockSpec((tm, d), lambda i: (i, 0)), pl.BlockSpec((1, d), lambda i: (0, 0))],
        out_specs=pl.BlockSpec((tm, d), lambda i: (i, 0)),
        compiler_params=_cp(("parallel",)),
        name="rms_norm",
    )(x, g.reshape(1, d))


def _add_norm_kernel(x_ref, y1_ref, y2_ref, g_ref, xo_ref, h_ref):
    x = x_ref[...] + (y1_ref[...] + y2_ref[...])
    xo_ref[...] = x
    h_ref[...] = _rms(x, g_ref[...]).astype(h_ref.dtype)


def _add_norm(x, y1, y2, g, out_dtype, tm=512):
    n, d = x.shape
    row = pl.BlockSpec((tm, d), lambda i: (i, 0))
    return pl.pallas_call(
        _add_norm_kernel,
        out_shape=(jax.ShapeDtypeStruct((n, d), F32), jax.ShapeDtypeStruct((n, d), out_dtype)),
        grid=(n // tm,),
        in_specs=[row, row, row, pl.BlockSpec((1, d), lambda i: (0, 0))],
        out_specs=(row, row),
        compiler_params=_cp(("parallel",)),
        name="moe_combine_norm",
    )(x, y1, y2, g.reshape(1, d))


def _matmul_kernel(x_ref, w_ref, o_ref):
    o_ref[...] = _dot(x_ref[...], w_ref[...]).astype(o_ref.dtype)


def _matmul(x, w, tm=512, tn=P_COLS // 2):
    m, k = x.shape
    n = w.shape[1]
    return pl.pallas_call(
        _matmul_kernel,
        out_shape=jax.ShapeDtypeStruct((m, n), F32),
        grid=(n // tn, m // tm),
        in_specs=[pl.BlockSpec((tm, k), lambda j, i: (i, 0)), pl.BlockSpec((k, tn), lambda j, i: (0, j))],
        out_specs=pl.BlockSpec((tm, tn), lambda j, i: (i, j)),
        compiler_params=_cp(("parallel", "parallel"), VMEM_LIMIT),
        name="in_proj",
    )(x, w)


def _kv_proj_kernel(x_ref, w_ref, c_ref, ks_ref, vs_ref, kw_ref, vw_ref):
    i = pl.program_id(1)
    tm = x_ref.shape[0]
    head = i == 0
    res = _dot(x_ref[...], w_ref[...])
    col = lax.broadcasted_iota(jnp.int32, (tm, NSA_DH), 1)
    blk = ((i - 1) * tm + lax.broadcasted_iota(jnp.int32, (tm, NSA_DH), 0)) // SLC_BLOCK
    first_col = jnp.where(col == 0, 1.0, 0.0)
    extra = {2: jnp.where(head, 1.0, jnp.where(blk == col, 1.0, 0.0)), 3: first_col,
             4: jnp.where(head, first_col, 0.0), 5: first_col}
    outs = {2: ks_ref, 3: vs_ref, 4: kw_ref, 5: vw_ref}
    for t in range(N_KV):
        for g in range(NSA_G):
            c0 = (t * NSA_G + g) * NSA_DH
            val = res[:, c0:c0 + NSA_DH]
            if t < 2:
                c_ref[t, 0, g] = val.astype(c_ref.dtype)
            else:
                feat = jnp.where(head, 0.0, val)
                outs[t][0, g] = jnp.concatenate([feat, extra[t]], axis=1).astype(outs[t].dtype)


def _kv_proj(x, w, batch, seq):
    k = x.shape[1]
    tm = KV_PAD
    nt = seq // tm
    aug = jax.ShapeDtypeStruct((batch, NSA_G, KV_PAD + seq, 2 * NSA_DH), BF16)
    aug_spec = pl.BlockSpec((1, NSA_G, tm, 2 * NSA_DH), lambda b, i: (b, 0, i, 0))
    prev = lambda i: jnp.maximum(i - 1, 0)
    return pl.pallas_call(
        _kv_proj_kernel,
        out_shape=(jax.ShapeDtypeStruct((2, batch, NSA_G, seq, NSA_DH), BF16), aug, aug, aug, aug),
        grid=(batch, nt + 1),
        in_specs=[pl.BlockSpec((tm, k), lambda b, i: (b * nt + prev(i), 0)),
                  pl.BlockSpec(w.shape, lambda b, i: (0, 0))],
        out_specs=(pl.BlockSpec((2, 1, NSA_G, tm, NSA_DH), lambda b, i: (0, b, 0, prev(i), 0)),
                   aug_spec, aug_spec, aug_spec, aug_spec),
        compiler_params=_cp(("parallel", "arbitrary"), VMEM_LIMIT),
        name="kv_proj",
    )(x, w)


def _pool_kernel(u_ref, halo_ref, w_ref, scale_ref, o_ref, buf_ref, *, tile):
    i = pl.program_id(1)
    u = u_ref[...]
    buf_ref[POOL_HALO:, :] = u
    buf_ref[:POOL_HALO, :] = jnp.where(i > 0, halo_ref[...], 0.0)
    t = i * tile + lax.broadcasted_iota(jnp.int32, (tile, 1), 0)
    outs = []
    for gi, win in enumerate(POOL_WINDOWS):
        cols = slice(gi * POOL_GW, (gi + 1) * POOL_GW)
        s = u[:, cols]
        for j in range(1, win):
            s = s + buf_ref[POOL_HALO - j:POOL_HALO - j + tile, cols]
        cnt = jnp.minimum(t + 1, win).astype(F32)
        pooled = s / cnt - u[:, cols]
        outs.append(_dot(pooled.astype(BF16), w_ref[gi]))
    o_ref[...] = (jnp.concatenate(outs, axis=1) * scale_ref[...]).astype(o_ref.dtype)


def _pool(proj, w_grp, scale, batch, seq, tile=512):
    nt = seq // tile
    hb = tile // POOL_HALO
    return pl.pallas_call(
        functools.partial(_pool_kernel, tile=tile),
        out_shape=jax.ShapeDtypeStruct((batch * seq, POOL_WIDTH), BF16),
        grid=(batch, nt),
        in_specs=[
            pl.BlockSpec((tile, POOL_WIDTH), lambda b, i: (b * nt + i, 0)),
            pl.BlockSpec((POOL_HALO, POOL_WIDTH), lambda b, i: (jnp.maximum((b * nt + i) * hb - 1, 0), 0)),
            pl.BlockSpec((len(POOL_WINDOWS), POOL_GW, POOL_GW), lambda b, i: (0, 0, 0)),
            pl.BlockSpec((1, POOL_WIDTH), lambda b, i: (0, 0)),
        ],
        out_specs=pl.BlockSpec((tile, POOL_WIDTH), lambda b, i: (b * nt + i, 0)),
        scratch_shapes=[pltpu.VMEM((tile + POOL_HALO, POOL_WIDTH), F32)],
        compiler_params=_cp(("parallel", "parallel")),
        name="pool_mixer",
    )(proj, proj, w_grp.astype(BF16), scale.reshape(1, POOL_WIDTH))


def _token_shift(u, halo, mu, first):
    prev_row = jnp.where(first, 0.0, halo[7:8, :])
    rolled = pltpu.roll(u, 1, 0)
    row = lax.broadcasted_iota(jnp.int32, u.shape, 0)
    prev = jnp.where(row == 0, prev_row, rolled)
    return u + (prev - u) * mu


def _rwkv_chunk_kernel(r_ref, k_ref, v_ref, l_ref, rh_ref, kh_ref, vh_ref, lh_ref,
                       mur_ref, muk_ref, muv_ref, mul_ref, w0_ref, wup_ref, a0_ref, aup_ref, gup_ref,
                       kk_ref, ka_ref, rk_ref, bd_ref, qy_ref, mn_ref, g_ref, bonus_ref):
    first = pl.program_id(1) == 0
    c = RW_CHUNK
    r = _token_shift(r_ref[...], rh_ref[...], mur_ref[...], first)
    k = _token_shift(k_ref[...], kh_ref[...], muk_ref[...], first)
    v = _token_shift(v_ref[...], vh_ref[...], muv_ref[...], first)
    lo = _token_shift(l_ref[...], lh_ref[...], mul_ref[...], first)
    wd = lo[:, :DECAY_LORA]
    ad = lo[:, DECAY_LORA:DECAY_LORA + AAA_LORA]
    gd = lo[:, DECAY_LORA + AAA_LORA:RW_LORA]
    z = -(w0_ref[...] + _dot(jnp.tanh(wd), wup_ref[...], HI))
    w_log = -(jnp.maximum(z, 0.0) + jnp.log(1.0 + jnp.exp(-jnp.abs(z)))) - 0.5
    logw = -jnp.exp(w_log)
    a = jax.nn.sigmoid(a0_ref[...] + _dot(ad, aup_ref[...], HI))
    g_ref[0] = _dot(jax.nn.sigmoid(gd), gup_ref[...], HI)
    kkraw = k * kk_ref[...]
    k2 = k * (1.0 + (a - 1.0) * ka_ref[...])
    rkr = r * k2 * rk_ref[...]

    cum_all = logw
    trow = lax.broadcasted_iota(jnp.int32, logw.shape, 0) % c
    step = 1
    while step < c:
        cum_all = cum_all + jnp.where(trow >= step, pltpu.roll(cum_all, step, 0), 0.0)
        step *= 2

    ti = lax.broadcasted_iota(jnp.int32, (c, c), 0)
    si = lax.broadcasted_iota(jnp.int32, (c, c), 1)
    incl = ti >= si
    strict = ti > si
    eye = ti == si
    zeros = jnp.zeros((c, c), F32)
    bd = bd_ref[...]

    def head_sum(t):
        hi, lo = _split(t)
        return _dot(hi, bd) + _dot(lo, bd)

    kk = kkraw / jnp.maximum(jnp.sqrt(head_sum(kkraw * kkraw)), 1e-12)
    bonus_ref[0] = head_sum(rkr) * v
    nchunk = logw.shape[0] // c
    ends = [cum_all[(j + 1) * c - 1:(j + 1) * c, :] for j in range(nchunk)]
    cum_end = jnp.concatenate([jnp.broadcast_to(e, (c, e.shape[1])) for e in ends], axis=0)
    ginv = jnp.exp(-cum_all)
    gtail = jnp.exp(cum_end - cum_all)
    gend = jnp.exp(cum_end)
    kka = kk * a
    at = -kk * jnp.exp(cum_all - logw)
    bt = kka * ginv
    kt = k2 * ginv
    rt = r * jnp.exp(cum_all)
    bhat = kka * gtail
    khat = k2 * gtail

    heads = [(slice(j * c, (j + 1) * c), slice(h * RW_DH, (h + 1) * RW_DH))
             for j in range(nchunk) for h in range(RW_HEADS)]
    stack = lambda x, y, s: jnp.concatenate([x[s], y[s]], axis=0).astype(BF16)
    gram = [_dot_nt(stack(at, rt, s), stack(bt, kt, s)) for s in heads]
    a_ab = [jnp.where(strict, g[:c, :c], 0.0) for g in gram]
    a_ak = [jnp.where(strict, g[:c, c:], 0.0) for g in gram]
    incl2 = (lax.broadcasted_iota(jnp.int32, (c, 2 * c), 0)
             >= lax.broadcasted_iota(jnp.int32, (c, 2 * c), 1) % c)
    a_r = [jnp.where(incl2, g[c:, :], 0.0) for g in gram]
    p = a_ab
    tinv = [eye.astype(F32) + x for x in p]
    for _ in range(int(math.log2(c)) - 1):
        p = [_bdot(x, x) for x in p]
        tinv = [t + _bdot(t, x) for t, x in zip(tinv, p)]
    av = [_bdot(x, v[s]) for x, s in zip(a_ak, heads)]
    w12 = [_bdot(t, jnp.concatenate([at[s], x], axis=1)) for t, x, s in zip(tinv, av, heads)]
    zmat = [jnp.concatenate([w, jnp.concatenate([zeros, v[s]], axis=1)], axis=0).astype(BF16)
            for w, s in zip(w12, heads)]
    out1 = [_dot(x.astype(BF16), z) for x, z in zip(a_r, zmat)]
    out2 = [_dot_tn(stack(bhat, khat, s), z) for s, z in zip(heads, zmat)]
    qy = [o + jnp.concatenate([rt[s], zeros], axis=1) for o, s in zip(out1, heads)]
    for j in range(nchunk):
        qy_ref[0, j * c:(j + 1) * c, :] = jnp.concatenate(qy[j * RW_HEADS:(j + 1) * RW_HEADS], axis=1)
        for h in range(RW_HEADS):
            idx = j * RW_HEADS + h
            diag = jnp.where(eye, gend[heads[idx]], 0.0)
            mn_ref[0, j, h] = out2[idx] + jnp.concatenate([diag, zeros], axis=1)


def _rwkv_scan_kernel(qy_ref, mn_ref, g_ref, bonus_ref, gnw_ref, gnb_ref, bd_ref, y_ref, st_ref, *, batch):
    @pl.when(pl.program_id(0) == 0)
    def _():
        st_ref[...] = jnp.zeros_like(st_ref)

    bd = bd_ref[...]

    def head_mean(t):
        hi, lo = _split(t)
        return (_dot(hi, bd) + _dot(lo, bd)) * (1.0 / RW_DH)

    pairs = [(b, h) for b in range(batch) for h in range(RW_HEADS)]
    sts = [st_ref[b * RW_HEADS + h] for b, h in pairs]
    ys = [_bdot(qy_ref[b, :, 2 * RW_DH * h:2 * RW_DH * h + RW_DH], st)
          + qy_ref[b, :, 2 * RW_DH * h + RW_DH:2 * RW_DH * (h + 1)] for (b, h), st in zip(pairs, sts)]
    for (b, h), st in zip(pairs, sts):
        mn = mn_ref[b, 0, h]
        st_ref[b * RW_HEADS + h] = _dot3(mn[:, :RW_DH], st) + mn[:, RW_DH:]
    for b in range(batch):
        y = jnp.concatenate(ys[b * RW_HEADS:(b + 1) * RW_HEADS], axis=1)
        dev = y - head_mean(y)
        yn = dev * lax.rsqrt(head_mean(dev * dev) + RW_GN_EPS) * gnw_ref[...] + gnb_ref[...]
        y_ref[b] = ((yn + bonus_ref[b]) * g_ref[b]).astype(y_ref.dtype)


def _rwkv(proj, p, batch, seq):
    head_ones = jnp.asarray(np.kron(np.eye(RW_HEADS), np.ones((RW_DH, RW_DH))), BF16)
    c = RW_CHUNK
    nc = seq // c
    tb = RW_NB * c
    nt = seq // tb
    hb = tb // 8
    row512 = lambda col: pl.BlockSpec((tb, RW_WIDTH), lambda b, i: (b * nt + i, col))
    halo512 = lambda col: pl.BlockSpec((8, RW_WIDTH), lambda b, i: (jnp.maximum((b * nt + i) * hb - 1, 0), col))
    const = lambda shape: pl.BlockSpec(shape, lambda b, i: (0,) * len(shape))
    vec = const((1, RW_WIDTH))
    out_row = lambda w: pl.BlockSpec((1, tb, w), lambda b, i: (b, i, 0))
    qy, mn, g, bonus = pl.pallas_call(
        _rwkv_chunk_kernel,
        out_shape=(jax.ShapeDtypeStruct((batch, seq, 2 * RW_WIDTH), F32),
                   jax.ShapeDtypeStruct((batch, nc, RW_HEADS, RW_DH, 2 * RW_DH), F32),
                   jax.ShapeDtypeStruct((batch, seq, RW_WIDTH), F32),
                   jax.ShapeDtypeStruct((batch, seq, RW_WIDTH), F32)),
        grid=(batch, nt),
        in_specs=[row512(C_R // RW_WIDTH), row512(C_K // RW_WIDTH), row512(C_V // RW_WIDTH),
                  pl.BlockSpec((tb, LORA_PAD), lambda b, i: (b * nt + i, C_LORA // LORA_PAD)),
                  halo512(C_R // RW_WIDTH), halo512(C_K // RW_WIDTH), halo512(C_V // RW_WIDTH),
                  pl.BlockSpec((8, LORA_PAD), lambda b, i: (jnp.maximum((b * nt + i) * hb - 1, 0), C_LORA // LORA_PAD)),
                  vec, vec, vec, const((1, LORA_PAD)),
                  vec, const((DECAY_LORA, RW_WIDTH)), vec, const((AAA_LORA, RW_WIDTH)), const((GATE_LORA, RW_WIDTH)),
                  vec, vec, vec, const((RW_WIDTH, RW_WIDTH))],
        out_specs=(out_row(2 * RW_WIDTH),
                   pl.BlockSpec((1, RW_NB, RW_HEADS, RW_DH, 2 * RW_DH), lambda b, i: (b, i, 0, 0, 0)),
                   out_row(RW_WIDTH), out_row(RW_WIDTH)),
        compiler_params=_cp(("parallel", "parallel"), VMEM_LIMIT),
        name="rwkv_chunk",
    )(proj, proj, proj, proj, proj, proj, proj, proj,
      p["mu_r"], p["mu_k"], p["mu_v"], p["mu_l"], p["w0"], p["w_up"], p["a0"], p["a_up"], p["g_up"],
      p["k_k"], p["k_a"], p["r_k"], head_ones)

    full = lambda w: pl.BlockSpec((batch, c, w), lambda i: (0, i, 0))
    return pl.pallas_call(
        functools.partial(_rwkv_scan_kernel, batch=batch),
        out_shape=jax.ShapeDtypeStruct((batch, seq, RW_WIDTH), BF16),
        grid=(nc,),
        in_specs=[full(2 * RW_WIDTH),
                  pl.BlockSpec((batch, 1, RW_HEADS, RW_DH, 2 * RW_DH), lambda i: (0, i, 0, 0, 0)),
                  full(RW_WIDTH), full(RW_WIDTH),
                  pl.BlockSpec((1, RW_WIDTH), lambda i: (0, 0)), pl.BlockSpec((1, RW_WIDTH), lambda i: (0, 0)),
                  pl.BlockSpec((RW_WIDTH, RW_WIDTH), lambda i: (0, 0))],
        out_specs=full(RW_WIDTH),
        scratch_shapes=[pltpu.VMEM((batch * RW_HEADS, RW_DH, RW_DH), F32)],
        compiler_params=_cp(("arbitrary",), VMEM_LIMIT),
        name="rwkv_scan",
    )(qy, mn, g, bonus, p["gn_w"], p["gn_b"], head_ones).reshape(batch * seq, RW_WIDTH)


def _gelu_tanh(x):
    return 0.5 * x * (1.0 + jnp.tanh(math.sqrt(2.0 / math.pi) * (x + 0.044715 * (x * x * x))))


def _compress_kernel(x_ref, pos_ref, w1_ref, w2_ref, o_ref):
    half = CMP_STRIDE * NSA_DH
    x = x_ref[0, 0, 0]
    w1 = w1_ref[0]
    posb = _dot(pos_ref[0], w1)[0:1, :]
    h1 = _dot(x, w1[:half])
    h2 = _dot(x, w1[half:])
    n = h2.shape[0]
    row = lax.broadcasted_iota(jnp.int32, h2.shape, 0)
    h2s = jnp.where(row < n - 1, pltpu.roll(h2, n - 1, 0), 0.0)
    hid = _gelu_tanh(h1 + h2s + posb)
    o_ref[0, 0, 0] = _dot(hid.astype(BF16), w2_ref[0]).astype(o_ref.dtype)


def _compress(xkv, pos, w1, w2, batch, seq):
    nr = seq // CMP_STRIDE
    wide = CMP_STRIDE * NSA_DH
    return pl.pallas_call(
        _compress_kernel,
        out_shape=jax.ShapeDtypeStruct((2, batch, NSA_G, nr, NSA_DH), BF16),
        grid=(2, batch, NSA_G),
        in_specs=[pl.BlockSpec((1, 1, 1, nr, wide), lambda t, b, g: (t, b, g, 0, 0)),
                  pl.BlockSpec((1, 8, 2 * wide), lambda t, b, g: (t, 0, 0)),
                  pl.BlockSpec((1, 2 * wide, CMP_HIDDEN), lambda t, b, g: (t, 0, 0)),
                  pl.BlockSpec((1, CMP_HIDDEN, NSA_DH), lambda t, b, g: (t, 0, 0))],
        out_specs=pl.BlockSpec((1, 1, 1, nr, NSA_DH), lambda t, b, g: (t, b, g, 0, 0)),
        compiler_params=_cp(("parallel", "parallel", "parallel")),
        name="nsa_compress",
    )(xkv, pos, w1, w2)


def _softmax_parts(s):
    m = jnp.max(s, axis=1, keepdims=True)
    e = jnp.exp(s - m)
    return m, e, jnp.sum(e, axis=1, keepdims=True)


def _lane_tile_fold(x, op, init):
    for t in range(x.shape[1] // 128):
        init = op(init, x[:, 128 * t:128 * (t + 1)])
    return init


def _nsa_kernel(q_ref, gate_ref, kc_ref, vc_ref, ks_ref, vs_ref, kw_ref, vw_ref,
                tblc_ref, tbln_ref, tblw_ref, ovt_ref, gexp_ref, o_ref, s_ref, *, ncmp):
    nb = NSA_NB
    blocks = [pl.program_id(2) * nb + u for u in range(nb)]
    rows = NSA_HG * QB
    each = lambda f, *ls: [f(*xs) for xs in zip(*ls)]
    qt = q_ref[...]
    qs = [jnp.concatenate([qt[u * QB:(u + 1) * QB, NSA_DH * h:NSA_DH * (h + 1)] for h in range(NSA_HG)], axis=0)
          * (NSA_DH ** -0.5 * LOG2E) for u in range(nb)]
    qb = each(lambda x: x.astype(BF16), qs)
    rmax = lambda x: jnp.max(x, axis=1, keepdims=True)
    rsum = lambda x: jnp.sum(x, axis=1, keepdims=True)

    win_w = WINDOW + QB
    kw = [kw_ref[0, 0, pl.ds(pl.multiple_of(i * QB, QB), win_w), :] for i in blocks]
    vw = [vw_ref[0, 0, pl.ds(pl.multiple_of(i * QB, QB), win_w), :] for i in blocks]
    flag = lax.broadcasted_iota(jnp.int32, (rows, kw_ref.shape[3] - NSA_DH), 1) == 0
    q_win = each(lambda x: jnp.concatenate([x, jnp.where(flag, NEG, 0.0)], axis=1).astype(BF16), qs)
    tblw = tblw_ref[0]
    s_w = each(lambda x, k: _dot_nt(x, k) + tblw, q_win, kw)
    m_w = each(rmax, s_w)
    e_w = each(lambda s, m: jnp.exp2(s - m), s_w, m_w)
    pv = lambda x: x[:, :NSA_DH] / x[:, NSA_DH:NSA_DH + 1]
    o_w = each(lambda e, v: pv(_dot(e.astype(BF16), v)), e_w, vw)

    kc = kc_ref[0, 0]
    vc = vc_ref[0, 0]
    tblc = tblc_ref[0]
    cidx = lax.broadcasted_iota(jnp.int32, (rows, ncmp), 1)
    qrow = lax.broadcasted_iota(jnp.int32, (rows, 1), 0) % QB
    lc = [jnp.where(cidx < (QB // CMP_STRIDE) * i + CMP_AHEAD,
                    _dot_nt(x, kc) + pltpu.roll(tblc, (4 * i - CMP_NEAR // 2 + ncmp) % ncmp, 1), NEG)
          for x, i in zip(qb, blocks)]
    m_c = each(rmax, lc)
    e_c = each(lambda s, m: jnp.exp2(s - m), lc, m_c)
    den_c = each(rsum, e_c)
    pc = [e * jnp.where(i * QB + qrow >= CMP_BLOCK - 1, 1.0 / d, 0.0) for e, d, i in zip(e_c, den_c, blocks)]
    o_c = each(lambda x: _dot(x.astype(BF16), vc), pc)
    pcs_hi, pcs_lo = _split(jnp.concatenate(
        each(lambda x: x[0:QB] + x[QB:2 * QB] + x[2 * QB:3 * QB] + x[3 * QB:4 * QB], pc), axis=0))
    ovt = ovt_ref[...]
    imp = _dot_nt(ovt, pcs_hi) + _dot_nt(ovt, pcs_lo)

    nslc = ovt.shape[0]
    nidx = lax.broadcasted_iota(jnp.int32, (nslc, nb * QB), 0)
    cur = blocks[0] + lax.broadcasted_iota(jnp.int32, (nslc, nb * QB), 1) // QB
    forced = (nidx == 0) | (nidx == cur) | (nidx == cur - 1)
    work = jnp.where(nidx <= cur, imp + jnp.where(forced, FORCE_BONUS, 0.0), -1.0)
    sel_all = jnp.zeros((nslc, nb * QB), F32)
    for _ in range(N_SELECT):
        m = jnp.max(work, axis=0, keepdims=True)
        first = jnp.min(jnp.where(work == m, nidx, nslc), axis=0, keepdims=True)
        pick = nidx == first
        sel_all = jnp.where(pick & (m >= 0.0), 1.0, sel_all)
        work = jnp.where(pick, -2.0, work)
    sel_t = sel_all.T

    ind_w = ks_ref.shape[3] - NSA_DH
    selq = [sel_t[u * QB:(u + 1) * QB] for u in range(nb)]
    if nslc < ind_w:
        selq = each(lambda x: jnp.concatenate([x, jnp.zeros((QB, ind_w - nslc), F32)], axis=1), selq)
    bidx = lax.broadcasted_iota(jnp.int32, (QB, ind_w), 1)
    tile4 = lambda t: jnp.concatenate([t] * NSA_HG, axis=0)
    with_mask = lambda x, keep: jnp.concatenate([x, tile4(jnp.where(keep, 0.0, NEG))], axis=1).astype(BF16)
    q_sel = each(lambda x, s: with_mask(x, s > 0.0), qs, selq)
    q_far = jnp.concatenate([with_mask(x, (s > 0.0) & (bidx <= i - NEAR_BLOCKS))
                             for x, s, i in zip(qs, selq, blocks)], axis=0)
    near_w = NEAR_BLOCKS * SLC_BLOCK
    pad_s = KV_PAD
    near0 = pad_s - (NEAR_BLOCKS - 1) * SLC_BLOCK
    kn = [ks_ref[0, 0, pl.ds(pl.multiple_of(near0 + i * QB, QB), near_w), :] for i in blocks]
    vn = [vs_ref[0, 0, pl.ds(pl.multiple_of(near0 + i * QB, QB), near_w), :] for i in blocks]
    tbln = tbln_ref[0]
    s_near = each(lambda x, k: _dot_nt(x, k) + tbln, q_sel, kn)
    m_near = jnp.concatenate(each(rmax, s_near), axis=0)

    far_w = FAR_CHUNK_BLOCKS * SLC_BLOCK
    n_far =jnp.maximum(blocks[-1] - (NEAR_BLOCKS - 1) + FAR_CHUNK_BLOCKS - 1, 0) // FAR_CHUNK_BLOCKS

    def far_logits(j, mvec):
        start = pl.multiple_of(j * far_w, far_w)
        kf = ks_ref[0, 0, pl.ds(pl.multiple_of(pad_s + j * far_w, SLC_BLOCK), far_w), :]
        s = _dot_nt(q_far, kf)
        s_ref[:, pl.ds(start, far_w)] = s
        return _lane_tile_fold(s, jnp.maximum, mvec)

    mvec = lax.fori_loop(0, n_far, far_logits, jnp.full((nb * rows, 128), NEG, F32))
    m_s = jnp.maximum(m_near, rmax(mvec))

    e_near = [jnp.exp2(s - m_s[u * rows:(u + 1) * rows]) for u, s in enumerate(s_near)]
    acc0 = jnp.concatenate(each(lambda e, v: _dot(e.astype(BF16), v), e_near, vn), axis=0)

    def far_values(j, acc):
        start = pl.multiple_of(j * far_w, far_w)
        vf = vs_ref[0, 0, pl.ds(pl.multiple_of(pad_s + j * far_w, SLC_BLOCK), far_w), :]
        e = jnp.exp2(s_ref[:, pl.ds(start, far_w)] - m_s)
        return acc + _dot(e.astype(BF16), vf)

    o_s = pv(lax.fori_loop(0, n_far, far_values, acc0))

    g_hi, g_lo = _split(jax.nn.sigmoid(gate_ref[0, 0]))
    gexp = gexp_ref[...]
    ge = _dot(g_hi, gexp) + _dot(g_lo, gexp)
    for u in range(nb):
        outs = []
        for h in range(NSA_HG):
            r0 = slice(h * QB, (h + 1) * QB)
            gt = lambda br: ge[u * QB:(u + 1) * QB, (3 * h + br) * 128:(3 * h + br) * 128 + NSA_DH]
            outs.append(gt(0) * o_c[u][r0] + gt(1) * o_s[u * rows + h * QB:u * rows + (h + 1) * QB]
                        + gt(2) * o_w[u][r0])
        o_ref[u * QB:(u + 1) * QB, :] = jnp.concatenate(outs, axis=1).astype(o_ref.dtype)


def _bias_tables(rel_bias, ncmp):
    def table(dist, keep, base):
        onehot = jnp.asarray(np.eye(N_BUCKETS, dtype=np.float32)[_t5_bucket_np(dist)])
        tbl = jnp.einsum("qkb,bh->qkh", onehot, rel_bias, precision=HI)
        tbl = (tbl - base) * LOG2E
        tbl = jnp.where(jnp.asarray(keep)[..., None], tbl, NEG)
        k = dist.shape[1]
        return tbl.transpose(2, 0, 1).reshape(NSA_G, NSA_HG * QB, k)

    far = rel_bias[N_BUCKETS - 1]
    qi = np.arange(QB)[:, None]
    dist_c = qi - CMP_STRIDE * (np.arange(CMP_NEAR)[None, :] - CMP_NEAR // 2) - (CMP_BLOCK - 1)
    tblc = table(dist_c, dist_c >= 0, far)
    tblc = tblc * jnp.asarray(np.arange(CMP_NEAR) < CMP_NEAR // 2 + CMP_AHEAD, F32)
    tblc = jnp.pad(tblc, ((0, 0), (0, 0), (0, ncmp - CMP_NEAR)))
    jn = np.arange(NEAR_BLOCKS * SLC_BLOCK)[None, :]
    dist_n = (NEAR_BLOCKS - 1) * SLC_BLOCK + qi - jn
    tbln = table(dist_n, dist_n >= 0, far)
    jw = np.arange(WINDOW + QB)[None, :]
    dist_w = WINDOW + qi - jw
    tblw = table(dist_w, (dist_w >= 0) & (dist_w < WINDOW), 0.0)
    return tblc, tbln, tblw


def _nsa(proj, gate_logits, kvc, ks, vs, kw, vw, tables, batch, seq):
    nq = seq // QB
    ncmp = seq // CMP_STRIDE
    nslc = seq // SLC_BLOCK
    tblc, tbln, tblw = tables
    cstart = np.arange(ncmp) * CMP_STRIDE
    sstart = np.arange(nslc) * SLC_BLOCK
    overlap_t = ((cstart[None, :] <= sstart[:, None] + SLC_BLOCK - 1)
                 & (cstart[None, :] + CMP_BLOCK - 1 >= sstart[:, None])
                 & (cstart[None, :] + CMP_BLOCK <= seq)).astype(np.float32)
    rows = NSA_HG * QB
    n_gate = 3 * NSA_HG
    gate_spread = (np.arange(n_gate * 128)[None, :] // 128 == np.arange(n_gate)[:, None]) \
        & (np.arange(n_gate * 128)[None, :] % 128 < NSA_DH)
    kv_spec = lambda t: pl.BlockSpec((1, 1) + t.shape[2:], lambda b, g, i: (b, g, 0, 0))
    tbl_spec = lambda k: pl.BlockSpec((1, rows, k), lambda b, g, i: (g, 0, 0))
    qcol = C_Q // (NSA_HG * NSA_DH)
    return pl.pallas_call(
        functools.partial(_nsa_kernel, ncmp=ncmp),
        out_shape=jax.ShapeDtypeStruct((batch * seq, NSA_HEADS * NSA_DH), BF16),
        grid=(batch, NSA_G, nq // NSA_NB),
        in_specs=[pl.BlockSpec((NSA_NB * QB, NSA_HG * NSA_DH), lambda b, g, i: (b * (nq // NSA_NB) + i, qcol + g)),
                  pl.BlockSpec((1, 1, NSA_NB * QB, 3 * NSA_HG), lambda b, g, i: (b, g, i, 0)),
                  pl.BlockSpec((1, 1, ncmp, NSA_DH), lambda b, g, i: (b, g, 0, 0)),
                  pl.BlockSpec((1, 1, ncmp, NSA_DH), lambda b, g, i: (b, g, 0, 0)),
                  kv_spec(ks), kv_spec(vs), kv_spec(kw), kv_spec(vw),
                  tbl_spec(ncmp), tbl_spec(NEAR_BLOCKS * SLC_BLOCK), tbl_spec(WINDOW + QB),
                  pl.BlockSpec((nslc, ncmp), lambda b, g, i: (0, 0)),
                  pl.BlockSpec(gate_spread.shape, lambda b, g, i: (0, 0))],
        out_specs=pl.BlockSpec((NSA_NB * QB, NSA_HG * NSA_DH), lambda b, g, i: (b * (nq // NSA_NB) + i, g)),
        scratch_shapes=[pltpu.VMEM((NSA_NB * rows, seq), F32)],
        compiler_params=_cp(("parallel", "parallel", "arbitrary"), VMEM_LIMIT),
        name="nsa_attention",
    )(proj, gate_logits, kvc[0], kvc[1], ks, vs, kw, vw, tblc, tbln, tblw, jnp.asarray(overlap_t, BF16),
      jnp.asarray(gate_spread, BF16))


def _merge_kernel(h_ref, yp_ref, yr_ref, yn_ref, x_ref, wbp_ref, wbr_ref, wbn_ref, wm_ref, bm_ref, wo_ref,
                  gn_ref, wr_ref, br_ref, xo_ref, h2_ref, route_ref):
    d = D_MODEL
    gl = jax.nn.sigmoid(_dot(h_ref[...], wm_ref[...]) + bm_ref[...])
    merged = (gl[:, :d] * _dot(yp_ref[...], wbp_ref[...]) + gl[:, d:2 * d] * _dot(yr_ref[...], wbr_ref[...])
              + gl[:, 2 * d:] * _dot(yn_ref[...], wbn_ref[...]))
    x = x_ref[...] + _dot(merged.astype(BF16), wo_ref[...])
    xo_ref[...] = x
    h2 = _rms(x, gn_ref[...])
    h2_ref[...] = h2.astype(h2_ref.dtype)

    h2_hi, h2_lo = _split(h2)
    nl = br_ref.shape[1]
    both = _dot(h2_hi, wr_ref[...])
    logits = both[:, :nl] + (both[:, nl:] + _dot(h2_lo, wr_ref[:, :nl])) + br_ref[...]
    lane = lax.broadcasted_iota(jnp.int32, logits.shape, 1)
    big = logits.shape[1]
    lg = jnp.where(lane < N_GROUPS, logits, NEG)
    mg = jnp.max(lg, axis=1, keepdims=True)
    p_top = 1.0 / jnp.sum(jnp.exp(lg - mg), axis=1, keepdims=True)
    grp = jnp.min(jnp.where(lg == mg, lane, big), axis=1, keepdims=True)
    lo = N_GROUPS + EPG * grp
    le = jnp.where((lane >= lo) & (lane < lo + EPG), logits, NEG)
    e1 = jnp.max(le, axis=1, keepdims=True)
    i1 = jnp.min(jnp.where(le == e1, lane, big), axis=1, keepdims=True)
    le = jnp.where(lane == i1, NEG, le)
    e2 = jnp.max(le, axis=1, keepdims=True)
    i2 = jnp.min(jnp.where(le == e2, lane, big), axis=1, keepdims=True)
    t = jnp.exp(e2 - e1)
    w1 = p_top / (1.0 + t)
    w2 = p_top * t / (1.0 + t)
    route_ref[...] = jnp.where(lane == 0, (i1 - N_GROUPS).astype(F32),
                               jnp.where(lane == 1, (i2 - N_GROUPS).astype(F32),
                                         jnp.where(lane == 2, w1, jnp.where(lane == 3, w2, 0.0))))


def _merge(h, y_pool, y_rwkv, y_nsa, x, p, tm=256):
    n, d = x.shape
    row = lambda w: pl.BlockSpec((tm, w), lambda i: (i, 0))
    const = lambda a: pl.BlockSpec(a.shape, lambda i: (0, 0))
    ws = [p["wb_pool"], p["wb_rwkv"], p["wb_nsa"], p["w_merge"], p["b_merge"], p["w_out"],
          p["norm_ffn"], p["w_router"], p["b_router"]]
    return pl.pallas_call(
        _merge_kernel,
        out_shape=(jax.ShapeDtypeStruct((n, d), F32), jax.ShapeDtypeStruct((n, d), F32),
                   jax.ShapeDtypeStruct((n, 128), F32)),
        grid=(n // tm,),
        in_specs=[row(d), row(POOL_WIDTH), row(RW_WIDTH), row(d), row(d)] + [const(w) for w in ws],
        out_specs=(row(d), row(d), row(128)),
        compiler_params=_cp(("parallel",), VMEM_LIMIT),
        name="merge_router",
    )(h, y_pool, y_rwkv, y_nsa, x, *ws)


def _expert_kernel(te_ref, nt_ref, tok_ref, h_hbm, w_ref, wg_ref, wu_ref, wd_ref, o_ref,
                   xbuf, sem, wg_s, wu_s, wd_s):
    i = pl.program_id(0)
    tm = MOE_TM
    n_tiles = nt_ref[0]

    def row_copy(tile, slot, r):
        return pltpu.make_async_copy(h_hbm.at[pl.ds(tok_ref[tile * tm + r], 1), :],
                                     xbuf.at[slot, pl.ds(r, 1), :], sem.at[slot])

    def tile_wait(slot):
        pltpu.make_async_copy(h_hbm.at[pl.ds(0, tm), :], xbuf.at[slot], sem.at[slot]).wait()

    @pl.when((i == 0) & (n_tiles > 0))
    def _():
        def body(r, carry):
            row_copy(0, 0, r).start()
            return carry
        lax.fori_loop(0, tm, body, 0, unroll=8)

    @pl.when((i == 0) | (te_ref[i] != te_ref[jnp.maximum(i - 1, 0)]))
    def _():
        wg_s[...] = wg_ref[0, 0].astype(BF16)
        wu_s[...] = wu_ref[0, 0].astype(BF16)
        wd_s[...] = wd_ref[0, 0].astype(BF16)

    @pl.when(i < n_tiles)
    def _():
        slot = i % 2
        quarter = tm // 4

        def fetch_next(part):
            for r in range(part * quarter, (part + 1) * quarter):
                row_copy(i + 1, 1 - slot, r).start()

        tile_wait(slot)
        xb = xbuf[slot].astype(BF16)
        fetch_next(0)
        gate = _dot(xb, wg_s[...])
        fetch_next(1)
        up = _dot(xb, wu_s[...])
        fetch_next(2)
        hid = (gate * jax.nn.sigmoid(gate) * up).astype(BF16)
        fetch_next(3)
        o_ref[...] = w_ref[...] * _dot(hid, wd_s[...])

    @pl.when((i == n_tiles) & (n_tiles > 0))
    def _():
        tile_wait(i % 2)

    @pl.when(i >= nt_ref[0])
    def _():
        o_ref[...] = jnp.zeros_like(o_ref)


def _experts(h2, rowtok, roww, tile_expert, n_tiles, layer, wg, wu, wd):
    d = h2.shape[1]
    r = rowtok.shape[0]
    tm = MOE_TM
    return pl.pallas_call(
        _expert_kernel,
        out_shape=jax.ShapeDtypeStruct((r, d), F32),
        grid_spec=pltpu.PrefetchScalarGridSpec(
            num_scalar_prefetch=3,
            grid=(r // tm,),
            in_specs=[pl.BlockSpec(memory_space=pl.ANY),
                      pl.BlockSpec((tm, 1), lambda i, te, nt, tok: (i, 0)),
                      pl.BlockSpec((1, 1, d, D_EXPERT), lambda i, te, nt, tok: (layer, te[i], 0, 0)),
                      pl.BlockSpec((1, 1, d, D_EXPERT), lambda i, te, nt, tok: (layer, te[i], 0, 0)),
                      pl.BlockSpec((1, 1, D_EXPERT, d), lambda i, te, nt, tok: (layer, te[i], 0, 0))],
            out_specs=pl.BlockSpec((tm, d), lambda i, te, nt, tok: (i, 0)),
            scratch_shapes=[pltpu.VMEM((2, tm, d), F32), pltpu.SemaphoreType.DMA((2,)),
                            pltpu.VMEM((d, D_EXPERT), BF16), pltpu.VMEM((d, D_EXPERT), BF16),
                            pltpu.VMEM((D_EXPERT, d), BF16)]),
        compiler_params=_cp(("arbitrary",), VMEM_LIMIT),
        name="moe_experts",
    )(tile_expert, n_tiles, rowtok, h2, roww, wg, wu, wd)


def _moe(h2, route, layer, wg, wu, wd):
    n = h2.shape[0]
    tm = MOE_TM
    r = 2 * n + (N_EXPERTS + 1) * tm
    ids = route[:, 0:2].astype(jnp.int32).reshape(-1)
    wts = route[:, 2:4].reshape(-1)
    onehot = (ids[:, None] == jnp.arange(N_EXPERTS)[None, :]).astype(jnp.int32)
    rank = jnp.sum((jnp.cumsum(onehot, axis=0) - onehot) * onehot, axis=1)
    counts = jnp.sum(onehot, axis=0)
    tiles = (counts + tm - 1) // tm
    tile_end = jnp.cumsum(tiles)
    starts = (tile_end - tiles) * tm
    pos = starts[ids] + rank
    row_assign = jnp.full((r,), -1, jnp.int32).at[pos].set(jnp.arange(2 * n, dtype=jnp.int32))
    rowtok = jnp.maximum(row_assign, 0) // 2
    roww = jnp.where(row_assign >= 0, wts[jnp.maximum(row_assign, 0)], 0.0)
    n_tiles = tile_end[-1:].astype(jnp.int32)
    tile_expert = jnp.minimum(jnp.sum(tile_end[None, :] <= jnp.arange(r // tm)[:, None], axis=1),
                              N_EXPERTS - 1).astype(jnp.int32)
    ys = _experts(h2, rowtok, roww.reshape(r, 1), tile_expert, n_tiles, layer, wg, wu, wd)
    return ys[pos[0::2]], ys[pos[1::2]]


def _layer_params(l, a):
    f = lambda t: t[l]
    row = lambda t: t[l].reshape(1, -1)
    w_in = a["w_in"][l]
    w_in_p = jnp.concatenate([w_in[:, :SRC_RW_END], jnp.zeros((D_MODEL, C_Q - SRC_RW_END), F32),
                              w_in[:, SRC_Q:SRC_KV], w_in[:, SRC_GATE:],
                              jnp.zeros((D_MODEL, P_COLS - C_GATE - (w_in.shape[1] - SRC_GATE)), F32)], axis=1)
    mu = a["rw_mu"][l]
    wb = a["w_branch"][l].astype(BF16)
    w_router = jnp.zeros((D_MODEL, 128), F32)
    w_router = w_router.at[:, :N_GROUPS].set(a["w_router_grp"][l]).at[:, N_GROUPS:N_GROUPS + N_EXPERTS].set(
        a["w_router_exp"][l])
    b_router = jnp.zeros((1, 128), F32)
    b_router = b_router.at[0, :N_GROUPS].set(a["b_router_grp"][l]).at[0, N_GROUPS:N_GROUPS + N_EXPERTS].set(
        a["b_router_exp"][l])
    pos = jnp.stack([a["cmp_pos_k"][l].reshape(-1), a["cmp_pos_v"][l].reshape(-1)])
    return {
        "w_in": w_in_p.astype(BF16), "w_kv": w_in[:, SRC_KV:SRC_GATE].astype(BF16),
        "pool_w": f(a["pool_w"]), "pool_scale": f(a["pool_scale"]),
        "mu_r": mu[None, 0:RW_WIDTH], "mu_k": mu[None, RW_WIDTH:2 * RW_WIDTH],
        "mu_v": mu[None, 2 * RW_WIDTH:3 * RW_WIDTH],
        "mu_l": jnp.concatenate([mu[3 * RW_WIDTH:], jnp.zeros((LORA_PAD - RW_LORA,), F32)])[None],
        "w0": row(a["rw_w0"]), "w_up": f(a["rw_w_up"]), "a0": row(a["rw_a0"]), "a_up": f(a["rw_a_up"]),
        "g_up": f(a["rw_g_up"]), "k_k": row(a["rw_k_k"]), "k_a": row(a["rw_k_a"]), "r_k": row(a["rw_r_k"]),
        "gn_w": row(a["rw_gn_w"]), "gn_b": row(a["rw_gn_b"]),
        "cmp_pos": jnp.broadcast_to(pos[:, None, :], (2, 8, pos.shape[1])).astype(BF16),
        "cmp_w1": jnp.stack([a["cmp_w1_k"][l], a["cmp_w1_v"][l]]).astype(BF16),
        "cmp_w2": jnp.stack([a["cmp_w2_k"][l], a["cmp_w2_v"][l]]).astype(BF16),
        "wb_pool": wb[:POOL_WIDTH], "wb_rwkv": wb[POOL_WIDTH:POOL_WIDTH + RW_WIDTH],
        "wb_nsa": wb[POOL_WIDTH + RW_WIDTH:],
        "w_merge": a["w_merge"][l].astype(BF16), "b_merge": row(a["b_merge"]),
        "w_out": a["w_out"][l].astype(BF16), "norm_ffn": row(a["norm_ffn"]),
        "w_router": jnp.concatenate(_split(w_router), axis=1), "b_router": b_router,
    }


def _mixers(proj, kv, p, tables, batch, seq):
    kv_cmp, ks, vs, kw, vw = kv
    y_pool = _pool(proj, p["pool_w"], p["pool_scale"], batch, seq)
    y_rwkv = _rwkv(proj, p, batch, seq)
    xkv = kv_cmp.reshape(2, batch, NSA_G, seq // CMP_STRIDE, CMP_STRIDE * NSA_DH)
    kvc = _compress(xkv, p["cmp_pos"], p["cmp_w1"], p["cmp_w2"], batch, seq)
    gate_logits = proj[:, C_GATE:C_GATE + 3 * NSA_HEADS].reshape(batch, seq, NSA_G, 3 * NSA_HG).transpose(0, 2, 1, 3)
    y_nsa = _nsa(proj, gate_logits, kvc, ks, vs, kw, vw, tables, batch, seq)
    return y_pool, y_rwkv, y_nsa


def kernel(x, rel_bias, norm_mix, w_in, pool_w, pool_scale, rw_mu, rw_w0, rw_w_up, rw_a0, rw_a_up, rw_g_up, rw_k_k, rw_k_a, rw_r_k, rw_gn_w, rw_gn_b, cmp_pos_k, cmp_w1_k, cmp_w2_k, cmp_pos_v, cmp_w1_v, cmp_w2_v, w_branch, w_merge, b_merge, w_out, norm_ffn, w_router_grp, b_router_grp, w_router_exp, b_router_exp, w_exp_gate, w_exp_up, w_exp_down, norm_final):
    a = dict(w_in=w_in, pool_w=pool_w, pool_scale=pool_scale, rw_mu=rw_mu, rw_w0=rw_w0, rw_w_up=rw_w_up,
             rw_a0=rw_a0, rw_a_up=rw_a_up, rw_g_up=rw_g_up, rw_k_k=rw_k_k, rw_k_a=rw_k_a, rw_r_k=rw_r_k,
             rw_gn_w=rw_gn_w, rw_gn_b=rw_gn_b, cmp_pos_k=cmp_pos_k, cmp_w1_k=cmp_w1_k, cmp_w2_k=cmp_w2_k,
             cmp_pos_v=cmp_pos_v, cmp_w1_v=cmp_w1_v, cmp_w2_v=cmp_w2_v, w_branch=w_branch, w_merge=w_merge,
             b_merge=b_merge, w_out=w_out, norm_ffn=norm_ffn, w_router_grp=w_router_grp,
             b_router_grp=b_router_grp, w_router_exp=w_router_exp, b_router_exp=b_router_exp,
             w_exp_gate=w_exp_gate, w_exp_up=w_exp_up, w_exp_down=w_exp_down)
    batch, seq, d = x.shape
    depth = norm_mix.shape[0]
    tables = _bias_tables(rel_bias, seq // CMP_STRIDE)
    xf = x.reshape(batch * seq, d)
    h = _norm(xf, norm_mix[0], BF16)
    for l in range(depth):
        p = _layer_params(l, a)
        proj = _matmul(h, p["w_in"])
        kv = _kv_proj(h, p["w_kv"], batch, seq)
        y_pool, y_rwkv, y_nsa = _mixers(proj, kv, p, tables, batch, seq)
        xf, h2, route = _merge(h, y_pool, y_rwkv, y_nsa, xf, p)
        y1, y2 = _moe(h2, route, l, w_exp_gate, w_exp_up, w_exp_down)
        last = l == depth - 1
        g_next = norm_final if last else norm_mix[l + 1]
        xf, h = _add_norm(xf, y1, y2, g_next, F32 if last else BF16)
    return h.reshape(batch, seq, d)
```

```python
import functools
import math

import jax
import jax.numpy as jnp
import numpy as np
from jax import lax
from jax.experimental import pallas as pl
from jax.experimental.pallas import tpu as pltpu

F32 = jnp.float32
BF16 = jnp.bfloat16
HI = lax.Precision.HIGHEST

D_MODEL = 1024
RMS_EPS = 1e-6
NEG = -1e30
LOG2E = math.log2(math.e)

POOL_WINDOWS = (2, 4, 8, 16)
POOL_WIDTH = 512
POOL_GW = 128
POOL_HALO = 16

RW_HEADS = 8
RW_DH = 64
RW_WIDTH = 512
DECAY_LORA, AAA_LORA, GATE_LORA = 32, 32, 96
RW_LORA = DECAY_LORA + AAA_LORA + GATE_LORA
RW_COLS = 3 * RW_WIDTH + RW_LORA
RW_GN_EPS = 64e-5
RW_CHUNK = 64
RW_NB = 4

NSA_DH = 64
NSA_HEADS = 16
NSA_G = 4
NSA_HG = 4
NSA_KVW = NSA_G * NSA_DH
CMP_BLOCK, CMP_STRIDE, CMP_HIDDEN = 32, 16, 256
SLC_BLOCK = 64
N_SELECT = 8
WINDOW = 512
KV_PAD = WINDOW
QB = 64
NSA_NB = 8
FORCE_BONUS = 1e3
N_BUCKETS, MAX_EXACT, MAX_DISTANCE = 32, 16, 128
NEAR_BLOCKS = 3
FAR_CHUNK_BLOCKS = 8
CMP_NEAR = 32
CMP_AHEAD = (QB - CMP_BLOCK) // CMP_STRIDE + 1

N_GROUPS, EPG, N_EXPERTS, D_EXPERT = 4, 8, 32, 256
MOE_TM = 256

C_POOL, C_R, C_K, C_V, C_LORA, C_Q, C_GATE, P_COLS = 0, 512, 1024, 1536, 2048, 2304, 3328, 3584
SRC_Q, SRC_KV, SRC_GATE = 2208, 3232, 4768
N_KV = 6
LORA_PAD = 256
SRC_RW_END = POOL_WIDTH + RW_COLS

VMEM_LIMIT = 56 * 1024 * 1024


def _t5_bucket_np(dist):
    n = np.maximum(dist, 0)
    nf = np.maximum(n, 1).astype(np.float32)
    large = MAX_EXACT + (np.log(nf / MAX_EXACT) / math.log(MAX_DISTANCE / MAX_EXACT)
                         * (N_BUCKETS - MAX_EXACT)).astype(np.int32)
    large = np.minimum(large, N_BUCKETS - 1)
    return np.where(n < MAX_EXACT, n, large)


def _cp(sem, vmem=None):
    return pltpu.CompilerParams(dimension_semantics=sem, vmem_limit_bytes=vmem)


def _dot(a, b, precision=None):
    return jnp.dot(a, b, preferred_element_type=F32, precision=precision)


def _dot_nt(a, b, precision=None):
    return lax.dot_general(a, b, (((1,), (1,)), ((), ())), preferred_element_type=F32, precision=precision)


def _dot_tn(a, b, precision=None):
    return lax.dot_general(a, b, (((0,), (0,)), ((), ())), preferred_element_type=F32, precision=precision)


def _bdot(a, b):
    return _dot(a.astype(BF16), b.astype(BF16))


def _bdot_nt(a, b):
    return _dot_nt(a.astype(BF16), b.astype(BF16))


def _bdot_tn(a, b):
    return _dot_tn(a.astype(BF16), b.astype(BF16))


def _split(a):
    hi = a.astype(BF16)
    return hi, (a - hi.astype(F32)).astype(BF16)


def _dot3(a, b):
    ah, al = _split(a)
    bh, bl = _split(b)
    return _dot(ah, bh) + (_dot(ah, bl) + _dot(al, bh))


def _rms(x, g):
    return x * lax.rsqrt(jnp.mean(x * x, axis=-1, keepdims=True) + RMS_EPS) * g


def _norm_kernel(x_ref, g_ref, h_ref):
    h_ref[...] = _rms(x_ref[...], g_ref[...]).astype(h_ref.dtype)


def _norm(x, g, out_dtype, tm=512):
    n, d = x.shape
    return pl.pallas_call(
        _norm_kernel,
        out_shape=jax.ShapeDtypeStruct((n, d), out_dtype),
        grid=(n // tm,),
        in_specs=[pl.BlockSpec((tm, d), lambda i: (i, 0)), pl.BlockSpec((1, d), lambda i: (0, 0))],
        out_specs=pl.BlockSpec((tm, d), lambda i: (i, 0)),
        compiler_params=_cp(("parallel",)),
        name="rms_norm",
    )(x, g.reshape(1, d))


def _add_norm_kernel(x_ref, y1_ref, y2_ref, g_ref, xo_ref, h_ref):
    x = x_ref[...] + (y1_ref[...] + y2_ref[...])
    xo_ref[...] = x
    h_ref[...] = _rms(x, g_ref[...]).astype(h_ref.dtype)


def _add_norm(x, y1, y2, g, out_dtype, tm=512):
    n, d = x.shape
    row = pl.BlockSpec((tm, d), lambda i: (i, 0))
    return pl.pallas_call(
        _add_norm_kernel,
        out_shape=(jax.ShapeDtypeStruct((n, d), F32), jax.ShapeDtypeStruct((n, d), out_dtype)),
        grid=(n // tm,),
        in_specs=[row, row, row, pl.BlockSpec((1, d), lambda i: (0, 0))],
        out_specs=(row, row),
        compiler_params=_cp(("parallel",)),
        name="moe_combine_norm",
    )(x, y1, y2, g.reshape(1, d))


def _matmul_kernel(x_ref, w_ref, o_ref):
    o_ref[...] = _dot(x_ref[...], w_ref[...]).astype(o_ref.dtype)


def _matmul(x, w, tm=512, tn=P_COLS // 2):
    m, k = x.shape
    n = w.shape[1]
    return pl.pallas_call(
        _matmul_kernel,
        out_shape=jax.ShapeDtypeStruct((m, n), F32),
        grid=(n // tn, m // tm),
        in_specs=[pl.BlockSpec((tm, k), lambda j, i: (i, 0)), pl.BlockSpec((k, tn), lambda j, i: (0, j))],
        out_specs=pl.BlockSpec((tm, tn), lambda j, i: (i, j)),
        compiler_params=_cp(("parallel", "parallel"), VMEM_LIMIT),
        name="in_proj",
    )(x, w)


def _kv_proj_kernel(x_ref, w_ref, c_ref, ks_ref, vs_ref, kw_ref, vw_ref):
    i = pl.program_id(1)
    tm = x_ref.shape[0]
    head = i == 0
    res = _dot(x_ref[...], w_ref[...])
    col = lax.broadcasted_iota(jnp.int32, (tm, NSA_DH), 1)
    blk = ((i - 1) * tm + lax.broadcasted_iota(jnp.int32, (tm, NSA_DH), 0)) // SLC_BLOCK
    first_col = jnp.where(col == 0, 1.0, 0.0)
    extra = {2: jnp.where(head, 1.0, jnp.where(blk == col, 1.0, 0.0)), 3: first_col,
             4: jnp.where(head, first_col, 0.0), 5: first_col}
    outs = {2: ks_ref, 3: vs_ref, 4: kw_ref, 5: vw_ref}
    for t in range(N_KV):
        for g in range(NSA_G):
            c0 = (t * NSA_G + g) * NSA_DH
            val = res[:, c0:c0 + NSA_DH]
            if t < 2:
                c_ref[t, 0, g] = val.astype(c_ref.dtype)
            else:
                feat = jnp.where(head, 0.0, val)
                outs[t][0, g] = jnp.concatenate([feat, extra[t]], axis=1).astype(outs[t].dtype)


def _kv_proj(x, w, batch, seq):
    k = x.shape[1]
    tm = KV_PAD
    nt = seq // tm
    aug = jax.ShapeDtypeStruct((batch, NSA_G, KV_PAD + seq, 2 * NSA_DH), BF16)
    aug_spec = pl.BlockSpec((1, NSA_G, tm, 2 * NSA_DH), lambda b, i: (b, 0, i, 0))
    prev = lambda i: jnp.maximum(i - 1, 0)
    return pl.pallas_call(
        _kv_proj_kernel,
        out_shape=(jax.ShapeDtypeStruct((2, batch, NSA_G, seq, NSA_DH), BF16), aug, aug, aug, aug),
        grid=(batch, nt + 1),
        in_specs=[pl.BlockSpec((tm, k), lambda b, i: (b * nt + prev(i), 0)),
                  pl.BlockSpec(w.shape, lambda b, i: (0, 0))],
        out_specs=(pl.BlockSpec((2, 1, NSA_G, tm, NSA_DH), lambda b, i: (0, b, 0, prev(i), 0)),
                   aug_spec, aug_spec, aug_spec, aug_spec),
        compiler_params=_cp(("parallel", "arbitrary"), VMEM_LIMIT),
        name="kv_proj",
    )(x, w)


def _pool_kernel(u_ref, halo_ref, w_ref, scale_ref, o_ref, buf_ref, *, tile):
    i = pl.program_id(1)
    u = u_ref[...]
    buf_ref[POOL_HALO:, :] = u
    buf_ref[:POOL_HALO, :] = jnp.where(i > 0, halo_ref[...], 0.0)
    t = i * tile + lax.broadcasted_iota(jnp.int32, (tile, 1), 0)
    outs = []
    for gi, win in enumerate(POOL_WINDOWS):
        cols = slice(gi * POOL_GW, (gi + 1) * POOL_GW)
        s = u[:, cols]
        for j in range(1, win):
            s = s + buf_ref[POOL_HALO - j:POOL_HALO - j + tile, cols]
        cnt = jnp.minimum(t + 1, win).astype(F32)
        pooled = s / cnt - u[:, cols]
        outs.append(_dot(pooled.astype(BF16), w_ref[gi]))
    o_ref[...] = (jnp.concatenate(outs, axis=1) * scale_ref[...]).astype(o_ref.dtype)


def _pool(proj, w_grp, scale, batch, seq, tile=512):
    nt = seq // tile
    hb = tile // POOL_HALO
    return pl.pallas_call(
        functools.partial(_pool_kernel, tile=tile),
        out_shape=jax.ShapeDtypeStruct((batch * seq, POOL_WIDTH), BF16),
        grid=(batch, nt),
        in_specs=[
            pl.BlockSpec((tile, POOL_WIDTH), lambda b, i: (b * nt + i, 0)),
            pl.BlockSpec((POOL_HALO, POOL_WIDTH), lambda b, i: (jnp.maximum((b * nt + i) * hb - 1, 0), 0)),
            pl.BlockSpec((len(POOL_WINDOWS), POOL_GW, POOL_GW), lambda b, i: (0, 0, 0)),
            pl.BlockSpec((1, POOL_WIDTH), lambda b, i: (0, 0)),
        ],
        out_specs=pl.BlockSpec((tile, POOL_WIDTH), lambda b, i: (b * nt + i, 0)),
        scratch_shapes=[pltpu.VMEM((tile + POOL_HALO, POOL_WIDTH), F32)],
        compiler_params=_cp(("parallel", "parallel")),
        name="pool_mixer",
    )(proj, proj, w_grp.astype(BF16), scale.reshape(1, POOL_WIDTH))


def _token_shift(u, halo, mu, first):
    prev_row = jnp.where(first, 0.0, halo[7:8, :])
    rolled = pltpu.roll(u, 1, 0)
    row = lax.broadcasted_iota(jnp.int32, u.shape, 0)
    prev = jnp.where(row == 0, prev_row, rolled)
    return u + (prev - u) * mu


def _rwkv_chunk_kernel(r_ref, k_ref, v_ref, l_ref, rh_ref, kh_ref, vh_ref, lh_ref,
                       mur_ref, muk_ref, muv_ref, mul_ref, w0_ref, wup_ref, a0_ref, aup_ref, gup_ref,
                       kk_ref, ka_ref, rk_ref, bd_ref, qy_ref, mn_ref, g_ref, bonus_ref):
    first = pl.program_id(1) == 0
    c = RW_CHUNK
    r = _token_shift(r_ref[...], rh_ref[...], mur_ref[...], first)
    k = _token_shift(k_ref[...], kh_ref[...], muk_ref[...], first)
    v = _token_shift(v_ref[...], vh_ref[...], muv_ref[...], first)
    lo = _token_shift(l_ref[...], lh_ref[...], mul_ref[...], first)
    wd = lo[:, :DECAY_LORA]
    ad = lo[:, DECAY_LORA:DECAY_LORA + AAA_LORA]
    gd = lo[:, DECAY_LORA + AAA_LORA:RW_LORA]
    z = -(w0_ref[...] + _dot(jnp.tanh(wd), wup_ref[...], HI))
    w_log = -(jnp.maximum(z, 0.0) + jnp.log(1.0 + jnp.exp(-jnp.abs(z)))) - 0.5
    logw = -jnp.exp(w_log)
    a = jax.nn.sigmoid(a0_ref[...] + _dot(ad, aup_ref[...], HI))
    g_ref[0] = _dot(jax.nn.sigmoid(gd), gup_ref[...], HI)
    kkraw = k * kk_ref[...]
    k2 = k * (1.0 + (a - 1.0) * ka_ref[...])
    rkr = r * k2 * rk_ref[...]

    cum_all = logw
    trow = lax.broadcasted_iota(jnp.int32, logw.shape, 0) % c
    step = 1
    while step < c:
        cum_all = cum_all + jnp.where(trow >= step, pltpu.roll(cum_all, step, 0), 0.0)
        step *= 2

    ti = lax.broadcasted_iota(jnp.int32, (c, c), 0)
    si = lax.broadcasted_iota(jnp.int32, (c, c), 1)
    incl = ti >= si
    strict = ti > si
    eye = ti == si
    zeros = jnp.zeros((c, c), F32)
    bd = bd_ref[...]

    def head_sum(t):
        hi, lo = _split(t)
        return _dot(hi, bd) + _dot(lo, bd)

    kk = kkraw / jnp.maximum(jnp.sqrt(head_sum(kkraw * kkraw)), 1e-12)
    bonus_ref[0] = head_sum(rkr) * v
    nchunk = logw.shape[0] // c
    ends = [cum_all[(j + 1) * c - 1:(j + 1) * c, :] for j in range(nchunk)]
    cum_end = jnp.concatenate([jnp.broadcast_to(e, (c, e.shape[1])) for e in ends], axis=0)
    ginv = jnp.exp(-cum_all)
    gtail = jnp.exp(cum_end - cum_all)
    gend = jnp.exp(cum_end)
    kka = kk * a
    at = -kk * jnp.exp(cum_all - logw)
    bt = kka * ginv
    kt = k2 * ginv
    rt = r * jnp.exp(cum_all)
    bhat = kka * gtail
    khat = k2 * gtail

    heads = [(slice(j * c, (j + 1) * c), slice(h * RW_DH, (h + 1) * RW_DH))
             for j in range(nchunk) for h in range(RW_HEADS)]
    stack = lambda x, y, s: jnp.concatenate([x[s], y[s]], axis=0).astype(BF16)
    gram = [_dot_nt(stack(at, rt, s), stack(bt, kt, s)) for s in heads]
    a_ab = [jnp.where(strict, g[:c, :c], 0.0) for g in gram]
    a_ak = [jnp.where(strict, g[:c, c:], 0.0) for g in gram]
    incl2 = (lax.broadcasted_iota(jnp.int32, (c, 2 * c), 0)
             >= lax.broadcasted_iota(jnp.int32, (c, 2 * c), 1) % c)
    a_r = [jnp.where(incl2, g[c:, :], 0.0) for g in gram]
    p = a_ab
    tinv = [eye.astype(F32) + x for x in p]
    for _ in range(int(math.log2(c)) - 1):
        p = [_bdot(x, x) for x in p]
        tinv = [t + _bdot(t, x) for t, x in zip(tinv, p)]
    av = [_bdot(x, v[s]) for x, s in zip(a_ak, heads)]
    w12 = [_bdot(t, jnp.concatenate([at[s], x], axis=1)) for t, x, s in zip(tinv, av, heads)]
    zmat = [jnp.concatenate([w, jnp.concatenate([zeros, v[s]], axis=1)], axis=0).astype(BF16)
            for w, s in zip(w12, heads)]
    out1 = [_dot(x.astype(BF16), z) for x, z in zip(a_r, zmat)]
    out2 = [_dot_tn(stack(bhat, khat, s), z) for s, z in zip(heads, zmat)]
    qy = [o + jnp.concatenate([rt[s], zeros], axis=1) for o, s in zip(out1, heads)]
    for j in range(nchunk):
        qy_ref[0, j * c:(j + 1) * c, :] = jnp.concatenate(qy[j * RW_HEADS:(j + 1) * RW_HEADS], axis=1)
        for h in range(RW_HEADS):
            idx = j * RW_HEADS + h
            diag = jnp.where(eye, gend[heads[idx]], 0.0)
            mn_ref[0, j, h] = out2[idx] + jnp.concatenate([diag, zeros], axis=1)


def _rwkv_scan_kernel(qy_ref, mn_ref, g_ref, bonus_ref, gnw_ref, gnb_ref, bd_ref, y_ref, st_ref, *, batch):
    @pl.when(pl.program_id(0) == 0)
    def _():
        st_ref[...] = jnp.zeros_like(st_ref)

    bd = bd_ref[...]

    def head_mean(t):
        hi, lo = _split(t)
        return (_dot(hi, bd) + _dot(lo, bd)) * (1.0 / RW_DH)

    pairs = [(b, h) for b in range(batch) for h in range(RW_HEADS)]
    sts = [st_ref[b * RW_HEADS + h] for b, h in pairs]
    ys = [_bdot(qy_ref[b, :, 2 * RW_DH * h:2 * RW_DH * h + RW_DH], st)
          + qy_ref[b, :, 2 * RW_DH * h + RW_DH:2 * RW_DH * (h + 1)] for (b, h), st in zip(pairs, sts)]
    for (b, h), st in zip(pairs, sts):
        mn = mn_ref[b, 0, h]
        st_ref[b * RW_HEADS + h] = _dot3(mn[:, :RW_DH], st) + mn[:, RW_DH:]
    for b in range(batch):
        y = jnp.concatenate(ys[b * RW_HEADS:(b + 1) * RW_HEADS], axis=1)
        dev = y - head_mean(y)
        yn = dev * lax.rsqrt(head_mean(dev * dev) + RW_GN_EPS) * gnw_ref[...] + gnb_ref[...]
        y_ref[b] = ((yn + bonus_ref[b]) * g_ref[b]).astype(y_ref.dtype)


def _rwkv(proj, p, batch, seq):
    head_ones = jnp.asarray(np.kron(np.eye(RW_HEADS), np.ones((RW_DH, RW_DH))), BF16)
    c = RW_CHUNK
    nc = seq // c
    tb = RW_NB * c
    nt = seq // tb
    hb = tb // 8
    row512 = lambda col: pl.BlockSpec((tb, RW_WIDTH), lambda b, i: (b * nt + i, col))
    halo512 = lambda col: pl.BlockSpec((8, RW_WIDTH), lambda b, i: (jnp.maximum((b * nt + i) * hb - 1, 0), col))
    const = lambda shape: pl.BlockSpec(shape, lambda b, i: (0,) * len(shape))
    vec = const((1, RW_WIDTH))
    out_row = lambda w: pl.BlockSpec((1, tb, w), lambda b, i: (b, i, 0))
    qy, mn, g, bonus = pl.pallas_call(
        _rwkv_chunk_kernel,
        out_shape=(jax.ShapeDtypeStruct((batch, seq, 2 * RW_WIDTH), F32),
                   jax.ShapeDtypeStruct((batch, nc, RW_HEADS, RW_DH, 2 * RW_DH), F32),
                   jax.ShapeDtypeStruct((batch, seq, RW_WIDTH), F32),
                   jax.ShapeDtypeStruct((batch, seq, RW_WIDTH), F32)),
        grid=(batch, nt),
        in_specs=[row512(C_R // RW_WIDTH), row512(C_K // RW_WIDTH), row512(C_V // RW_WIDTH),
                  pl.BlockSpec((tb, LORA_PAD), lambda b, i: (b * nt + i, C_LORA // LORA_PAD)),
                  halo512(C_R // RW_WIDTH), halo512(C_K // RW_WIDTH), halo512(C_V // RW_WIDTH),
                  pl.BlockSpec((8, LORA_PAD), lambda b, i: (jnp.maximum((b * nt + i) * hb - 1, 0), C_LORA // LORA_PAD)),
                  vec, vec, vec, const((1, LORA_PAD)),
                  vec, const((DECAY_LORA, RW_WIDTH)), vec, const((AAA_LORA, RW_WIDTH)), const((GATE_LORA, RW_WIDTH)),
                  vec, vec, vec, const((RW_WIDTH, RW_WIDTH))],
        out_specs=(out_row(2 * RW_WIDTH),
                   pl.BlockSpec((1, RW_NB, RW_HEADS, RW_DH, 2 * RW_DH), lambda b, i: (b, i, 0, 0, 0)),
                   out_row(RW_WIDTH), out_row(RW_WIDTH)),
        compiler_params=_cp(("parallel", "parallel"), VMEM_LIMIT),
        name="rwkv_chunk",
    )(proj, proj, proj, proj, proj, proj, proj, proj,
      p["mu_r"], p["mu_k"], p["mu_v"], p["mu_l"], p["w0"], p["w_up"], p["a0"], p["a_up"], p["g_up"],
      p["k_k"], p["k_a"], p["r_k"], head_ones)

    full = lambda w: pl.BlockSpec((batch, c, w), lambda i: (0, i, 0))
    return pl.pallas_call(
        functools.partial(_rwkv_scan_kernel, batch=batch),
        out_shape=jax.ShapeDtypeStruct((batch, seq, RW_WIDTH), BF16),
        grid=(nc,),
        in_specs=[full(2 * RW_WIDTH),
                  pl.BlockSpec((batch, 1, RW_HEADS, RW_DH, 2 * RW_DH), lambda i: (0, i, 0, 0, 0)),
                  full(RW_WIDTH), full(RW_WIDTH),
                  pl.BlockSpec((1, RW_WIDTH), lambda i: (0, 0)), pl.BlockSpec((1, RW_WIDTH), lambda i: (0, 0)),
                  pl.BlockSpec((RW_WIDTH, RW_WIDTH), lambda i: (0, 0))],
        out_specs=full(RW_WIDTH),
        scratch_shapes=[pltpu.VMEM((batch * RW_HEADS, RW_DH, RW_DH), F32)],
        compiler_params=_cp(("arbitrary",), VMEM_LIMIT),
        name="rwkv_scan",
    )(qy, mn, g, bonus, p["gn_w"], p["gn_b"], head_ones).reshape(batch * seq, RW_WIDTH)


def _gelu_tanh(x):
    return 0.5 * x * (1.0 + jnp.tanh(math.sqrt(2.0 / math.pi) * (x + 0.044715 * (x * x * x))))


def _compress_kernel(x_ref, pos_ref, w1_ref, w2_ref, o_ref):
    half = CMP_STRIDE * NSA_DH
    x = x_ref[0, 0, 0]
    w1 = w1_ref[0]
    posb = _dot(pos_ref[0], w1)[0:1, :]
    h1 = _dot(x, w1[:half])
    h2 = _dot(x, w1[half:])
    n = h2.shape[0]
    row = lax.broadcasted_iota(jnp.int32, h2.shape, 0)
    h2s = jnp.where(row < n - 1, pltpu.roll(h2, n - 1, 0), 0.0)
    hid = _gelu_tanh(h1 + h2s + posb)
    o_ref[0, 0, 0] = _dot(hid.astype(BF16), w2_ref[0]).astype(o_ref.dtype)


def _compress(xkv, pos, w1, w2, batch, seq):
    nr = seq // CMP_STRIDE
    wide = CMP_STRIDE * NSA_DH
    return pl.pallas_call(
        _compress_kernel,
        out_shape=jax.ShapeDtypeStruct((2, batch, NSA_G, nr, NSA_DH), BF16),
        grid=(2, batch, NSA_G),
        in_specs=[pl.BlockSpec((1, 1, 1, nr, wide), lambda t, b, g: (t, b, g, 0, 0)),
                  pl.BlockSpec((1, 8, 2 * wide), lambda t, b, g: (t, 0, 0)),
                  pl.BlockSpec((1, 2 * wide, CMP_HIDDEN), lambda t, b, g: (t, 0, 0)),
                  pl.BlockSpec((1, CMP_HIDDEN, NSA_DH), lambda t, b, g: (t, 0, 0))],
        out_specs=pl.BlockSpec((1, 1, 1, nr, NSA_DH), lambda t, b, g: (t, b, g, 0, 0)),
        compiler_params=_cp(("parallel", "parallel", "parallel")),
        name="nsa_compress",
    )(xkv, pos, w1, w2)


def _softmax_parts(s):
    m = jnp.max(s, axis=1, keepdims=True)
    e = jnp.exp(s - m)
    return m, e, jnp.sum(e, axis=1, keepdims=True)


def _lane_tile_fold(x, op, init):
    for t in range(x.shape[1] // 128):
        init = op(init, x[:, 128 * t:128 * (t + 1)])
    return init


def _nsa_kernel(q_ref, gate_ref, kc_ref, vc_ref, ks_ref, vs_ref, kw_ref, vw_ref,
                tblc_ref, tbln_ref, tblw_ref, ovt_ref, gexp_ref, o_ref, s_ref, *, ncmp):
    nb = NSA_NB
    blocks = [pl.program_id(2) * nb + u for u in range(nb)]
    rows = NSA_HG * QB
    each = lambda f, *ls: [f(*xs) for xs in zip(*ls)]
    qt = q_ref[...]
    qs = [jnp.concatenate([qt[u * QB:(u + 1) * QB, NSA_DH * h:NSA_DH * (h + 1)] for h in range(NSA_HG)], axis=0)
          * (NSA_DH ** -0.5 * LOG2E) for u in range(nb)]
    qb = each(lambda x: x.astype(BF16), qs)
    rmax = lambda x: jnp.max(x, axis=1, keepdims=True)
    rsum = lambda x: jnp.sum(x, axis=1, keepdims=True)

    win_w = WINDOW + QB
    kw = [kw_ref[0, 0, pl.ds(pl.multiple_of(i * QB, QB), win_w), :] for i in blocks]
    vw = [vw_ref[0, 0, pl.ds(pl.multiple_of(i * QB, QB), win_w), :] for i in blocks]
    flag = lax.broadcasted_iota(jnp.int32, (rows, kw_ref.shape[3] - NSA_DH), 1) == 0
    q_win = each(lambda x: jnp.concatenate([x, jnp.where(flag, NEG, 0.0)], axis=1).astype(BF16), qs)
    tblw = tblw_ref[0]
    pv = lambda x: x[:, :NSA_DH] / x[:, NSA_DH:NSA_DH + 1]
    s_w = each(lambda x, k: _dot_nt(x, k) + tblw, q_win, kw)
    m_w = each(rmax, s_w)
    e_w = each(lambda s, m: jnp.exp2(s - m), s_w, m_w)
    o_w = each(lambda e, v: pv(_dot(e.astype(BF16), v)), e_w, vw)

    kc = kc_ref[0, 0]
    vc = vc_ref[0, 0]
    tblc = tblc_ref[0]
    cidx = lax.broadcasted_iota(jnp.int32, (rows, ncmp), 1)
    qrow = lax.broadcasted_iota(jnp.int32, (rows, 1), 0) % QB
    lc = [jnp.where(cidx < (QB // CMP_STRIDE) * i + CMP_AHEAD,
                    _dot_nt(x, kc) + pltpu.roll(tblc, (4 * i - CMP_NEAR // 2 + ncmp) % ncmp, 1), NEG)
          for x, i in zip(qb, blocks)]
    m_c = each(rmax, lc)
    e_c = each(lambda s, m: jnp.exp2(s - m), lc, m_c)
    den_c = each(rsum, e_c)
    pc = [e * jnp.where(i * QB + qrow >= CMP_BLOCK - 1, 1.0 / d, 0.0) for e, d, i in zip(e_c, den_c, blocks)]
    o_c = each(lambda x: _dot(x.astype(BF16), vc), pc)
    pcs_hi, pcs_lo = _split(jnp.concatenate(
        each(lambda x: x[0:QB] + x[QB:2 * QB] + x[2 * QB:3 * QB] + x[3 * QB:4 * QB], pc), axis=0))
    ovt = ovt_ref[...]
    imp = _dot_nt(ovt, pcs_hi) + _dot_nt(ovt, pcs_lo)

    nslc = ovt.shape[0]
    nidx = lax.broadcasted_iota(jnp.int32, (nslc, nb * QB), 0)
    cur = blocks[0] + lax.broadcasted_iota(jnp.int32, (nslc, nb * QB), 1) // QB
    forced = (nidx == 0) | (nidx == cur) | (nidx == cur - 1)
    work = jnp.where(nidx <= cur, imp + jnp.where(forced, FORCE_BONUS, 0.0), -1.0)
    sel_all = jnp.zeros((nslc, nb * QB), F32)
    for _ in range(N_SELECT):
        m = jnp.max(work, axis=0, keepdims=True)
        first = jnp.min(jnp.where(work == m, nidx, nslc), axis=0, keepdims=True)
        pick = nidx == first
        sel_all = jnp.where(pick & (m >= 0.0), 1.0, sel_all)
        work = jnp.where(pick, -2.0, work)
    sel_t = sel_all.T

    ind_w = ks_ref.shape[3] - NSA_DH
    selq = [sel_t[u * QB:(u + 1) * QB] for u in range(nb)]
    if nslc < ind_w:
        selq = each(lambda x: jnp.concatenate([x, jnp.zeros((QB, ind_w - nslc), F32)], axis=1), selq)
    bidx = lax.broadcasted_iota(jnp.int32, (QB, ind_w), 1)
    tile4 = lambda t: jnp.concatenate([t] * NSA_HG, axis=0)
    with_mask = lambda x, keep: jnp.concatenate([x, tile4(jnp.where(keep, 0.0, NEG))], axis=1).astype(BF16)
    q_sel = each(lambda x, s: with_mask(x, s > 0.0), qs, selq)
    q_far = jnp.concatenate([with_mask(x, (s > 0.0) & (bidx <= i - NEAR_BLOCKS))
                             for x, s, i in zip(qs, selq, blocks)], axis=0)
    near_w = NEAR_BLOCKS * SLC_BLOCK
    pad_s = KV_PAD
    near0 = pad_s - (NEAR_BLOCKS - 1) * SLC_BLOCK
    kn = [ks_ref[0, 0, pl.ds(pl.multiple_of(near0 + i * QB, QB), near_w), :] for i in blocks]
    vn = [vs_ref[0, 0, pl.ds(pl.multiple_of(near0 + i * QB, QB), near_w), :] for i in blocks]
    tbln = tbln_ref[0]
    s_near = each(lambda x, k: _dot_nt(x, k) + tbln, q_sel, kn)
    m_near = jnp.concatenate(each(rmax, s_near), axis=0)

    far_w = FAR_CHUNK_BLOCKS * SLC_BLOCK
    n_far =jnp.maximum(blocks[-1] - (NEAR_BLOCKS - 1) + FAR_CHUNK_BLOCKS - 1, 0) // FAR_CHUNK_BLOCKS

    def far_logits(j, mvec):
        start = pl.multiple_of(j * far_w, far_w)
        kf = ks_ref[0, 0, pl.ds(pl.multiple_of(pad_s + j * far_w, SLC_BLOCK), far_w), :]
        s = _dot_nt(q_far, kf)
        s_ref[:, pl.ds(start, far_w)] = s
        return _lane_tile_fold(s, jnp.maximum, mvec)

    mvec = lax.fori_loop(0, n_far, far_logits, jnp.full((nb * rows, 128), NEG, F32))
    m_s = jnp.maximum(m_near, rmax(mvec))

    e_near = [jnp.exp2(s - m_s[u * rows:(u + 1) * rows]) for u, s in enumerate(s_near)]
    acc0 = jnp.concatenate(each(lambda e, v: _dot(e.astype(BF16), v), e_near, vn), axis=0)

    def far_values(j, acc):
        start = pl.multiple_of(j * far_w, far_w)
        vf = vs_ref[0, 0, pl.ds(pl.multiple_of(pad_s + j * far_w, SLC_BLOCK), far_w), :]
        e = jnp.exp2(s_ref[:, pl.ds(start, far_w)] - m_s)
        return acc + _dot(e.astype(BF16), vf)

    o_s = pv(lax.fori_loop(0, n_far, far_values, acc0))

    g_hi, g_lo = _split(jax.nn.sigmoid(gate_ref[0, 0]))
    gexp = gexp_ref[...]
    ge = _dot(g_hi, gexp) + _dot(g_lo, gexp)
    for u in range(nb):
        outs = []
        for h in range(NSA_HG):
            r0 = slice(h * QB, (h + 1) * QB)
            gt = lambda br: ge[u * QB:(u + 1) * QB, (3 * h + br) * 128:(3 * h + br) * 128 + NSA_DH]
            outs.append(gt(0) * o_c[u][r0] + gt(1) * o_s[u * rows + h * QB:u * rows + (h + 1) * QB]
                        + gt(2) * o_w[u][r0])
        o_ref[u * QB:(u + 1) * QB, :] = jnp.concatenate(outs, axis=1).astype(o_ref.dtype)


def _bias_tables(rel_bias, ncmp):
    def table(dist, keep, base):
        onehot = jnp.asarray(np.eye(N_BUCKETS, dtype=np.float32)[_t5_bucket_np(dist)])
        tbl = jnp.einsum("qkb,bh->qkh", onehot, rel_bias, precision=HI)
        tbl = (tbl - base) * LOG2E
        tbl = jnp.where(jnp.asarray(keep)[..., None], tbl, NEG)
        k = dist.shape[1]
        return tbl.transpose(2, 0, 1).reshape(NSA_G, NSA_HG * QB, k)

    far = rel_bias[N_BUCKETS - 1]
    qi = np.arange(QB)[:, None]
    dist_c = qi - CMP_STRIDE * (np.arange(CMP_NEAR)[None, :] - CMP_NEAR // 2) - (CMP_BLOCK - 1)
    tblc = table(dist_c, dist_c >= 0, far)
    tblc = tblc * jnp.asarray(np.arange(CMP_NEAR) < CMP_NEAR // 2 + CMP_AHEAD, F32)
    tblc = jnp.pad(tblc, ((0, 0), (0, 0), (0, ncmp - CMP_NEAR)))
    jn = np.arange(NEAR_BLOCKS * SLC_BLOCK)[None, :]
    dist_n = (NEAR_BLOCKS - 1) * SLC_BLOCK + qi - jn
    tbln = table(dist_n, dist_n >= 0, far)
    jw = np.arange(WINDOW + QB)[None, :]
    dist_w = WINDOW + qi - jw
    tblw = table(dist_w, (dist_w >= 0) & (dist_w < WINDOW), 0.0)
    return tblc, tbln, tblw


def _nsa(proj, gate_logits, kvc, ks, vs, kw, vw, tables, batch, seq):
    nq = seq // QB
    ncmp = seq // CMP_STRIDE
    nslc = seq // SLC_BLOCK
    tblc, tbln, tblw = tables
    cstart = np.arange(ncmp) * CMP_STRIDE
    sstart = np.arange(nslc) * SLC_BLOCK
    overlap_t = ((cstart[None, :] <= sstart[:, None] + SLC_BLOCK - 1)
                 & (cstart[None, :] + CMP_BLOCK - 1 >= sstart[:, None])
                 & (cstart[None, :] + CMP_BLOCK <= seq)).astype(np.float32)
    rows = NSA_HG * QB
    n_gate = 3 * NSA_HG
    gate_spread = (np.arange(n_gate * 128)[None, :] // 128 == np.arange(n_gate)[:, None]) \
        & (np.arange(n_gate * 128)[None, :] % 128 < NSA_DH)
    kv_spec = lambda t: pl.BlockSpec((1, 1) + t.shape[2:], lambda b, g, i: (b, g, 0, 0))
    tbl_spec = lambda k: pl.BlockSpec((1, rows, k), lambda b, g, i: (g, 0, 0))
    qcol = C_Q // (NSA_HG * NSA_DH)
    return pl.pallas_call(
        functools.partial(_nsa_kernel, ncmp=ncmp),
        out_shape=jax.ShapeDtypeStruct((batch * seq, NSA_HEADS * NSA_DH), BF16),
        grid=(batch, NSA_G, nq // NSA_NB),
        in_specs=[pl.BlockSpec((NSA_NB * QB, NSA_HG * NSA_DH), lambda b, g, i: (b * (nq // NSA_NB) + i, qcol + g)),
                  pl.BlockSpec((1, 1, NSA_NB * QB, 3 * NSA_HG), lambda b, g, i: (b, g, i, 0)),
                  pl.BlockSpec((1, 1, ncmp, NSA_DH), lambda b, g, i: (b, g, 0, 0)),
                  pl.BlockSpec((1, 1, ncmp, NSA_DH), lambda b, g, i: (b, g, 0, 0)),
                  kv_spec(ks), kv_spec(vs), kv_spec(kw), kv_spec(vw),
                  tbl_spec(ncmp), tbl_spec(NEAR_BLOCKS * SLC_BLOCK), tbl_spec(WINDOW + QB),
                  pl.BlockSpec((nslc, ncmp), lambda b, g, i: (0, 0)),
                  pl.BlockSpec(gate_spread.shape, lambda b, g, i: (0, 0))],
        out_specs=pl.BlockSpec((NSA_NB * QB, NSA_HG * NSA_DH), lambda b, g, i: (b * (nq // NSA_NB) + i, g)),
        scratch_shapes=[pltpu.VMEM((NSA_NB * rows, seq), F32)],
        compiler_params=_cp(("parallel", "parallel", "arbitrary"), VMEM_LIMIT),
        name="nsa_attention",
    )(proj, gate_logits, kvc[0], kvc[1], ks, vs, kw, vw, tblc, tbln, tblw, jnp.asarray(overlap_t, BF16),
      jnp.asarray(gate_spread, BF16))


def _merge_kernel(h_ref, yp_ref, yr_ref, yn_ref, x_ref, wbp_ref, wbr_ref, wbn_ref, wm_ref, bm_ref, wo_ref,
                  gn_ref, wr_ref, br_ref, xo_ref, h2_ref, route_ref):
    d = D_MODEL
    gl = jax.nn.sigmoid(_dot(h_ref[...], wm_ref[...]) + bm_ref[...])
    merged = (gl[:, :d] * _dot(yp_ref[...], wbp_ref[...]) + gl[:, d:2 * d] * _dot(yr_ref[...], wbr_ref[...])
              + gl[:, 2 * d:] * _dot(yn_ref[...], wbn_ref[...]))
    x = x_ref[...] + _dot(merged.astype(BF16), wo_ref[...])
    xo_ref[...] = x
    h2 = _rms(x, gn_ref[...])
    h2_ref[...] = h2.astype(h2_ref.dtype)

    h2_hi, h2_lo = _split(h2)
    nl = br_ref.shape[1]
    both = _dot(h2_hi, wr_ref[...])
    logits = both[:, :nl] + (both[:, nl:] + _dot(h2_lo, wr_ref[:, :nl])) + br_ref[...]
    lane = lax.broadcasted_iota(jnp.int32, logits.shape, 1)
    big = logits.shape[1]
    lg = jnp.where(lane < N_GROUPS, logits, NEG)
    mg = jnp.max(lg, axis=1, keepdims=True)
    p_top = 1.0 / jnp.sum(jnp.exp(lg - mg), axis=1, keepdims=True)
    grp = jnp.min(jnp.where(lg == mg, lane, big), axis=1, keepdims=True)
    lo = N_GROUPS + EPG * grp
    le = jnp.where((lane >= lo) & (lane < lo + EPG), logits, NEG)
    e1 = jnp.max(le, axis=1, keepdims=True)
    i1 = jnp.min(jnp.where(le == e1, lane, big), axis=1, keepdims=True)
    le = jnp.where(lane == i1, NEG, le)
    e2 = jnp.max(le, axis=1, keepdims=True)
    i2 = jnp.min(jnp.where(le == e2, lane, big), axis=1, keepdims=True)
    t = jnp.exp(e2 - e1)
    w1 = p_top / (1.0 + t)
    w2 = p_top * t / (1.0 + t)
    route_ref[...] = jnp.where(lane == 0, (i1 - N_GROUPS).astype(F32),
                               jnp.where(lane == 1, (i2 - N_GROUPS).astype(F32),
                                         jnp.where(lane == 2, w1, jnp.where(lane == 3, w2, 0.0))))


def _merge(h, y_pool, y_rwkv, y_nsa, x, p, tm=256):
    n, d = x.shape
    row = lambda w: pl.BlockSpec((tm, w), lambda i: (i, 0))
    const = lambda a: pl.BlockSpec(a.shape, lambda i: (0, 0))
    ws = [p["wb_pool"], p["wb_rwkv"], p["wb_nsa"], p["w_merge"], p["b_merge"], p["w_out"],
          p["norm_ffn"], p["w_router"], p["b_router"]]
    return pl.pallas_call(
        _merge_kernel,
        out_shape=(jax.ShapeDtypeStruct((n, d), F32), jax.ShapeDtypeStruct((n, d), F32),
                   jax.ShapeDtypeStruct((n, 128), F32)),
        grid=(n // tm,),
        in_specs=[row(d), row(POOL_WIDTH), row(RW_WIDTH), row(d), row(d)] + [const(w) for w in ws],
        out_specs=(row(d), row(d), row(128)),
        compiler_params=_cp(("parallel",), VMEM_LIMIT),
        name="merge_router",
    )(h, y_pool, y_rwkv, y_nsa, x, *ws)


def _expert_kernel(te_ref, nt_ref, tok_ref, h_hbm, w_ref, wg_ref, wu_ref, wd_ref, o_ref,
                   xbuf, sem, wg_s, wu_s, wd_s):
    i = pl.program_id(0)
    tm = MOE_TM
    n_tiles = nt_ref[0]

    def row_copy(tile, slot, r):
        return pltpu.make_async_copy(h_hbm.at[pl.ds(tok_ref[tile * tm + r], 1), :],
                                     xbuf.at[slot, pl.ds(r, 1), :], sem.at[slot])

    def tile_wait(slot):
        pltpu.make_async_copy(h_hbm.at[pl.ds(0, tm), :], xbuf.at[slot], sem.at[slot]).wait()

    @pl.when((i == 0) & (n_tiles > 0))
    def _():
        def body(r, carry):
            row_copy(0, 0, r).start()
            return carry
        lax.fori_loop(0, tm, body, 0, unroll=8)

    @pl.when((i == 0) | (te_ref[i] != te_ref[jnp.maximum(i - 1, 0)]))
    def _():
        wg_s[...] = wg_ref[0, 0].astype(BF16)
        wu_s[...] = wu_ref[0, 0].astype(BF16)
        wd_s[...] = wd_ref[0, 0].astype(BF16)

    @pl.when(i < n_tiles)
    def _():
        slot = i % 2
        quarter = tm // 4

        def fetch_next(part):
            for r in range(part * quarter, (part + 1) * quarter):
                row_copy(i + 1, 1 - slot, r).start()

        tile_wait(slot)
        xb = xbuf[slot].astype(BF16)
        fetch_next(0)
        gate = _dot(xb, wg_s[...])
        fetch_next(1)
        up = _dot(xb, wu_s[...])
        fetch_next(2)
        hid = (gate * jax.nn.sigmoid(gate) * up).astype(BF16)
        fetch_next(3)
        o_ref[...] = w_ref[...] * _dot(hid, wd_s[...])

    @pl.when((i == n_tiles) & (n_tiles > 0))
    def _():
        tile_wait(i % 2)

    @pl.when(i >= nt_ref[0])
    def _():
        o_ref[...] = jnp.zeros_like(o_ref)


def _experts(h2, rowtok, roww, tile_expert, n_tiles, layer, wg, wu, wd):
    d = h2.shape[1]
    r = rowtok.shape[0]
    tm = MOE_TM
    return pl.pallas_call(
        _expert_kernel,
        out_shape=jax.ShapeDtypeStruct((r, d), F32),
        grid_spec=pltpu.PrefetchScalarGridSpec(
            num_scalar_prefetch=3,
            grid=(r // tm,),
            in_specs=[pl.BlockSpec(memory_space=pl.ANY),
                      pl.BlockSpec((tm, 1), lambda i, te, nt, tok: (i, 0)),
                      pl.BlockSpec((1, 1, d, D_EXPERT), lambda i, te, nt, tok: (layer, te[i], 0, 0)),
                      pl.BlockSpec((1, 1, d, D_EXPERT), lambda i, te, nt, tok: (layer, te[i], 0, 0)),
                      pl.BlockSpec((1, 1, D_EXPERT, d), lambda i, te, nt, tok: (layer, te[i], 0, 0))],
            out_specs=pl.BlockSpec((tm, d), lambda i, te, nt, tok: (i, 0)),
            scratch_shapes=[pltpu.VMEM((2, tm, d), F32), pltpu.SemaphoreType.DMA((2,)),
                            pltpu.VMEM((d, D_EXPERT), BF16), pltpu.VMEM((d, D_EXPERT), BF16),
                            pltpu.VMEM((D_EXPERT, d), BF16)]),
        compiler_params=_cp(("arbitrary",), VMEM_LIMIT),
        name="moe_experts",
    )(tile_expert, n_tiles, rowtok, h2, roww, wg, wu, wd)


def _moe(h2, route, layer, wg, wu, wd):
    n = h2.shape[0]
    tm = MOE_TM
    r = 2 * n + (N_EXPERTS + 1) * tm
    ids = route[:, 0:2].astype(jnp.int32).reshape(-1)
    wts = route[:, 2:4].reshape(-1)
    onehot = (ids[:, None] == jnp.arange(N_EXPERTS)[None, :]).astype(jnp.int32)
    rank = jnp.sum((jnp.cumsum(onehot, axis=0) - onehot) * onehot, axis=1)
    counts = jnp.sum(onehot, axis=0)
    tiles = (counts + tm - 1) // tm
    tile_end = jnp.cumsum(tiles)
    starts = (tile_end - tiles) * tm
    pos = starts[ids] + rank
    row_assign = jnp.full((r,), -1, jnp.int32).at[pos].set(
        jnp.arange(2 * n, dtype=jnp.int32), unique_indices=True, mode="promise_in_bounds")
    rowtok = jnp.maximum(row_assign, 0) // 2
    roww = jnp.where(row_assign >= 0, wts[jnp.maximum(row_assign, 0)], 0.0)
    n_tiles = tile_end[-1:].astype(jnp.int32)
    tile_expert = jnp.minimum(jnp.sum(tile_end[None, :] <= jnp.arange(r // tm)[:, None], axis=1),
                              N_EXPERTS - 1).astype(jnp.int32)
    ys = _experts(h2, rowtok, roww.reshape(r, 1), tile_expert, n_tiles, layer, wg, wu, wd)
    return ys[pos[0::2]], ys[pos[1::2]]


def _layer_params(l, a):
    f = lambda t: t[l]
    row = lambda t: t[l].reshape(1, -1)
    w_in = a["w_in"][l]
    w_in_p = jnp.concatenate([w_in[:, :SRC_RW_END], jnp.zeros((D_MODEL, C_Q - SRC_RW_END), F32),
                              w_in[:, SRC_Q:SRC_KV], w_in[:, SRC_GATE:],
                              jnp.zeros((D_MODEL, P_COLS - C_GATE - (w_in.shape[1] - SRC_GATE)), F32)], axis=1)
    mu = a["rw_mu"][l]
    wb = a["w_branch"][l].astype(BF16)
    w_router = jnp.zeros((D_MODEL, 128), F32)
    w_router = w_router.at[:, :N_GROUPS].set(a["w_router_grp"][l]).at[:, N_GROUPS:N_GROUPS + N_EXPERTS].set(
        a["w_router_exp"][l])
    b_router = jnp.zeros((1, 128), F32)
    b_router = b_router.at[0, :N_GROUPS].set(a["b_router_grp"][l]).at[0, N_GROUPS:N_GROUPS + N_EXPERTS].set(
        a["b_router_exp"][l])
    pos = jnp.stack([a["cmp_pos_k"][l].reshape(-1), a["cmp_pos_v"][l].reshape(-1)])
    return {
        "w_in": w_in_p.astype(BF16), "w_kv": w_in[:, SRC_KV:SRC_GATE].astype(BF16),
        "pool_w": f(a["pool_w"]), "pool_scale": f(a["pool_scale"]),
        "mu_r": mu[None, 0:RW_WIDTH], "mu_k": mu[None, RW_WIDTH:2 * RW_WIDTH],
        "mu_v": mu[None, 2 * RW_WIDTH:3 * RW_WIDTH],
        "mu_l": jnp.concatenate([mu[3 * RW_WIDTH:], jnp.zeros((LORA_PAD - RW_LORA,), F32)])[None],
        "w0": row(a["rw_w0"]), "w_up": f(a["rw_w_up"]), "a0": row(a["rw_a0"]), "a_up": f(a["rw_a_up"]),
        "g_up": f(a["rw_g_up"]), "k_k": row(a["rw_k_k"]), "k_a": row(a["rw_k_a"]), "r_k": row(a["rw_r_k"]),
        "gn_w": row(a["rw_gn_w"]), "gn_b": row(a["rw_gn_b"]),
        "cmp_pos": jnp.broadcast_to(pos[:, None, :], (2, 8, pos.shape[1])).astype(BF16),
        "cmp_w1": jnp.stack([a["cmp_w1_k"][l], a["cmp_w1_v"][l]]).astype(BF16),
        "cmp_w2": jnp.stack([a["cmp_w2_k"][l], a["cmp_w2_v"][l]]).astype(BF16),
        "wb_pool": wb[:POOL_WIDTH], "wb_rwkv": wb[POOL_WIDTH:POOL_WIDTH + RW_WIDTH],
        "wb_nsa": wb[POOL_WIDTH + RW_WIDTH:],
        "w_merge": a["w_merge"][l].astype(BF16), "b_merge": row(a["b_merge"]),
        "w_out": a["w_out"][l].astype(BF16), "norm_ffn": row(a["norm_ffn"]),
        "w_router": jnp.concatenate(_split(w_router), axis=1), "b_router": b_router,
    }


def _mixers(proj, kv, p, tables, batch, seq):
    kv_cmp, ks, vs, kw, vw = kv
    y_pool = _pool(proj, p["pool_w"], p["pool_scale"], batch, seq)
    y_rwkv = _rwkv(proj, p, batch, seq)
    xkv = kv_cmp.reshape(2, batch, NSA_G, seq // CMP_STRIDE, CMP_STRIDE * NSA_DH)
    kvc = _compress(xkv, p["cmp_pos"], p["cmp_w1"], p["cmp_w2"], batch, seq)
    gate_logits = proj[:, C_GATE:C_GATE + 3 * NSA_HEADS].reshape(batch, seq, NSA_G, 3 * NSA_HG).transpose(0, 2, 1, 3)
    y_nsa = _nsa(proj, gate_logits, kvc, ks, vs, kw, vw, tables, batch, seq)
    return y_pool, y_rwkv, y_nsa


def kernel(x, rel_bias, norm_mix, w_in, pool_w, pool_scale, rw_mu, rw_w0, rw_w_up, rw_a0, rw_a_up, rw_g_up, rw_k_k, rw_k_a, rw_r_k, rw_gn_w, rw_gn_b, cmp_pos_k, cmp_w1_k, cmp_w2_k, cmp_pos_v, cmp_w1_v, cmp_w2_v, w_branch, w_merge, b_merge, w_out, norm_ffn, w_router_grp, b_router_grp, w_router_exp, b_router_exp, w_exp_gate, w_exp_up, w_exp_down, norm_final):
    a = dict(w_in=w_in, pool_w=pool_w, pool_scale=pool_scale, rw_mu=rw_mu, rw_w0=rw_w0, rw_w_up=rw_w_up,
             rw_a0=rw_a0, rw_a_up=rw_a_up, rw_g_up=rw_g_up, rw_k_k=rw_k_k, rw_k_a=rw_k_a, rw_r_k=rw_r_k,
             rw_gn_w=rw_gn_w, rw_gn_b=rw_gn_b, cmp_pos_k=cmp_pos_k, cmp_w1_k=cmp_w1_k, cmp_w2_k=cmp_w2_k,
             cmp_pos_v=cmp_pos_v, cmp_w1_v=cmp_w1_v, cmp_w2_v=cmp_w2_v, w_branch=w_branch, w_merge=w_merge,
             b_merge=b_merge, w_out=w_out, norm_ffn=norm_ffn, w_router_grp=w_router_grp,
             b_router_grp=b_router_grp, w_router_exp=w_router_exp, b_router_exp=b_router_exp,
             w_exp_gate=w_exp_gate, w_exp_up=w_exp_up, w_exp_down=w_exp_down)
    batch, seq, d = x.shape
    depth = norm_mix.shape[0]
    tables = _bias_tables(rel_bias, seq // CMP_STRIDE)
    xf = x.reshape(batch * seq, d)
    h = _norm(xf, norm_mix[0], BF16)
    for l in range(depth):
        p = _layer_params(l, a)
        proj = _matmul(h, p["w_in"])
        kv = _kv_proj(h, p["w_kv"], batch, seq)
        y_pool, y_rwkv, y_nsa = _mixers(proj, kv, p, tables, batch, seq)
        xf, h2, route = _merge(h, y_pool, y_rwkv, y_nsa, xf, p)
        y1, y2 = _moe(h2, route, l, w_exp_gate, w_exp_up, w_exp_down)
        last = l == depth - 1
        g_next = norm_final if last else norm_mix[l + 1]
        xf, h = _add_norm(xf, y1, y2, g_next, F32 if last else BF16)
    return h.reshape(batch, seq, d)
```

```python
import functools
import math

import jax
import jax.numpy as jnp
import numpy as np
from jax import lax
from jax.experimental import pallas as pl
from jax.experimental.pallas import tpu as pltpu

F32 = jnp.float32
BF16 = jnp.bfloat16
HI = lax.Precision.HIGHEST

D_MODEL = 1024
RMS_EPS = 1e-6
NEG = -1e30
LOG2E = math.log2(math.e)

POOL_WINDOWS = (2, 4, 8, 16)
POOL_WIDTH = 512
POOL_GW = 128
POOL_HALO = 16

RW_HEADS = 8
RW_DH = 64
RW_WIDTH = 512
DECAY_LORA, AAA_LORA, GATE_LORA = 32, 32, 96
RW_LORA = DECAY_LORA + AAA_LORA + GATE_LORA
RW_COLS = 3 * RW_WIDTH + RW_LORA
RW_GN_EPS = 64e-5
RW_CHUNK = 64
RW_NB = 4

NSA_DH = 64
NSA_HEADS = 16
NSA_G = 4
NSA_HG = 4
NSA_KVW = NSA_G * NSA_DH
CMP_BLOCK, CMP_STRIDE, CMP_HIDDEN = 32, 16, 256
SLC_BLOCK = 64
N_SELECT = 8
WINDOW = 512
KV_PAD = WINDOW
QB = 64
NSA_NB = 8
FORCE_BONUS = 1e3
N_BUCKETS, MAX_EXACT, MAX_DISTANCE = 32, 16, 128
NEAR_BLOCKS = 3
FAR_CHUNK_BLOCKS = 8
CMP_NEAR = 32
CMP_AHEAD = (QB - CMP_BLOCK) // CMP_STRIDE + 1

N_GROUPS, EPG, N_EXPERTS, D_EXPERT = 4, 8, 32, 256
MOE_TM = 256

C_POOL, C_R, C_K, C_V, C_LORA, C_Q, C_GATE, P_COLS = 0, 512, 1024, 1536, 2048, 2304, 3328, 3584
SRC_Q, SRC_KV, SRC_GATE = 2208, 3232, 4768
N_KV = 6
LORA_PAD = 256
SRC_RW_END = POOL_WIDTH + RW_COLS

VMEM_LIMIT = 56 * 1024 * 1024


def _t5_bucket_np(dist):
    n = np.maximum(dist, 0)
    nf = np.maximum(n, 1).astype(np.float32)
    large = MAX_EXACT + (np.log(nf / MAX_EXACT) / math.log(MAX_DISTANCE / MAX_EXACT)
                         * (N_BUCKETS - MAX_EXACT)).astype(np.int32)
    large = np.minimum(large, N_BUCKETS - 1)
    return np.where(n < MAX_EXACT, n, large)


def _cp(sem, vmem=None):
    return pltpu.CompilerParams(dimension_semantics=sem, vmem_limit_bytes=vmem)


def _dot(a, b, precision=None):
    return jnp.dot(a, b, preferred_element_type=F32, precision=precision)


def _dot_nt(a, b, precision=None):
    return lax.dot_general(a, b, (((1,), (1,)), ((), ())), preferred_element_type=F32, precision=precision)


def _dot_tn(a, b, precision=None):
    return lax.dot_general(a, b, (((0,), (0,)), ((), ())), preferred_element_type=F32, precision=precision)


def _bdot(a, b):
    return _dot(a.astype(BF16), b.astype(BF16))


def _bdot_nt(a, b):
    return _dot_nt(a.astype(BF16), b.astype(BF16))


def _bdot_tn(a, b):
    return _dot_tn(a.astype(BF16), b.astype(BF16))


def _split(a):
    hi = a.astype(BF16)
    return hi, (a - hi.astype(F32)).astype(BF16)


def _dot3(a, b):
    ah, al = _split(a)
    bh, bl = _split(b)
    return _dot(ah, bh) + (_dot(ah, bl) + _dot(al, bh))


def _rms(x, g):
    return x * lax.rsqrt(jnp.mean(x * x, axis=-1, keepdims=True) + RMS_EPS) * g


def _norm_kernel(x_ref, g_ref, h_ref):
    h_ref[...] = _rms(x_ref[...], g_ref[...]).astype(h_ref.dtype)


def _norm(x, g, out_dtype, tm=512):
    n, d = x.shape
    return pl.pallas_call(
        _norm_kernel,
        out_shape=jax.ShapeDtypeStruct((n, d), out_dtype),
        grid=(n // tm,),
        in_specs=[pl.BlockSpec((tm, d), lambda i: (i, 0)), pl.BlockSpec((1, d), lambda i: (0, 0))],
        out_specs=pl.BlockSpec((tm, d), lambda i: (i, 0)),
        compiler_params=_cp(("parallel",)),
        name="rms_norm",
    )(x, g.reshape(1, d))


def _add_norm_kernel(x_ref, y1_ref, y2_ref, g_ref, xo_ref, h_ref):
    x = x_ref[...] + (y1_ref[...] + y2_ref[...])
    xo_ref[...] = x
    h_ref[...] = _rms(x, g_ref[...]).astype(h_ref.dtype)


def _add_norm(x, y1, y2, g, out_dtype, tm=512):
    n, d = x.shape
    row = pl.BlockSpec((tm, d), lambda i: (i, 0))
    return pl.pallas_call(
        _add_norm_kernel,
        out_shape=(jax.ShapeDtypeStruct((n, d), F32), jax.ShapeDtypeStruct((n, d), out_dtype)),
        grid=(n // tm,),
        in_specs=[row, row, row, pl.BlockSpec((1, d), lambda i: (0, 0))],
        out_specs=(row, row),
        compiler_params=_cp(("parallel",)),
        name="moe_combine_norm",
    )(x, y1, y2, g.reshape(1, d))


def _matmul_kernel(x_ref, w_ref, o_ref):
    o_ref[...] = _dot(x_ref[...], w_ref[...]).astype(o_ref.dtype)


def _matmul(x, w, tm=512, tn=P_COLS // 2):
    m, k = x.shape
    n = w.shape[1]
    return pl.pallas_call(
        _matmul_kernel,
        out_shape=jax.ShapeDtypeStruct((m, n), F32),
        grid=(n // tn, m // tm),
        in_specs=[pl.BlockSpec((tm, k), lambda j, i: (i, 0)), pl.BlockSpec((k, tn), lambda j, i: (0, j))],
        out_specs=pl.BlockSpec((tm, tn), lambda j, i: (i, j)),
        compiler_params=_cp(("parallel", "parallel"), VMEM_LIMIT),
        name="in_proj",
    )(x, w)


def _kv_proj_kernel(x_ref, w_ref, c_ref, ks_ref, vs_ref, kw_ref, vw_ref, stage_ref):
    i = pl.program_id(1)
    tm = x_ref.shape[0]
    head = i == 0
    res = _dot(x_ref[...], w_ref[...])
    col = lax.broadcasted_iota(jnp.int32, (tm, NSA_DH), 1)
    blk = ((i - 1) * tm + lax.broadcasted_iota(jnp.int32, (tm, NSA_DH), 0)) // SLC_BLOCK
    first_col = jnp.where(col == 0, 1.0, 0.0)
    extra = {2: jnp.where(head, 1.0, jnp.where(blk == col, 1.0, 0.0)), 3: first_col,
             4: jnp.where(head, first_col, 0.0), 5: first_col}
    outs = {2: ks_ref, 3: vs_ref, 4: kw_ref, 5: vw_ref}
    for t in range(N_KV):
        for g in range(NSA_G):
            c0 = (t * NSA_G + g) * NSA_DH
            val = res[:, c0:c0 + NSA_DH]
            if t < 2:
                stage_ref[...] = val
                c_ref[t, 0, g] = jnp.concatenate(
                    [stage_ref[pl.ds(j, tm // CMP_STRIDE, stride=CMP_STRIDE), :] for j in range(CMP_STRIDE)],
                    axis=1).astype(c_ref.dtype)
            else:
                feat = jnp.where(head, 0.0, val)
                outs[t][0, g] = jnp.concatenate([feat, extra[t]], axis=1).astype(outs[t].dtype)


def _kv_proj(x, w, batch, seq):
    k = x.shape[1]
    tm = KV_PAD
    nt = seq // tm
    aug = jax.ShapeDtypeStruct((batch, NSA_G, KV_PAD + seq, 2 * NSA_DH), BF16)
    aug_spec = pl.BlockSpec((1, NSA_G, tm, 2 * NSA_DH), lambda b, i: (b, 0, i, 0))
    prev = lambda i: jnp.maximum(i - 1, 0)
    return pl.pallas_call(
        _kv_proj_kernel,
        out_shape=(jax.ShapeDtypeStruct((2, batch, NSA_G, seq // CMP_STRIDE, CMP_STRIDE * NSA_DH), BF16),
                   aug, aug, aug, aug),
        grid=(batch, nt + 1),
        in_specs=[pl.BlockSpec((tm, k), lambda b, i: (b * nt + prev(i), 0)),
                  pl.BlockSpec(w.shape, lambda b, i: (0, 0))],
        out_specs=(pl.BlockSpec((2, 1, NSA_G, tm // CMP_STRIDE, CMP_STRIDE * NSA_DH),
                                lambda b, i: (0, b, 0, prev(i), 0)),
                   aug_spec, aug_spec, aug_spec, aug_spec),
        scratch_shapes=[pltpu.VMEM((tm, NSA_DH), F32)],
        compiler_params=_cp(("parallel", "arbitrary"), VMEM_LIMIT),
        name="kv_proj",
    )(x, w)


def _pool_kernel(u_ref, halo_ref, w_ref, scale_ref, o_ref, buf_ref, *, tile):
    i = pl.program_id(1)
    u = u_ref[...]
    buf_ref[POOL_HALO:, :] = u
    buf_ref[:POOL_HALO, :] = jnp.where(i > 0, halo_ref[...], 0.0)
    t = i * tile + lax.broadcasted_iota(jnp.int32, (tile, 1), 0)
    outs = []
    for gi, win in enumerate(POOL_WINDOWS):
        cols = slice(gi * POOL_GW, (gi + 1) * POOL_GW)
        s = u[:, cols]
        for j in range(1, win):
            s = s + buf_ref[POOL_HALO - j:POOL_HALO - j + tile, cols]
        cnt = jnp.minimum(t + 1, win).astype(F32)
        pooled = s / cnt - u[:, cols]
        outs.append(_dot(pooled.astype(BF16), w_ref[gi]))
    o_ref[...] = (jnp.concatenate(outs, axis=1) * scale_ref[...]).astype(o_ref.dtype)


def _pool(proj, w_grp, scale, batch, seq, tile=512):
    nt = seq // tile
    hb = tile // POOL_HALO
    return pl.pallas_call(
        functools.partial(_pool_kernel, tile=tile),
        out_shape=jax.ShapeDtypeStruct((batch * seq, POOL_WIDTH), BF16),
        grid=(batch, nt),
        in_specs=[
            pl.BlockSpec((tile, POOL_WIDTH), lambda b, i: (b * nt + i, 0)),
            pl.BlockSpec((POOL_HALO, POOL_WIDTH), lambda b, i: (jnp.maximum((b * nt + i) * hb - 1, 0), 0)),
            pl.BlockSpec((len(POOL_WINDOWS), POOL_GW, POOL_GW), lambda b, i: (0, 0, 0)),
            pl.BlockSpec((1, POOL_WIDTH), lambda b, i: (0, 0)),
        ],
        out_specs=pl.BlockSpec((tile, POOL_WIDTH), lambda b, i: (b * nt + i, 0)),
        scratch_shapes=[pltpu.VMEM((tile + POOL_HALO, POOL_WIDTH), F32)],
        compiler_params=_cp(("parallel", "parallel")),
        name="pool_mixer",
    )(proj, proj, w_grp.astype(BF16), scale.reshape(1, POOL_WIDTH))


def _token_shift(u, halo, mu, first):
    prev_row = jnp.where(first, 0.0, halo[7:8, :])
    rolled = pltpu.roll(u, 1, 0)
    row = lax.broadcasted_iota(jnp.int32, u.shape, 0)
    prev = jnp.where(row == 0, prev_row, rolled)
    return u + (prev - u) * mu


def _rwkv_chunk_kernel(r_ref, k_ref, v_ref, l_ref, rh_ref, kh_ref, vh_ref, lh_ref,
                       mur_ref, muk_ref, muv_ref, mul_ref, w0_ref, wup_ref, a0_ref, aup_ref, gup_ref,
                       kk_ref, ka_ref, rk_ref, bd_ref, qy_ref, mn_ref, g_ref, bonus_ref):
    first = pl.program_id(1) == 0
    c = RW_CHUNK
    r = _token_shift(r_ref[...], rh_ref[...], mur_ref[...], first)
    k = _token_shift(k_ref[...], kh_ref[...], muk_ref[...], first)
    v = _token_shift(v_ref[...], vh_ref[...], muv_ref[...], first)
    lo = _token_shift(l_ref[...], lh_ref[...], mul_ref[...], first)
    wd = lo[:, :DECAY_LORA]
    ad = lo[:, DECAY_LORA:DECAY_LORA + AAA_LORA]
    gd = lo[:, DECAY_LORA + AAA_LORA:RW_LORA]
    z = -(w0_ref[...] + _dot(jnp.tanh(wd), wup_ref[...], HI))
    w_log = -(jnp.maximum(z, 0.0) + jnp.log(1.0 + jnp.exp(-jnp.abs(z)))) - 0.5
    logw = -jnp.exp(w_log)
    a = jax.nn.sigmoid(a0_ref[...] + _dot(ad, aup_ref[...], HI))
    g_ref[0] = _dot(jax.nn.sigmoid(gd), gup_ref[...], HI)
    kkraw = k * kk_ref[...]
    k2 = k * (1.0 + (a - 1.0) * ka_ref[...])
    rkr = r * k2 * rk_ref[...]

    cum_all = logw
    trow = lax.broadcasted_iota(jnp.int32, logw.shape, 0) % c
    step = 1
    while step < c:
        cum_all = cum_all + jnp.where(trow >= step, pltpu.roll(cum_all, step, 0), 0.0)
        step *= 2

    ti = lax.broadcasted_iota(jnp.int32, (c, c), 0)
    si = lax.broadcasted_iota(jnp.int32, (c, c), 1)
    incl = ti >= si
    strict = ti > si
    eye = ti == si
    zeros = jnp.zeros((c, c), F32)
    bd = bd_ref[...]

    def head_sum(t):
        hi, lo = _split(t)
        return _dot(hi, bd) + _dot(lo, bd)

    kk = kkraw / jnp.maximum(jnp.sqrt(head_sum(kkraw * kkraw)), 1e-12)
    bonus_ref[0] = head_sum(rkr) * v
    nchunk = logw.shape[0] // c
    ends = [cum_all[(j + 1) * c - 1:(j + 1) * c, :] for j in range(nchunk)]
    cum_end = jnp.concatenate([jnp.broadcast_to(e, (c, e.shape[1])) for e in ends], axis=0)
    ginv = jnp.exp(-cum_all)
    gtail = jnp.exp(cum_end - cum_all)
    gend = jnp.exp(cum_end)
    kka = kk * a
    at = -kk * jnp.exp(cum_all - logw)
    bt = kka * ginv
    kt = k2 * ginv
    rt = r * jnp.exp(cum_all)
    bhat = kka * gtail
    khat = k2 * gtail

    heads = [(slice(j * c, (j + 1) * c), slice(h * RW_DH, (h + 1) * RW_DH))
             for j in range(nchunk) for h in range(RW_HEADS)]
    stack = lambda x, y, s: jnp.concatenate([x[s], y[s]], axis=0).astype(BF16)
    gram = [_dot_nt(stack(at, rt, s), stack(bt, kt, s)) for s in heads]
    a_ab = [jnp.where(strict, g[:c, :c], 0.0) for g in gram]
    a_ak = [jnp.where(strict, g[:c, c:], 0.0) for g in gram]
    incl2 = (lax.broadcasted_iota(jnp.int32, (c, 2 * c), 0)
             >= lax.broadcasted_iota(jnp.int32, (c, 2 * c), 1) % c)
    a_r = [jnp.where(incl2, g[c:, :], 0.0) for g in gram]
    p = a_ab
    tinv = [eye.astype(F32) + x for x in p]
    for _ in range(int(math.log2(c)) - 1):
        p = [_bdot(x, x) for x in p]
        tinv = [t + _bdot(t, x) for t, x in zip(tinv, p)]
    av = [_bdot(x, v[s]) for x, s in zip(a_ak, heads)]
    w12 = [_bdot(t, jnp.concatenate([at[s], x], axis=1)) for t, x, s in zip(tinv, av, heads)]
    zmat = [jnp.concatenate([w, jnp.concatenate([zeros, v[s]], axis=1)], axis=0).astype(BF16)
            for w, s in zip(w12, heads)]
    out1 = [_dot(x.astype(BF16), z) for x, z in zip(a_r, zmat)]
    out2 = [_dot_tn(stack(bhat, khat, s), z) for s, z in zip(heads, zmat)]
    qy = [o + jnp.concatenate([rt[s], zeros], axis=1) for o, s in zip(out1, heads)]
    for j in range(nchunk):
        qy_ref[0, j * c:(j + 1) * c, :] = jnp.concatenate(qy[j * RW_HEADS:(j + 1) * RW_HEADS], axis=1)
        for h in range(RW_HEADS):
            idx = j * RW_HEADS + h
            diag = jnp.where(eye, gend[heads[idx]], 0.0)
            mn_ref[0, j, h] = out2[idx] + jnp.concatenate([diag, zeros], axis=1)


def _rwkv_scan_kernel(qy_ref, mn_ref, g_ref, bonus_ref, gnw_ref, gnb_ref, bd_ref, y_ref, st_ref, *, batch):
    @pl.when(pl.program_id(0) == 0)
    def _():
        st_ref[...] = jnp.zeros_like(st_ref)

    bd = bd_ref[...]

    def head_mean(t):
        hi, lo = _split(t)
        return (_dot(hi, bd) + _dot(lo, bd)) * (1.0 / RW_DH)

    pairs = [(b, h) for b in range(batch) for h in range(RW_HEADS)]
    sts = [st_ref[b * RW_HEADS + h] for b, h in pairs]
    ys = [_bdot(qy_ref[b, :, 2 * RW_DH * h:2 * RW_DH * h + RW_DH], st)
          + qy_ref[b, :, 2 * RW_DH * h + RW_DH:2 * RW_DH * (h + 1)] for (b, h), st in zip(pairs, sts)]
    for (b, h), st in zip(pairs, sts):
        mn = mn_ref[b, 0, h]
        st_ref[b * RW_HEADS + h] = _dot3(mn[:, :RW_DH], st) + mn[:, RW_DH:]
    for b in range(batch):
        y = jnp.concatenate(ys[b * RW_HEADS:(b + 1) * RW_HEADS], axis=1)
        dev = y - head_mean(y)
        yn = dev * lax.rsqrt(head_mean(dev * dev) + RW_GN_EPS) * gnw_ref[...] + gnb_ref[...]
        y_ref[b] = ((yn + bonus_ref[b]) * g_ref[b]).astype(y_ref.dtype)


def _rwkv(proj, p, batch, seq):
    head_ones = jnp.asarray(np.kron(np.eye(RW_HEADS), np.ones((RW_DH, RW_DH))), BF16)
    c = RW_CHUNK
    nc = seq // c
    tb = RW_NB * c
    nt = seq // tb
    hb = tb // 8
    row512 = lambda col: pl.BlockSpec((tb, RW_WIDTH), lambda b, i: (b * nt + i, col))
    halo512 = lambda col: pl.BlockSpec((8, RW_WIDTH), lambda b, i: (jnp.maximum((b * nt + i) * hb - 1, 0), col))
    const = lambda shape: pl.BlockSpec(shape, lambda b, i: (0,) * len(shape))
    vec = const((1, RW_WIDTH))
    out_row = lambda w: pl.BlockSpec((1, tb, w), lambda b, i: (b, i, 0))
    qy, mn, g, bonus = pl.pallas_call(
        _rwkv_chunk_kernel,
        out_shape=(jax.ShapeDtypeStruct((batch, seq, 2 * RW_WIDTH), F32),
                   jax.ShapeDtypeStruct((batch, nc, RW_HEADS, RW_DH, 2 * RW_DH), F32),
                   jax.ShapeDtypeStruct((batch, seq, RW_WIDTH), F32),
                   jax.ShapeDtypeStruct((batch, seq, RW_WIDTH), F32)),
        grid=(batch, nt),
        in_specs=[row512(C_R // RW_WIDTH), row512(C_K // RW_WIDTH), row512(C_V // RW_WIDTH),
                  pl.BlockSpec((tb, LORA_PAD), lambda b, i: (b * nt + i, C_LORA // LORA_PAD)),
                  halo512(C_R // RW_WIDTH), halo512(C_K // RW_WIDTH), halo512(C_V // RW_WIDTH),
                  pl.BlockSpec((8, LORA_PAD), lambda b, i: (jnp.maximum((b * nt + i) * hb - 1, 0), C_LORA // LORA_PAD)),
                  vec, vec, vec, const((1, LORA_PAD)),
                  vec, const((DECAY_LORA, RW_WIDTH)), vec, const((AAA_LORA, RW_WIDTH)), const((GATE_LORA, RW_WIDTH)),
                  vec, vec, vec, const((RW_WIDTH, RW_WIDTH))],
        out_specs=(out_row(2 * RW_WIDTH),
                   pl.BlockSpec((1, RW_NB, RW_HEADS, RW_DH, 2 * RW_DH), lambda b, i: (b, i, 0, 0, 0)),
                   out_row(RW_WIDTH), out_row(RW_WIDTH)),
        compiler_params=_cp(("parallel", "parallel"), VMEM_LIMIT),
        name="rwkv_chunk",
    )(proj, proj, proj, proj, proj, proj, proj, proj,
      p["mu_r"], p["mu_k"], p["mu_v"], p["mu_l"], p["w0"], p["w_up"], p["a0"], p["a_up"], p["g_up"],
      p["k_k"], p["k_a"], p["r_k"], head_ones)

    full = lambda w: pl.BlockSpec((batch, c, w), lambda i: (0, i, 0))
    return pl.pallas_call(
        functools.partial(_rwkv_scan_kernel, batch=batch),
        out_shape=jax.ShapeDtypeStruct((batch, seq, RW_WIDTH), BF16),
        grid=(nc,),
        in_specs=[full(2 * RW_WIDTH),
                  pl.BlockSpec((batch, 1, RW_HEADS, RW_DH, 2 * RW_DH), lambda i: (0, i, 0, 0, 0)),
                  full(RW_WIDTH), full(RW_WIDTH),
                  pl.BlockSpec((1, RW_WIDTH), lambda i: (0, 0)), pl.BlockSpec((1, RW_WIDTH), lambda i: (0, 0)),
                  pl.BlockSpec((RW_WIDTH, RW_WIDTH), lambda i: (0, 0))],
        out_specs=full(RW_WIDTH),
        scratch_shapes=[pltpu.VMEM((batch * RW_HEADS, RW_DH, RW_DH), F32)],
        compiler_params=_cp(("arbitrary",), VMEM_LIMIT),
        name="rwkv_scan",
    )(qy, mn, g, bonus, p["gn_w"], p["gn_b"], head_ones).reshape(batch * seq, RW_WIDTH)


def _gelu_tanh(x):
    return 0.5 * x * (1.0 + jnp.tanh(math.sqrt(2.0 / math.pi) * (x + 0.044715 * (x * x * x))))


def _compress_kernel(x_ref, pos_ref, w1_ref, w2_ref, o_ref):
    half = CMP_STRIDE * NSA_DH
    x = x_ref[0, 0, 0]
    w1 = w1_ref[0]
    posb = _dot(pos_ref[0], w1)[0:1, :]
    h1 = _dot(x, w1[:half])
    h2 = _dot(x, w1[half:])
    n = h2.shape[0]
    row = lax.broadcasted_iota(jnp.int32, h2.shape, 0)
    h2s = jnp.where(row < n - 1, pltpu.roll(h2, n - 1, 0), 0.0)
    hid = _gelu_tanh(h1 + h2s + posb)
    o_ref[0, 0, 0] = _dot(hid.astype(BF16), w2_ref[0]).astype(o_ref.dtype)


def _compress(xkv, pos, w1, w2, batch, seq):
    nr = seq // CMP_STRIDE
    wide = CMP_STRIDE * NSA_DH
    return pl.pallas_call(
        _compress_kernel,
        out_shape=jax.ShapeDtypeStruct((2, batch, NSA_G, nr, NSA_DH), BF16),
        grid=(2, batch, NSA_G),
        in_specs=[pl.BlockSpec((1, 1, 1, nr, wide), lambda t, b, g: (t, b, g, 0, 0)),
                  pl.BlockSpec((1, 8, 2 * wide), lambda t, b, g: (t, 0, 0)),
                  pl.BlockSpec((1, 2 * wide, CMP_HIDDEN), lambda t, b, g: (t, 0, 0)),
                  pl.BlockSpec((1, CMP_HIDDEN, NSA_DH), lambda t, b, g: (t, 0, 0))],
        out_specs=pl.BlockSpec((1, 1, 1, nr, NSA_DH), lambda t, b, g: (t, b, g, 0, 0)),
        compiler_params=_cp(("parallel", "parallel", "parallel")),
        name="nsa_compress",
    )(xkv, pos, w1, w2)


def _softmax_parts(s):
    m = jnp.max(s, axis=1, keepdims=True)
    e = jnp.exp(s - m)
    return m, e, jnp.sum(e, axis=1, keepdims=True)


def _lane_tile_fold(x, op, init):
    for t in range(x.shape[1] // 128):
        init = op(init, x[:, 128 * t:128 * (t + 1)])
    return init


def _nsa_kernel(q_ref, gate_ref, kc_ref, vc_ref, ks_ref, vs_ref, kw_ref, vw_ref,
                tblc_ref, tbln_ref, tblw_ref, ovt_ref, gexp_ref, o_ref, s_ref, *, ncmp):
    nb = NSA_NB
    blocks = [pl.program_id(2) * nb + u for u in range(nb)]
    rows = NSA_HG * QB
    each = lambda f, *ls: [f(*xs) for xs in zip(*ls)]
    qt = q_ref[...]
    qs = [jnp.concatenate([qt[u * QB:(u + 1) * QB, NSA_DH * h:NSA_DH * (h + 1)] for h in range(NSA_HG)], axis=0)
          * (NSA_DH ** -0.5 * LOG2E) for u in range(nb)]
    qb = each(lambda x: x.astype(BF16), qs)
    rmax = lambda x: jnp.max(x, axis=1, keepdims=True)
    rsum = lambda x: jnp.sum(x, axis=1, keepdims=True)

    win_w = WINDOW + QB
    kw = [kw_ref[0, 0, pl.ds(pl.multiple_of(i * QB, QB), win_w), :] for i in blocks]
    vw = [vw_ref[0, 0, pl.ds(pl.multiple_of(i * QB, QB), win_w), :] for i in blocks]
    flag = lax.broadcasted_iota(jnp.int32, (rows, kw_ref.shape[3] - NSA_DH), 1) == 0
    q_win = each(lambda x: jnp.concatenate([x, jnp.where(flag, NEG, 0.0)], axis=1).astype(BF16), qs)
    tblw = tblw_ref[0]
    pv = lambda x: x[:, :NSA_DH] / x[:, NSA_DH:NSA_DH + 1]
    s_w = each(lambda x, k: _dot_nt(x, k) + tblw, q_win, kw)
    m_w = each(rmax, s_w)
    e_w = each(lambda s, m: jnp.exp2(s - m), s_w, m_w)
    o_w = each(lambda e, v: pv(_dot(e.astype(BF16), v)), e_w, vw)

    kc = kc_ref[0, 0]
    vc = vc_ref[0, 0]
    tblc = tblc_ref[0]
    cidx = lax.broadcasted_iota(jnp.int32, (rows, ncmp), 1)
    qrow = lax.broadcasted_iota(jnp.int32, (rows, 1), 0) % QB
    lc = [jnp.where(cidx < (QB // CMP_STRIDE) * i + CMP_AHEAD,
                    _dot_nt(x, kc) + pltpu.roll(tblc, (4 * i - CMP_NEAR // 2 + ncmp) % ncmp, 1), NEG)
          for x, i in zip(qb, blocks)]
    m_c = each(rmax, lc)
    e_c = each(lambda s, m: jnp.exp2(s - m), lc, m_c)
    den_c = each(rsum, e_c)
    pc = [e * jnp.where(i * QB + qrow >= CMP_BLOCK - 1, 1.0 / d, 0.0) for e, d, i in zip(e_c, den_c, blocks)]
    o_c = each(lambda x: _dot(x.astype(BF16), vc), pc)
    pcs_hi, pcs_lo = _split(jnp.concatenate(
        each(lambda x: x[0:QB] + x[QB:2 * QB] + x[2 * QB:3 * QB] + x[3 * QB:4 * QB], pc), axis=0))
    ovt = ovt_ref[...]
    imp = _dot_nt(ovt, pcs_hi) + _dot_nt(ovt, pcs_lo)

    nslc = ovt.shape[0]
    nidx = lax.broadcasted_iota(jnp.int32, (nslc, nb * QB), 0)
    cur = blocks[0] + lax.broadcasted_iota(jnp.int32, (nslc, nb * QB), 1) // QB
    forced = (nidx == 0) | (nidx == cur) | (nidx == cur - 1)
    work = jnp.where(nidx <= cur, imp + jnp.where(forced, FORCE_BONUS, 0.0), -1.0)
    sel_all = jnp.zeros((nslc, nb * QB), F32)
    for _ in range(N_SELECT):
        m = jnp.max(work, axis=0, keepdims=True)
        first = jnp.min(jnp.where(work == m, nidx, nslc), axis=0, keepdims=True)
        pick = nidx == first
        sel_all = jnp.where(pick & (m >= 0.0), 1.0, sel_all)
        work = jnp.where(pick, -2.0, work)
    sel_t = sel_all.T

    ind_w = ks_ref.shape[3] - NSA_DH
    selq = [sel_t[u * QB:(u + 1) * QB] for u in range(nb)]
    if nslc < ind_w:
        selq = each(lambda x: jnp.concatenate([x, jnp.zeros((QB, ind_w - nslc), F32)], axis=1), selq)
    bidx = lax.broadcasted_iota(jnp.int32, (QB, ind_w), 1)
    tile4 = lambda t: jnp.concatenate([t] * NSA_HG, axis=0)
    with_mask = lambda x, keep: jnp.concatenate([x, tile4(jnp.where(keep, 0.0, NEG))], axis=1).astype(BF16)
    q_sel = each(lambda x, s: with_mask(x, s > 0.0), qs, selq)
    q_far = jnp.concatenate([with_mask(x, (s > 0.0) & (bidx <= i - NEAR_BLOCKS))
                             for x, s, i in zip(qs, selq, blocks)], axis=0)
    near_w = NEAR_BLOCKS * SLC_BLOCK
    pad_s = KV_PAD
    near0 = pad_s - (NEAR_BLOCKS - 1) * SLC_BLOCK
    kn = [ks_ref[0, 0, pl.ds(pl.multiple_of(near0 + i * QB, QB), near_w), :] for i in blocks]
    vn = [vs_ref[0, 0, pl.ds(pl.multiple_of(near0 + i * QB, QB), near_w), :] for i in blocks]
    tbln = tbln_ref[0]
    s_near = each(lambda x, k: _dot_nt(x, k) + tbln, q_sel, kn)
    m_near = jnp.concatenate(each(rmax, s_near), axis=0)

    far_w = FAR_CHUNK_BLOCKS * SLC_BLOCK
    n_far =jnp.maximum(blocks[-1] - (NEAR_BLOCKS - 1) + FAR_CHUNK_BLOCKS - 1, 0) // FAR_CHUNK_BLOCKS

    def far_logits(js, mvec):
        ss = [_dot_nt(q_far, ks_ref[0, 0, pl.ds(pl.multiple_of(pad_s + j * far_w, SLC_BLOCK), far_w), :])
              for j in js]
        for j, s in zip(js, ss):
            s_ref[:, pl.ds(pl.multiple_of(j * far_w, far_w), far_w)] = s
        for s in ss:
            mvec = _lane_tile_fold(s, jnp.maximum, mvec)
        return mvec

    n_pair = n_far // 2
    odd = n_far % 2 == 1
    mvec = lax.fori_loop(0, n_pair, lambda jp, m: far_logits([2 * jp, 2 * jp + 1], m),
                         jnp.full((nb * rows, 128), NEG, F32))
    mvec = lax.cond(odd, lambda m: far_logits([n_far - 1], m), lambda m: m, mvec)
    m_s = jnp.maximum(m_near, rmax(mvec))

    e_near = [jnp.exp2(s - m_s[u * rows:(u + 1) * rows]) for u, s in enumerate(s_near)]
    acc0 = jnp.concatenate(each(lambda e, v: _dot(e.astype(BF16), v), e_near, vn), axis=0)

    def far_values(js, acc):
        es = [jnp.exp2(s_ref[:, pl.ds(pl.multiple_of(j * far_w, far_w), far_w)] - m_s).astype(BF16) for j in js]
        for j, e in zip(js, es):
            acc = acc + _dot(e, vs_ref[0, 0, pl.ds(pl.multiple_of(pad_s + j * far_w, SLC_BLOCK), far_w), :])
        return acc

    acc = lax.fori_loop(0, n_pair, lambda jp, a: far_values([2 * jp, 2 * jp + 1], a), acc0)
    o_s = pv(lax.cond(odd, lambda a: far_values([n_far - 1], a), lambda a: a, acc))

    g_hi, g_lo = _split(jax.nn.sigmoid(gate_ref[0, 0]))
    gexp = gexp_ref[...]
    ge = _dot(g_hi, gexp) + _dot(g_lo, gexp)
    for u in range(nb):
        outs = []
        for h in range(NSA_HG):
            r0 = slice(h * QB, (h + 1) * QB)
            gt = lambda br: ge[u * QB:(u + 1) * QB, (3 * h + br) * 128:(3 * h + br) * 128 + NSA_DH]
            outs.append(gt(0) * o_c[u][r0] + gt(1) * o_s[u * rows + h * QB:u * rows + (h + 1) * QB]
                        + gt(2) * o_w[u][r0])
        o_ref[u * QB:(u + 1) * QB, :] = jnp.concatenate(outs, axis=1).astype(o_ref.dtype)


def _bias_tables(rel_bias, ncmp):
    def table(dist, keep, base):
        onehot = jnp.asarray(np.eye(N_BUCKETS, dtype=np.float32)[_t5_bucket_np(dist)])
        tbl = jnp.einsum("qkb,bh->qkh", onehot, rel_bias, precision=HI)
        tbl = (tbl - base) * LOG2E
        tbl = jnp.where(jnp.asarray(keep)[..., None], tbl, NEG)
        k = dist.shape[1]
        return tbl.transpose(2, 0, 1).reshape(NSA_G, NSA_HG * QB, k)

    far = rel_bias[N_BUCKETS - 1]
    qi = np.arange(QB)[:, None]
    dist_c = qi - CMP_STRIDE * (np.arange(CMP_NEAR)[None, :] - CMP_NEAR // 2) - (CMP_BLOCK - 1)
    tblc = table(dist_c, dist_c >= 0, far)
    tblc = tblc * jnp.asarray(np.arange(CMP_NEAR) < CMP_NEAR // 2 + CMP_AHEAD, F32)
    tblc = jnp.pad(tblc, ((0, 0), (0, 0), (0, ncmp - CMP_NEAR)))
    jn = np.arange(NEAR_BLOCKS * SLC_BLOCK)[None, :]
    dist_n = (NEAR_BLOCKS - 1) * SLC_BLOCK + qi - jn
    tbln = table(dist_n, dist_n >= 0, far)
    jw = np.arange(WINDOW + QB)[None, :]
    dist_w = WINDOW + qi - jw
    tblw = table(dist_w, (dist_w >= 0) & (dist_w < WINDOW), 0.0)
    return tblc, tbln, tblw


def _nsa(proj, gate_logits, kvc, ks, vs, kw, vw, tables, batch, seq):
    nq = seq // QB
    ncmp = seq // CMP_STRIDE
    nslc = seq // SLC_BLOCK
    tblc, tbln, tblw = tables
    cstart = np.arange(ncmp) * CMP_STRIDE
    sstart = np.arange(nslc) * SLC_BLOCK
    overlap_t = ((cstart[None, :] <= sstart[:, None] + SLC_BLOCK - 1)
                 & (cstart[None, :] + CMP_BLOCK - 1 >= sstart[:, None])
                 & (cstart[None, :] + CMP_BLOCK <= seq)).astype(np.float32)
    rows = NSA_HG * QB
    n_gate = 3 * NSA_HG
    gate_spread = (np.arange(n_gate * 128)[None, :] // 128 == np.arange(n_gate)[:, None]) \
        & (np.arange(n_gate * 128)[None, :] % 128 < NSA_DH)
    kv_spec = lambda t: pl.BlockSpec((1, 1) + t.shape[2:], lambda b, g, i: (b, g, 0, 0))
    tbl_spec = lambda k: pl.BlockSpec((1, rows, k), lambda b, g, i: (g, 0, 0))
    qcol = C_Q // (NSA_HG * NSA_DH)
    return pl.pallas_call(
        functools.partial(_nsa_kernel, ncmp=ncmp),
        out_shape=jax.ShapeDtypeStruct((batch * seq, NSA_HEADS * NSA_DH), BF16),
        grid=(batch, NSA_G, nq // NSA_NB),
        in_specs=[pl.BlockSpec((NSA_NB * QB, NSA_HG * NSA_DH), lambda b, g, i: (b * (nq // NSA_NB) + i, qcol + g)),
                  pl.BlockSpec((1, 1, NSA_NB * QB, 3 * NSA_HG), lambda b, g, i: (b, g, i, 0)),
                  pl.BlockSpec((1, 1, ncmp, NSA_DH), lambda b, g, i: (b, g, 0, 0)),
                  pl.BlockSpec((1, 1, ncmp, NSA_DH), lambda b, g, i: (b, g, 0, 0)),
                  kv_spec(ks), kv_spec(vs), kv_spec(kw), kv_spec(vw),
                  tbl_spec(ncmp), tbl_spec(NEAR_BLOCKS * SLC_BLOCK), tbl_spec(WINDOW + QB),
                  pl.BlockSpec((nslc, ncmp), lambda b, g, i: (0, 0)),
                  pl.BlockSpec(gate_spread.shape, lambda b, g, i: (0, 0))],
        out_specs=pl.BlockSpec((NSA_NB * QB, NSA_HG * NSA_DH), lambda b, g, i: (b * (nq // NSA_NB) + i, g)),
        scratch_shapes=[pltpu.VMEM((NSA_NB * rows, seq), F32)],
        compiler_params=_cp(("parallel", "parallel", "arbitrary"), VMEM_LIMIT),
        name="nsa_attention",
    )(proj, gate_logits, kvc[0], kvc[1], ks, vs, kw, vw, tblc, tbln, tblw, jnp.asarray(overlap_t, BF16),
      jnp.asarray(gate_spread, BF16))


def _merge_kernel(h_ref, yp_ref, yr_ref, yn_ref, x_ref, wbp_ref, wbr_ref, wbn_ref, wm_ref, bm_ref, wo_ref,
                  gn_ref, wr_ref, br_ref, xo_ref, h2_ref, route_ref):
    d = D_MODEL
    gl = jax.nn.sigmoid(_dot(h_ref[...], wm_ref[...]) + bm_ref[...])
    merged = (gl[:, :d] * _dot(yp_ref[...], wbp_ref[...]) + gl[:, d:2 * d] * _dot(yr_ref[...], wbr_ref[...])
              + gl[:, 2 * d:] * _dot(yn_ref[...], wbn_ref[...]))
    x = x_ref[...] + _dot(merged.astype(BF16), wo_ref[...])
    xo_ref[...] = x
    h2 = _rms(x, gn_ref[...])
    h2_ref[...] = h2.astype(h2_ref.dtype)

    h2_hi, h2_lo = _split(h2)
    nl = br_ref.shape[1]
    both = _dot(h2_hi, wr_ref[...])
    logits = both[:, :nl] + (both[:, nl:] + _dot(h2_lo, wr_ref[:, :nl])) + br_ref[...]
    lane = lax.broadcasted_iota(jnp.int32, logits.shape, 1)
    big = logits.shape[1]
    lg = jnp.where(lane < N_GROUPS, logits, NEG)
    mg = jnp.max(lg, axis=1, keepdims=True)
    p_top = 1.0 / jnp.sum(jnp.exp(lg - mg), axis=1, keepdims=True)
    grp = jnp.min(jnp.where(lg == mg, lane, big), axis=1, keepdims=True)
    lo = N_GROUPS + EPG * grp
    le = jnp.where((lane >= lo) & (lane < lo + EPG), logits, NEG)
    e1 = jnp.max(le, axis=1, keepdims=True)
    i1 = jnp.min(jnp.where(le == e1, lane, big), axis=1, keepdims=True)
    le = jnp.where(lane == i1, NEG, le)
    e2 = jnp.max(le, axis=1, keepdims=True)
    i2 = jnp.min(jnp.where(le == e2, lane, big), axis=1, keepdims=True)
    t = jnp.exp(e2 - e1)
    w1 = p_top / (1.0 + t)
    w2 = p_top * t / (1.0 + t)
    route_ref[...] = jnp.where(lane == 0, (i1 - N_GROUPS).astype(F32),
                               jnp.where(lane == 1, (i2 - N_GROUPS).astype(F32),
                                         jnp.where(lane == 2, w1, jnp.where(lane == 3, w2, 0.0))))


def _merge(h, y_pool, y_rwkv, y_nsa, x, p, tm=256):
    n, d = x.shape
    row = lambda w: pl.BlockSpec((tm, w), lambda i: (i, 0))
    const = lambda a: pl.BlockSpec(a.shape, lambda i: (0, 0))
    ws = [p["wb_pool"], p["wb_rwkv"], p["wb_nsa"], p["w_merge"], p["b_merge"], p["w_out"],
          p["norm_ffn"], p["w_router"], p["b_router"]]
    return pl.pallas_call(
        _merge_kernel,
        out_shape=(jax.ShapeDtypeStruct((n, d), F32), jax.ShapeDtypeStruct((n, d), F32),
                   jax.ShapeDtypeStruct((n, 128), F32)),
        grid=(n // tm,),
        in_specs=[row(d), row(POOL_WIDTH), row(RW_WIDTH), row(d), row(d)] + [const(w) for w in ws],
        out_specs=(row(d), row(d), row(128)),
        compiler_params=_cp(("parallel",), VMEM_LIMIT),
        name="merge_router",
    )(h, y_pool, y_rwkv, y_nsa, x, *ws)


def _expert_kernel(te_ref, nt_ref, tok_ref, h_hbm, w_ref, wg_ref, wu_ref, wd_ref, o_ref,
                   xbuf, sem, wg_s, wu_s, wd_s):
    i = pl.program_id(0)
    tm = MOE_TM
    n_tiles = nt_ref[0]

    def row_copy(tile, slot, r):
        return pltpu.make_async_copy(h_hbm.at[pl.ds(tok_ref[tile * tm + r], 1), :],
                                     xbuf.at[slot, pl.ds(r, 1), :], sem.at[slot])

    def tile_wait(slot):
        pltpu.make_async_copy(h_hbm.at[pl.ds(0, tm), :], xbuf.at[slot], sem.at[slot]).wait()

    @pl.when((i == 0) & (n_tiles > 0))
    def _():
        def body(r, carry):
            row_copy(0, 0, r).start()
            return carry
        lax.fori_loop(0, tm, body, 0, unroll=8)

    @pl.when((i == 0) | (te_ref[i] != te_ref[jnp.maximum(i - 1, 0)]))
    def _():
        wg_s[...] = wg_ref[0, 0].astype(BF16)
        wu_s[...] = wu_ref[0, 0].astype(BF16)
        wd_s[...] = wd_ref[0, 0].astype(BF16)

    @pl.when(i < n_tiles)
    def _():
        slot = i % 2
        quarter = tm // 4

        def fetch_next(part):
            for r in range(part * quarter, (part + 1) * quarter):
                row_copy(i + 1, 1 - slot, r).start()

        tile_wait(slot)
        xb = xbuf[slot].astype(BF16)
        fetch_next(0)
        gate = _dot(xb, wg_s[...])
        fetch_next(1)
        up = _dot(xb, wu_s[...])
        fetch_next(2)
        hid = (gate * jax.nn.sigmoid(gate) * up).astype(BF16)
        fetch_next(3)
        o_ref[...] = w_ref[...] * _dot(hid, wd_s[...])

    @pl.when((i == n_tiles) & (n_tiles > 0))
    def _():
        tile_wait(i % 2)

    @pl.when(i >= nt_ref[0])
    def _():
        o_ref[...] = jnp.zeros_like(o_ref)


def _experts(h2, rowtok, roww, tile_expert, n_tiles, layer, wg, wu, wd):
    d = h2.shape[1]
    r = rowtok.shape[0]
    tm = MOE_TM
    return pl.pallas_call(
        _expert_kernel,
        out_shape=jax.ShapeDtypeStruct((r, d), F32),
        grid_spec=pltpu.PrefetchScalarGridSpec(
            num_scalar_prefetch=3,
            grid=(r // tm,),
            in_specs=[pl.BlockSpec(memory_space=pl.ANY),
                      pl.BlockSpec((tm, 1), lambda i, te, nt, tok: (i, 0)),
                      pl.BlockSpec((1, 1, d, D_EXPERT), lambda i, te, nt, tok: (layer, te[i], 0, 0)),
                      pl.BlockSpec((1, 1, d, D_EXPERT), lambda i, te, nt, tok: (layer, te[i], 0, 0)),
                      pl.BlockSpec((1, 1, D_EXPERT, d), lambda i, te, nt, tok: (layer, te[i], 0, 0))],
            out_specs=pl.BlockSpec((tm, d), lambda i, te, nt, tok: (i, 0)),
            scratch_shapes=[pltpu.VMEM((2, tm, d), F32), pltpu.SemaphoreType.DMA((2,)),
                            pltpu.VMEM((d, D_EXPERT), BF16), pltpu.VMEM((d, D_EXPERT), BF16),
                            pltpu.VMEM((D_EXPERT, d), BF16)]),
        compiler_params=_cp(("arbitrary",), VMEM_LIMIT),
        name="moe_experts",
    )(tile_expert, n_tiles, rowtok, h2, roww, wg, wu, wd)


def _moe(h2, route, layer, wg, wu, wd):
    n = h2.shape[0]
    tm = MOE_TM
    r = 2 * n + (N_EXPERTS + 1) * tm
    ids = route[:, 0:2].astype(jnp.int32).reshape(-1)
    wts = route[:, 2:4].reshape(-1)
    onehot = (ids[:, None] == jnp.arange(N_EXPERTS)[None, :]).astype(jnp.int32)
    rank = jnp.sum((jnp.cumsum(onehot, axis=0) - onehot) * onehot, axis=1)
    counts = jnp.sum(onehot, axis=0)
    tiles = (counts + tm - 1) // tm
    tile_end = jnp.cumsum(tiles)
    starts = (tile_end - tiles) * tm
    pos = starts[ids] + rank
    row_assign = jnp.full((r,), -1, jnp.int32).at[pos].set(
        jnp.arange(2 * n, dtype=jnp.int32), unique_indices=True, mode="promise_in_bounds")
    rowtok = jnp.maximum(row_assign, 0) // 2
    roww = jnp.where(row_assign >= 0, wts[jnp.maximum(row_assign, 0)], 0.0)
    n_tiles = tile_end[-1:].astype(jnp.int32)
    tile_expert = jnp.minimum(jnp.sum(tile_end[None, :] <= jnp.arange(r // tm)[:, None], axis=1),
                              N_EXPERTS - 1).astype(jnp.int32)
    ys = _experts(h2, rowtok, roww.reshape(r, 1), tile_expert, n_tiles, layer, wg, wu, wd)
    return ys[pos[0::2]], ys[pos[1::2]]


def _layer_params(l, a):
    f = lambda t: t[l]
    row = lambda t: t[l].reshape(1, -1)
    w_in = a["w_in"][l]
    w_in_p = jnp.concatenate([w_in[:, :SRC_RW_END], jnp.zeros((D_MODEL, C_Q - SRC_RW_END), F32),
                              w_in[:, SRC_Q:SRC_KV], w_in[:, SRC_GATE:],
                              jnp.zeros((D_MODEL, P_COLS - C_GATE - (w_in.shape[1] - SRC_GATE)), F32)], axis=1)
    mu = a["rw_mu"][l]
    wb = a["w_branch"][l].astype(BF16)
    w_router = jnp.zeros((D_MODEL, 128), F32)
    w_router = w_router.at[:, :N_GROUPS].set(a["w_router_grp"][l]).at[:, N_GROUPS:N_GROUPS + N_EXPERTS].set(
        a["w_router_exp"][l])
    b_router = jnp.zeros((1, 128), F32)
    b_router = b_router.at[0, :N_GROUPS].set(a["b_router_grp"][l]).at[0, N_GROUPS:N_GROUPS + N_EXPERTS].set(
        a["b_router_exp"][l])
    pos = jnp.stack([a["cmp_pos_k"][l].reshape(-1), a["cmp_pos_v"][l].reshape(-1)])
    return {
        "w_in": w_in_p.astype(BF16), "w_kv": w_in[:, SRC_KV:SRC_GATE].astype(BF16),
        "pool_w": f(a["pool_w"]), "pool_scale": f(a["pool_scale"]),
        "mu_r": mu[None, 0:RW_WIDTH], "mu_k": mu[None, RW_WIDTH:2 * RW_WIDTH],
        "mu_v": mu[None, 2 * RW_WIDTH:3 * RW_WIDTH],
        "mu_l": jnp.concatenate([mu[3 * RW_WIDTH:], jnp.zeros((LORA_PAD - RW_LORA,), F32)])[None],
        "w0": row(a["rw_w0"]), "w_up": f(a["rw_w_up"]), "a0": row(a["rw_a0"]), "a_up": f(a["rw_a_up"]),
        "g_up": f(a["rw_g_up"]), "k_k": row(a["rw_k_k"]), "k_a": row(a["rw_k_a"]), "r_k": row(a["rw_r_k"]),
        "gn_w": row(a["rw_gn_w"]), "gn_b": row(a["rw_gn_b"]),
        "cmp_pos": jnp.broadcast_to(pos[:, None, :], (2, 8, pos.shape[1])).astype(BF16),
        "cmp_w1": jnp.stack([a["cmp_w1_k"][l], a["cmp_w1_v"][l]]).astype(BF16),
        "cmp_w2": jnp.stack([a["cmp_w2_k"][l], a["cmp_w2_v"][l]]).astype(BF16),
        "wb_pool": wb[:POOL_WIDTH], "wb_rwkv": wb[POOL_WIDTH:POOL_WIDTH + RW_WIDTH],
        "wb_nsa": wb[POOL_WIDTH + RW_WIDTH:],
        "w_merge": a["w_merge"][l].astype(BF16), "b_merge": row(a["b_merge"]),
        "w_out": a["w_out"][l].astype(BF16), "norm_ffn": row(a["norm_ffn"]),
        "w_router": jnp.concatenate(_split(w_router), axis=1), "b_router": b_router,
    }


def _mixers(proj, kv, p, tables, batch, seq):
    kv_cmp, ks, vs, kw, vw = kv
    y_pool = _pool(proj, p["pool_w"], p["pool_scale"], batch, seq)
    y_rwkv = _rwkv(proj, p, batch, seq)
    kvc = _compress(kv_cmp, p["cmp_pos"], p["cmp_w1"], p["cmp_w2"], batch, seq)
    gate_logits = proj[:, C_GATE:C_GATE + 3 * NSA_HEADS].reshape(batch, seq, NSA_G, 3 * NSA_HG).transpose(0, 2, 1, 3)
    y_nsa = _nsa(proj, gate_logits, kvc, ks, vs, kw, vw, tables, batch, seq)
    return y_pool, y_rwkv, y_nsa


def kernel(x, rel_bias, norm_mix, w_in, pool_w, pool_scale, rw_mu, rw_w0, rw_w_up, rw_a0, rw_a_up, rw_g_up, rw_k_k, rw_k_a, rw_r_k, rw_gn_w, rw_gn_b, cmp_pos_k, cmp_w1_k, cmp_w2_k, cmp_pos_v, cmp_w1_v, cmp_w2_v, w_branch, w_merge, b_merge, w_out, norm_ffn, w_router_grp, b_router_grp, w_router_exp, b_router_exp, w_exp_gate, w_exp_up, w_exp_down, norm_final):
    a = dict(w_in=w_in, pool_w=pool_w, pool_scale=pool_scale, rw_mu=rw_mu, rw_w0=rw_w0, rw_w_up=rw_w_up,
             rw_a0=rw_a0, rw_a_up=rw_a_up, rw_g_up=rw_g_up, rw_k_k=rw_k_k, rw_k_a=rw_k_a, rw_r_k=rw_r_k,
             rw_gn_w=rw_gn_w, rw_gn_b=rw_gn_b, cmp_pos_k=cmp_pos_k, cmp_w1_k=cmp_w1_k, cmp_w2_k=cmp_w2_k,
             cmp_pos_v=cmp_pos_v, cmp_w1_v=cmp_w1_v, cmp_w2_v=cmp_w2_v, w_branch=w_branch, w_merge=w_merge,
             b_merge=b_merge, w_out=w_out, norm_ffn=norm_ffn, w_router_grp=w_router_grp,
             b_router_grp=b_router_grp, w_router_exp=w_router_exp, b_router_exp=b_router_exp,
             w_exp_gate=w_exp_gate, w_exp_up=w_exp_up, w_exp_down=w_exp_down)
    batch, seq, d = x.shape
    depth = norm_mix.shape[0]
    tables = _bias_tables(rel_bias, seq // CMP_STRIDE)
    xf = x.reshape(batch * seq, d)
    h = _norm(xf, norm_mix[0], BF16)
    for l in range(depth):
        p = _layer_params(l, a)
        proj = _matmul(h, p["w_in"])
        kv = _kv_proj(h, p["w_kv"], batch, seq)
        y_pool, y_rwkv, y_nsa = _mixers(proj, kv, p, tables, batch, seq)
        xf, h2, route = _merge(h, y_pool, y_rwkv, y_nsa, xf, p)
        y1, y2 = _moe(h2, route, l, w_exp_gate, w_exp_up, w_exp_down)
        last = l == depth - 1
        g_next = norm_final if last else norm_mix[l + 1]
        xf, h = _add_norm(xf, y1, y2, g_next, F32 if last else BF16)
    return h.reshape(batch, seq, d)
```

```python
import functools
import math

import jax
import jax.numpy as jnp
import numpy as np
from jax import lax
from jax.experimental import pallas as pl
from jax.experimental.pallas import tpu as pltpu

F32 = jnp.float32
BF16 = jnp.bfloat16
HI = lax.Precision.HIGHEST

D_MODEL = 1024
RMS_EPS = 1e-6
NEG = -1e30
LOG2E = math.log2(math.e)

POOL_WINDOWS = (2, 4, 8, 16)
POOL_WIDTH = 512
POOL_GW = 128
POOL_HALO = 16

RW_HEADS = 8
RW_DH = 64
RW_WIDTH = 512
DECAY_LORA, AAA_LORA, GATE_LORA = 32, 32, 96
RW_LORA = DECAY_LORA + AAA_LORA + GATE_LORA
RW_COLS = 3 * RW_WIDTH + RW_LORA
RW_GN_EPS = 64e-5
RW_CHUNK = 64
RW_NB = 4

NSA_DH = 64
NSA_HEADS = 16
NSA_G = 4
NSA_HG = 4
NSA_KVW = NSA_G * NSA_DH
CMP_BLOCK, CMP_STRIDE, CMP_HIDDEN = 32, 16, 256
SLC_BLOCK = 64
N_SELECT = 8
WINDOW = 512
KV_PAD = WINDOW
QB = 64
NSA_NB = 8
FORCE_BONUS = 1e3
N_BUCKETS, MAX_EXACT, MAX_DISTANCE = 32, 16, 128
NEAR_BLOCKS = 3
FAR_CHUNK_BLOCKS = 8
CMP_NEAR = 32
CMP_AHEAD = (QB - CMP_BLOCK) // CMP_STRIDE + 1

N_GROUPS, EPG, N_EXPERTS, D_EXPERT = 4, 8, 32, 256
MOE_TM = 256
DMA_QUEUES = 2

C_POOL, C_R, C_K, C_V, C_LORA, C_Q, C_GATE, P_COLS = 0, 512, 1024, 1536, 2048, 2304, 3328, 3584
SRC_Q, SRC_KV, SRC_GATE = 2208, 3232, 4768
N_KV = 6
LORA_PAD = 256
SRC_RW_END = POOL_WIDTH + RW_COLS

VMEM_LIMIT = 56 * 1024 * 1024


def _t5_bucket_np(dist):
    n = np.maximum(dist, 0)
    nf = np.maximum(n, 1).astype(np.float32)
    large = MAX_EXACT + (np.log(nf / MAX_EXACT) / math.log(MAX_DISTANCE / MAX_EXACT)
                         * (N_BUCKETS - MAX_EXACT)).astype(np.int32)
    large = np.minimum(large, N_BUCKETS - 1)
    return np.where(n < MAX_EXACT, n, large)


def _cp(sem, vmem=None):
    return pltpu.CompilerParams(dimension_semantics=sem, vmem_limit_bytes=vmem)


def _dot(a, b, precision=None):
    return jnp.dot(a, b, preferred_element_type=F32, precision=precision)


def _dot_nt(a, b, precision=None):
    return lax.dot_general(a, b, (((1,), (1,)), ((), ())), preferred_element_type=F32, precision=precision)


def _dot_tn(a, b, precision=None):
    return lax.dot_general(a, b, (((0,), (0,)), ((), ())), preferred_element_type=F32, precision=precision)


def _bdot(a, b):
    return _dot(a.astype(BF16), b.astype(BF16))


def _bdot_nt(a, b):
    return _dot_nt(a.astype(BF16), b.astype(BF16))


def _bdot_tn(a, b):
    return _dot_tn(a.astype(BF16), b.astype(BF16))


def _split(a):
    hi = a.astype(BF16)
    return hi, (a - hi.astype(F32)).astype(BF16)


def _dot3(a, b):
    ah, al = _split(a)
    bh, bl = _split(b)
    return _dot(ah, bh) + (_dot(ah, bl) + _dot(al, bh))


def _rms(x, g):
    return x * lax.rsqrt(jnp.mean(x * x, axis=-1, keepdims=True) + RMS_EPS) * g


def _norm_kernel(x_ref, g_ref, h_ref):
    h_ref[...] = _rms(x_ref[...], g_ref[...]).astype(h_ref.dtype)


def _norm(x, g, out_dtype, tm=512):
    n, d = x.shape
    return pl.pallas_call(
        _norm_kernel,
        out_shape=jax.ShapeDtypeStruct((n, d), out_dtype),
        grid=(n // tm,),
        in_specs=[pl.BlockSpec((tm, d), lambda i: (i, 0)), pl.BlockSpec((1, d), lambda i: (0, 0))],
        out_specs=pl.BlockSpec((tm, d), lambda i: (i, 0)),
        compiler_params=_cp(("parallel",)),
        name="rms_norm",
    )(x, g.reshape(1, d))


def _add_norm_kernel(x_ref, y1_ref, y2_ref, g_ref, xo_ref, h_ref):
    x = x_ref[...] + (y1_ref[...] + y2_ref[...])
    xo_ref[...] = x
    h_ref[...] = _rms(x, g_ref[...]).astype(h_ref.dtype)


def _add_norm(x, y1, y2, g, out_dtype, tm=512):
    n, d = x.shape
    row = pl.BlockSpec((tm, d), lambda i: (i, 0))
    return pl.pallas_call(
        _add_norm_kernel,
        out_shape=(jax.ShapeDtypeStruct((n, d), F32), jax.ShapeDtypeStruct((n, d), out_dtype)),
        grid=(n // tm,),
        in_specs=[row, row, row, pl.BlockSpec((1, d), lambda i: (0, 0))],
        out_specs=(row, row),
        compiler_params=_cp(("parallel",)),
        name="moe_combine_norm",
    )(x, y1, y2, g.reshape(1, d))


def _matmul_kernel(x_ref, w_ref, o_ref):
    o_ref[...] = _dot(x_ref[...], w_ref[...]).astype(o_ref.dtype)


def _matmul(x, w, tm=512, tn=P_COLS // 2):
    m, k = x.shape
    n = w.shape[1]
    return pl.pallas_call(
        _matmul_kernel,
        out_shape=jax.ShapeDtypeStruct((m, n), F32),
        grid=(n // tn, m // tm),
        in_specs=[pl.BlockSpec((tm, k), lambda j, i: (i, 0)), pl.BlockSpec((k, tn), lambda j, i: (0, j))],
        out_specs=pl.BlockSpec((tm, tn), lambda j, i: (i, j)),
        compiler_params=_cp(("parallel", "parallel"), VMEM_LIMIT),
        name="in_proj",
    )(x, w)


def _kv_proj_kernel(x_ref, w_ref, c_ref, ks_ref, vs_ref, kw_ref, vw_ref, stage_ref):
    i = pl.program_id(1)
    tm = x_ref.shape[0]
    head = i == 0
    res = _dot(x_ref[...], w_ref[...])
    col = lax.broadcasted_iota(jnp.int32, (tm, NSA_DH), 1)
    blk = ((i - 1) * tm + lax.broadcasted_iota(jnp.int32, (tm, NSA_DH), 0)) // SLC_BLOCK
    first_col = jnp.where(col == 0, 1.0, 0.0)
    extra = {2: jnp.where(head, 1.0, jnp.where(blk == col, 1.0, 0.0)), 3: first_col,
             4: jnp.where(head, first_col, 0.0), 5: first_col}
    outs = {2: ks_ref, 3: vs_ref, 4: kw_ref, 5: vw_ref}
    for t in range(N_KV):
        for g in range(NSA_G):
            c0 = (t * NSA_G + g) * NSA_DH
            val = res[:, c0:c0 + NSA_DH]
            if t < 2:
                stage_ref[...] = val
                c_ref[t, 0, g] = jnp.concatenate(
                    [stage_ref[pl.ds(j, tm // CMP_STRIDE, stride=CMP_STRIDE), :] for j in range(CMP_STRIDE)],
                    axis=1).astype(c_ref.dtype)
            else:
                feat = jnp.where(head, 0.0, val)
                outs[t][0, g] = jnp.concatenate([feat, extra[t]], axis=1).astype(outs[t].dtype)


def _kv_proj(x, w, batch, seq):
    k = x.shape[1]
    tm = KV_PAD
    nt = seq // tm
    aug = jax.ShapeDtypeStruct((batch, NSA_G, KV_PAD + seq, 2 * NSA_DH), BF16)
    aug_spec = pl.BlockSpec((1, NSA_G, tm, 2 * NSA_DH), lambda b, i: (b, 0, i, 0))
    prev = lambda i: jnp.maximum(i - 1, 0)
    return pl.pallas_call(
        _kv_proj_kernel,
        out_shape=(jax.ShapeDtypeStruct((2, batch, NSA_G, seq // CMP_STRIDE, CMP_STRIDE * NSA_DH), BF16),
                   aug, aug, aug, aug),
        grid=(batch, nt + 1),
        in_specs=[pl.BlockSpec((tm, k), lambda b, i: (b * nt + prev(i), 0)),
                  pl.BlockSpec(w.shape, lambda b, i: (0, 0))],
        out_specs=(pl.BlockSpec((2, 1, NSA_G, tm // CMP_STRIDE, CMP_STRIDE * NSA_DH),
                                lambda b, i: (0, b, 0, prev(i), 0)),
                   aug_spec, aug_spec, aug_spec, aug_spec),
        scratch_shapes=[pltpu.VMEM((tm, NSA_DH), F32)],
        compiler_params=_cp(("parallel", "arbitrary"), VMEM_LIMIT),
        name="kv_proj",
    )(x, w)


def _pool_kernel(u_ref, halo_ref, w_ref, scale_ref, o_ref, buf_ref, *, tile):
    i = pl.program_id(1)
    u = u_ref[...]
    buf_ref[POOL_HALO:, :] = u
    buf_ref[:POOL_HALO, :] = jnp.where(i > 0, halo_ref[...], 0.0)
    t = i * tile + lax.broadcasted_iota(jnp.int32, (tile, 1), 0)
    outs = []
    for gi, win in enumerate(POOL_WINDOWS):
        cols = slice(gi * POOL_GW, (gi + 1) * POOL_GW)
        s = u[:, cols]
        for j in range(1, win):
            s = s + buf_ref[POOL_HALO - j:POOL_HALO - j + tile, cols]
        cnt = jnp.minimum(t + 1, win).astype(F32)
        pooled = s / cnt - u[:, cols]
        outs.append(_dot(pooled.astype(BF16), w_ref[gi]))
    o_ref[...] = (jnp.concatenate(outs, axis=1) * scale_ref[...]).astype(o_ref.dtype)


def _pool(proj, w_grp, scale, batch, seq, tile=512):
    nt = seq // tile
    hb = tile // POOL_HALO
    return pl.pallas_call(
        functools.partial(_pool_kernel, tile=tile),
        out_shape=jax.ShapeDtypeStruct((batch * seq, POOL_WIDTH), BF16),
        grid=(batch, nt),
        in_specs=[
            pl.BlockSpec((tile, POOL_WIDTH), lambda b, i: (b * nt + i, 0)),
            pl.BlockSpec((POOL_HALO, POOL_WIDTH), lambda b, i: (jnp.maximum((b * nt + i) * hb - 1, 0), 0)),
            pl.BlockSpec((len(POOL_WINDOWS), POOL_GW, POOL_GW), lambda b, i: (0, 0, 0)),
            pl.BlockSpec((1, POOL_WIDTH), lambda b, i: (0, 0)),
        ],
        out_specs=pl.BlockSpec((tile, POOL_WIDTH), lambda b, i: (b * nt + i, 0)),
        scratch_shapes=[pltpu.VMEM((tile + POOL_HALO, POOL_WIDTH), F32)],
        compiler_params=_cp(("parallel", "parallel")),
        name="pool_mixer",
    )(proj, proj, w_grp.astype(BF16), scale.reshape(1, POOL_WIDTH))


def _token_shift(u, halo, mu, first):
    prev_row = jnp.where(first, 0.0, halo[7:8, :])
    rolled = pltpu.roll(u, 1, 0)
    row = lax.broadcasted_iota(jnp.int32, u.shape, 0)
    prev = jnp.where(row == 0, prev_row, rolled)
    return u + (prev - u) * mu


def _rwkv_chunk_kernel(r_ref, k_ref, v_ref, l_ref, rh_ref, kh_ref, vh_ref, lh_ref,
                       mur_ref, muk_ref, muv_ref, mul_ref, w0_ref, wup_ref, a0_ref, aup_ref, gup_ref,
                       kk_ref, ka_ref, rk_ref, bd_ref, qy_ref, mn_ref, g_ref, bonus_ref):
    first = pl.program_id(1) == 0
    c = RW_CHUNK
    r = _token_shift(r_ref[...], rh_ref[...], mur_ref[...], first)
    k = _token_shift(k_ref[...], kh_ref[...], muk_ref[...], first)
    v = _token_shift(v_ref[...], vh_ref[...], muv_ref[...], first)
    lo = _token_shift(l_ref[...], lh_ref[...], mul_ref[...], first)
    wd = lo[:, :DECAY_LORA]
    ad = lo[:, DECAY_LORA:DECAY_LORA + AAA_LORA]
    gd = lo[:, DECAY_LORA + AAA_LORA:RW_LORA]
    z = -(w0_ref[...] + _dot(jnp.tanh(wd), wup_ref[...], HI))
    w_log = -(jnp.maximum(z, 0.0) + jnp.log(1.0 + jnp.exp(-jnp.abs(z)))) - 0.5
    logw = -jnp.exp(w_log)
    a = jax.nn.sigmoid(a0_ref[...] + _dot(ad, aup_ref[...], HI))
    g_ref[0] = _dot(jax.nn.sigmoid(gd), gup_ref[...], HI)
    kkraw = k * kk_ref[...]
    k2 = k * (1.0 + (a - 1.0) * ka_ref[...])
    rkr = r * k2 * rk_ref[...]

    cum_all = logw
    trow = lax.broadcasted_iota(jnp.int32, logw.shape, 0) % c
    step = 1
    while step < c:
        cum_all = cum_all + jnp.where(trow >= step, pltpu.roll(cum_all, step, 0), 0.0)
        step *= 2

    ti = lax.broadcasted_iota(jnp.int32, (c, c), 0)
    si = lax.broadcasted_iota(jnp.int32, (c, c), 1)
    incl = ti >= si
    strict = ti > si
    eye = ti == si
    zeros = jnp.zeros((c, c), F32)
    bd = bd_ref[...]

    def head_sum(t):
        hi, lo = _split(t)
        return _dot(hi, bd) + _dot(lo, bd)

    kk = kkraw / jnp.maximum(jnp.sqrt(head_sum(kkraw * kkraw)), 1e-12)
    bonus_ref[0] = head_sum(rkr) * v
    nchunk = logw.shape[0] // c
    ends = [cum_all[(j + 1) * c - 1:(j + 1) * c, :] for j in range(nchunk)]
    cum_end = jnp.concatenate([jnp.broadcast_to(e, (c, e.shape[1])) for e in ends], axis=0)
    ginv = jnp.exp(-cum_all)
    gtail = jnp.exp(cum_end - cum_all)
    gend = jnp.exp(cum_end)
    kka = kk * a
    at = -kk * jnp.exp(cum_all - logw)
    bt = kka * ginv
    kt = k2 * ginv
    rt = r * jnp.exp(cum_all)
    bhat = kka * gtail
    khat = k2 * gtail

    heads = [(slice(j * c, (j + 1) * c), slice(h * RW_DH, (h + 1) * RW_DH))
             for j in range(nchunk) for h in range(RW_HEADS)]
    stack = lambda x, y, s: jnp.concatenate([x[s], y[s]], axis=0).astype(BF16)
    gram = [_dot_nt(stack(at, rt, s), stack(bt, kt, s)) for s in heads]
    a_ab = [jnp.where(strict, g[:c, :c], 0.0) for g in gram]
    a_ak = [jnp.where(strict, g[:c, c:], 0.0) for g in gram]
    incl2 = (lax.broadcasted_iota(jnp.int32, (c, 2 * c), 0)
             >= lax.broadcasted_iota(jnp.int32, (c, 2 * c), 1) % c)
    a_r = [jnp.where(incl2, g[c:, :], 0.0) for g in gram]
    p = a_ab
    tinv = [eye.astype(F32) + x for x in p]
    for _ in range(int(math.log2(c)) - 1):
        p = [_bdot(x, x) for x in p]
        tinv = [t + _bdot(t, x) for t, x in zip(tinv, p)]
    av = [_bdot(x, v[s]) for x, s in zip(a_ak, heads)]
    w12 = [_bdot(t, jnp.concatenate([at[s], x], axis=1)) for t, x, s in zip(tinv, av, heads)]
    zmat = [jnp.concatenate([w, jnp.concatenate([zeros, v[s]], axis=1)], axis=0).astype(BF16)
            for w, s in zip(w12, heads)]
    out1 = [_dot(x.astype(BF16), z) for x, z in zip(a_r, zmat)]
    out2 = [_dot_tn(stack(bhat, khat, s), z) for s, z in zip(heads, zmat)]
    qy = [o + jnp.concatenate([rt[s], zeros], axis=1) for o, s in zip(out1, heads)]
    for j in range(nchunk):
        qy_ref[0, j * c:(j + 1) * c, :] = jnp.concatenate(qy[j * RW_HEADS:(j + 1) * RW_HEADS], axis=1)
        for h in range(RW_HEADS):
            idx = j * RW_HEADS + h
            diag = jnp.where(eye, gend[heads[idx]], 0.0)
            mn_ref[0, j, h] = out2[idx] + jnp.concatenate([diag, zeros], axis=1)


def _rwkv_scan_kernel(qy_ref, mn_ref, g_ref, bonus_ref, gnw_ref, gnb_ref, bd_ref, y_ref, st_ref, *, batch):
    @pl.when(pl.program_id(0) == 0)
    def _():
        st_ref[...] = jnp.zeros_like(st_ref)

    bd = bd_ref[...]

    def head_mean(t):
        hi, lo = _split(t)
        return (_dot(hi, bd) + _dot(lo, bd)) * (1.0 / RW_DH)

    pairs = [(b, h) for b in range(batch) for h in range(RW_HEADS)]
    sts = [st_ref[b * RW_HEADS + h] for b, h in pairs]
    ys = [_bdot(qy_ref[b, :, 2 * RW_DH * h:2 * RW_DH * h + RW_DH], st)
          + qy_ref[b, :, 2 * RW_DH * h + RW_DH:2 * RW_DH * (h + 1)] for (b, h), st in zip(pairs, sts)]
    for (b, h), st in zip(pairs, sts):
        mn = mn_ref[b, 0, h]
        st_ref[b * RW_HEADS + h] = _dot3(mn[:, :RW_DH], st) + mn[:, RW_DH:]
    for b in range(batch):
        y = jnp.concatenate(ys[b * RW_HEADS:(b + 1) * RW_HEADS], axis=1)
        dev = y - head_mean(y)
        yn = dev * lax.rsqrt(head_mean(dev * dev) + RW_GN_EPS) * gnw_ref[...] + gnb_ref[...]
        y_ref[b] = ((yn + bonus_ref[b]) * g_ref[b]).astype(y_ref.dtype)


def _rwkv(proj, p, batch, seq):
    head_ones = jnp.asarray(np.kron(np.eye(RW_HEADS), np.ones((RW_DH, RW_DH))), BF16)
    c = RW_CHUNK
    nc = seq // c
    tb = RW_NB * c
    nt = seq // tb
    hb = tb // 8
    row512 = lambda col: pl.BlockSpec((tb, RW_WIDTH), lambda b, i: (b * nt + i, col))
    halo512 = lambda col: pl.BlockSpec((8, RW_WIDTH), lambda b, i: (jnp.maximum((b * nt + i) * hb - 1, 0), col))
    const = lambda shape: pl.BlockSpec(shape, lambda b, i: (0,) * len(shape))
    vec = const((1, RW_WIDTH))
    out_row = lambda w: pl.BlockSpec((1, tb, w), lambda b, i: (b, i, 0))
    qy, mn, g, bonus = pl.pallas_call(
        _rwkv_chunk_kernel,
        out_shape=(jax.ShapeDtypeStruct((batch, seq, 2 * RW_WIDTH), F32),
                   jax.ShapeDtypeStruct((batch, nc, RW_HEADS, RW_DH, 2 * RW_DH), F32),
                   jax.ShapeDtypeStruct((batch, seq, RW_WIDTH), F32),
                   jax.ShapeDtypeStruct((batch, seq, RW_WIDTH), F32)),
        grid=(batch, nt),
        in_specs=[row512(C_R // RW_WIDTH), row512(C_K // RW_WIDTH), row512(C_V // RW_WIDTH),
                  pl.BlockSpec((tb, LORA_PAD), lambda b, i: (b * nt + i, C_LORA // LORA_PAD)),
                  halo512(C_R // RW_WIDTH), halo512(C_K // RW_WIDTH), halo512(C_V // RW_WIDTH),
                  pl.BlockSpec((8, LORA_PAD), lambda b, i: (jnp.maximum((b * nt + i) * hb - 1, 0), C_LORA // LORA_PAD)),
                  vec, vec, vec, const((1, LORA_PAD)),
                  vec, const((DECAY_LORA, RW_WIDTH)), vec, const((AAA_LORA, RW_WIDTH)), const((GATE_LORA, RW_WIDTH)),
                  vec, vec, vec, const((RW_WIDTH, RW_WIDTH))],
        out_specs=(out_row(2 * RW_WIDTH),
                   pl.BlockSpec((1, RW_NB, RW_HEADS, RW_DH, 2 * RW_DH), lambda b, i: (b, i, 0, 0, 0)),
                   out_row(RW_WIDTH), out_row(RW_WIDTH)),
        compiler_params=_cp(("parallel", "parallel"), VMEM_LIMIT),
        name="rwkv_chunk",
    )(proj, proj, proj, proj, proj, proj, proj, proj,
      p["mu_r"], p["mu_k"], p["mu_v"], p["mu_l"], p["w0"], p["w_up"], p["a0"], p["a_up"], p["g_up"],
      p["k_k"], p["k_a"], p["r_k"], head_ones)

    full = lambda w: pl.BlockSpec((batch, c, w), lambda i: (0, i, 0))
    return pl.pallas_call(
        functools.partial(_rwkv_scan_kernel, batch=batch),
        out_shape=jax.ShapeDtypeStruct((batch, seq, RW_WIDTH), BF16),
        grid=(nc,),
        in_specs=[full(2 * RW_WIDTH),
                  pl.BlockSpec((batch, 1, RW_HEADS, RW_DH, 2 * RW_DH), lambda i: (0, i, 0, 0, 0)),
                  full(RW_WIDTH), full(RW_WIDTH),
                  pl.BlockSpec((1, RW_WIDTH), lambda i: (0, 0)), pl.BlockSpec((1, RW_WIDTH), lambda i: (0, 0)),
                  pl.BlockSpec((RW_WIDTH, RW_WIDTH), lambda i: (0, 0))],
        out_specs=full(RW_WIDTH),
        scratch_shapes=[pltpu.VMEM((batch * RW_HEADS, RW_DH, RW_DH), F32)],
        compiler_params=_cp(("arbitrary",), VMEM_LIMIT),
        name="rwkv_scan",
    )(qy, mn, g, bonus, p["gn_w"], p["gn_b"], head_ones).reshape(batch * seq, RW_WIDTH)


def _gelu_tanh(x):
    return 0.5 * x * (1.0 + jnp.tanh(math.sqrt(2.0 / math.pi) * (x + 0.044715 * (x * x * x))))


def _compress_kernel(x_ref, pos_ref, w1_ref, w2_ref, o_ref):
    half = CMP_STRIDE * NSA_DH
    x = x_ref[0, 0, 0]
    w1 = w1_ref[0]
    posb = _dot(pos_ref[0], w1)[0:1, :]
    h1 = _dot(x, w1[:half])
    h2 = _dot(x, w1[half:])
    n = h2.shape[0]
    row = lax.broadcasted_iota(jnp.int32, h2.shape, 0)
    h2s = jnp.where(row < n - 1, pltpu.roll(h2, n - 1, 0), 0.0)
    hid = _gelu_tanh(h1 + h2s + posb)
    o_ref[0, 0, 0] = _dot(hid.astype(BF16), w2_ref[0]).astype(o_ref.dtype)


def _compress(xkv, pos, w1, w2, batch, seq):
    nr = seq // CMP_STRIDE
    wide = CMP_STRIDE * NSA_DH
    return pl.pallas_call(
        _compress_kernel,
        out_shape=jax.ShapeDtypeStruct((2, batch, NSA_G, nr, NSA_DH), BF16),
        grid=(2, batch, NSA_G),
        in_specs=[pl.BlockSpec((1, 1, 1, nr, wide), lambda t, b, g: (t, b, g, 0, 0)),
                  pl.BlockSpec((1, 8, 2 * wide), lambda t, b, g: (t, 0, 0)),
                  pl.BlockSpec((1, 2 * wide, CMP_HIDDEN), lambda t, b, g: (t, 0, 0)),
                  pl.BlockSpec((1, CMP_HIDDEN, NSA_DH), lambda t, b, g: (t, 0, 0))],
        out_specs=pl.BlockSpec((1, 1, 1, nr, NSA_DH), lambda t, b, g: (t, b, g, 0, 0)),
        compiler_params=_cp(("parallel", "parallel", "parallel")),
        name="nsa_compress",
    )(xkv, pos, w1, w2)


def _softmax_parts(s):
    m = jnp.max(s, axis=1, keepdims=True)
    e = jnp.exp(s - m)
    return m, e, jnp.sum(e, axis=1, keepdims=True)


def _lane_tile_fold(x, op, init):
    for t in range(x.shape[1] // 128):
        init = op(init, x[:, 128 * t:128 * (t + 1)])
    return init


def _nsa_kernel(q_ref, gate_ref, kc_ref, vc_ref, ks_ref, vs_ref, kw_ref, vw_ref,
                tblc_ref, tbln_ref, tblw_ref, ovt_ref, gexp_ref, o_ref, s_ref, *, ncmp):
    nb = NSA_NB
    blocks = [pl.program_id(2) * nb + u for u in range(nb)]
    rows = NSA_HG * QB
    each = lambda f, *ls: [f(*xs) for xs in zip(*ls)]
    qt = q_ref[...]
    qs = [jnp.concatenate([qt[u * QB:(u + 1) * QB, NSA_DH * h:NSA_DH * (h + 1)] for h in range(NSA_HG)], axis=0)
          * (NSA_DH ** -0.5 * LOG2E) for u in range(nb)]
    qb = each(lambda x: x.astype(BF16), qs)
    rmax = lambda x: jnp.max(x, axis=1, keepdims=True)
    rsum = lambda x: jnp.sum(x, axis=1, keepdims=True)

    win_w = WINDOW + QB
    kw = [kw_ref[0, 0, pl.ds(pl.multiple_of(i * QB, QB), win_w), :] for i in blocks]
    vw = [vw_ref[0, 0, pl.ds(pl.multiple_of(i * QB, QB), win_w), :] for i in blocks]
    flag = lax.broadcasted_iota(jnp.int32, (rows, kw_ref.shape[3] - NSA_DH), 1) == 0
    q_win = each(lambda x: jnp.concatenate([x, jnp.where(flag, NEG, 0.0)], axis=1).astype(BF16), qs)
    tblw = tblw_ref[0]
    pv = lambda x: x[:, :NSA_DH] / x[:, NSA_DH:NSA_DH + 1]
    s_w = each(lambda x, k: _dot_nt(x, k) + tblw, q_win, kw)
    m_w = each(rmax, s_w)
    e_w = each(lambda s, m: jnp.exp2(s - m), s_w, m_w)
    o_w = each(lambda e, v: pv(_dot(e.astype(BF16), v)), e_w, vw)

    kc = kc_ref[0, 0]
    vc = vc_ref[0, 0]
    tblc = tblc_ref[0]
    cidx = lax.broadcasted_iota(jnp.int32, (rows, ncmp), 1)
    qrow = lax.broadcasted_iota(jnp.int32, (rows, 1), 0) % QB
    lc = [jnp.where(cidx < (QB // CMP_STRIDE) * i + CMP_AHEAD,
                    _dot_nt(x, kc) + pltpu.roll(tblc, (4 * i - CMP_NEAR // 2 + ncmp) % ncmp, 1), NEG)
          for x, i in zip(qb, blocks)]
    m_c = each(rmax, lc)
    e_c = each(lambda s, m: jnp.exp2(s - m), lc, m_c)
    den_c = each(rsum, e_c)
    pc = [e * jnp.where(i * QB + qrow >= CMP_BLOCK - 1, 1.0 / d, 0.0) for e, d, i in zip(e_c, den_c, blocks)]
    o_c = each(lambda x: _dot(x.astype(BF16), vc), pc)
    pcs_hi, pcs_lo = _split(jnp.concatenate(
        each(lambda x: x[0:QB] + x[QB:2 * QB] + x[2 * QB:3 * QB] + x[3 * QB:4 * QB], pc), axis=0))
    ovt = ovt_ref[...]
    imp = _dot_nt(ovt, pcs_hi) + _dot_nt(ovt, pcs_lo)

    nslc = ovt.shape[0]
    nidx = lax.broadcasted_iota(jnp.int32, (nslc, nb * QB), 0)
    cur = blocks[0] + lax.broadcasted_iota(jnp.int32, (nslc, nb * QB), 1) // QB
    forced = (nidx == 0) | (nidx == cur) | (nidx == cur - 1)
    work = jnp.where(nidx <= cur, imp + jnp.where(forced, FORCE_BONUS, 0.0), -1.0)
    sel_all = jnp.zeros((nslc, nb * QB), F32)
    for _ in range(N_SELECT):
        m = jnp.max(work, axis=0, keepdims=True)
        first = jnp.min(jnp.where(work == m, nidx, nslc), axis=0, keepdims=True)
        pick = nidx == first
        sel_all = jnp.where(pick & (m >= 0.0), 1.0, sel_all)
        work = jnp.where(pick, -2.0, work)
    sel_t = sel_all.T

    ind_w = ks_ref.shape[3] - NSA_DH
    selq = [sel_t[u * QB:(u + 1) * QB] for u in range(nb)]
    if nslc < ind_w:
        selq = each(lambda x: jnp.concatenate([x, jnp.zeros((QB, ind_w - nslc), F32)], axis=1), selq)
    bidx = lax.broadcasted_iota(jnp.int32, (QB, ind_w), 1)
    tile4 = lambda t: jnp.concatenate([t] * NSA_HG, axis=0)
    with_mask = lambda x, keep: jnp.concatenate([x, tile4(jnp.where(keep, 0.0, NEG))], axis=1).astype(BF16)
    q_sel = each(lambda x, s: with_mask(x, s > 0.0), qs, selq)
    q_far = jnp.concatenate([with_mask(x, (s > 0.0) & (bidx <= i - NEAR_BLOCKS))
                             for x, s, i in zip(qs, selq, blocks)], axis=0)
    near_w = NEAR_BLOCKS * SLC_BLOCK
    pad_s = KV_PAD
    near0 = pad_s - (NEAR_BLOCKS - 1) * SLC_BLOCK
    kn = [ks_ref[0, 0, pl.ds(pl.multiple_of(near0 + i * QB, QB), near_w), :] for i in blocks]
    vn = [vs_ref[0, 0, pl.ds(pl.multiple_of(near0 + i * QB, QB), near_w), :] for i in blocks]
    tbln = tbln_ref[0]
    s_near = each(lambda x, k: _dot_nt(x, k) + tbln, q_sel, kn)
    m_near = jnp.concatenate(each(rmax, s_near), axis=0)

    far_w = FAR_CHUNK_BLOCKS * SLC_BLOCK
    n_far =jnp.maximum(blocks[-1] - (NEAR_BLOCKS - 1) + FAR_CHUNK_BLOCKS - 1, 0) // FAR_CHUNK_BLOCKS

    def far_logits(js, mvec):
        ss = [_dot_nt(q_far, ks_ref[0, 0, pl.ds(pl.multiple_of(pad_s + j * far_w, SLC_BLOCK), far_w), :])
              for j in js]
        for j, s in zip(js, ss):
            s_ref[:, pl.ds(pl.multiple_of(j * far_w, far_w), far_w)] = s
        for s in ss:
            mvec = _lane_tile_fold(s, jnp.maximum, mvec)
        return mvec

    n_pair = n_far // 2
    odd = n_far % 2 == 1
    mvec = lax.fori_loop(0, n_pair, lambda jp, m: far_logits([2 * jp, 2 * jp + 1], m),
                         jnp.full((nb * rows, 128), NEG, F32))
    mvec = lax.cond(odd, lambda m: far_logits([n_far - 1], m), lambda m: m, mvec)
    m_s = jnp.maximum(m_near, rmax(mvec))

    e_near = [jnp.exp2(s - m_s[u * rows:(u + 1) * rows]) for u, s in enumerate(s_near)]
    acc0 = jnp.concatenate(each(lambda e, v: _dot(e.astype(BF16), v), e_near, vn), axis=0)

    def far_values(js, acc):
        es = [jnp.exp2(s_ref[:, pl.ds(pl.multiple_of(j * far_w, far_w), far_w)] - m_s).astype(BF16) for j in js]
        for j, e in zip(js, es):
            acc = acc + _dot(e, vs_ref[0, 0, pl.ds(pl.multiple_of(pad_s + j * far_w, SLC_BLOCK), far_w), :])
        return acc

    acc = lax.fori_loop(0, n_pair, lambda jp, a: far_values([2 * jp, 2 * jp + 1], a), acc0)
    o_s = pv(lax.cond(odd, lambda a: far_values([n_far - 1], a), lambda a: a, acc))

    g_hi, g_lo = _split(jax.nn.sigmoid(gate_ref[0, 0]))
    gexp = gexp_ref[...]
    ge = _dot(g_hi, gexp) + _dot(g_lo, gexp)
    for u in range(nb):
        outs = []
        for h in range(NSA_HG):
            r0 = slice(h * QB, (h + 1) * QB)
            gt = lambda br: ge[u * QB:(u + 1) * QB, (3 * h + br) * 128:(3 * h + br) * 128 + NSA_DH]
            outs.append(gt(0) * o_c[u][r0] + gt(1) * o_s[u * rows + h * QB:u * rows + (h + 1) * QB]
                        + gt(2) * o_w[u][r0])
        o_ref[u * QB:(u + 1) * QB, :] = jnp.concatenate(outs, axis=1).astype(o_ref.dtype)


def _bias_tables(rel_bias, ncmp):
    def table(dist, keep, base):
        onehot = jnp.asarray(np.eye(N_BUCKETS, dtype=np.float32)[_t5_bucket_np(dist)])
        tbl = jnp.einsum("qkb,bh->qkh", onehot, rel_bias, precision=HI)
        tbl = (tbl - base) * LOG2E
        tbl = jnp.where(jnp.asarray(keep)[..., None], tbl, NEG)
        k = dist.shape[1]
        return tbl.transpose(2, 0, 1).reshape(NSA_G, NSA_HG * QB, k)

    far = rel_bias[N_BUCKETS - 1]
    qi = np.arange(QB)[:, None]
    dist_c = qi - CMP_STRIDE * (np.arange(CMP_NEAR)[None, :] - CMP_NEAR // 2) - (CMP_BLOCK - 1)
    tblc = table(dist_c, dist_c >= 0, far)
    tblc = tblc * jnp.asarray(np.arange(CMP_NEAR) < CMP_NEAR // 2 + CMP_AHEAD, F32)
    tblc = jnp.pad(tblc, ((0, 0), (0, 0), (0, ncmp - CMP_NEAR)))
    jn = np.arange(NEAR_BLOCKS * SLC_BLOCK)[None, :]
    dist_n = (NEAR_BLOCKS - 1) * SLC_BLOCK + qi - jn
    tbln = table(dist_n, dist_n >= 0, far)
    jw = np.arange(WINDOW + QB)[None, :]
    dist_w = WINDOW + qi - jw
    tblw = table(dist_w, (dist_w >= 0) & (dist_w < WINDOW), 0.0)
    return tblc, tbln, tblw


def _nsa(proj, gate_logits, kvc, ks, vs, kw, vw, tables, batch, seq):
    nq = seq // QB
    ncmp = seq // CMP_STRIDE
    nslc = seq // SLC_BLOCK
    tblc, tbln, tblw = tables
    cstart = np.arange(ncmp) * CMP_STRIDE
    sstart = np.arange(nslc) * SLC_BLOCK
    overlap_t = ((cstart[None, :] <= sstart[:, None] + SLC_BLOCK - 1)
                 & (cstart[None, :] + CMP_BLOCK - 1 >= sstart[:, None])
                 & (cstart[None, :] + CMP_BLOCK <= seq)).astype(np.float32)
    rows = NSA_HG * QB
    n_gate = 3 * NSA_HG
    gate_spread = (np.arange(n_gate * 128)[None, :] // 128 == np.arange(n_gate)[:, None]) \
        & (np.arange(n_gate * 128)[None, :] % 128 < NSA_DH)
    kv_spec = lambda t: pl.BlockSpec((1, 1) + t.shape[2:], lambda b, g, i: (b, g, 0, 0))
    tbl_spec = lambda k: pl.BlockSpec((1, rows, k), lambda b, g, i: (g, 0, 0))
    qcol = C_Q // (NSA_HG * NSA_DH)
    return pl.pallas_call(
        functools.partial(_nsa_kernel, ncmp=ncmp),
        out_shape=jax.ShapeDtypeStruct((batch * seq, NSA_HEADS * NSA_DH), BF16),
        grid=(batch, NSA_G, nq // NSA_NB),
        in_specs=[pl.BlockSpec((NSA_NB * QB, NSA_HG * NSA_DH), lambda b, g, i: (b * (nq // NSA_NB) + i, qcol + g)),
                  pl.BlockSpec((1, 1, NSA_NB * QB, 3 * NSA_HG), lambda b, g, i: (b, g, i, 0)),
                  pl.BlockSpec((1, 1, ncmp, NSA_DH), lambda b, g, i: (b, g, 0, 0)),
                  pl.BlockSpec((1, 1, ncmp, NSA_DH), lambda b, g, i: (b, g, 0, 0)),
                  kv_spec(ks), kv_spec(vs), kv_spec(kw), kv_spec(vw),
                  tbl_spec(ncmp), tbl_spec(NEAR_BLOCKS * SLC_BLOCK), tbl_spec(WINDOW + QB),
                  pl.BlockSpec((nslc, ncmp), lambda b, g, i: (0, 0)),
                  pl.BlockSpec(gate_spread.shape, lambda b, g, i: (0, 0))],
        out_specs=pl.BlockSpec((NSA_NB * QB, NSA_HG * NSA_DH), lambda b, g, i: (b * (nq // NSA_NB) + i, g)),
        scratch_shapes=[pltpu.VMEM((NSA_NB * rows, seq), F32)],
        compiler_params=_cp(("parallel", "parallel", "arbitrary"), VMEM_LIMIT),
        name="nsa_attention",
    )(proj, gate_logits, kvc[0], kvc[1], ks, vs, kw, vw, tblc, tbln, tblw, jnp.asarray(overlap_t, BF16),
      jnp.asarray(gate_spread, BF16))


def _merge_kernel(h_ref, yp_ref, yr_ref, yn_ref, x_ref, wbp_ref, wbr_ref, wbn_ref, wm_ref, bm_ref, wo_ref,
                  gn_ref, wr_ref, br_ref, xo_ref, h2_ref, route_ref):
    d = D_MODEL
    gl = jax.nn.sigmoid(_dot(h_ref[...], wm_ref[...]) + bm_ref[...])
    merged = (gl[:, :d] * _dot(yp_ref[...], wbp_ref[...]) + gl[:, d:2 * d] * _dot(yr_ref[...], wbr_ref[...])
              + gl[:, 2 * d:] * _dot(yn_ref[...], wbn_ref[...]))
    x = x_ref[...] + _dot(merged.astype(BF16), wo_ref[...])
    xo_ref[...] = x
    h2 = _rms(x, gn_ref[...])
    h2_ref[...] = h2.astype(h2_ref.dtype)

    h2_hi, h2_lo = _split(h2)
    nl = br_ref.shape[1]
    both = _dot(h2_hi, wr_ref[...])
    logits = both[:, :nl] + (both[:, nl:] + _dot(h2_lo, wr_ref[:, :nl])) + br_ref[...]
    lane = lax.broadcasted_iota(jnp.int32, logits.shape, 1)
    big = logits.shape[1]
    lg = jnp.where(lane < N_GROUPS, logits, NEG)
    mg = jnp.max(lg, axis=1, keepdims=True)
    p_top = 1.0 / jnp.sum(jnp.exp(lg - mg), axis=1, keepdims=True)
    grp = jnp.min(jnp.where(lg == mg, lane, big), axis=1, keepdims=True)
    lo = N_GROUPS + EPG * grp
    le = jnp.where((lane >= lo) & (lane < lo + EPG), logits, NEG)
    e1 = jnp.max(le, axis=1, keepdims=True)
    i1 = jnp.min(jnp.where(le == e1, lane, big), axis=1, keepdims=True)
    le = jnp.where(lane == i1, NEG, le)
    e2 = jnp.max(le, axis=1, keepdims=True)
    i2 = jnp.min(jnp.where(le == e2, lane, big), axis=1, keepdims=True)
    t = jnp.exp(e2 - e1)
    w1 = p_top / (1.0 + t)
    w2 = p_top * t / (1.0 + t)
    route_ref[...] = jnp.where(lane == 0, (i1 - N_GROUPS).astype(F32),
                               jnp.where(lane == 1, (i2 - N_GROUPS).astype(F32),
                                         jnp.where(lane == 2, w1, jnp.where(lane == 3, w2, 0.0))))


def _merge(h, y_pool, y_rwkv, y_nsa, x, p, tm=256):
    n, d = x.shape
    row = lambda w: pl.BlockSpec((tm, w), lambda i: (i, 0))
    const = lambda a: pl.BlockSpec(a.shape, lambda i: (0, 0))
    ws = [p["wb_pool"], p["wb_rwkv"], p["wb_nsa"], p["w_merge"], p["b_merge"], p["w_out"],
          p["norm_ffn"], p["w_router"], p["b_router"]]
    return pl.pallas_call(
        _merge_kernel,
        out_shape=(jax.ShapeDtypeStruct((n, d), F32), jax.ShapeDtypeStruct((n, d), F32),
                   jax.ShapeDtypeStruct((n, 128), F32)),
        grid=(n // tm,),
        in_specs=[row(d), row(POOL_WIDTH), row(RW_WIDTH), row(d), row(d)] + [const(w) for w in ws],
        out_specs=(row(d), row(d), row(128)),
        compiler_params=_cp(("parallel",), VMEM_LIMIT),
        name="merge_router",
    )(h, y_pool, y_rwkv, y_nsa, x, *ws)


def _expert_kernel(te_ref, nt_ref, tok_ref, h_hbm, w_ref, wg_ref, wu_ref, wd_ref, o_ref,
                   xbuf, sem, wg_s, wu_s, wd_s):
    i = pl.program_id(0)
    tm = MOE_TM
    n_tiles = nt_ref[0]

    def row_copy(tile, slot, r):
        return pltpu.make_async_copy(h_hbm.at[pl.ds(tok_ref[tile * tm + r], 1), :],
                                     xbuf.at[slot, pl.ds(r, 1), :], sem.at[slot])

    def tile_wait(slot):
        pltpu.make_async_copy(h_hbm.at[pl.ds(0, tm), :], xbuf.at[slot], sem.at[slot]).wait()

    @pl.when((i == 0) & (n_tiles > 0))
    def _():
        def body(k, carry):
            for q in range(DMA_QUEUES):
                row_copy(0, 0, DMA_QUEUES * k + q).start(priority=q)
            return carry
        lax.fori_loop(0, tm // DMA_QUEUES, body, 0, unroll=4)

    @pl.when((i == 0) | (te_ref[i] != te_ref[jnp.maximum(i - 1, 0)]))
    def _():
        wg_s[...] = wg_ref[0, 0].astype(BF16)
        wu_s[...] = wu_ref[0, 0].astype(BF16)
        wd_s[...] = wd_ref[0, 0].astype(BF16)

    @pl.when(i < n_tiles)
    def _():
        slot = i % 2
        quarter = tm // 4

        def fetch_next(part):
            for r in range(part * quarter, (part + 1) * quarter):
                row_copy(i + 1, 1 - slot, r).start(priority=r % DMA_QUEUES)

        tile_wait(slot)
        xb = xbuf[slot].astype(BF16)
        fetch_next(0)
        gate = _dot(xb, wg_s[...])
        fetch_next(1)
        up = _dot(xb, wu_s[...])
        fetch_next(2)
        hid = (gate * jax.nn.sigmoid(gate) * up).astype(BF16)
        fetch_next(3)
        o_ref[...] = w_ref[...] * _dot(hid, wd_s[...])

    @pl.when((i == n_tiles) & (n_tiles > 0))
    def _():
        tile_wait(i % 2)

    @pl.when(i >= nt_ref[0])
    def _():
        o_ref[...] = jnp.zeros_like(o_ref)


def _experts(h2, rowtok, roww, tile_expert, n_tiles, layer, wg, wu, wd):
    d = h2.shape[1]
    r = rowtok.shape[0]
    tm = MOE_TM
    return pl.pallas_call(
        _expert_kernel,
        out_shape=jax.ShapeDtypeStruct((r, d), F32),
        grid_spec=pltpu.PrefetchScalarGridSpec(
            num_scalar_prefetch=3,
            grid=(r // tm,),
            in_specs=[pl.BlockSpec(memory_space=pl.ANY),
                      pl.BlockSpec((tm, 1), lambda i, te, nt, tok: (i, 0)),
                      pl.BlockSpec((1, 1, d, D_EXPERT), lambda i, te, nt, tok: (layer, te[i], 0, 0)),
                      pl.BlockSpec((1, 1, d, D_EXPERT), lambda i, te, nt, tok: (layer, te[i], 0, 0)),
                      pl.BlockSpec((1, 1, D_EXPERT, d), lambda i, te, nt, tok: (layer, te[i], 0, 0))],
            out_specs=pl.BlockSpec((tm, d), lambda i, te, nt, tok: (i, 0)),
            scratch_shapes=[pltpu.VMEM((2, tm, d), F32), pltpu.SemaphoreType.DMA((2,)),
                            pltpu.VMEM((d, D_EXPERT), BF16), pltpu.VMEM((d, D_EXPERT), BF16),
                            pltpu.VMEM((D_EXPERT, d), BF16)]),
        compiler_params=_cp(("arbitrary",), VMEM_LIMIT),
        name="moe_experts",
    )(tile_expert, n_tiles, rowtok, h2, roww, wg, wu, wd)


def _moe(h2, route, layer, wg, wu, wd):
    n = h2.shape[0]
    tm = MOE_TM
    r = 2 * n + (N_EXPERTS + 1) * tm
    ids = route[:, 0:2].astype(jnp.int32).reshape(-1)
    wts = route[:, 2:4].reshape(-1)
    onehot = (ids[:, None] == jnp.arange(N_EXPERTS)[None, :]).astype(jnp.int32)
    rank = jnp.sum((jnp.cumsum(onehot, axis=0) - onehot) * onehot, axis=1)
    counts = jnp.sum(onehot, axis=0)
    tiles = (counts + tm - 1) // tm
    tile_end = jnp.cumsum(tiles)
    starts = (tile_end - tiles) * tm
    pos = starts[ids] + rank
    row_assign = jnp.full((r,), -1, jnp.int32).at[pos].set(
        jnp.arange(2 * n, dtype=jnp.int32), unique_indices=True, mode="promise_in_bounds")
    rowtok = jnp.maximum(row_assign, 0) // 2
    roww = jnp.where(row_assign >= 0, wts[jnp.maximum(row_assign, 0)], 0.0)
    n_tiles = tile_end[-1:].astype(jnp.int32)
    tile_expert = jnp.minimum(jnp.sum(tile_end[None, :] <= jnp.arange(r // tm)[:, None], axis=1),
                              N_EXPERTS - 1).astype(jnp.int32)
    ys = _experts(h2, rowtok, roww.reshape(r, 1), tile_expert, n_tiles, layer, wg, wu, wd)
    return ys[pos[0::2]], ys[pos[1::2]]


def _layer_params(l, a):
    f = lambda t: t[l]
    row = lambda t: t[l].reshape(1, -1)
    w_in = a["w_in"][l]
    w_in_p = jnp.concatenate([w_in[:, :SRC_RW_END], jnp.zeros((D_MODEL, C_Q - SRC_RW_END), F32),
                              w_in[:, SRC_Q:SRC_KV], w_in[:, SRC_GATE:],
                              jnp.zeros((D_MODEL, P_COLS - C_GATE - (w_in.shape[1] - SRC_GATE)), F32)], axis=1)
    mu = a["rw_mu"][l]
    wb = a["w_branch"][l].astype(BF16)
    w_router = jnp.zeros((D_MODEL, 128), F32)
    w_router = w_router.at[:, :N_GROUPS].set(a["w_router_grp"][l]).at[:, N_GROUPS:N_GROUPS + N_EXPERTS].set(
        a["w_router_exp"][l])
    b_router = jnp.zeros((1, 128), F32)
    b_router = b_router.at[0, :N_GROUPS].set(a["b_router_grp"][l]).at[0, N_GROUPS:N_GROUPS + N_EXPERTS].set(
        a["b_router_exp"][l])
    pos = jnp.stack([a["cmp_pos_k"][l].reshape(-1), a["cmp_pos_v"][l].reshape(-1)])
    return {
        "w_in": w_in_p.astype(BF16), "w_kv": w_in[:, SRC_KV:SRC_GATE].astype(BF16),
        "pool_w": f(a["pool_w"]), "pool_scale": f(a["pool_scale"]),
        "mu_r": mu[None, 0:RW_WIDTH], "mu_k": mu[None, RW_WIDTH:2 * RW_WIDTH],
        "mu_v": mu[None, 2 * RW_WIDTH:3 * RW_WIDTH],
        "mu_l": jnp.concatenate([mu[3 * RW_WIDTH:], jnp.zeros((LORA_PAD - RW_LORA,), F32)])[None],
        "w0": row(a["rw_w0"]), "w_up": f(a["rw_w_up"]), "a0": row(a["rw_a0"]), "a_up": f(a["rw_a_up"]),
        "g_up": f(a["rw_g_up"]), "k_k": row(a["rw_k_k"]), "k_a": row(a["rw_k_a"]), "r_k": row(a["rw_r_k"]),
        "gn_w": row(a["rw_gn_w"]), "gn_b": row(a["rw_gn_b"]),
        "cmp_pos": jnp.broadcast_to(pos[:, None, :], (2, 8, pos.shape[1])).astype(BF16),
        "cmp_w1": jnp.stack([a["cmp_w1_k"][l], a["cmp_w1_v"][l]]).astype(BF16),
        "cmp_w2": jnp.stack([a["cmp_w2_k"][l], a["cmp_w2_v"][l]]).astype(BF16),
        "wb_pool": wb[:POOL_WIDTH], "wb_rwkv": wb[POOL_WIDTH:POOL_WIDTH + RW_WIDTH],
        "wb_nsa": wb[POOL_WIDTH + RW_WIDTH:],
        "w_merge": a["w_merge"][l].astype(BF16), "b_merge": row(a["b_merge"]),
        "w_out": a["w_out"][l].astype(BF16), "norm_ffn": row(a["norm_ffn"]),
        "w_router": jnp.concatenate(_split(w_router), axis=1), "b_router": b_router,
    }


def _mixers(proj, kv, p, tables, batch, seq):
    kv_cmp, ks, vs, kw, vw = kv
    y_pool = _pool(proj, p["pool_w"], p["pool_scale"], batch, seq)
    y_rwkv = _rwkv(proj, p, batch, seq)
    kvc = _compress(kv_cmp, p["cmp_pos"], p["cmp_w1"], p["cmp_w2"], batch, seq)
    gate_logits = proj[:, C_GATE:C_GATE + 3 * NSA_HEADS].reshape(batch, seq, NSA_G, 3 * NSA_HG).transpose(0, 2, 1, 3)
    y_nsa = _nsa(proj, gate_logits, kvc, ks, vs, kw, vw, tables, batch, seq)
    return y_pool, y_rwkv, y_nsa


def kernel(x, rel_bias, norm_mix, w_in, pool_w, pool_scale, rw_mu, rw_w0, rw_w_up, rw_a0, rw_a_up, rw_g_up, rw_k_k, rw_k_a, rw_r_k, rw_gn_w, rw_gn_b, cmp_pos_k, cmp_w1_k, cmp_w2_k, cmp_pos_v, cmp_w1_v, cmp_w2_v, w_branch, w_merge, b_merge, w_out, norm_ffn, w_router_grp, b_router_grp, w_router_exp, b_router_exp, w_exp_gate, w_exp_up, w_exp_down, norm_final):
    a = dict(w_in=w_in, pool_w=pool_w, pool_scale=pool_scale, rw_mu=rw_mu, rw_w0=rw_w0, rw_w_up=rw_w_up,
             rw_a0=rw_a0, rw_a_up=rw_a_up, rw_g_up=rw_g_up, rw_k_k=rw_k_k, rw_k_a=rw_k_a, rw_r_k=rw_r_k,
             rw_gn_w=rw_gn_w, rw_gn_b=rw_gn_b, cmp_pos_k=cmp_pos_k, cmp_w1_k=cmp_w1_k, cmp_w2_k=cmp_w2_k,
             cmp_pos_v=cmp_pos_v, cmp_w1_v=cmp_w1_v, cmp_w2_v=cmp_w2_v, w_branch=w_branch, w_merge=w_merge,
             b_merge=b_merge, w_out=w_out, norm_ffn=norm_ffn, w_router_grp=w_router_grp,
             b_router_grp=b_router_grp, w_router_exp=w_router_exp, b_router_exp=b_router_exp,
             w_exp_gate=w_exp_gate, w_exp_up=w_exp_up, w_exp_down=w_exp_down)
    batch, seq, d = x.shape
    depth = norm_mix.shape[0]
    tables = _bias_tables(rel_bias, seq // CMP_STRIDE)
    xf = x.reshape(batch * seq, d)
    h = _norm(xf, norm_mix[0], BF16)
    for l in range(depth):
        p = _layer_params(l, a)
        proj = _matmul(h, p["w_in"])
        kv = _kv_proj(h, p["w_kv"], batch, seq)
        y_pool, y_rwkv, y_nsa = _mixers(proj, kv, p, tables, batch, seq)
        xf, h2, route = _merge(h, y_pool, y_rwkv, y_nsa, xf, p)
        y1, y2 = _moe(h2, route, l, w_exp_gate, w_exp_up, w_exp_down)
        last = l == depth - 1
        g_next = norm_final if last else norm_mix[l + 1]
        xf, h = _add_norm(xf, y1, y2, g_next, F32 if last else BF16)
    return h.reshape(batch, seq, d)
```

```python
import functools
import math

import jax
import jax.numpy as jnp
import numpy as np
from jax import lax
from jax.experimental import pallas as pl
from jax.experimental.pallas import tpu as pltpu

F32 = jnp.float32
BF16 = jnp.bfloat16
HI = lax.Precision.HIGHEST

D_MODEL = 1024
RMS_EPS = 1e-6
NEG = -1e30
LOG2E = math.log2(math.e)

POOL_WINDOWS = (2, 4, 8, 16)
POOL_WIDTH = 512
POOL_GW = 128
POOL_HALO = 16

RW_HEADS = 8
RW_DH = 64
RW_WIDTH = 512
DECAY_LORA, AAA_LORA, GATE_LORA = 32, 32, 96
RW_LORA = DECAY_LORA + AAA_LORA + GATE_LORA
RW_COLS = 3 * RW_WIDTH + RW_LORA
RW_GN_EPS = 64e-5
RW_CHUNK = 64
RW_NB = 4

NSA_DH = 64
NSA_HEADS = 16
NSA_G = 4
NSA_HG = 4
NSA_KVW = NSA_G * NSA_DH
CMP_BLOCK, CMP_STRIDE, CMP_HIDDEN = 32, 16, 256
SLC_BLOCK = 64
N_SELECT = 8
WINDOW = 512
KV_PAD = WINDOW
QB = 64
NSA_NB = 8
FORCE_BONUS = 1e3
N_BUCKETS, MAX_EXACT, MAX_DISTANCE = 32, 16, 128
NEAR_BLOCKS = 3
FAR_CHUNK_BLOCKS = 8
CMP_NEAR = 32
CMP_AHEAD = (QB - CMP_BLOCK) // CMP_STRIDE + 1

N_GROUPS, EPG, N_EXPERTS, D_EXPERT = 4, 8, 32, 256
MOE_TM = 128

C_POOL, C_R, C_K, C_V, C_LORA, C_Q, C_GATE, P_COLS = 0, 512, 1024, 1536, 2048, 2304, 3328, 3584
SRC_Q, SRC_KV, SRC_GATE = 2208, 3232, 4768
N_KV = 6
LORA_PAD = 256
SRC_RW_END = POOL_WIDTH + RW_COLS

VMEM_LIMIT = 56 * 1024 * 1024


def _t5_bucket_np(dist):
    n = np.maximum(dist, 0)
    nf = np.maximum(n, 1).astype(np.float32)
    large = MAX_EXACT + (np.log(nf / MAX_EXACT) / math.log(MAX_DISTANCE / MAX_EXACT)
                         * (N_BUCKETS - MAX_EXACT)).astype(np.int32)
    large = np.minimum(large, N_BUCKETS - 1)
    return np.where(n < MAX_EXACT, n, large)


def _cp(sem, vmem=None):
    return pltpu.CompilerParams(dimension_semantics=sem, vmem_limit_bytes=vmem)


def _dot(a, b, precision=None):
    return jnp.dot(a, b, preferred_element_type=F32, precision=precision)


def _dot_nt(a, b, precision=None):
    return lax.dot_general(a, b, (((1,), (1,)), ((), ())), preferred_element_type=F32, precision=precision)


def _dot_tn(a, b, precision=None):
    return lax.dot_general(a, b, (((0,), (0,)), ((), ())), preferred_element_type=F32, precision=precision)


def _bdot(a, b):
    return _dot(a.astype(BF16), b.astype(BF16))


def _bdot_nt(a, b):
    return _dot_nt(a.astype(BF16), b.astype(BF16))


def _bdot_tn(a, b):
    return _dot_tn(a.astype(BF16), b.astype(BF16))


def _split(a):
    hi = a.astype(BF16)
    return hi, (a - hi.astype(F32)).astype(BF16)


def _dot3(a, b):
    ah, al = _split(a)
    bh, bl = _split(b)
    return _dot(ah, bh) + (_dot(ah, bl) + _dot(al, bh))


def _rms(x, g):
    return x * lax.rsqrt(jnp.mean(x * x, axis=-1, keepdims=True) + RMS_EPS) * g


def _norm_kernel(x_ref, g_ref, h_ref):
    h_ref[...] = _rms(x_ref[...], g_ref[...]).astype(h_ref.dtype)


def _norm(x, g, out_dtype, tm=512):
    n, d = x.shape
    return pl.pallas_call(
        _norm_kernel,
        out_shape=jax.ShapeDtypeStruct((n, d), out_dtype),
        grid=(n // tm,),
        in_specs=[pl.BlockSpec((tm, d), lambda i: (i, 0)), pl.BlockSpec((1, d), lambda i: (0, 0))],
        out_specs=pl.BlockSpec((tm, d), lambda i: (i, 0)),
        compiler_params=_cp(("parallel",)),
        name="rms_norm",
    )(x, g.reshape(1, d))


def _add_norm_kernel(x_ref, y1_ref, y2_ref, g_ref, xo_ref, h_ref):
    x = x_ref[...] + (y1_ref[...] + y2_ref[...])
    xo_ref[...] = x
    h_ref[...] = _rms(x, g_ref[...]).astype(h_ref.dtype)


def _add_norm(x, y1, y2, g, out_dtype, tm=512):
    n, d = x.shape
    row = pl.BlockSpec((tm, d), lambda i: (i, 0))
    return pl.pallas_call(
        _add_norm_kernel,
        out_shape=(jax.ShapeDtypeStruct((n, d), F32), jax.ShapeDtypeStruct((n, d), out_dtype)),
        grid=(n // tm,),
        in_specs=[row, row, row, pl.BlockSpec((1, d), lambda i: (0, 0))],
        out_specs=(row, row),
        compiler_params=_cp(("parallel",)),
        name="moe_combine_norm",
    )(x, y1, y2, g.reshape(1, d))


def _matmul_kernel(x_ref, w_ref, o_ref):
    o_ref[...] = _dot(x_ref[...], w_ref[...]).astype(o_ref.dtype)


def _matmul(x, w, tm=512, tn=P_COLS // 2):
    m, k = x.shape
    n = w.shape[1]
    return pl.pallas_call(
        _matmul_kernel,
        out_shape=jax.ShapeDtypeStruct((m, n), F32),
        grid=(n // tn, m // tm),
        in_specs=[pl.BlockSpec((tm, k), lambda j, i: (i, 0)), pl.BlockSpec((k, tn), lambda j, i: (0, j))],
        out_specs=pl.BlockSpec((tm, tn), lambda j, i: (i, j)),
        compiler_params=_cp(("parallel", "parallel"), VMEM_LIMIT),
        name="in_proj",
    )(x, w)


def _kv_proj_kernel(x_ref, w_ref, c_ref, ks_ref, vs_ref, kw_ref, vw_ref, stage_ref):
    i = pl.program_id(1)
    tm = x_ref.shape[0]
    head = i == 0
    res = _dot(x_ref[...], w_ref[...])
    col = lax.broadcasted_iota(jnp.int32, (tm, NSA_DH), 1)
    blk = ((i - 1) * tm + lax.broadcasted_iota(jnp.int32, (tm, NSA_DH), 0)) // SLC_BLOCK
    first_col = jnp.where(col == 0, 1.0, 0.0)
    extra = {2: jnp.where(head, 1.0, jnp.where(blk == col, 1.0, 0.0)), 3: first_col,
             4: jnp.where(head, first_col, 0.0), 5: first_col}
    outs = {2: ks_ref, 3: vs_ref, 4: kw_ref, 5: vw_ref}
    for t in range(N_KV):
        for g in range(NSA_G):
            c0 = (t * NSA_G + g) * NSA_DH
            val = res[:, c0:c0 + NSA_DH]
            if t < 2:
                stage_ref[...] = val
                c_ref[t, 0, g] = jnp.concatenate(
                    [stage_ref[pl.ds(j, tm // CMP_STRIDE, stride=CMP_STRIDE), :] for j in range(CMP_STRIDE)],
                    axis=1).astype(c_ref.dtype)
            else:
                feat = jnp.where(head, 0.0, val)
                outs[t][0, g] = jnp.concatenate([feat, extra[t]], axis=1).astype(outs[t].dtype)


def _kv_proj(x, w, batch, seq):
    k = x.shape[1]
    tm = KV_PAD
    nt = seq // tm
    aug = jax.ShapeDtypeStruct((batch, NSA_G, KV_PAD + seq, 2 * NSA_DH), BF16)
    aug_spec = pl.BlockSpec((1, NSA_G, tm, 2 * NSA_DH), lambda b, i: (b, 0, i, 0))
    prev = lambda i: jnp.maximum(i - 1, 0)
    return pl.pallas_call(
        _kv_proj_kernel,
        out_shape=(jax.ShapeDtypeStruct((2, batch, NSA_G, seq // CMP_STRIDE, CMP_STRIDE * NSA_DH), BF16),
                   aug, aug, aug, aug),
        grid=(batch, nt + 1),
        in_specs=[pl.BlockSpec((tm, k), lambda b, i: (b * nt + prev(i), 0)),
                  pl.BlockSpec(w.shape, lambda b, i: (0, 0))],
        out_specs=(pl.BlockSpec((2, 1, NSA_G, tm // CMP_STRIDE, CMP_STRIDE * NSA_DH),
                                lambda b, i: (0, b, 0, prev(i), 0)),
                   aug_spec, aug_spec, aug_spec, aug_spec),
        scratch_shapes=[pltpu.VMEM((tm, NSA_DH), F32)],
        compiler_params=_cp(("parallel", "arbitrary"), VMEM_LIMIT),
        name="kv_proj",
    )(x, w)


def _pool_kernel(u_ref, halo_ref, w_ref, scale_ref, o_ref, buf_ref, *, tile):
    i = pl.program_id(1)
    u = u_ref[...]
    buf_ref[POOL_HALO:, :] = u
    buf_ref[:POOL_HALO, :] = jnp.where(i > 0, halo_ref[...], 0.0)
    t = i * tile + lax.broadcasted_iota(jnp.int32, (tile, 1), 0)
    outs = []
    for gi, win in enumerate(POOL_WINDOWS):
        cols = slice(gi * POOL_GW, (gi + 1) * POOL_GW)
        s = u[:, cols]
        for j in range(1, win):
            s = s + buf_ref[POOL_HALO - j:POOL_HALO - j + tile, cols]
        cnt = jnp.minimum(t + 1, win).astype(F32)
        pooled = s / cnt - u[:, cols]
        outs.append(_dot(pooled.astype(BF16), w_ref[gi]))
    o_ref[...] = (jnp.concatenate(outs, axis=1) * scale_ref[...]).astype(o_ref.dtype)


def _pool(proj, w_grp, scale, batch, seq, tile=512):
    nt = seq // tile
    hb = tile // POOL_HALO
    return pl.pallas_call(
        functools.partial(_pool_kernel, tile=tile),
        out_shape=jax.ShapeDtypeStruct((batch * seq, POOL_WIDTH), BF16),
        grid=(batch, nt),
        in_specs=[
            pl.BlockSpec((tile, POOL_WIDTH), lambda b, i: (b * nt + i, 0)),
            pl.BlockSpec((POOL_HALO, POOL_WIDTH), lambda b, i: (jnp.maximum((b * nt + i) * hb - 1, 0), 0)),
            pl.BlockSpec((len(POOL_WINDOWS), POOL_GW, POOL_GW), lambda b, i: (0, 0, 0)),
            pl.BlockSpec((1, POOL_WIDTH), lambda b, i: (0, 0)),
        ],
        out_specs=pl.BlockSpec((tile, POOL_WIDTH), lambda b, i: (b * nt + i, 0)),
        scratch_shapes=[pltpu.VMEM((tile + POOL_HALO, POOL_WIDTH), F32)],
        compiler_params=_cp(("parallel", "parallel")),
        name="pool_mixer",
    )(proj, proj, w_grp.astype(BF16), scale.reshape(1, POOL_WIDTH))


def _token_shift(u, halo, mu, first):
    prev_row = jnp.where(first, 0.0, halo[7:8, :])
    rolled = pltpu.roll(u, 1, 0)
    row = lax.broadcasted_iota(jnp.int32, u.shape, 0)
    prev = jnp.where(row == 0, prev_row, rolled)
    return u + (prev - u) * mu


def _rwkv_chunk_kernel(r_ref, k_ref, v_ref, l_ref, rh_ref, kh_ref, vh_ref, lh_ref,
                       mur_ref, muk_ref, muv_ref, mul_ref, w0_ref, wup_ref, a0_ref, aup_ref, gup_ref,
                       kk_ref, ka_ref, rk_ref, bd_ref, qy_ref, mn_ref, g_ref, bonus_ref):
    first = pl.program_id(1) == 0
    c = RW_CHUNK
    r = _token_shift(r_ref[...], rh_ref[...], mur_ref[...], first)
    k = _token_shift(k_ref[...], kh_ref[...], muk_ref[...], first)
    v = _token_shift(v_ref[...], vh_ref[...], muv_ref[...], first)
    lo = _token_shift(l_ref[...], lh_ref[...], mul_ref[...], first)
    wd = lo[:, :DECAY_LORA]
    ad = lo[:, DECAY_LORA:DECAY_LORA + AAA_LORA]
    gd = lo[:, DECAY_LORA + AAA_LORA:RW_LORA]
    z = -(w0_ref[...] + _dot(jnp.tanh(wd), wup_ref[...], HI))
    w_log = -(jnp.maximum(z, 0.0) + jnp.log(1.0 + jnp.exp(-jnp.abs(z)))) - 0.5
    logw = -jnp.exp(w_log)
    a = jax.nn.sigmoid(a0_ref[...] + _dot(ad, aup_ref[...], HI))
    g_ref[0] = _dot(jax.nn.sigmoid(gd), gup_ref[...], HI)
    kkraw = k * kk_ref[...]
    k2 = k * (1.0 + (a - 1.0) * ka_ref[...])
    rkr = r * k2 * rk_ref[...]

    cum_all = logw
    trow = lax.broadcasted_iota(jnp.int32, logw.shape, 0) % c
    step = 1
    while step < c:
        cum_all = cum_all + jnp.where(trow >= step, pltpu.roll(cum_all, step, 0), 0.0)
        step *= 2

    ti = lax.broadcasted_iota(jnp.int32, (c, c), 0)
    si = lax.broadcasted_iota(jnp.int32, (c, c), 1)
    incl = ti >= si
    strict = ti > si
    eye = ti == si
    zeros = jnp.zeros((c, c), F32)
    bd = bd_ref[...]

    def head_sum(t):
        hi, lo = _split(t)
        return _dot(hi, bd) + _dot(lo, bd)

    kk = kkraw / jnp.maximum(jnp.sqrt(head_sum(kkraw * kkraw)), 1e-12)
    bonus_ref[0] = head_sum(rkr) * v
    nchunk = logw.shape[0] // c
    ends = [cum_all[(j + 1) * c - 1:(j + 1) * c, :] for j in range(nchunk)]
    cum_end = jnp.concatenate([jnp.broadcast_to(e, (c, e.shape[1])) for e in ends], axis=0)
    ginv = jnp.exp(-cum_all)
    gtail = jnp.exp(cum_end - cum_all)
    gend = jnp.exp(cum_end)
    kka = kk * a
    at = -kk * jnp.exp(cum_all - logw)
    bt = kka * ginv
    kt = k2 * ginv
    rt = r * jnp.exp(cum_all)
    bhat = kka * gtail
    khat = k2 * gtail

    heads = [(slice(j * c, (j + 1) * c), slice(h * RW_DH, (h + 1) * RW_DH))
             for j in range(nchunk) for h in range(RW_HEADS)]
    stack = lambda x, y, s: jnp.concatenate([x[s], y[s]], axis=0).astype(BF16)
    gram = [_dot_nt(stack(at, rt, s), stack(bt, kt, s)) for s in heads]
    a_ab = [jnp.where(strict, g[:c, :c], 0.0) for g in gram]
    a_ak = [jnp.where(strict, g[:c, c:], 0.0) for g in gram]
    incl2 = (lax.broadcasted_iota(jnp.int32, (c, 2 * c), 0)
             >= lax.broadcasted_iota(jnp.int32, (c, 2 * c), 1) % c)
    a_r = [jnp.where(incl2, g[c:, :], 0.0) for g in gram]
    p = a_ab
    tinv = [eye.astype(F32) + x for x in p]
    for _ in range(int(math.log2(c)) - 1):
        p = [_bdot(x, x) for x in p]
        tinv = [t + _bdot(t, x) for t, x in zip(tinv, p)]
    av = [_bdot(x, v[s]) for x, s in zip(a_ak, heads)]
    w12 = [_bdot(t, jnp.concatenate([at[s], x], axis=1)) for t, x, s in zip(tinv, av, heads)]
    zmat = [jnp.concatenate([w, jnp.concatenate([zeros, v[s]], axis=1)], axis=0).astype(BF16)
            for w, s in zip(w12, heads)]
    out1 = [_dot(x.astype(BF16), z) for x, z in zip(a_r, zmat)]
    out2 = [_dot_tn(stack(bhat, khat, s), z) for s, z in zip(heads, zmat)]
    qy = [o + jnp.concatenate([rt[s], zeros], axis=1) for o, s in zip(out1, heads)]
    for j in range(nchunk):
        qy_ref[0, j * c:(j + 1) * c, :] = jnp.concatenate(qy[j * RW_HEADS:(j + 1) * RW_HEADS], axis=1)
        for h in range(RW_HEADS):
            idx = j * RW_HEADS + h
            diag = jnp.where(eye, gend[heads[idx]], 0.0)
            mn_ref[0, j, h] = out2[idx] + jnp.concatenate([diag, zeros], axis=1)


def _rwkv_scan_kernel(qy_ref, mn_ref, g_ref, bonus_ref, gnw_ref, gnb_ref, bd_ref, y_ref, st_ref, *, batch):
    @pl.when(pl.program_id(0) == 0)
    def _():
        st_ref[...] = jnp.zeros_like(st_ref)

    bd = bd_ref[...]

    def head_mean(t):
        hi, lo = _split(t)
        return (_dot(hi, bd) + _dot(lo, bd)) * (1.0 / RW_DH)

    pairs = [(b, h) for b in range(batch) for h in range(RW_HEADS)]
    sts = [st_ref[b * RW_HEADS + h] for b, h in pairs]
    ys = [_bdot(qy_ref[b, :, 2 * RW_DH * h:2 * RW_DH * h + RW_DH], st)
          + qy_ref[b, :, 2 * RW_DH * h + RW_DH:2 * RW_DH * (h + 1)] for (b, h), st in zip(pairs, sts)]
    for (b, h), st in zip(pairs, sts):
        mn = mn_ref[b, 0, h]
        st_ref[b * RW_HEADS + h] = _dot3(mn[:, :RW_DH], st) + mn[:, RW_DH:]
    for b in range(batch):
        y = jnp.concatenate(ys[b * RW_HEADS:(b + 1) * RW_HEADS], axis=1)
        dev = y - head_mean(y)
        yn = dev * lax.rsqrt(head_mean(dev * dev) + RW_GN_EPS) * gnw_ref[...] + gnb_ref[...]
        y_ref[b] = ((yn + bonus_ref[b]) * g_ref[b]).astype(y_ref.dtype)


def _rwkv(proj, p, batch, seq):
    head_ones = jnp.asarray(np.kron(np.eye(RW_HEADS), np.ones((RW_DH, RW_DH))), BF16)
    c = RW_CHUNK
    nc = seq // c
    tb = RW_NB * c
    nt = seq // tb
    hb = tb // 8
    row512 = lambda col: pl.BlockSpec((tb, RW_WIDTH), lambda b, i: (b * nt + i, col))
    halo512 = lambda col: pl.BlockSpec((8, RW_WIDTH), lambda b, i: (jnp.maximum((b * nt + i) * hb - 1, 0), col))
    const = lambda shape: pl.BlockSpec(shape, lambda b, i: (0,) * len(shape))
    vec = const((1, RW_WIDTH))
    out_row = lambda w: pl.BlockSpec((1, tb, w), lambda b, i: (b, i, 0))
    qy, mn, g, bonus = pl.pallas_call(
        _rwkv_chunk_kernel,
        out_shape=(jax.ShapeDtypeStruct((batch, seq, 2 * RW_WIDTH), F32),
                   jax.ShapeDtypeStruct((batch, nc, RW_HEADS, RW_DH, 2 * RW_DH), F32),
                   jax.ShapeDtypeStruct((batch, seq, RW_WIDTH), F32),
                   jax.ShapeDtypeStruct((batch, seq, RW_WIDTH), F32)),
        grid=(batch, nt),
        in_specs=[row512(C_R // RW_WIDTH), row512(C_K // RW_WIDTH), row512(C_V // RW_WIDTH),
                  pl.BlockSpec((tb, LORA_PAD), lambda b, i: (b * nt + i, C_LORA // LORA_PAD)),
                  halo512(C_R // RW_WIDTH), halo512(C_K // RW_WIDTH), halo512(C_V // RW_WIDTH),
                  pl.BlockSpec((8, LORA_PAD), lambda b, i: (jnp.maximum((b * nt + i) * hb - 1, 0), C_LORA // LORA_PAD)),
                  vec, vec, vec, const((1, LORA_PAD)),
                  vec, const((DECAY_LORA, RW_WIDTH)), vec, const((AAA_LORA, RW_WIDTH)), const((GATE_LORA, RW_WIDTH)),
                  vec, vec, vec, const((RW_WIDTH, RW_WIDTH))],
        out_specs=(out_row(2 * RW_WIDTH),
                   pl.BlockSpec((1, RW_NB, RW_HEADS, RW_DH, 2 * RW_DH), lambda b, i: (b, i, 0, 0, 0)),
                   out_row(RW_WIDTH), out_row(RW_WIDTH)),
        compiler_params=_cp(("parallel", "parallel"), VMEM_LIMIT),
        name="rwkv_chunk",
    )(proj, proj, proj, proj, proj, proj, proj, proj,
      p["mu_r"], p["mu_k"], p["mu_v"], p["mu_l"], p["w0"], p["w_up"], p["a0"], p["a_up"], p["g_up"],
      p["k_k"], p["k_a"], p["r_k"], head_ones)

    full = lambda w: pl.BlockSpec((batch, c, w), lambda i: (0, i, 0))
    return pl.pallas_call(
        functools.partial(_rwkv_scan_kernel, batch=batch),
        out_shape=jax.ShapeDtypeStruct((batch, seq, RW_WIDTH), BF16),
        grid=(nc,),
        in_specs=[full(2 * RW_WIDTH),
                  pl.BlockSpec((batch, 1, RW_HEADS, RW_DH, 2 * RW_DH), lambda i: (0, i, 0, 0, 0)),
                  full(RW_WIDTH), full(RW_WIDTH),
                  pl.BlockSpec((1, RW_WIDTH), lambda i: (0, 0)), pl.BlockSpec((1, RW_WIDTH), lambda i: (0, 0)),
                  pl.BlockSpec((RW_WIDTH, RW_WIDTH), lambda i: (0, 0))],
        out_specs=full(RW_WIDTH),
        scratch_shapes=[pltpu.VMEM((batch * RW_HEADS, RW_DH, RW_DH), F32)],
        compiler_params=_cp(("arbitrary",), VMEM_LIMIT),
        name="rwkv_scan",
    )(qy, mn, g, bonus, p["gn_w"], p["gn_b"], head_ones).reshape(batch * seq, RW_WIDTH)


def _gelu_tanh(x):
    return 0.5 * x * (1.0 + jnp.tanh(math.sqrt(2.0 / math.pi) * (x + 0.044715 * (x * x * x))))


def _compress_kernel(x_ref, pos_ref, w1_ref, w2_ref, o_ref):
    half = CMP_STRIDE * NSA_DH
    x = x_ref[0, 0, 0]
    w1 = w1_ref[0]
    posb = _dot(pos_ref[0], w1)[0:1, :]
    h1 = _dot(x, w1[:half])
    h2 = _dot(x, w1[half:])
    n = h2.shape[0]
    row = lax.broadcasted_iota(jnp.int32, h2.shape, 0)
    h2s = jnp.where(row < n - 1, pltpu.roll(h2, n - 1, 0), 0.0)
    hid = _gelu_tanh(h1 + h2s + posb)
    o_ref[0, 0, 0] = _dot(hid.astype(BF16), w2_ref[0]).astype(o_ref.dtype)


def _compress(xkv, pos, w1, w2, batch, seq):
    nr = seq // CMP_STRIDE
    wide = CMP_STRIDE * NSA_DH
    return pl.pallas_call(
        _compress_kernel,
        out_shape=jax.ShapeDtypeStruct((2, batch, NSA_G, nr, NSA_DH), BF16),
        grid=(2, batch, NSA_G),
        in_specs=[pl.BlockSpec((1, 1, 1, nr, wide), lambda t, b, g: (t, b, g, 0, 0)),
                  pl.BlockSpec((1, 8, 2 * wide), lambda t, b, g: (t, 0, 0)),
                  pl.BlockSpec((1, 2 * wide, CMP_HIDDEN), lambda t, b, g: (t, 0, 0)),
                  pl.BlockSpec((1, CMP_HIDDEN, NSA_DH), lambda t, b, g: (t, 0, 0))],
        out_specs=pl.BlockSpec((1, 1, 1, nr, NSA_DH), lambda t, b, g: (t, b, g, 0, 0)),
        compiler_params=_cp(("parallel", "parallel", "parallel")),
        name="nsa_compress",
    )(xkv, pos, w1, w2)


def _softmax_parts(s):
    m = jnp.max(s, axis=1, keepdims=True)
    e = jnp.exp(s - m)
    return m, e, jnp.sum(e, axis=1, keepdims=True)


def _lane_tile_fold(x, op, init):
    for t in range(x.shape[1] // 128):
        init = op(init, x[:, 128 * t:128 * (t + 1)])
    return init


def _nsa_kernel(q_ref, gate_ref, kc_ref, vc_ref, ks_ref, vs_ref, kw_ref, vw_ref,
                tblc_ref, tbln_ref, tblw_ref, ovt_ref, gexp_ref, o_ref, s_ref, *, ncmp):
    nb = NSA_NB
    blocks = [pl.program_id(2) * nb + u for u in range(nb)]
    rows = NSA_HG * QB
    each = lambda f, *ls: [f(*xs) for xs in zip(*ls)]
    qt = q_ref[...]
    qs = [jnp.concatenate([qt[u * QB:(u + 1) * QB, NSA_DH * h:NSA_DH * (h + 1)] for h in range(NSA_HG)], axis=0)
          * (NSA_DH ** -0.5 * LOG2E) for u in range(nb)]
    qb = each(lambda x: x.astype(BF16), qs)
    rmax = lambda x: jnp.max(x, axis=1, keepdims=True)
    rsum = lambda x: jnp.sum(x, axis=1, keepdims=True)

    win_w = WINDOW + QB
    kw = [kw_ref[0, 0, pl.ds(pl.multiple_of(i * QB, QB), win_w), :] for i in blocks]
    vw = [vw_ref[0, 0, pl.ds(pl.multiple_of(i * QB, QB), win_w), :] for i in blocks]
    flag = lax.broadcasted_iota(jnp.int32, (rows, kw_ref.shape[3] - NSA_DH), 1) == 0
    q_win = each(lambda x: jnp.concatenate([x, jnp.where(flag, NEG, 0.0)], axis=1).astype(BF16), qs)
    tblw = tblw_ref[0]
    pv = lambda x: x[:, :NSA_DH] / x[:, NSA_DH:NSA_DH + 1]
    s_w = each(lambda x, k: _dot_nt(x, k) + tblw, q_win, kw)
    m_w = each(rmax, s_w)
    e_w = each(lambda s, m: jnp.exp2(s - m), s_w, m_w)
    o_w = each(lambda e, v: pv(_dot(e.astype(BF16), v)), e_w, vw)

    kc = kc_ref[0, 0]
    vc = vc_ref[0, 0]
    tblc = tblc_ref[0]
    cidx = lax.broadcasted_iota(jnp.int32, (rows, ncmp), 1)
    qrow = lax.broadcasted_iota(jnp.int32, (rows, 1), 0) % QB
    lc = [jnp.where(cidx < (QB // CMP_STRIDE) * i + CMP_AHEAD,
                    _dot_nt(x, kc) + pltpu.roll(tblc, (4 * i - CMP_NEAR // 2 + ncmp) % ncmp, 1), NEG)
          for x, i in zip(qb, blocks)]
    m_c = each(rmax, lc)
    e_c = each(lambda s, m: jnp.exp2(s - m), lc, m_c)
    den_c = each(rsum, e_c)
    pc = [e * jnp.where(i * QB + qrow >= CMP_BLOCK - 1, 1.0 / d, 0.0) for e, d, i in zip(e_c, den_c, blocks)]
    o_c = each(lambda x: _dot(x.astype(BF16), vc), pc)
    pcs_hi, pcs_lo = _split(jnp.concatenate(
        each(lambda x: x[0:QB] + x[QB:2 * QB] + x[2 * QB:3 * QB] + x[3 * QB:4 * QB], pc), axis=0))
    ovt = ovt_ref[...]
    imp = _dot_nt(ovt, pcs_hi) + _dot_nt(ovt, pcs_lo)

    nslc = ovt.shape[0]
    nidx = lax.broadcasted_iota(jnp.int32, (nslc, nb * QB), 0)
    cur = blocks[0] + lax.broadcasted_iota(jnp.int32, (nslc, nb * QB), 1) // QB
    forced = (nidx == 0) | (nidx == cur) | (nidx == cur - 1)
    work = jnp.where(nidx <= cur, imp + jnp.where(forced, FORCE_BONUS, 0.0), -1.0)
    sel_all = jnp.zeros((nslc, nb * QB), F32)
    for _ in range(N_SELECT):
        m = jnp.max(work, axis=0, keepdims=True)
        first = jnp.min(jnp.where(work == m, nidx, nslc), axis=0, keepdims=True)
        pick = nidx == first
        sel_all = jnp.where(pick & (m >= 0.0), 1.0, sel_all)
        work = jnp.where(pick, -2.0, work)
    sel_t = sel_all.T

    ind_w = ks_ref.shape[3] - NSA_DH
    selq = [sel_t[u * QB:(u + 1) * QB] for u in range(nb)]
    if nslc < ind_w:
        selq = each(lambda x: jnp.concatenate([x, jnp.zeros((QB, ind_w - nslc), F32)], axis=1), selq)
    bidx = lax.broadcasted_iota(jnp.int32, (QB, ind_w), 1)
    tile4 = lambda t: jnp.concatenate([t] * NSA_HG, axis=0)
    with_mask = lambda x, keep: jnp.concatenate([x, tile4(jnp.where(keep, 0.0, NEG))], axis=1).astype(BF16)
    q_sel = each(lambda x, s: with_mask(x, s > 0.0), qs, selq)
    q_far = jnp.concatenate([with_mask(x, (s > 0.0) & (bidx <= i - NEAR_BLOCKS))
                             for x, s, i in zip(qs, selq, blocks)], axis=0)
    near_w = NEAR_BLOCKS * SLC_BLOCK
    pad_s = KV_PAD
    near0 = pad_s - (NEAR_BLOCKS - 1) * SLC_BLOCK
    kn = [ks_ref[0, 0, pl.ds(pl.multiple_of(near0 + i * QB, QB), near_w), :] for i in blocks]
    vn = [vs_ref[0, 0, pl.ds(pl.multiple_of(near0 + i * QB, QB), near_w), :] for i in blocks]
    tbln = tbln_ref[0]
    s_near = each(lambda x, k: _dot_nt(x, k) + tbln, q_sel, kn)
    m_near = jnp.concatenate(each(rmax, s_near), axis=0)

    far_w = FAR_CHUNK_BLOCKS * SLC_BLOCK
    n_far =jnp.maximum(blocks[-1] - (NEAR_BLOCKS - 1) + FAR_CHUNK_BLOCKS - 1, 0) // FAR_CHUNK_BLOCKS

    def far_logits(js, mvec):
        ss = [_dot_nt(q_far, ks_ref[0, 0, pl.ds(pl.multiple_of(pad_s + j * far_w, SLC_BLOCK), far_w), :])
              for j in js]
        for j, s in zip(js, ss):
            s_ref[:, pl.ds(pl.multiple_of(j * far_w, far_w), far_w)] = s
        for s in ss:
            mvec = _lane_tile_fold(s, jnp.maximum, mvec)
        return mvec

    n_pair = n_far // 2
    odd = n_far % 2 == 1
    mvec = lax.fori_loop(0, n_pair, lambda jp, m: far_logits([2 * jp, 2 * jp + 1], m),
                         jnp.full((nb * rows, 128), NEG, F32))
    mvec = lax.cond(odd, lambda m: far_logits([n_far - 1], m), lambda m: m, mvec)
    m_s = jnp.maximum(m_near, rmax(mvec))

    e_near = [jnp.exp2(s - m_s[u * rows:(u + 1) * rows]) for u, s in enumerate(s_near)]
    acc0 = jnp.concatenate(each(lambda e, v: _dot(e.astype(BF16), v), e_near, vn), axis=0)

    def far_values(js, acc):
        es = [jnp.exp2(s_ref[:, pl.ds(pl.multiple_of(j * far_w, far_w), far_w)] - m_s).astype(BF16) for j in js]
        for j, e in zip(js, es):
            acc = acc + _dot(e, vs_ref[0, 0, pl.ds(pl.multiple_of(pad_s + j * far_w, SLC_BLOCK), far_w), :])
        return acc

    acc = lax.fori_loop(0, n_pair, lambda jp, a: far_values([2 * jp, 2 * jp + 1], a), acc0)
    o_s = pv(lax.cond(odd, lambda a: far_values([n_far - 1], a), lambda a: a, acc))

    g_hi, g_lo = _split(jax.nn.sigmoid(gate_ref[0, 0]))
    gexp = gexp_ref[...]
    ge = _dot(g_hi, gexp) + _dot(g_lo, gexp)
    for u in range(nb):
        outs = []
        for h in range(NSA_HG):
            r0 = slice(h * QB, (h + 1) * QB)
            gt = lambda br: ge[u * QB:(u + 1) * QB, (3 * h + br) * 128:(3 * h + br) * 128 + NSA_DH]
            outs.append(gt(0) * o_c[u][r0] + gt(1) * o_s[u * rows + h * QB:u * rows + (h + 1) * QB]
                        + gt(2) * o_w[u][r0])
        o_ref[u * QB:(u + 1) * QB, :] = jnp.concatenate(outs, axis=1).astype(o_ref.dtype)


def _bias_tables(rel_bias, ncmp):
    def table(dist, keep, base):
        onehot = jnp.asarray(np.eye(N_BUCKETS, dtype=np.float32)[_t5_bucket_np(dist)])
        tbl = jnp.einsum("qkb,bh->qkh", onehot, rel_bias, precision=HI)
        tbl = (tbl - base) * LOG2E
        tbl = jnp.where(jnp.asarray(keep)[..., None], tbl, NEG)
        k = dist.shape[1]
        return tbl.transpose(2, 0, 1).reshape(NSA_G, NSA_HG * QB, k)

    far = rel_bias[N_BUCKETS - 1]
    qi = np.arange(QB)[:, None]
    dist_c = qi - CMP_STRIDE * (np.arange(CMP_NEAR)[None, :] - CMP_NEAR // 2) - (CMP_BLOCK - 1)
    tblc = table(dist_c, dist_c >= 0, far)
    tblc = tblc * jnp.asarray(np.arange(CMP_NEAR) < CMP_NEAR // 2 + CMP_AHEAD, F32)
    tblc = jnp.pad(tblc, ((0, 0), (0, 0), (0, ncmp - CMP_NEAR)))
    jn = np.arange(NEAR_BLOCKS * SLC_BLOCK)[None, :]
    dist_n = (NEAR_BLOCKS - 1) * SLC_BLOCK + qi - jn
    tbln = table(dist_n, dist_n >= 0, far)
    jw = np.arange(WINDOW + QB)[None, :]
    dist_w = WINDOW + qi - jw
    tblw = table(dist_w, (dist_w >= 0) & (dist_w < WINDOW), 0.0)
    return tblc, tbln, tblw


def _nsa(proj, gate_logits, kvc, ks, vs, kw, vw, tables, batch, seq):
    nq = seq // QB
    ncmp = seq // CMP_STRIDE
    nslc = seq // SLC_BLOCK
    tblc, tbln, tblw = tables
    cstart = np.arange(ncmp) * CMP_STRIDE
    sstart = np.arange(nslc) * SLC_BLOCK
    overlap_t = ((cstart[None, :] <= sstart[:, None] + SLC_BLOCK - 1)
                 & (cstart[None, :] + CMP_BLOCK - 1 >= sstart[:, None])
                 & (cstart[None, :] + CMP_BLOCK <= seq)).astype(np.float32)
    rows = NSA_HG * QB
    n_gate = 3 * NSA_HG
    gate_spread = (np.arange(n_gate * 128)[None, :] // 128 == np.arange(n_gate)[:, None]) \
        & (np.arange(n_gate * 128)[None, :] % 128 < NSA_DH)
    kv_spec = lambda t: pl.BlockSpec((1, 1) + t.shape[2:], lambda b, g, i: (b, g, 0, 0))
    tbl_spec = lambda k: pl.BlockSpec((1, rows, k), lambda b, g, i: (g, 0, 0))
    qcol = C_Q // (NSA_HG * NSA_DH)
    return pl.pallas_call(
        functools.partial(_nsa_kernel, ncmp=ncmp),
        out_shape=jax.ShapeDtypeStruct((batch * seq, NSA_HEADS * NSA_DH), BF16),
        grid=(batch, NSA_G, nq // NSA_NB),
        in_specs=[pl.BlockSpec((NSA_NB * QB, NSA_HG * NSA_DH), lambda b, g, i: (b * (nq // NSA_NB) + i, qcol + g)),
                  pl.BlockSpec((1, 1, NSA_NB * QB, 3 * NSA_HG), lambda b, g, i: (b, g, i, 0)),
                  pl.BlockSpec((1, 1, ncmp, NSA_DH), lambda b, g, i: (b, g, 0, 0)),
                  pl.BlockSpec((1, 1, ncmp, NSA_DH), lambda b, g, i: (b, g, 0, 0)),
                  kv_spec(ks), kv_spec(vs), kv_spec(kw), kv_spec(vw),
                  tbl_spec(ncmp), tbl_spec(NEAR_BLOCKS * SLC_BLOCK), tbl_spec(WINDOW + QB),
                  pl.BlockSpec((nslc, ncmp), lambda b, g, i: (0, 0)),
                  pl.BlockSpec(gate_spread.shape, lambda b, g, i: (0, 0))],
        out_specs=pl.BlockSpec((NSA_NB * QB, NSA_HG * NSA_DH), lambda b, g, i: (b * (nq // NSA_NB) + i, g)),
        scratch_shapes=[pltpu.VMEM((NSA_NB * rows, seq), F32)],
        compiler_params=_cp(("parallel", "parallel", "arbitrary"), VMEM_LIMIT),
        name="nsa_attention",
    )(proj, gate_logits, kvc[0], kvc[1], ks, vs, kw, vw, tblc, tbln, tblw, jnp.asarray(overlap_t, BF16),
      jnp.asarray(gate_spread, BF16))


def _merge_kernel(h_ref, yp_ref, yr_ref, yn_ref, x_ref, wbp_ref, wbr_ref, wbn_ref, wm_ref, bm_ref, wo_ref,
                  gn_ref, wr_ref, br_ref, xo_ref, h2_ref, route_ref):
    d = D_MODEL
    gl = jax.nn.sigmoid(_dot(h_ref[...], wm_ref[...]) + bm_ref[...])
    merged = (gl[:, :d] * _dot(yp_ref[...], wbp_ref[...]) + gl[:, d:2 * d] * _dot(yr_ref[...], wbr_ref[...])
              + gl[:, 2 * d:] * _dot(yn_ref[...], wbn_ref[...]))
    x = x_ref[...] + _dot(merged.astype(BF16), wo_ref[...])
    xo_ref[...] = x
    h2 = _rms(x, gn_ref[...])
    h2_ref[...] = h2.astype(h2_ref.dtype)

    h2_hi, h2_lo = _split(h2)
    nl = br_ref.shape[1]
    both = _dot(h2_hi, wr_ref[...])
    logits = both[:, :nl] + (both[:, nl:] + _dot(h2_lo, wr_ref[:, :nl])) + br_ref[...]
    lane = lax.broadcasted_iota(jnp.int32, logits.shape, 1)
    big = logits.shape[1]
    lg = jnp.where(lane < N_GROUPS, logits, NEG)
    mg = jnp.max(lg, axis=1, keepdims=True)
    p_top = 1.0 / jnp.sum(jnp.exp(lg - mg), axis=1, keepdims=True)
    grp = jnp.min(jnp.where(lg == mg, lane, big), axis=1, keepdims=True)
    lo = N_GROUPS + EPG * grp
    le = jnp.where((lane >= lo) & (lane < lo + EPG), logits, NEG)
    e1 = jnp.max(le, axis=1, keepdims=True)
    i1 = jnp.min(jnp.where(le == e1, lane, big), axis=1, keepdims=True)
    le = jnp.where(lane == i1, NEG, le)
    e2 = jnp.max(le, axis=1, keepdims=True)
    i2 = jnp.min(jnp.where(le == e2, lane, big), axis=1, keepdims=True)
    t = jnp.exp(e2 - e1)
    w1 = p_top / (1.0 + t)
    w2 = p_top * t / (1.0 + t)
    route_ref[...] = jnp.where(lane == 0, (i1 - N_GROUPS).astype(F32),
                               jnp.where(lane == 1, (i2 - N_GROUPS).astype(F32),
                                         jnp.where(lane == 2, w1, jnp.where(lane == 3, w2, 0.0))))


def _merge(h, y_pool, y_rwkv, y_nsa, x, p, tm=256):
    n, d = x.shape
    row = lambda w: pl.BlockSpec((tm, w), lambda i: (i, 0))
    const = lambda a: pl.BlockSpec(a.shape, lambda i: (0, 0))
    ws = [p["wb_pool"], p["wb_rwkv"], p["wb_nsa"], p["w_merge"], p["b_merge"], p["w_out"],
          p["norm_ffn"], p["w_router"], p["b_router"]]
    return pl.pallas_call(
        _merge_kernel,
        out_shape=(jax.ShapeDtypeStruct((n, d), F32), jax.ShapeDtypeStruct((n, d), F32),
                   jax.ShapeDtypeStruct((n, 128), F32)),
        grid=(n // tm,),
        in_specs=[row(d), row(POOL_WIDTH), row(RW_WIDTH), row(d), row(d)] + [const(w) for w in ws],
        out_specs=(row(d), row(d), row(128)),
        compiler_params=_cp(("parallel",), VMEM_LIMIT),
        name="merge_router",
    )(h, y_pool, y_rwkv, y_nsa, x, *ws)


def _expert_kernel(te_ref, nt_ref, tok_ref, h_hbm, w_ref, wg_ref, wu_ref, wd_ref, o_ref,
                   xbuf, sem, wg_s, wu_s, wd_s):
    i = pl.program_id(0)
    tm = MOE_TM
    n_tiles = nt_ref[0]

    def row_copy(tile, slot, r):
        return pltpu.make_async_copy(h_hbm.at[pl.ds(tok_ref[tile * tm + r], 1), :],
                                     xbuf.at[slot, pl.ds(r, 1), :], sem.at[slot])

    def tile_wait(slot):
        pltpu.make_async_copy(h_hbm.at[pl.ds(0, tm), :], xbuf.at[slot], sem.at[slot]).wait()

    @pl.when((i == 0) & (n_tiles > 0))
    def _():
        def body(r, carry):
            row_copy(0, 0, r).start()
            return carry
        lax.fori_loop(0, tm, body, 0, unroll=8)

    @pl.when((i == 0) | (te_ref[i] != te_ref[jnp.maximum(i - 1, 0)]))
    def _():
        wg_s[...] = wg_ref[0, 0].astype(BF16)
        wu_s[...] = wu_ref[0, 0].astype(BF16)
        wd_s[...] = wd_ref[0, 0].astype(BF16)

    @pl.when(i < n_tiles)
    def _():
        slot = i % 2
        quarter = tm // 4

        def fetch_next(part):
            for r in range(part * quarter, (part + 1) * quarter):
                row_copy(i + 1, 1 - slot, r).start()

        tile_wait(slot)
        xb = xbuf[slot].astype(BF16)
        fetch_next(0)
        gate = _dot(xb, wg_s[...])
        fetch_next(1)
        up = _dot(xb, wu_s[...])
        fetch_next(2)
        hid = (gate * jax.nn.sigmoid(gate) * up).astype(BF16)
        fetch_next(3)
        o_ref[...] = w_ref[...] * _dot(hid, wd_s[...])

    @pl.when((i == n_tiles) & (n_tiles > 0))
    def _():
        tile_wait(i % 2)

    @pl.when(i >= nt_ref[0])
    def _():
        o_ref[...] = jnp.zeros_like(o_ref)


def _experts(h2, rowtok, roww, tile_expert, n_tiles, layer, wg, wu, wd):
    d = h2.shape[1]
    r = rowtok.shape[0]
    tm = MOE_TM
    return pl.pallas_call(
        _expert_kernel,
        out_shape=jax.ShapeDtypeStruct((r, d), F32),
        grid_spec=pltpu.PrefetchScalarGridSpec(
            num_scalar_prefetch=3,
            grid=(r // tm,),
            in_specs=[pl.BlockSpec(memory_space=pl.ANY),
                      pl.BlockSpec((tm, 1), lambda i, te, nt, tok: (i, 0)),
                      pl.BlockSpec((1, 1, d, D_EXPERT), lambda i, te, nt, tok: (layer, te[i], 0, 0)),
                      pl.BlockSpec((1, 1, d, D_EXPERT), lambda i, te, nt, tok: (layer, te[i], 0, 0)),
                      pl.BlockSpec((1, 1, D_EXPERT, d), lambda i, te, nt, tok: (layer, te[i], 0, 0))],
            out_specs=pl.BlockSpec((tm, d), lambda i, te, nt, tok: (i, 0)),
            scratch_shapes=[pltpu.VMEM((2, tm, d), F32), pltpu.SemaphoreType.DMA((2,)),
                            pltpu.VMEM((d, D_EXPERT), BF16), pltpu.VMEM((d, D_EXPERT), BF16),
                            pltpu.VMEM((D_EXPERT, d), BF16)]),
        compiler_params=_cp(("arbitrary",), VMEM_LIMIT),
        name="moe_experts",
    )(tile_expert, n_tiles, rowtok, h2, roww, wg, wu, wd)


def _moe(h2, route, layer, wg, wu, wd):
    n = h2.shape[0]
    tm = MOE_TM
    r = 2 * n + (N_EXPERTS + 1) * tm
    ids = route[:, 0:2].astype(jnp.int32).reshape(-1)
    wts = route[:, 2:4].reshape(-1)
    onehot = (ids[:, None] == jnp.arange(N_EXPERTS)[None, :]).astype(jnp.int32)
    rank = jnp.sum((jnp.cumsum(onehot, axis=0) - onehot) * onehot, axis=1)
    counts = jnp.sum(onehot, axis=0)
    tiles = (counts + tm - 1) // tm
    tile_end = jnp.cumsum(tiles)
    starts = (tile_end - tiles) * tm
    pos = starts[ids] + rank
    row_assign = jnp.full((r,), -1, jnp.int32).at[pos].set(
        jnp.arange(2 * n, dtype=jnp.int32), unique_indices=True, mode="promise_in_bounds")
    rowtok = jnp.maximum(row_assign, 0) // 2
    roww = jnp.where(row_assign >= 0, wts[jnp.maximum(row_assign, 0)], 0.0)
    n_tiles = tile_end[-1:].astype(jnp.int32)
    tile_expert = jnp.minimum(jnp.sum(tile_end[None, :] <= jnp.arange(r // tm)[:, None], axis=1),
                              N_EXPERTS - 1).astype(jnp.int32)
    ys = _experts(h2, rowtok, roww.reshape(r, 1), tile_expert, n_tiles, layer, wg, wu, wd)
    return ys[pos[0::2]], ys[pos[1::2]]


def _layer_params(l, a):
    f = lambda t: t[l]
    row = lambda t: t[l].reshape(1, -1)
    w_in = a["w_in"][l]
    w_in_p = jnp.concatenate([w_in[:, :SRC_RW_END], jnp.zeros((D_MODEL, C_Q - SRC_RW_END), F32),
                              w_in[:, SRC_Q:SRC_KV], w_in[:, SRC_GATE:],
                              jnp.zeros((D_MODEL, P_COLS - C_GATE - (w_in.shape[1] - SRC_GATE)), F32)], axis=1)
    mu = a["rw_mu"][l]
    wb = a["w_branch"][l].astype(BF16)
    w_router = jnp.zeros((D_MODEL, 128), F32)
    w_router = w_router.at[:, :N_GROUPS].set(a["w_router_grp"][l]).at[:, N_GROUPS:N_GROUPS + N_EXPERTS].set(
        a["w_router_exp"][l])
    b_router = jnp.zeros((1, 128), F32)
    b_router = b_router.at[0, :N_GROUPS].set(a["b_router_grp"][l]).at[0, N_GROUPS:N_GROUPS + N_EXPERTS].set(
        a["b_router_exp"][l])
    pos = jnp.stack([a["cmp_pos_k"][l].reshape(-1), a["cmp_pos_v"][l].reshape(-1)])
    return {
        "w_in": w_in_p.astype(BF16), "w_kv": w_in[:, SRC_KV:SRC_GATE].astype(BF16),
        "pool_w": f(a["pool_w"]), "pool_scale": f(a["pool_scale"]),
        "mu_r": mu[None, 0:RW_WIDTH], "mu_k": mu[None, RW_WIDTH:2 * RW_WIDTH],
        "mu_v": mu[None, 2 * RW_WIDTH:3 * RW_WIDTH],
        "mu_l": jnp.concatenate([mu[3 * RW_WIDTH:], jnp.zeros((LORA_PAD - RW_LORA,), F32)])[None],
        "w0": row(a["rw_w0"]), "w_up": f(a["rw_w_up"]), "a0": row(a["rw_a0"]), "a_up": f(a["rw_a_up"]),
        "g_up": f(a["rw_g_up"]), "k_k": row(a["rw_k_k"]), "k_a": row(a["rw_k_a"]), "r_k": row(a["rw_r_k"]),
        "gn_w": row(a["rw_gn_w"]), "gn_b": row(a["rw_gn_b"]),
        "cmp_pos": jnp.broadcast_to(pos[:, None, :], (2, 8, pos.shape[1])).astype(BF16),
        "cmp_w1": jnp.stack([a["cmp_w1_k"][l], a["cmp_w1_v"][l]]).astype(BF16),
        "cmp_w2": jnp.stack([a["cmp_w2_k"][l], a["cmp_w2_v"][l]]).astype(BF16),
        "wb_pool": wb[:POOL_WIDTH], "wb_rwkv": wb[POOL_WIDTH:POOL_WIDTH + RW_WIDTH],
        "wb_nsa": wb[POOL_WIDTH + RW_WIDTH:],
        "w_merge": a["w_merge"][l].astype(BF16), "b_merge": row(a["b_merge"]),
        "w_out": a["w_out"][l].astype(BF16), "norm_ffn": row(a["norm_ffn"]),
        "w_router": jnp.concatenate(_split(w_router), axis=1), "b_router": b_router,
    }


def _mixers(proj, kv, p, tables, batch, seq):
    kv_cmp, ks, vs, kw, vw = kv
    y_pool = _pool(proj, p["pool_w"], p["pool_scale"], batch, seq)
    y_rwkv = _rwkv(proj, p, batch, seq)
    kvc = _compress(kv_cmp, p["cmp_pos"], p["cmp_w1"], p["cmp_w2"], batch, seq)
    gate_logits = proj[:, C_GATE:C_GATE + 3 * NSA_HEADS].reshape(batch, seq, NSA_G, 3 * NSA_HG).transpose(0, 2, 1, 3)
    y_nsa = _nsa(proj, gate_logits, kvc, ks, vs, kw, vw, tables, batch, seq)
    return y_pool, y_rwkv, y_nsa


def kernel(x, rel_bias, norm_mix, w_in, pool_w, pool_scale, rw_mu, rw_w0, rw_w_up, rw_a0, rw_a_up, rw_g_up, rw_k_k, rw_k_a, rw_r_k, rw_gn_w, rw_gn_b, cmp_pos_k, cmp_w1_k, cmp_w2_k, cmp_pos_v, cmp_w1_v, cmp_w2_v, w_branch, w_merge, b_merge, w_out, norm_ffn, w_router_grp, b_router_grp, w_router_exp, b_router_exp, w_exp_gate, w_exp_up, w_exp_down, norm_final):
    a = dict(w_in=w_in, pool_w=pool_w, pool_scale=pool_scale, rw_mu=rw_mu, rw_w0=rw_w0, rw_w_up=rw_w_up,
             rw_a0=rw_a0, rw_a_up=rw_a_up, rw_g_up=rw_g_up, rw_k_k=rw_k_k, rw_k_a=rw_k_a, rw_r_k=rw_r_k,
             rw_gn_w=rw_gn_w, rw_gn_b=rw_gn_b, cmp_pos_k=cmp_pos_k, cmp_w1_k=cmp_w1_k, cmp_w2_k=cmp_w2_k,
             cmp_pos_v=cmp_pos_v, cmp_w1_v=cmp_w1_v, cmp_w2_v=cmp_w2_v, w_branch=w_branch, w_merge=w_merge,
             b_merge=b_merge, w_out=w_out, norm_ffn=norm_ffn, w_router_grp=w_router_grp,
             b_router_grp=b_router_grp, w_router_exp=w_router_exp, b_router_exp=b_router_exp,
             w_exp_gate=w_exp_gate, w_exp_up=w_exp_up, w_exp_down=w_exp_down)
    batch, seq, d = x.shape
    depth = norm_mix.shape[0]
    tables = _bias_tables(rel_bias, seq // CMP_STRIDE)
    xf = x.reshape(batch * seq, d)
    h = _norm(xf, norm_mix[0], BF16)
    for l in range(depth):
        p = _layer_params(l, a)
        proj = _matmul(h, p["w_in"])
        kv = _kv_proj(h, p["w_kv"], batch, seq)
        y_pool, y_rwkv, y_nsa = _mixers(proj, kv, p, tables, batch, seq)
        xf, h2, route = _merge(h, y_pool, y_rwkv, y_nsa, xf, p)
        y1, y2 = _moe(h2, route, l, w_exp_gate, w_exp_up, w_exp_down)
        last = l == depth - 1
        g_next = norm_final if last else norm_mix[l + 1]
        xf, h = _add_norm(xf, y1, y2, g_next, F32 if last else BF16)
    return h.reshape(batch, seq, d)
```

```python
import functools
import math

import jax
import jax.numpy as jnp
import numpy as np
from jax import lax
from jax.experimental import pallas as pl
from jax.experimental.pallas import tpu as pltpu

F32 = jnp.float32
BF16 = jnp.bfloat16
HI = lax.Precision.HIGHEST

D_MODEL = 1024
RMS_EPS = 1e-6
NEG = -1e30
LOG2E = math.log2(math.e)

POOL_WINDOWS = (2, 4, 8, 16)
POOL_WIDTH = 512
POOL_GW = 128
POOL_HALO = 16

RW_HEADS = 8
RW_DH = 64
RW_WIDTH = 512
DECAY_LORA, AAA_LORA, GATE_LORA = 32, 32, 96
RW_LORA = DECAY_LORA + AAA_LORA + GATE_LORA
RW_COLS = 3 * RW_WIDTH + RW_LORA
RW_GN_EPS = 64e-5
RW_CHUNK = 64
RW_NB = 4

NSA_DH = 64
NSA_HEADS = 16
NSA_G = 4
NSA_HG = 4
CMP_BLOCK, CMP_STRIDE, CMP_HIDDEN = 32, 16, 256
SLC_BLOCK = 64
N_SELECT = 8
WINDOW = 512
KV_PAD = WINDOW
QB = 64
NSA_NB = 8
FORCE_BONUS = 1e3
N_BUCKETS, MAX_EXACT, MAX_DISTANCE = 32, 16, 128
NEAR_BLOCKS = 3
FAR_CHUNK_BLOCKS = 8
CMP_NEAR = 32
CMP_AHEAD = (QB - CMP_BLOCK) // CMP_STRIDE + 1

N_GROUPS, EPG, N_EXPERTS, D_EXPERT = 4, 8, 32, 256
MOE_TM = 256

C_POOL, C_R, C_K, C_V, C_LORA, C_Q, C_GATE, P_COLS = 0, 512, 1024, 1536, 2048, 2304, 3328, 3584
SRC_Q, SRC_KV, SRC_GATE = 2208, 3232, 4768
N_KV = 6
LORA_PAD = 256
SRC_RW_END = POOL_WIDTH + RW_COLS

VMEM_LIMIT = 56 * 1024 * 1024


def _t5_bucket_np(dist):
    n = np.maximum(dist, 0)
    nf = np.maximum(n, 1).astype(np.float32)
    large = MAX_EXACT + (np.log(nf / MAX_EXACT) / math.log(MAX_DISTANCE / MAX_EXACT)
                         * (N_BUCKETS - MAX_EXACT)).astype(np.int32)
    large = np.minimum(large, N_BUCKETS - 1)
    return np.where(n < MAX_EXACT, n, large)


def _cp(sem, vmem=None):
    return pltpu.CompilerParams(dimension_semantics=sem, vmem_limit_bytes=vmem)


def _dot(a, b, precision=None):
    return jnp.dot(a, b, preferred_element_type=F32, precision=precision)


def _dot_nt(a, b, precision=None):
    return lax.dot_general(a, b, (((1,), (1,)), ((), ())), preferred_element_type=F32, precision=precision)


def _dot_tn(a, b, precision=None):
    return lax.dot_general(a, b, (((0,), (0,)), ((), ())), preferred_element_type=F32, precision=precision)


def _bdot(a, b):
    return _dot(a.astype(BF16), b.astype(BF16))


def _split(a):
    hi = a.astype(BF16)
    return hi, (a - hi.astype(F32)).astype(BF16)


def _dot3(a, b):
    ah, al = _split(a)
    bh, bl = _split(b)
    return _dot(ah, bh) + (_dot(ah, bl) + _dot(al, bh))


def _rms(x, g):
    return x * lax.rsqrt(jnp.mean(x * x, axis=-1, keepdims=True) + RMS_EPS) * g


def _norm_kernel(x_ref, g_ref, h_ref):
    h_ref[...] = _rms(x_ref[...], g_ref[...]).astype(h_ref.dtype)


def _norm(x, g, out_dtype, tm=512):
    n, d = x.shape
    return pl.pallas_call(
        _norm_kernel,
        out_shape=jax.ShapeDtypeStruct((n, d), out_dtype),
        grid=(n // tm,),
        in_specs=[pl.BlockSpec((tm, d), lambda i: (i, 0)), pl.BlockSpec((1, d), lambda i: (0, 0))],
        out_specs=pl.BlockSpec((tm, d), lambda i: (i, 0)),
        compiler_params=_cp(("parallel",)),
        name="rms_norm",
    )(x, g.reshape(1, d))


def _add_norm_kernel(x_ref, y1_ref, y2_ref, g_ref, xo_ref, h_ref):
    x = x_ref[...] + (y1_ref[...] + y2_ref[...])
    xo_ref[...] = x
    h_ref[...] = _rms(x, g_ref[...]).astype(h_ref.dtype)


def _add_norm(x, y1, y2, g, out_dtype, tm=512):
    n, d = x.shape
    row = pl.BlockSpec((tm, d), lambda i: (i, 0))
    return pl.pallas_call(
        _add_norm_kernel,
        out_shape=(jax.ShapeDtypeStruct((n, d), F32), jax.ShapeDtypeStruct((n, d), out_dtype)),
        grid=(n // tm,),
        in_specs=[row, row, row, pl.BlockSpec((1, d), lambda i: (0, 0))],
        out_specs=(row, row),
        compiler_params=_cp(("parallel",)),
        name="moe_combine_norm",
    )(x, y1, y2, g.reshape(1, d))


def _matmul_kernel(x_ref, w_ref, o_ref):
    o_ref[...] = _dot(x_ref[...], w_ref[...]).astype(o_ref.dtype)


def _matmul(x, w, tm=512, tn=P_COLS // 2):
    m, k = x.shape
    n = w.shape[1]
    return pl.pallas_call(
        _matmul_kernel,
        out_shape=jax.ShapeDtypeStruct((m, n), F32),
        grid=(n // tn, m // tm),
        in_specs=[pl.BlockSpec((tm, k), lambda j, i: (i, 0)), pl.BlockSpec((k, tn), lambda j, i: (0, j))],
        out_specs=pl.BlockSpec((tm, tn), lambda j, i: (i, j)),
        compiler_params=_cp(("parallel", "parallel"), VMEM_LIMIT),
        name="in_proj",
    )(x, w)


def _kv_proj_kernel(x_ref, w_ref, c_ref, ks_ref, vs_ref, kw_ref, vw_ref, stage_ref):
    i = pl.program_id(1)
    tm = x_ref.shape[0]
    head = i == 0
    res = _dot(x_ref[...], w_ref[...])
    col = lax.broadcasted_iota(jnp.int32, (tm, NSA_DH), 1)
    blk = ((i - 1) * tm + lax.broadcasted_iota(jnp.int32, (tm, NSA_DH), 0)) // SLC_BLOCK
    first_col = jnp.where(col == 0, 1.0, 0.0)
    extra = {2: jnp.where(head, 1.0, jnp.where(blk == col, 1.0, 0.0)), 3: first_col,
             4: jnp.where(head, first_col, 0.0), 5: first_col}
    outs = {2: ks_ref, 3: vs_ref, 4: kw_ref, 5: vw_ref}
    for t in range(N_KV):
        for g in range(NSA_G):
            c0 = (t * NSA_G + g) * NSA_DH
            val = res[:, c0:c0 + NSA_DH]
            if t < 2:
                stage_ref[...] = val
                c_ref[t, 0, g] = jnp.concatenate(
                    [stage_ref[pl.ds(j, tm // CMP_STRIDE, stride=CMP_STRIDE), :] for j in range(CMP_STRIDE)],
                    axis=1).astype(c_ref.dtype)
            else:
                feat = jnp.where(head, 0.0, val)
                outs[t][0, g] = jnp.concatenate([feat, extra[t]], axis=1).astype(outs[t].dtype)


def _kv_proj(x, w, batch, seq):
    k = x.shape[1]
    tm = KV_PAD
    nt = seq // tm
    aug = jax.ShapeDtypeStruct((batch, NSA_G, KV_PAD + seq, 2 * NSA_DH), BF16)
    aug_spec = pl.BlockSpec((1, NSA_G, tm, 2 * NSA_DH), lambda b, i: (b, 0, i, 0))
    prev = lambda i: jnp.maximum(i - 1, 0)
    return pl.pallas_call(
        _kv_proj_kernel,
        out_shape=(jax.ShapeDtypeStruct((2, batch, NSA_G, seq // CMP_STRIDE, CMP_STRIDE * NSA_DH), BF16),
                   aug, aug, aug, aug),
        grid=(batch, nt + 1),
        in_specs=[pl.BlockSpec((tm, k), lambda b, i: (b * nt + prev(i), 0)),
                  pl.BlockSpec(w.shape, lambda b, i: (0, 0))],
        out_specs=(pl.BlockSpec((2, 1, NSA_G, tm // CMP_STRIDE, CMP_STRIDE * NSA_DH),
                                lambda b, i: (0, b, 0, prev(i), 0)),
                   aug_spec, aug_spec, aug_spec, aug_spec),
        scratch_shapes=[pltpu.VMEM((tm, NSA_DH), F32)],
        compiler_params=_cp(("parallel", "arbitrary"), VMEM_LIMIT),
        name="kv_proj",
    )(x, w)


def _pool_kernel(u_ref, halo_ref, w_ref, scale_ref, o_ref, buf_ref, *, tile):
    i = pl.program_id(1)
    u = u_ref[...]
    buf_ref[POOL_HALO:, :] = u
    buf_ref[:POOL_HALO, :] = jnp.where(i > 0, halo_ref[...], 0.0)
    t = i * tile + lax.broadcasted_iota(jnp.int32, (tile, 1), 0)
    outs = []
    for gi, win in enumerate(POOL_WINDOWS):
        cols = slice(gi * POOL_GW, (gi + 1) * POOL_GW)
        s = u[:, cols]
        for j in range(1, win):
            s = s + buf_ref[POOL_HALO - j:POOL_HALO - j + tile, cols]
        cnt = jnp.minimum(t + 1, win).astype(F32)
        pooled = s / cnt - u[:, cols]
        outs.append(_dot(pooled.astype(BF16), w_ref[gi]))
    o_ref[...] = (jnp.concatenate(outs, axis=1) * scale_ref[...]).astype(o_ref.dtype)


def _pool(proj, w_grp, scale, batch, seq, tile=512):
    nt = seq // tile
    hb = tile // POOL_HALO
    return pl.pallas_call(
        functools.partial(_pool_kernel, tile=tile),
        out_shape=jax.ShapeDtypeStruct((batch * seq, POOL_WIDTH), BF16),
        grid=(batch, nt),
        in_specs=[
            pl.BlockSpec((tile, POOL_WIDTH), lambda b, i: (b * nt + i, 0)),
            pl.BlockSpec((POOL_HALO, POOL_WIDTH), lambda b, i: (jnp.maximum((b * nt + i) * hb - 1, 0), 0)),
            pl.BlockSpec((len(POOL_WINDOWS), POOL_GW, POOL_GW), lambda b, i: (0, 0, 0)),
            pl.BlockSpec((1, POOL_WIDTH), lambda b, i: (0, 0)),
        ],
        out_specs=pl.BlockSpec((tile, POOL_WIDTH), lambda b, i: (b * nt + i, 0)),
        scratch_shapes=[pltpu.VMEM((tile + POOL_HALO, POOL_WIDTH), F32)],
        compiler_params=_cp(("parallel", "parallel")),
        name="pool_mixer",
    )(proj, proj, w_grp.astype(BF16), scale.reshape(1, POOL_WIDTH))


def _token_shift(u, halo, mu, first):
    prev_row = jnp.where(first, 0.0, halo[7:8, :])
    rolled = pltpu.roll(u, 1, 0)
    row = lax.broadcasted_iota(jnp.int32, u.shape, 0)
    prev = jnp.where(row == 0, prev_row, rolled)
    return u + (prev - u) * mu


def _rwkv_chunk_kernel(r_ref, k_ref, v_ref, l_ref, rh_ref, kh_ref, vh_ref, lh_ref,
                       mur_ref, muk_ref, muv_ref, mul_ref, w0_ref, wup_ref, a0_ref, aup_ref, gup_ref,
                       kk_ref, ka_ref, rk_ref, bd_ref, qy_ref, mn_ref, g_ref, bonus_ref):
    first = pl.program_id(1) == 0
    c = RW_CHUNK
    r = _token_shift(r_ref[...], rh_ref[...], mur_ref[...], first)
    k = _token_shift(k_ref[...], kh_ref[...], muk_ref[...], first)
    v = _token_shift(v_ref[...], vh_ref[...], muv_ref[...], first)
    lo = _token_shift(l_ref[...], lh_ref[...], mul_ref[...], first)
    wd = lo[:, :DECAY_LORA]
    ad = lo[:, DECAY_LORA:DECAY_LORA + AAA_LORA]
    gd = lo[:, DECAY_LORA + AAA_LORA:RW_LORA]
    z = -(w0_ref[...] + _dot(jnp.tanh(wd), wup_ref[...], HI))
    w_log = -(jnp.maximum(z, 0.0) + jnp.log(1.0 + jnp.exp(-jnp.abs(z)))) - 0.5
    logw = -jnp.exp(w_log)
    a = jax.nn.sigmoid(a0_ref[...] + _dot(ad, aup_ref[...], HI))
    g_ref[0] = _dot(jax.nn.sigmoid(gd), gup_ref[...], HI)
    kkraw = k * kk_ref[...]
    k2 = k * (1.0 + (a - 1.0) * ka_ref[...])
    rkr = r * k2 * rk_ref[...]

    cum_all = logw
    trow = lax.broadcasted_iota(jnp.int32, logw.shape, 0) % c
    step = 1
    while step < c:
        cum_all = cum_all + jnp.where(trow >= step, pltpu.roll(cum_all, step, 0), 0.0)
        step *= 2

    ti = lax.broadcasted_iota(jnp.int32, (c, c), 0)
    si = lax.broadcasted_iota(jnp.int32, (c, c), 1)
    incl = ti >= si
    strict = ti > si
    eye = ti == si
    zeros = jnp.zeros((c, c), F32)
    bd = bd_ref[...]

    def head_sum(t):
        hi, lo = _split(t)
        return _dot(hi, bd) + _dot(lo, bd)

    kk = kkraw / jnp.maximum(jnp.sqrt(head_sum(kkraw * kkraw)), 1e-12)
    bonus_ref[0] = head_sum(rkr) * v
    nchunk = logw.shape[0] // c
    ends = [cum_all[(j + 1) * c - 1:(j + 1) * c, :] for j in range(nchunk)]
    cum_end = jnp.concatenate([jnp.broadcast_to(e, (c, e.shape[1])) for e in ends], axis=0)
    ginv = jnp.exp(-cum_all)
    gtail = jnp.exp(cum_end - cum_all)
    gend = jnp.exp(cum_end)
    kka = kk * a
    at = -kk * jnp.exp(cum_all - logw)
    bt = kka * ginv
    kt = k2 * ginv
    rt = r * jnp.exp(cum_all)
    bhat = kka * gtail
    khat = k2 * gtail

    heads = [(slice(j * c, (j + 1) * c), slice(h * RW_DH, (h + 1) * RW_DH))
             for j in range(nchunk) for h in range(RW_HEADS)]
    stack = lambda x, y, s: jnp.concatenate([x[s], y[s]], axis=0).astype(BF16)
    gram = [_dot_nt(stack(at, rt, s), stack(bt, kt, s)) for s in heads]
    a_ab = [jnp.where(strict, g[:c, :c], 0.0) for g in gram]
    a_ak = [jnp.where(strict, g[:c, c:], 0.0) for g in gram]
    incl2 = (lax.broadcasted_iota(jnp.int32, (c, 2 * c), 0)
             >= lax.broadcasted_iota(jnp.int32, (c, 2 * c), 1) % c)
    a_r = [jnp.where(incl2, g[c:, :], 0.0) for g in gram]
    p = a_ab
    tinv = [eye.astype(F32) + x for x in p]
    for _ in range(int(math.log2(c)) - 1):
        p = [_bdot(x, x) for x in p]
        tinv = [t + _bdot(t, x) for t, x in zip(tinv, p)]
    av = [_bdot(x, v[s]) for x, s in zip(a_ak, heads)]
    w12 = [_bdot(t, jnp.concatenate([at[s], x], axis=1)) for t, x, s in zip(tinv, av, heads)]
    zmat = [jnp.concatenate([w, jnp.concatenate([zeros, v[s]], axis=1)], axis=0).astype(BF16)
            for w, s in zip(w12, heads)]
    out1 = [_dot(x.astype(BF16), z) for x, z in zip(a_r, zmat)]
    out2 = [_dot_tn(stack(bhat, khat, s), z) for s, z in zip(heads, zmat)]
    qy = [o + jnp.concatenate([rt[s], zeros], axis=1) for o, s in zip(out1, heads)]
    for j in range(nchunk):
        qy_ref[0, j * c:(j + 1) * c, :] = jnp.concatenate(qy[j * RW_HEADS:(j + 1) * RW_HEADS], axis=1)
        for h in range(RW_HEADS):
            idx = j * RW_HEADS + h
            diag = jnp.where(eye, gend[heads[idx]], 0.0)
            mn_ref[0, j, h] = out2[idx] + jnp.concatenate([diag, zeros], axis=1)


def _rwkv_scan_kernel(qy_ref, mn_ref, g_ref, bonus_ref, gnw_ref, gnb_ref, bd_ref, y_ref, st_ref, *, batch):
    @pl.when(pl.program_id(0) == 0)
    def _():
        st_ref[...] = jnp.zeros_like(st_ref)

    bd = bd_ref[...]

    def head_mean(t):
        hi, lo = _split(t)
        return (_dot(hi, bd) + _dot(lo, bd)) * (1.0 / RW_DH)

    pairs = [(b, h) for b in range(batch) for h in range(RW_HEADS)]
    sts = [st_ref[b * RW_HEADS + h] for b, h in pairs]
    ys = [_bdot(qy_ref[b, :, 2 * RW_DH * h:2 * RW_DH * h + RW_DH], st)
          + qy_ref[b, :, 2 * RW_DH * h + RW_DH:2 * RW_DH * (h + 1)] for (b, h), st in zip(pairs, sts)]
    for (b, h), st in zip(pairs, sts):
        mn = mn_ref[b, 0, h]
        st_ref[b * RW_HEADS + h] = _dot3(mn[:, :RW_DH], st) + mn[:, RW_DH:]
    for b in range(batch):
        y = jnp.concatenate(ys[b * RW_HEADS:(b + 1) * RW_HEADS], axis=1)
        dev = y - head_mean(y)
        yn = dev * lax.rsqrt(head_mean(dev * dev) + RW_GN_EPS) * gnw_ref[...] + gnb_ref[...]
        y_ref[b] = ((yn + bonus_ref[b]) * g_ref[b]).astype(y_ref.dtype)


def _rwkv(proj, p, batch, seq):
    head_ones = jnp.asarray(np.kron(np.eye(RW_HEADS), np.ones((RW_DH, RW_DH))), BF16)
    c = RW_CHUNK
    nc = seq // c
    tb = RW_NB * c
    nt = seq // tb
    hb = tb // 8
    row512 = lambda col: pl.BlockSpec((tb, RW_WIDTH), lambda b, i: (b * nt + i, col))
    halo512 = lambda col: pl.BlockSpec((8, RW_WIDTH), lambda b, i: (jnp.maximum((b * nt + i) * hb - 1, 0), col))
    const = lambda shape: pl.BlockSpec(shape, lambda b, i: (0,) * len(shape))
    vec = const((1, RW_WIDTH))
    out_row = lambda w: pl.BlockSpec((1, tb, w), lambda b, i: (b, i, 0))
    qy, mn, g, bonus = pl.pallas_call(
        _rwkv_chunk_kernel,
        out_shape=(jax.ShapeDtypeStruct((batch, seq, 2 * RW_WIDTH), F32),
                   jax.ShapeDtypeStruct((batch, nc, RW_HEADS, RW_DH, 2 * RW_DH), F32),
                   jax.ShapeDtypeStruct((batch, seq, RW_WIDTH), F32),
                   jax.ShapeDtypeStruct((batch, seq, RW_WIDTH), F32)),
        grid=(batch, nt),
        in_specs=[row512(C_R // RW_WIDTH), row512(C_K // RW_WIDTH), row512(C_V // RW_WIDTH),
                  pl.BlockSpec((tb, LORA_PAD), lambda b, i: (b * nt + i, C_LORA // LORA_PAD)),
                  halo512(C_R // RW_WIDTH), halo512(C_K // RW_WIDTH), halo512(C_V // RW_WIDTH),
                  pl.BlockSpec((8, LORA_PAD), lambda b, i: (jnp.maximum((b * nt + i) * hb - 1, 0), C_LORA // LORA_PAD)),
                  vec, vec, vec, const((1, LORA_PAD)),
                  vec, const((DECAY_LORA, RW_WIDTH)), vec, const((AAA_LORA, RW_WIDTH)), const((GATE_LORA, RW_WIDTH)),
                  vec, vec, vec, const((RW_WIDTH, RW_WIDTH))],
        out_specs=(out_row(2 * RW_WIDTH),
                   pl.BlockSpec((1, RW_NB, RW_HEADS, RW_DH, 2 * RW_DH), lambda b, i: (b, i, 0, 0, 0)),
                   out_row(RW_WIDTH), out_row(RW_WIDTH)),
        compiler_params=_cp(("parallel", "parallel"), VMEM_LIMIT),
        name="rwkv_chunk",
    )(proj, proj, proj, proj, proj, proj, proj, proj,
      p["mu_r"], p["mu_k"], p["mu_v"], p["mu_l"], p["w0"], p["w_up"], p["a0"], p["a_up"], p["g_up"],
      p["k_k"], p["k_a"], p["r_k"], head_ones)

    full = lambda w: pl.BlockSpec((batch, c, w), lambda i: (0, i, 0))
    return pl.pallas_call(
        functools.partial(_rwkv_scan_kernel, batch=batch),
        out_shape=jax.ShapeDtypeStruct((batch, seq, RW_WIDTH), BF16),
        grid=(nc,),
        in_specs=[full(2 * RW_WIDTH),
                  pl.BlockSpec((batch, 1, RW_HEADS, RW_DH, 2 * RW_DH), lambda i: (0, i, 0, 0, 0)),
                  full(RW_WIDTH), full(RW_WIDTH),
                  pl.BlockSpec((1, RW_WIDTH), lambda i: (0, 0)), pl.BlockSpec((1, RW_WIDTH), lambda i: (0, 0)),
                  pl.BlockSpec((RW_WIDTH, RW_WIDTH), lambda i: (0, 0))],
        out_specs=full(RW_WIDTH),
        scratch_shapes=[pltpu.VMEM((batch * RW_HEADS, RW_DH, RW_DH), F32)],
        compiler_params=_cp(("arbitrary",), VMEM_LIMIT),
        name="rwkv_scan",
    )(qy, mn, g, bonus, p["gn_w"], p["gn_b"], head_ones).reshape(batch * seq, RW_WIDTH)


def _gelu_tanh(x):
    return 0.5 * x * (1.0 + jnp.tanh(math.sqrt(2.0 / math.pi) * (x + 0.044715 * (x * x * x))))


def _compress_kernel(x_ref, pos_ref, w1_ref, w2_ref, o_ref):
    half = CMP_STRIDE * NSA_DH
    x = x_ref[0, 0, 0]
    w1 = w1_ref[0]
    posb = _dot(pos_ref[0], w1)[0:1, :]
    h1 = _dot(x, w1[:half])
    h2 = _dot(x, w1[half:])
    n = h2.shape[0]
    row = lax.broadcasted_iota(jnp.int32, h2.shape, 0)
    h2s = jnp.where(row < n - 1, pltpu.roll(h2, n - 1, 0), 0.0)
    hid = _gelu_tanh(h1 + h2s + posb)
    o_ref[0, 0, 0] = _dot(hid.astype(BF16), w2_ref[0]).astype(o_ref.dtype)


def _compress(xkv, pos, w1, w2, batch, seq):
    nr = seq // CMP_STRIDE
    wide = CMP_STRIDE * NSA_DH
    return pl.pallas_call(
        _compress_kernel,
        out_shape=jax.ShapeDtypeStruct((2, batch, NSA_G, nr, NSA_DH), BF16),
        grid=(2, batch, NSA_G),
        in_specs=[pl.BlockSpec((1, 1, 1, nr, wide), lambda t, b, g: (t, b, g, 0, 0)),
                  pl.BlockSpec((1, 8, 2 * wide), lambda t, b, g: (t, 0, 0)),
                  pl.BlockSpec((1, 2 * wide, CMP_HIDDEN), lambda t, b, g: (t, 0, 0)),
                  pl.BlockSpec((1, CMP_HIDDEN, NSA_DH), lambda t, b, g: (t, 0, 0))],
        out_specs=pl.BlockSpec((1, 1, 1, nr, NSA_DH), lambda t, b, g: (t, b, g, 0, 0)),
        compiler_params=_cp(("parallel", "parallel", "parallel")),
        name="nsa_compress",
    )(xkv, pos, w1, w2)


def _lane_tile_fold(x, op, init):
    for t in range(x.shape[1] // 128):
        init = op(init, x[:, 128 * t:128 * (t + 1)])
    return init


def _nsa_kernel(q_ref, gate_ref, kc_ref, vc_ref, ks_ref, vs_ref, kw_ref, vw_ref,
                tblc_ref, tbln_ref, tblw_ref, ovt_ref, gexp_ref, o_ref, s_ref, *, ncmp):
    nb = NSA_NB
    blocks = [pl.program_id(2) * nb + u for u in range(nb)]
    rows = NSA_HG * QB
    each = lambda f, *ls: [f(*xs) for xs in zip(*ls)]
    qt = q_ref[...]
    qs = [jnp.concatenate([qt[u * QB:(u + 1) * QB, NSA_DH * h:NSA_DH * (h + 1)] for h in range(NSA_HG)], axis=0)
          * (NSA_DH ** -0.5 * LOG2E) for u in range(nb)]
    qb = each(lambda x: x.astype(BF16), qs)
    rmax = lambda x: jnp.max(x, axis=1, keepdims=True)
    rsum = lambda x: jnp.sum(x, axis=1, keepdims=True)

    win_w = WINDOW + QB
    kw = [kw_ref[0, 0, pl.ds(pl.multiple_of(i * QB, QB), win_w), :] for i in blocks]
    vw = [vw_ref[0, 0, pl.ds(pl.multiple_of(i * QB, QB), win_w), :] for i in blocks]
    flag = lax.broadcasted_iota(jnp.int32, (rows, kw_ref.shape[3] - NSA_DH), 1) == 0
    q_win = each(lambda x: jnp.concatenate([x, jnp.where(flag, NEG, 0.0)], axis=1).astype(BF16), qs)
    tblw = tblw_ref[0]
    pv = lambda x: x[:, :NSA_DH] / x[:, NSA_DH:NSA_DH + 1]
    s_w = each(lambda x, k: _dot_nt(x, k) + tblw, q_win, kw)
    m_w = each(rmax, s_w)
    e_w = each(lambda s, m: jnp.exp2(s - m), s_w, m_w)
    o_w = each(lambda e, v: pv(_dot(e.astype(BF16), v)), e_w, vw)

    kc = kc_ref[0, 0]
    vc = vc_ref[0, 0]
    tblc = tblc_ref[0]
    cidx = lax.broadcasted_iota(jnp.int32, (rows, ncmp), 1)
    qrow = lax.broadcasted_iota(jnp.int32, (rows, 1), 0) % QB
    lc = [jnp.where(cidx < (QB // CMP_STRIDE) * i + CMP_AHEAD,
                    _dot_nt(x, kc) + pltpu.roll(tblc, (4 * i - CMP_NEAR // 2 + ncmp) % ncmp, 1), NEG)
          for x, i in zip(qb, blocks)]
    m_c = each(rmax, lc)
    e_c = each(lambda s, m: jnp.exp2(s - m), lc, m_c)
    den_c = each(rsum, e_c)
    pc = [e * jnp.where(i * QB + qrow >= CMP_BLOCK - 1, 1.0 / d, 0.0) for e, d, i in zip(e_c, den_c, blocks)]
    o_c = each(lambda x: _dot(x.astype(BF16), vc), pc)
    pcs_hi, pcs_lo = _split(jnp.concatenate(
        each(lambda x: x[0:QB] + x[QB:2 * QB] + x[2 * QB:3 * QB] + x[3 * QB:4 * QB], pc), axis=0))
    ovt = ovt_ref[...]
    imp = _dot_nt(ovt, pcs_hi) + _dot_nt(ovt, pcs_lo)

    nslc = ovt.shape[0]
    nidx = lax.broadcasted_iota(jnp.int32, (nslc, nb * QB), 0)
    cur = blocks[0] + lax.broadcasted_iota(jnp.int32, (nslc, nb * QB), 1) // QB
    forced = (nidx == 0) | (nidx == cur) | (nidx == cur - 1)
    work = jnp.where(nidx <= cur, imp + jnp.where(forced, FORCE_BONUS, 0.0), -1.0)
    sel_all = jnp.zeros((nslc, nb * QB), F32)
    for _ in range(N_SELECT):
        m = jnp.max(work, axis=0, keepdims=True)
        first = jnp.min(jnp.where(work == m, nidx, nslc), axis=0, keepdims=True)
        pick = nidx == first
        sel_all = jnp.where(pick & (m >= 0.0), 1.0, sel_all)
        work = jnp.where(pick, -2.0, work)
    sel_t = sel_all.T

    ind_w = ks_ref.shape[3] - NSA_DH
    selq = [sel_t[u * QB:(u + 1) * QB] for u in range(nb)]
    if nslc < ind_w:
        selq = each(lambda x: jnp.concatenate([x, jnp.zeros((QB, ind_w - nslc), F32)], axis=1), selq)
    bidx = lax.broadcasted_iota(jnp.int32, (QB, ind_w), 1)
    tile4 = lambda t: jnp.concatenate([t] * NSA_HG, axis=0)
    with_mask = lambda x, keep: jnp.concatenate([x, tile4(jnp.where(keep, 0.0, NEG))], axis=1).astype(BF16)
    q_sel = each(lambda x, s: with_mask(x, s > 0.0), qs, selq)
    q_far = jnp.concatenate([with_mask(x, (s > 0.0) & (bidx <= i - NEAR_BLOCKS))
                             for x, s, i in zip(qs, selq, blocks)], axis=0)
    near_w = NEAR_BLOCKS * SLC_BLOCK
    pad_s = KV_PAD
    near0 = pad_s - (NEAR_BLOCKS - 1) * SLC_BLOCK
    kn = [ks_ref[0, 0, pl.ds(pl.multiple_of(near0 + i * QB, QB), near_w), :] for i in blocks]
    vn = [vs_ref[0, 0, pl.ds(pl.multiple_of(near0 + i * QB, QB), near_w), :] for i in blocks]
    tbln = tbln_ref[0]
    s_near = each(lambda x, k: _dot_nt(x, k) + tbln, q_sel, kn)
    m_near = jnp.concatenate(each(rmax, s_near), axis=0)

    far_w = FAR_CHUNK_BLOCKS * SLC_BLOCK
    n_far = jnp.maximum(blocks[-1] - (NEAR_BLOCKS - 1) + FAR_CHUNK_BLOCKS - 1, 0) // FAR_CHUNK_BLOCKS

    def far_logits(js, mvec):
        ss = [_dot_nt(q_far, ks_ref[0, 0, pl.ds(pl.multiple_of(pad_s + j * far_w, SLC_BLOCK), far_w), :])
              for j in js]
        for j, s in zip(js, ss):
            s_ref[:, pl.ds(pl.multiple_of(j * far_w, far_w), far_w)] = s
        for s in ss:
            mvec = _lane_tile_fold(s, jnp.maximum, mvec)
        return mvec

    n_pair = n_far // 2
    odd = n_far % 2 == 1
    mvec = lax.fori_loop(0, n_pair, lambda jp, m: far_logits([2 * jp, 2 * jp + 1], m),
                         jnp.full((nb * rows, 128), NEG, F32))
    mvec = lax.cond(odd, lambda m: far_logits([n_far - 1], m), lambda m: m, mvec)
    m_s = jnp.maximum(m_near, rmax(mvec))

    e_near = [jnp.exp2(s - m_s[u * rows:(u + 1) * rows]) for u, s in enumerate(s_near)]
    acc0 = jnp.concatenate(each(lambda e, v: _dot(e.astype(BF16), v), e_near, vn), axis=0)

    def far_values(js, acc):
        es = [jnp.exp2(s_ref[:, pl.ds(pl.multiple_of(j * far_w, far_w), far_w)] - m_s).astype(BF16) for j in js]
        for j, e in zip(js, es):
            acc = acc + _dot(e, vs_ref[0, 0, pl.ds(pl.multiple_of(pad_s + j * far_w, SLC_BLOCK), far_w), :])
        return acc

    acc = lax.fori_loop(0, n_pair, lambda jp, a: far_values([2 * jp, 2 * jp + 1], a), acc0)
    o_s = pv(lax.cond(odd, lambda a: far_values([n_far - 1], a), lambda a: a, acc))

    g_hi, g_lo = _split(jax.nn.sigmoid(gate_ref[0, 0]))
    gexp = gexp_ref[...]
    ge = _dot(g_hi, gexp) + _dot(g_lo, gexp)
    for u in range(nb):
        outs = []
        for h in range(NSA_HG):
            r0 = slice(h * QB, (h + 1) * QB)
            gt = lambda br: ge[u * QB:(u + 1) * QB, (3 * h + br) * 128:(3 * h + br) * 128 + NSA_DH]
            outs.append(gt(0) * o_c[u][r0] + gt(1) * o_s[u * rows + h * QB:u * rows + (h + 1) * QB]
                        + gt(2) * o_w[u][r0])
        o_ref[u * QB:(u + 1) * QB, :] = jnp.concatenate(outs, axis=1).astype(o_ref.dtype)


def _bias_tables(rel_bias, ncmp):
    def table(dist, keep, base):
        onehot = jnp.asarray(np.eye(N_BUCKETS, dtype=np.float32)[_t5_bucket_np(dist)])
        tbl = jnp.einsum("qkb,bh->qkh", onehot, rel_bias, precision=HI)
        tbl = (tbl - base) * LOG2E
        tbl = jnp.where(jnp.asarray(keep)[..., None], tbl, NEG)
        k = dist.shape[1]
        return tbl.transpose(2, 0, 1).reshape(NSA_G, NSA_HG * QB, k)

    far = rel_bias[N_BUCKETS - 1]
    qi = np.arange(QB)[:, None]
    dist_c = qi - CMP_STRIDE * (np.arange(CMP_NEAR)[None, :] - CMP_NEAR // 2) - (CMP_BLOCK - 1)
    tblc = table(dist_c, dist_c >= 0, far)
    tblc = tblc * jnp.asarray(np.arange(CMP_NEAR) < CMP_NEAR // 2 + CMP_AHEAD, F32)
    tblc = jnp.pad(tblc, ((0, 0), (0, 0), (0, ncmp - CMP_NEAR)))
    jn = np.arange(NEAR_BLOCKS * SLC_BLOCK)[None, :]
    dist_n = (NEAR_BLOCKS - 1) * SLC_BLOCK + qi - jn
    tbln = table(dist_n, dist_n >= 0, far)
    jw = np.arange(WINDOW + QB)[None, :]
    dist_w = WINDOW + qi - jw
    tblw = table(dist_w, (dist_w >= 0) & (dist_w < WINDOW), 0.0)
    return tblc, tbln, tblw


def _nsa(proj, gate_logits, kvc, ks, vs, kw, vw, tables, batch, seq):
    nq = seq // QB
    ncmp = seq // CMP_STRIDE
    nslc = seq // SLC_BLOCK
    tblc, tbln, tblw = tables
    cstart = np.arange(ncmp) * CMP_STRIDE
    sstart = np.arange(nslc) * SLC_BLOCK
    overlap_t = ((cstart[None, :] <= sstart[:, None] + SLC_BLOCK - 1)
                 & (cstart[None, :] + CMP_BLOCK - 1 >= sstart[:, None])
                 & (cstart[None, :] + CMP_BLOCK <= seq)).astype(np.float32)
    rows = NSA_HG * QB
    n_gate = 3 * NSA_HG
    gate_spread = (np.arange(n_gate * 128)[None, :] // 128 == np.arange(n_gate)[:, None]) \
        & (np.arange(n_gate * 128)[None, :] % 128 < NSA_DH)
    kv_spec = lambda t: pl.BlockSpec((1, 1) + t.shape[2:], lambda b, g, i: (b, g, 0, 0))
    tbl_spec = lambda k: pl.BlockSpec((1, rows, k), lambda b, g, i: (g, 0, 0))
    qcol = C_Q // (NSA_HG * NSA_DH)
    return pl.pallas_call(
        functools.partial(_nsa_kernel, ncmp=ncmp),
        out_shape=jax.ShapeDtypeStruct((batch * seq, NSA_HEADS * NSA_DH), BF16),
        grid=(batch, NSA_G, nq // NSA_NB),
        in_specs=[pl.BlockSpec((NSA_NB * QB, NSA_HG * NSA_DH), lambda b, g, i: (b * (nq // NSA_NB) + i, qcol + g)),
                  pl.BlockSpec((1, 1, NSA_NB * QB, 3 * NSA_HG), lambda b, g, i: (b, g, i, 0)),
                  pl.BlockSpec((1, 1, ncmp, NSA_DH), lambda b, g, i: (b, g, 0, 0)),
                  pl.BlockSpec((1, 1, ncmp, NSA_DH), lambda b, g, i: (b, g, 0, 0)),
                  kv_spec(ks), kv_spec(vs), kv_spec(kw), kv_spec(vw),
                  tbl_spec(ncmp), tbl_spec(NEAR_BLOCKS * SLC_BLOCK), tbl_spec(WINDOW + QB),
                  pl.BlockSpec((nslc, ncmp), lambda b, g, i: (0, 0)),
                  pl.BlockSpec(gate_spread.shape, lambda b, g, i: (0, 0))],
        out_specs=pl.BlockSpec((NSA_NB * QB, NSA_HG * NSA_DH), lambda b, g, i: (b * (nq // NSA_NB) + i, g)),
        scratch_shapes=[pltpu.VMEM((NSA_NB * rows, seq), F32)],
        compiler_params=_cp(("parallel", "parallel", "arbitrary"), VMEM_LIMIT),
        name="nsa_attention",
    )(proj, gate_logits, kvc[0], kvc[1], ks, vs, kw, vw, tblc, tbln, tblw, jnp.asarray(overlap_t, BF16),
      jnp.asarray(gate_spread, BF16))


def _merge_kernel(h_ref, yp_ref, yr_ref, yn_ref, x_ref, wbp_ref, wbr_ref, wbn_ref, wm_ref, bm_ref, wo_ref,
                  gn_ref, wr_ref, br_ref, xo_ref, h2_ref, route_ref):
    d = D_MODEL
    gl = jax.nn.sigmoid(_dot(h_ref[...], wm_ref[...]) + bm_ref[...])
    merged = (gl[:, :d] * _dot(yp_ref[...], wbp_ref[...]) + gl[:, d:2 * d] * _dot(yr_ref[...], wbr_ref[...])
              + gl[:, 2 * d:] * _dot(yn_ref[...], wbn_ref[...]))
    x = x_ref[...] + _dot(merged.astype(BF16), wo_ref[...])
    xo_ref[...] = x
    h2 = _rms(x, gn_ref[...])
    h2_ref[...] = h2.astype(h2_ref.dtype)

    h2_hi, h2_lo = _split(h2)
    nl = br_ref.shape[1]
    both = _dot(h2_hi, wr_ref[...])
    logits = both[:, :nl] + (both[:, nl:] + _dot(h2_lo, wr_ref[:, :nl])) + br_ref[...]
    lane = lax.broadcasted_iota(jnp.int32, logits.shape, 1)
    big = logits.shape[1]
    lg = jnp.where(lane < N_GROUPS, logits, NEG)
    mg = jnp.max(lg, axis=1, keepdims=True)
    p_top = 1.0 / jnp.sum(jnp.exp(lg - mg), axis=1, keepdims=True)
    grp = jnp.min(jnp.where(lg == mg, lane, big), axis=1, keepdims=True)
    lo = N_GROUPS + EPG * grp
    le = jnp.where((lane >= lo) & (lane < lo + EPG), logits, NEG)
    e1 = jnp.max(le, axis=1, keepdims=True)
    i1 = jnp.min(jnp.where(le == e1, lane, big), axis=1, keepdims=True)
    le = jnp.where(lane == i1, NEG, le)
    e2 = jnp.max(le, axis=1, keepdims=True)
    i2 = jnp.min(jnp.where(le == e2, lane, big), axis=1, keepdims=True)
    t = jnp.exp(e2 - e1)
    w1 = p_top / (1.0 + t)
    w2 = p_top * t / (1.0 + t)
    route_ref[...] = jnp.where(lane == 0, (i1 - N_GROUPS).astype(F32),
                               jnp.where(lane == 1, (i2 - N_GROUPS).astype(F32),
                                         jnp.where(lane == 2, w1, jnp.where(lane == 3, w2, 0.0))))


def _merge(h, y_pool, y_rwkv, y_nsa, x, p, tm=256):
    n, d = x.shape
    row = lambda w: pl.BlockSpec((tm, w), lambda i: (i, 0))
    const = lambda a: pl.BlockSpec(a.shape, lambda i: (0, 0))
    ws = [p["wb_pool"], p["wb_rwkv"], p["wb_nsa"], p["w_merge"], p["b_merge"], p["w_out"],
          p["norm_ffn"], p["w_router"], p["b_router"]]
    return pl.pallas_call(
        _merge_kernel,
        out_shape=(jax.ShapeDtypeStruct((n, d), F32), jax.ShapeDtypeStruct((n, d), F32),
                   jax.ShapeDtypeStruct((n, 128), F32)),
        grid=(n // tm,),
        in_specs=[row(d), row(POOL_WIDTH), row(RW_WIDTH), row(d), row(d)] + [const(w) for w in ws],
        out_specs=(row(d), row(d), row(128)),
        compiler_params=_cp(("parallel",), VMEM_LIMIT),
        name="merge_router",
    )(h, y_pool, y_rwkv, y_nsa, x, *ws)


def _expert_kernel(te_ref, nt_ref, tok_ref, h_hbm, w_ref, wg_ref, wu_ref, wd_ref, o_ref,
                   xbuf, sem, wg_s, wu_s, wd_s):
    i = pl.program_id(0)
    tm = MOE_TM
    n_tiles = nt_ref[0]

    def row_copy(tile, slot, r):
        return pltpu.make_async_copy(h_hbm.at[pl.ds(tok_ref[tile * tm + r], 1), :],
                                     xbuf.at[slot, pl.ds(r, 1), :], sem.at[slot])

    def tile_wait(slot):
        pltpu.make_async_copy(h_hbm.at[pl.ds(0, tm), :], xbuf.at[slot], sem.at[slot]).wait()

    @pl.when((i == 0) & (n_tiles > 0))
    def _():
        def body(r, carry):
            row_copy(0, 0, r).start()
            return carry
        lax.fori_loop(0, tm, body, 0, unroll=8)

    @pl.when((i == 0) | (te_ref[i] != te_ref[jnp.maximum(i - 1, 0)]))
    def _():
        wg_s[...] = wg_ref[0, 0].astype(BF16)
        wu_s[...] = wu_ref[0, 0].astype(BF16)
        wd_s[...] = wd_ref[0, 0].astype(BF16)

    @pl.when(i < n_tiles)
    def _():
        slot = i % 2
        quarter = tm // 4

        def fetch_next(part):
            for r in range(part * quarter, (part + 1) * quarter):
                row_copy(i + 1, 1 - slot, r).start()

        tile_wait(slot)
        xb = xbuf[slot].astype(BF16)
        fetch_next(0)
        gate = _dot(xb, wg_s[...])
        fetch_next(1)
        up = _dot(xb, wu_s[...])
        fetch_next(2)
        hid = (gate * jax.nn.sigmoid(gate) * up).astype(BF16)
        fetch_next(3)
        o_ref[...] = w_ref[...] * _dot(hid, wd_s[...])

    @pl.when((i == n_tiles) & (n_tiles > 0))
    def _():
        tile_wait(i % 2)

    @pl.when(i >= nt_ref[0])
    def _():
        o_ref[...] = jnp.zeros_like(o_ref)


def _experts(h2, rowtok, roww, tile_expert, n_tiles, layer, wg, wu, wd):
    d = h2.shape[1]
    r = rowtok.shape[0]
    tm = MOE_TM
    return pl.pallas_call(
        _expert_kernel,
        out_shape=jax.ShapeDtypeStruct((r, d), F32),
        grid_spec=pltpu.PrefetchScalarGridSpec(
            num_scalar_prefetch=3,
            grid=(r // tm,),
            in_specs=[pl.BlockSpec(memory_space=pl.ANY),
                      pl.BlockSpec((tm, 1), lambda i, te, nt, tok: (i, 0)),
                      pl.BlockSpec((1, 1, d, D_EXPERT), lambda i, te, nt, tok: (layer, te[i], 0, 0)),
                      pl.BlockSpec((1, 1, d, D_EXPERT), lambda i, te, nt, tok: (layer, te[i], 0, 0)),
                      pl.BlockSpec((1, 1, D_EXPERT, d), lambda i, te, nt, tok: (layer, te[i], 0, 0))],
            out_specs=pl.BlockSpec((tm, d), lambda i, te, nt, tok: (i, 0)),
            scratch_shapes=[pltpu.VMEM((2, tm, d), F32), pltpu.SemaphoreType.DMA((2,)),
                            pltpu.VMEM((d, D_EXPERT), BF16), pltpu.VMEM((d, D_EXPERT), BF16),
                            pltpu.VMEM((D_EXPERT, d), BF16)]),
        compiler_params=_cp(("arbitrary",), VMEM_LIMIT),
        name="moe_experts",
    )(tile_expert, n_tiles, rowtok, h2, roww, wg, wu, wd)


def _moe(h2, route, layer, wg, wu, wd):
    n = h2.shape[0]
    tm = MOE_TM
    r = 2 * n + (N_EXPERTS + 1) * tm
    ids = route[:, 0:2].astype(jnp.int32).reshape(-1)
    wts = route[:, 2:4].reshape(-1)
    onehot = (ids[:, None] == jnp.arange(N_EXPERTS)[None, :]).astype(jnp.int32)
    rank = jnp.sum((jnp.cumsum(onehot, axis=0) - onehot) * onehot, axis=1)
    counts = jnp.sum(onehot, axis=0)
    tiles = (counts + tm - 1) // tm
    tile_end = jnp.cumsum(tiles)
    starts = (tile_end - tiles) * tm
    pos = starts[ids] + rank
    row_assign = jnp.full((r,), -1, jnp.int32).at[pos].set(
        jnp.arange(2 * n, dtype=jnp.int32), unique_indices=True, mode="promise_in_bounds")
    rowtok = jnp.maximum(row_assign, 0) // 2
    roww = jnp.where(row_assign >= 0, wts[jnp.maximum(row_assign, 0)], 0.0)
    n_tiles = tile_end[-1:].astype(jnp.int32)
    tile_expert = jnp.minimum(jnp.sum(tile_end[None, :] <= jnp.arange(r // tm)[:, None], axis=1),
                              N_EXPERTS - 1).astype(jnp.int32)
    ys = _experts(h2, rowtok, roww.reshape(r, 1), tile_expert, n_tiles, layer, wg, wu, wd)
    return ys[pos[0::2]], ys[pos[1::2]]


def _layer_params(l, a):
    f = lambda t: t[l]
    row = lambda t: t[l].reshape(1, -1)
    w_in = a["w_in"][l]
    w_in_p = jnp.concatenate([w_in[:, :SRC_RW_END], jnp.zeros((D_MODEL, C_Q - SRC_RW_END), F32),
                              w_in[:, SRC_Q:SRC_KV], w_in[:, SRC_GATE:],
                              jnp.zeros((D_MODEL, P_COLS - C_GATE - (w_in.shape[1] - SRC_GATE)), F32)], axis=1)
    mu = a["rw_mu"][l]
    wb = a["w_branch"][l].astype(BF16)
    w_router = jnp.zeros((D_MODEL, 128), F32)
    w_router = w_router.at[:, :N_GROUPS].set(a["w_router_grp"][l]).at[:, N_GROUPS:N_GROUPS + N_EXPERTS].set(
        a["w_router_exp"][l])
    b_router = jnp.zeros((1, 128), F32)
    b_router = b_router.at[0, :N_GROUPS].set(a["b_router_grp"][l]).at[0, N_GROUPS:N_GROUPS + N_EXPERTS].set(
        a["b_router_exp"][l])
    pos = jnp.stack([a["cmp_pos_k"][l].reshape(-1), a["cmp_pos_v"][l].reshape(-1)])
    return {
        "w_in": w_in_p.astype(BF16), "w_kv": w_in[:, SRC_KV:SRC_GATE].astype(BF16),
        "pool_w": f(a["pool_w"]), "pool_scale": f(a["pool_scale"]),
        "mu_r": mu[None, 0:RW_WIDTH], "mu_k": mu[None, RW_WIDTH:2 * RW_WIDTH],
        "mu_v": mu[None, 2 * RW_WIDTH:3 * RW_WIDTH],
        "mu_l": jnp.concatenate([mu[3 * RW_WIDTH:], jnp.zeros((LORA_PAD - RW_LORA,), F32)])[None],
        "w0": row(a["rw_w0"]), "w_up": f(a["rw_w_up"]), "a0": row(a["rw_a0"]), "a_up": f(a["rw_a_up"]),
        "g_up": f(a["rw_g_up"]), "k_k": row(a["rw_k_k"]), "k_a": row(a["rw_k_a"]), "r_k": row(a["rw_r_k"]),
        "gn_w": row(a["rw_gn_w"]), "gn_b": row(a["rw_gn_b"]),
        "cmp_pos": jnp.broadcast_to(pos[:, None, :], (2, 8, pos.shape[1])).astype(BF16),
        "cmp_w1": jnp.stack([a["cmp_w1_k"][l], a["cmp_w1_v"][l]]).astype(BF16),
        "cmp_w2": jnp.stack([a["cmp_w2_k"][l], a["cmp_w2_v"][l]]).astype(BF16),
        "wb_pool": wb[:POOL_WIDTH], "wb_rwkv": wb[POOL_WIDTH:POOL_WIDTH + RW_WIDTH],
        "wb_nsa": wb[POOL_WIDTH + RW_WIDTH:],
        "w_merge": a["w_merge"][l].astype(BF16), "b_merge": row(a["b_merge"]),
        "w_out": a["w_out"][l].astype(BF16), "norm_ffn": row(a["norm_ffn"]),
        "w_router": jnp.concatenate(_split(w_router), axis=1), "b_router": b_router,
    }


def _mixers(proj, kv, p, tables, batch, seq):
    kv_cmp, ks, vs, kw, vw = kv
    y_pool = _pool(proj, p["pool_w"], p["pool_scale"], batch, seq)
    y_rwkv = _rwkv(proj, p, batch, seq)
    kvc = _compress(kv_cmp, p["cmp_pos"], p["cmp_w1"], p["cmp_w2"], batch, seq)
    gate_logits = proj[:, C_GATE:C_GATE + 3 * NSA_HEADS].reshape(batch, seq, NSA_G, 3 * NSA_HG).transpose(0, 2, 1, 3)
    y_nsa = _nsa(proj, gate_logits, kvc, ks, vs, kw, vw, tables, batch, seq)
    return y_pool, y_rwkv, y_nsa


def kernel(x, rel_bias, norm_mix, w_in, pool_w, pool_scale, rw_mu, rw_w0, rw_w_up, rw_a0, rw_a_up, rw_g_up, rw_k_k, rw_k_a, rw_r_k, rw_gn_w, rw_gn_b, cmp_pos_k, cmp_w1_k, cmp_w2_k, cmp_pos_v, cmp_w1_v, cmp_w2_v, w_branch, w_merge, b_merge, w_out, norm_ffn, w_router_grp, b_router_grp, w_router_exp, b_router_exp, w_exp_gate, w_exp_up, w_exp_down, norm_final):
    a = dict(w_in=w_in, pool_w=pool_w, pool_scale=pool_scale, rw_mu=rw_mu, rw_w0=rw_w0, rw_w_up=rw_w_up,
             rw_a0=rw_a0, rw_a_up=rw_a_up, rw_g_up=rw_g_up, rw_k_k=rw_k_k, rw_k_a=rw_k_a, rw_r_k=rw_r_k,
             rw_gn_w=rw_gn_w, rw_gn_b=rw_gn_b, cmp_pos_k=cmp_pos_k, cmp_w1_k=cmp_w1_k, cmp_w2_k=cmp_w2_k,
             cmp_pos_v=cmp_pos_v, cmp_w1_v=cmp_w1_v, cmp_w2_v=cmp_w2_v, w_branch=w_branch, w_merge=w_merge,
             b_merge=b_merge, w_out=w_out, norm_ffn=norm_ffn, w_router_grp=w_router_grp,
             b_router_grp=b_router_grp, w_router_exp=w_router_exp, b_router_exp=b_router_exp,
             w_exp_gate=w_exp_gate, w_exp_up=w_exp_up, w_exp_down=w_exp_down)
    batch, seq, d = x.shape
    depth = norm_mix.shape[0]
    tables = _bias_tables(rel_bias, seq // CMP_STRIDE)
    xf = x.reshape(batch * seq, d)
    h = _norm(xf, norm_mix[0], BF16)
    for l in range(depth):
        p = _layer_params(l, a)
        proj = _matmul(h, p["w_in"])
        kv = _kv_proj(h, p["w_kv"], batch, seq)
        y_pool, y_rwkv, y_nsa = _mixers(proj, kv, p, tables, batch, seq)
        xf, h2, route = _merge(h, y_pool, y_rwkv, y_nsa, xf, p)
        y1, y2 = _moe(h2, route, l, w_exp_gate, w_exp_up, w_exp_down)
        last = l == depth - 1
        g_next = norm_final if last else norm_mix[l + 1]
        xf, h = _add_norm(xf, y1, y2, g_next, F32 if last else BF16)
    return h.reshape(batch, seq, d)
```

```python
import functools
import math

import jax
import jax.numpy as jnp
import numpy as np
from jax import lax
from jax.experimental import pallas as pl
from jax.experimental.pallas import tpu as pltpu

F32 = jnp.float32
BF16 = jnp.bfloat16
HI = lax.Precision.HIGHEST

D_MODEL = 1024
RMS_EPS = 1e-6
NEG = -1e30
LOG2E = math.log2(math.e)

POOL_WINDOWS = (2, 4, 8, 16)
POOL_WIDTH = 512
POOL_GW = 128
POOL_HALO = 16

RW_HEADS = 8
RW_DH = 64
RW_WIDTH = 512
DECAY_LORA, AAA_LORA, GATE_LORA = 32, 32, 96
RW_LORA = DECAY_LORA + AAA_LORA + GATE_LORA
RW_COLS = 3 * RW_WIDTH + RW_LORA
RW_GN_EPS = 64e-5
RW_CHUNK = 64
RW_NB = 4

NSA_DH = 64
NSA_HEADS = 16
NSA_G = 4
NSA_HG = 4
CMP_BLOCK, CMP_STRIDE, CMP_HIDDEN = 32, 16, 256
SLC_BLOCK = 64
N_SELECT = 8
WINDOW = 512
KV_PAD = WINDOW
QB = 64
NSA_NB = 8
FORCE_BONUS = 1e3
N_BUCKETS, MAX_EXACT, MAX_DISTANCE = 32, 16, 128
NEAR_BLOCKS = 3
FAR_CHUNK_BLOCKS = 8
CMP_NEAR = 32
CMP_AHEAD = (QB - CMP_BLOCK) // CMP_STRIDE + 1

N_GROUPS, EPG, N_EXPERTS, D_EXPERT = 4, 8, 32, 256
MOE_TM = 256

C_POOL, C_R, C_K, C_V, C_LORA, C_Q, C_GATE, P_COLS = 0, 512, 1024, 1536, 2048, 2304, 3328, 3584
SRC_Q, SRC_KV, SRC_GATE = 2208, 3232, 4768
N_KV = 6
LORA_PAD = 256
SRC_RW_END = POOL_WIDTH + RW_COLS

VMEM_LIMIT = 56 * 1024 * 1024


def _t5_bucket_np(dist):
    n = np.maximum(dist, 0)
    nf = np.maximum(n, 1).astype(np.float32)
    large = MAX_EXACT + (np.log(nf / MAX_EXACT) / math.log(MAX_DISTANCE / MAX_EXACT)
                         * (N_BUCKETS - MAX_EXACT)).astype(np.int32)
    large = np.minimum(large, N_BUCKETS - 1)
    return np.where(n < MAX_EXACT, n, large)


def _cp(sem, vmem=None):
    return pltpu.CompilerParams(dimension_semantics=sem, vmem_limit_bytes=vmem)


def _dot(a, b, precision=None):
    return jnp.dot(a, b, preferred_element_type=F32, precision=precision)


def _dot_nt(a, b, precision=None):
    return lax.dot_general(a, b, (((1,), (1,)), ((), ())), preferred_element_type=F32, precision=precision)


def _dot_tn(a, b, precision=None):
    return lax.dot_general(a, b, (((0,), (0,)), ((), ())), preferred_element_type=F32, precision=precision)


def _bdot(a, b):
    return _dot(a.astype(BF16), b.astype(BF16))


def _split(a):
    hi = a.astype(BF16)
    return hi, (a - hi.astype(F32)).astype(BF16)


def _dot3(a, b):
    ah, al = _split(a)
    bh, bl = _split(b)
    return _dot(ah, bh) + (_dot(ah, bl) + _dot(al, bh))


def _rms(x, g):
    return x * lax.rsqrt(jnp.mean(x * x, axis=-1, keepdims=True) + RMS_EPS) * g


def _norm_kernel(x_ref, g_ref, h_ref):
    h_ref[...] = _rms(x_ref[...], g_ref[...]).astype(h_ref.dtype)


def _norm(x, g, out_dtype, tm=512):
    n, d = x.shape
    return pl.pallas_call(
        _norm_kernel,
        out_shape=jax.ShapeDtypeStruct((n, d), out_dtype),
        grid=(n // tm,),
        in_specs=[pl.BlockSpec((tm, d), lambda i: (i, 0)), pl.BlockSpec((1, d), lambda i: (0, 0))],
        out_specs=pl.BlockSpec((tm, d), lambda i: (i, 0)),
        compiler_params=_cp(("parallel",)),
        name="rms_norm",
    )(x, g.reshape(1, d))


def _add_norm_kernel(x_ref, y1_ref, y2_ref, g_ref, xo_ref, h_ref):
    x = x_ref[...] + (y1_ref[...] + y2_ref[...])
    xo_ref[...] = x
    h_ref[...] = _rms(x, g_ref[...]).astype(h_ref.dtype)


def _add_norm(x, y1, y2, g, out_dtype, tm=512):
    n, d = x.shape
    row = pl.BlockSpec((tm, d), lambda i: (i, 0))
    return pl.pallas_call(
        _add_norm_kernel,
        out_shape=(jax.ShapeDtypeStruct((n, d), F32), jax.ShapeDtypeStruct((n, d), out_dtype)),
        grid=(n // tm,),
        in_specs=[row, row, row, pl.BlockSpec((1, d), lambda i: (0, 0))],
        out_specs=(row, row),
        compiler_params=_cp(("parallel",)),
        name="moe_combine_norm",
    )(x, y1, y2, g.reshape(1, d))


def _matmul_kernel(x_ref, w_ref, o_ref):
    o_ref[...] = _dot(x_ref[...], w_ref[...]).astype(o_ref.dtype)


def _matmul(x, w, tm=512, tn=P_COLS // 2):
    m, k = x.shape
    n = w.shape[1]
    return pl.pallas_call(
        _matmul_kernel,
        out_shape=jax.ShapeDtypeStruct((m, n), F32),
        grid=(n // tn, m // tm),
        in_specs=[pl.BlockSpec((tm, k), lambda j, i: (i, 0)), pl.BlockSpec((k, tn), lambda j, i: (0, j))],
        out_specs=pl.BlockSpec((tm, tn), lambda j, i: (i, j)),
        compiler_params=_cp(("parallel", "parallel"), VMEM_LIMIT),
        name="in_proj",
    )(x, w)


def _kv_proj_kernel(x_ref, w_ref, c_ref, ks_ref, vs_ref, kw_ref, vw_ref, stage_ref):
    i = pl.program_id(1)
    tm = x_ref.shape[0]
    head = i == 0
    res = _dot(x_ref[...], w_ref[...])
    col = lax.broadcasted_iota(jnp.int32, (tm, NSA_DH), 1)
    blk = ((i - 1) * tm + lax.broadcasted_iota(jnp.int32, (tm, NSA_DH), 0)) // SLC_BLOCK
    first_col = jnp.where(col == 0, 1.0, 0.0)
    extra = {2: jnp.where(head, 1.0, jnp.where(blk == col, 1.0, 0.0)), 3: first_col,
             4: jnp.where(head, first_col, 0.0), 5: first_col}
    outs = {2: ks_ref, 3: vs_ref, 4: kw_ref, 5: vw_ref}
    for t in range(N_KV):
        for g in range(NSA_G):
            c0 = (t * NSA_G + g) * NSA_DH
            val = res[:, c0:c0 + NSA_DH]
            if t < 2:
                stage_ref[...] = val
                c_ref[t, 0, g] = jnp.concatenate(
                    [stage_ref[pl.ds(j, tm // CMP_STRIDE, stride=CMP_STRIDE), :] for j in range(CMP_STRIDE)],
                    axis=1).astype(c_ref.dtype)
            else:
                feat = jnp.where(head, 0.0, val)
                outs[t][0, g] = jnp.concatenate([feat, extra[t]], axis=1).astype(outs[t].dtype)


def _kv_proj(x, w, batch, seq):
    k = x.shape[1]
    tm = KV_PAD
    nt = seq // tm
    aug = jax.ShapeDtypeStruct((batch, NSA_G, KV_PAD + seq, 2 * NSA_DH), BF16)
    aug_spec = pl.BlockSpec((1, NSA_G, tm, 2 * NSA_DH), lambda b, i: (b, 0, i, 0))
    prev = lambda i: jnp.maximum(i - 1, 0)
    return pl.pallas_call(
        _kv_proj_kernel,
        out_shape=(jax.ShapeDtypeStruct((2, batch, NSA_G, seq // CMP_STRIDE, CMP_STRIDE * NSA_DH), BF16),
                   aug, aug, aug, aug),
        grid=(batch, nt + 1),
        in_specs=[pl.BlockSpec((tm, k), lambda b, i: (b * nt + prev(i), 0)),
                  pl.BlockSpec(w.shape, lambda b, i: (0, 0))],
        out_specs=(pl.BlockSpec((2, 1, NSA_G, tm // CMP_STRIDE, CMP_STRIDE * NSA_DH),
                                lambda b, i: (0, b, 0, prev(i), 0)),
                   aug_spec, aug_spec, aug_spec, aug_spec),
        scratch_shapes=[pltpu.VMEM((tm, NSA_DH), F32)],
        compiler_params=_cp(("parallel", "arbitrary"), VMEM_LIMIT),
        name="kv_proj",
    )(x, w)


def _pool_kernel(u_ref, halo_ref, w_ref, scale_ref, o_ref, buf_ref, *, tile):
    i = pl.program_id(1)
    u = u_ref[...]
    buf_ref[POOL_HALO:, :] = u
    buf_ref[:POOL_HALO, :] = jnp.where(i > 0, halo_ref[...], 0.0)
    t = i * tile + lax.broadcasted_iota(jnp.int32, (tile, 1), 0)
    outs = []
    for gi, win in enumerate(POOL_WINDOWS):
        cols = slice(gi * POOL_GW, (gi + 1) * POOL_GW)
        s = u[:, cols]
        for j in range(1, win):
            s = s + buf_ref[POOL_HALO - j:POOL_HALO - j + tile, cols]
        cnt = jnp.minimum(t + 1, win).astype(F32)
        pooled = s / cnt - u[:, cols]
        outs.append(_dot(pooled.astype(BF16), w_ref[gi]))
    o_ref[...] = (jnp.concatenate(outs, axis=1) * scale_ref[...]).astype(o_ref.dtype)


def _pool(proj, w_grp, scale, batch, seq, tile=512):
    nt = seq // tile
    hb = tile // POOL_HALO
    return pl.pallas_call(
        functools.partial(_pool_kernel, tile=tile),
        out_shape=jax.ShapeDtypeStruct((batch * seq, POOL_WIDTH), BF16),
        grid=(batch, nt),
        in_specs=[
            pl.BlockSpec((tile, POOL_WIDTH), lambda b, i: (b * nt + i, 0)),
            pl.BlockSpec((POOL_HALO, POOL_WIDTH), lambda b, i: (jnp.maximum((b * nt + i) * hb - 1, 0), 0)),
            pl.BlockSpec((len(POOL_WINDOWS), POOL_GW, POOL_GW), lambda b, i: (0, 0, 0)),
            pl.BlockSpec((1, POOL_WIDTH), lambda b, i: (0, 0)),
        ],
        out_specs=pl.BlockSpec((tile, POOL_WIDTH), lambda b, i: (b * nt + i, 0)),
        scratch_shapes=[pltpu.VMEM((tile + POOL_HALO, POOL_WIDTH), F32)],
        compiler_params=_cp(("parallel", "parallel")),
        name="pool_mixer",
    )(proj, proj, w_grp.astype(BF16), scale.reshape(1, POOL_WIDTH))


def _token_shift(u, halo, mu, first):
    prev_row = jnp.where(first, 0.0, halo[7:8, :])
    rolled = pltpu.roll(u, 1, 0)
    row = lax.broadcasted_iota(jnp.int32, u.shape, 0)
    prev = jnp.where(row == 0, prev_row, rolled)
    return u + (prev - u) * mu


def _rwkv_chunk_kernel(r_ref, k_ref, v_ref, l_ref, rh_ref, kh_ref, vh_ref, lh_ref,
                       mur_ref, muk_ref, muv_ref, mul_ref, w0_ref, wup_ref, a0_ref, aup_ref, gup_ref,
                       kk_ref, ka_ref, rk_ref, bd_ref, qy_ref, mn_ref, g_ref, bonus_ref):
    first = pl.program_id(1) == 0
    c = RW_CHUNK
    r = _token_shift(r_ref[...], rh_ref[...], mur_ref[...], first)
    k = _token_shift(k_ref[...], kh_ref[...], muk_ref[...], first)
    v = _token_shift(v_ref[...], vh_ref[...], muv_ref[...], first)
    lo = _token_shift(l_ref[...], lh_ref[...], mul_ref[...], first)
    wd = lo[:, :DECAY_LORA]
    ad = lo[:, DECAY_LORA:DECAY_LORA + AAA_LORA]
    gd = lo[:, DECAY_LORA + AAA_LORA:RW_LORA]
    z = -(w0_ref[...] + _dot(jnp.tanh(wd), wup_ref[...], HI))
    w_log = -(jnp.maximum(z, 0.0) + jnp.log(1.0 + jnp.exp(-jnp.abs(z)))) - 0.5
    logw = -jnp.exp(w_log)
    a = jax.nn.sigmoid(a0_ref[...] + _dot(ad, aup_ref[...], HI))
    g_ref[0] = _dot(jax.nn.sigmoid(gd), gup_ref[...], HI)
    kkraw = k * kk_ref[...]
    k2 = k * (1.0 + (a - 1.0) * ka_ref[...])
    rkr = r * k2 * rk_ref[...]

    cum_all = logw
    trow = lax.broadcasted_iota(jnp.int32, logw.shape, 0) % c
    step = 1
    while step < c:
        cum_all = cum_all + jnp.where(trow >= step, pltpu.roll(cum_all, step, 0), 0.0)
        step *= 2

    ti = lax.broadcasted_iota(jnp.int32, (c, c), 0)
    si = lax.broadcasted_iota(jnp.int32, (c, c), 1)
    incl = ti >= si
    strict = ti > si
    eye = ti == si
    zeros = jnp.zeros((c, c), F32)
    bd = bd_ref[...]

    def head_sum(t):
        hi, lo = _split(t)
        return _dot(hi, bd) + _dot(lo, bd)

    kk = kkraw / jnp.maximum(jnp.sqrt(head_sum(kkraw * kkraw)), 1e-12)
    bonus_ref[0] = head_sum(rkr) * v
    nchunk = logw.shape[0] // c
    ends = [cum_all[(j + 1) * c - 1:(j + 1) * c, :] for j in range(nchunk)]
    cum_end = jnp.concatenate([jnp.broadcast_to(e, (c, e.shape[1])) for e in ends], axis=0)
    ginv = jnp.exp(-cum_all)
    gtail = jnp.exp(cum_end - cum_all)
    gend = jnp.exp(cum_end)
    kka = kk * a
    at = -kk * jnp.exp(cum_all - logw)
    bt = kka * ginv
    kt = k2 * ginv
    rt = r * jnp.exp(cum_all)
    bhat = kka * gtail
    khat = k2 * gtail

    heads = [(slice(j * c, (j + 1) * c), slice(h * RW_DH, (h + 1) * RW_DH))
             for j in range(nchunk) for h in range(RW_HEADS)]
    stack = lambda x, y, s: jnp.concatenate([x[s], y[s]], axis=0).astype(BF16)
    gram = [_dot_nt(stack(at, rt, s), stack(bt, kt, s)) for s in heads]
    a_ab = [jnp.where(strict, g[:c, :c], 0.0) for g in gram]
    a_ak = [jnp.where(strict, g[:c, c:], 0.0) for g in gram]
    incl2 = (lax.broadcasted_iota(jnp.int32, (c, 2 * c), 0)
             >= lax.broadcasted_iota(jnp.int32, (c, 2 * c), 1) % c)
    a_r = [jnp.where(incl2, g[c:, :], 0.0) for g in gram]
    p = a_ab
    tinv = [eye.astype(F32) + x for x in p]
    for _ in range(int(math.log2(c)) - 1):
        p = [_bdot(x, x) for x in p]
        tinv = [t + _bdot(t, x) for t, x in zip(tinv, p)]
    av = [_bdot(x, v[s]) for x, s in zip(a_ak, heads)]
    w12 = [_bdot(t, jnp.concatenate([at[s], x], axis=1)) for t, x, s in zip(tinv, av, heads)]
    zmat = [jnp.concatenate([w, jnp.concatenate([zeros, v[s]], axis=1)], axis=0).astype(BF16)
            for w, s in zip(w12, heads)]
    out1 = [_dot(x.astype(BF16), z) for x, z in zip(a_r, zmat)]
    out2 = [_dot_tn(stack(bhat, khat, s), z) for s, z in zip(heads, zmat)]
    qy = [o + jnp.concatenate([rt[s], zeros], axis=1) for o, s in zip(out1, heads)]
    for j in range(nchunk):
        qy_ref[0, j * c:(j + 1) * c, :] = jnp.concatenate(qy[j * RW_HEADS:(j + 1) * RW_HEADS], axis=1)
        for h in range(RW_HEADS):
            idx = j * RW_HEADS + h
            diag = jnp.where(eye, gend[heads[idx]], 0.0)
            mn_ref[0, j, h] = out2[idx] + jnp.concatenate([diag, zeros], axis=1)


def _rwkv_scan_kernel(qy_ref, mn_ref, g_ref, bonus_ref, gnw_ref, gnb_ref, bd_ref, y_ref, st_ref, *, batch):
    @pl.when(pl.program_id(0) == 0)
    def _():
        st_ref[...] = jnp.zeros_like(st_ref)

    bd = bd_ref[...]

    def head_mean(t):
        return _dot(t.astype(BF16), bd) * (1.0 / RW_DH)

    pairs = [(b, h) for b in range(batch) for h in range(RW_HEADS)]
    sts = [st_ref[b * RW_HEADS + h] for b, h in pairs]
    ys = [_bdot(qy_ref[b, :, 2 * RW_DH * h:2 * RW_DH * h + RW_DH], st)
          + qy_ref[b, :, 2 * RW_DH * h + RW_DH:2 * RW_DH * (h + 1)] for (b, h), st in zip(pairs, sts)]
    for (b, h), st in zip(pairs, sts):
        mn = mn_ref[b, 0, h]
        st_ref[b * RW_HEADS + h] = _dot3(mn[:, :RW_DH], st) + mn[:, RW_DH:]
    for b in range(batch):
        y = jnp.concatenate(ys[b * RW_HEADS:(b + 1) * RW_HEADS], axis=1)
        dev = y - head_mean(y)
        yn = dev * lax.rsqrt(head_mean(dev * dev) + RW_GN_EPS) * gnw_ref[...] + gnb_ref[...]
        y_ref[b] = ((yn + bonus_ref[b]) * g_ref[b]).astype(y_ref.dtype)


def _rwkv(proj, p, batch, seq):
    head_ones = jnp.asarray(np.kron(np.eye(RW_HEADS), np.ones((RW_DH, RW_DH))), BF16)
    c = RW_CHUNK
    nc = seq // c
    tb = RW_NB * c
    nt = seq // tb
    hb = tb // 8
    row512 = lambda col: pl.BlockSpec((tb, RW_WIDTH), lambda b, i: (b * nt + i, col))
    halo512 = lambda col: pl.BlockSpec((8, RW_WIDTH), lambda b, i: (jnp.maximum((b * nt + i) * hb - 1, 0), col))
    const = lambda shape: pl.BlockSpec(shape, lambda b, i: (0,) * len(shape))
    vec = const((1, RW_WIDTH))
    out_row = lambda w: pl.BlockSpec((1, tb, w), lambda b, i: (b, i, 0))
    qy, mn, g, bonus = pl.pallas_call(
        _rwkv_chunk_kernel,
        out_shape=(jax.ShapeDtypeStruct((batch, seq, 2 * RW_WIDTH), F32),
                   jax.ShapeDtypeStruct((batch, nc, RW_HEADS, RW_DH, 2 * RW_DH), F32),
                   jax.ShapeDtypeStruct((batch, seq, RW_WIDTH), F32),
                   jax.ShapeDtypeStruct((batch, seq, RW_WIDTH), F32)),
        grid=(batch, nt),
        in_specs=[row512(C_R // RW_WIDTH), row512(C_K // RW_WIDTH), row512(C_V // RW_WIDTH),
                  pl.BlockSpec((tb, LORA_PAD), lambda b, i: (b * nt + i, C_LORA // LORA_PAD)),
                  halo512(C_R // RW_WIDTH), halo512(C_K // RW_WIDTH), halo512(C_V // RW_WIDTH),
                  pl.BlockSpec((8, LORA_PAD), lambda b, i: (jnp.maximum((b * nt + i) * hb - 1, 0), C_LORA // LORA_PAD)),
                  vec, vec, vec, const((1, LORA_PAD)),
                  vec, const((DECAY_LORA, RW_WIDTH)), vec, const((AAA_LORA, RW_WIDTH)), const((GATE_LORA, RW_WIDTH)),
                  vec, vec, vec, const((RW_WIDTH, RW_WIDTH))],
        out_specs=(out_row(2 * RW_WIDTH),
                   pl.BlockSpec((1, RW_NB, RW_HEADS, RW_DH, 2 * RW_DH), lambda b, i: (b, i, 0, 0, 0)),
                   out_row(RW_WIDTH), out_row(RW_WIDTH)),
        compiler_params=_cp(("parallel", "parallel"), VMEM_LIMIT),
        name="rwkv_chunk",
    )(proj, proj, proj, proj, proj, proj, proj, proj,
      p["mu_r"], p["mu_k"], p["mu_v"], p["mu_l"], p["w0"], p["w_up"], p["a0"], p["a_up"], p["g_up"],
      p["k_k"], p["k_a"], p["r_k"], head_ones)

    full = lambda w: pl.BlockSpec((batch, c, w), lambda i: (0, i, 0))
    return pl.pallas_call(
        functools.partial(_rwkv_scan_kernel, batch=batch),
        out_shape=jax.ShapeDtypeStruct((batch, seq, RW_WIDTH), BF16),
        grid=(nc,),
        in_specs=[full(2 * RW_WIDTH),
                  pl.BlockSpec((batch, 1, RW_HEADS, RW_DH, 2 * RW_DH), lambda i: (0, i, 0, 0, 0)),
                  full(RW_WIDTH), full(RW_WIDTH),
                  pl.BlockSpec((1, RW_WIDTH), lambda i: (0, 0)), pl.BlockSpec((1, RW_WIDTH), lambda i: (0, 0)),
                  pl.BlockSpec((RW_WIDTH, RW_WIDTH), lambda i: (0, 0))],
        out_specs=full(RW_WIDTH),
        scratch_shapes=[pltpu.VMEM((batch * RW_HEADS, RW_DH, RW_DH), F32)],
        compiler_params=_cp(("arbitrary",), VMEM_LIMIT),
        name="rwkv_scan",
    )(qy, mn, g, bonus, p["gn_w"], p["gn_b"], head_ones).reshape(batch * seq, RW_WIDTH)


def _gelu_tanh(x):
    return 0.5 * x * (1.0 + jnp.tanh(math.sqrt(2.0 / math.pi) * (x + 0.044715 * (x * x * x))))


def _compress_kernel(x_ref, pos_ref, w1_ref, w2_ref, o_ref):
    half = CMP_STRIDE * NSA_DH
    x = x_ref[0, 0, 0]
    w1 = w1_ref[0]
    posb = _dot(pos_ref[0], w1)[0:1, :]
    h1 = _dot(x, w1[:half])
    h2 = _dot(x, w1[half:])
    n = h2.shape[0]
    row = lax.broadcasted_iota(jnp.int32, h2.shape, 0)
    h2s = jnp.where(row < n - 1, pltpu.roll(h2, n - 1, 0), 0.0)
    hid = _gelu_tanh(h1 + h2s + posb)
    o_ref[0, 0, 0] = _dot(hid.astype(BF16), w2_ref[0]).astype(o_ref.dtype)


def _compress(xkv, pos, w1, w2, batch, seq):
    nr = seq // CMP_STRIDE
    wide = CMP_STRIDE * NSA_DH
    return pl.pallas_call(
        _compress_kernel,
        out_shape=jax.ShapeDtypeStruct((2, batch, NSA_G, nr, NSA_DH), BF16),
        grid=(2, batch, NSA_G),
        in_specs=[pl.BlockSpec((1, 1, 1, nr, wide), lambda t, b, g: (t, b, g, 0, 0)),
                  pl.BlockSpec((1, 8, 2 * wide), lambda t, b, g: (t, 0, 0)),
                  pl.BlockSpec((1, 2 * wide, CMP_HIDDEN), lambda t, b, g: (t, 0, 0)),
                  pl.BlockSpec((1, CMP_HIDDEN, NSA_DH), lambda t, b, g: (t, 0, 0))],
        out_specs=pl.BlockSpec((1, 1, 1, nr, NSA_DH), lambda t, b, g: (t, b, g, 0, 0)),
        compiler_params=_cp(("parallel", "parallel", "parallel")),
        name="nsa_compress",
    )(xkv, pos, w1, w2)


def _lane_tile_fold(x, op, init):
    for t in range(x.shape[1] // 128):
        init = op(init, x[:, 128 * t:128 * (t + 1)])
    return init


def _nsa_kernel(q_ref, gate_ref, kc_ref, vc_ref, ks_ref, vs_ref, kw_ref, vw_ref,
                tblc_ref, tbln_ref, tblw_ref, ovt_ref, gexp_ref, o_ref, s_ref, *, ncmp):
    nb = NSA_NB
    blocks = [pl.program_id(2) * nb + u for u in range(nb)]
    rows = NSA_HG * QB
    each = lambda f, *ls: [f(*xs) for xs in zip(*ls)]
    qt = q_ref[...]
    qs = [jnp.concatenate([qt[u * QB:(u + 1) * QB, NSA_DH * h:NSA_DH * (h + 1)] for h in range(NSA_HG)], axis=0)
          * (NSA_DH ** -0.5 * LOG2E) for u in range(nb)]
    qb = each(lambda x: x.astype(BF16), qs)
    rmax = lambda x: jnp.max(x, axis=1, keepdims=True)
    rsum = lambda x: jnp.sum(x, axis=1, keepdims=True)

    win_w = WINDOW + QB
    kw = [kw_ref[0, 0, pl.ds(pl.multiple_of(i * QB, QB), win_w), :] for i in blocks]
    vw = [vw_ref[0, 0, pl.ds(pl.multiple_of(i * QB, QB), win_w), :] for i in blocks]
    flag = lax.broadcasted_iota(jnp.int32, (rows, kw_ref.shape[3] - NSA_DH), 1) == 0
    q_win = each(lambda x: jnp.concatenate([x, jnp.where(flag, NEG, 0.0)], axis=1).astype(BF16), qs)
    tblw = tblw_ref[0]
    pv = lambda x: x[:, :NSA_DH] / x[:, NSA_DH:NSA_DH + 1]
    s_w = each(lambda x, k: _dot_nt(x, k) + tblw, q_win, kw)
    m_w = each(rmax, s_w)
    e_w = each(lambda s, m: jnp.exp2(s - m), s_w, m_w)
    o_w = each(lambda e, v: pv(_dot(e.astype(BF16), v)), e_w, vw)

    kc = kc_ref[0, 0]
    vc = vc_ref[0, 0]
    tblc = tblc_ref[0]
    cidx = lax.broadcasted_iota(jnp.int32, (rows, ncmp), 1)
    qrow = lax.broadcasted_iota(jnp.int32, (rows, 1), 0) % QB
    lc = [jnp.where(cidx < (QB // CMP_STRIDE) * i + CMP_AHEAD,
                    _dot_nt(x, kc) + pltpu.roll(tblc, (4 * i - CMP_NEAR // 2 + ncmp) % ncmp, 1), NEG)
          for x, i in zip(qb, blocks)]
    m_c = each(rmax, lc)
    e_c = each(lambda s, m: jnp.exp2(s - m), lc, m_c)
    den_c = each(rsum, e_c)
    pc = [e * jnp.where(i * QB + qrow >= CMP_BLOCK - 1, 1.0 / d, 0.0) for e, d, i in zip(e_c, den_c, blocks)]
    o_c = each(lambda x: _dot(x.astype(BF16), vc), pc)
    pcs_hi, pcs_lo = _split(jnp.concatenate(
        each(lambda x: x[0:QB] + x[QB:2 * QB] + x[2 * QB:3 * QB] + x[3 * QB:4 * QB], pc), axis=0))
    ovt = ovt_ref[...]
    imp = _dot_nt(ovt, pcs_hi) + _dot_nt(ovt, pcs_lo)

    nslc = ovt.shape[0]
    nidx = lax.broadcasted_iota(jnp.int32, (nslc, nb * QB), 0)
    cur = blocks[0] + lax.broadcasted_iota(jnp.int32, (nslc, nb * QB), 1) // QB
    forced = (nidx == 0) | (nidx == cur) | (nidx == cur - 1)
    work = jnp.where(nidx <= cur, imp + jnp.where(forced, FORCE_BONUS, 0.0), -1.0)
    sel_all = jnp.zeros((nslc, nb * QB), F32)
    for _ in range(N_SELECT):
        m = jnp.max(work, axis=0, keepdims=True)
        first = jnp.min(jnp.where(work == m, nidx, nslc), axis=0, keepdims=True)
        pick = nidx == first
        sel_all = jnp.where(pick & (m >= 0.0), 1.0, sel_all)
        work = jnp.where(pick, -2.0, work)
    sel_t = sel_all.T

    ind_w = ks_ref.shape[3] - NSA_DH
    selq = [sel_t[u * QB:(u + 1) * QB] for u in range(nb)]
    if nslc < ind_w:
        selq = each(lambda x: jnp.concatenate([x, jnp.zeros((QB, ind_w - nslc), F32)], axis=1), selq)
    bidx = lax.broadcasted_iota(jnp.int32, (QB, ind_w), 1)
    tile4 = lambda t: jnp.concatenate([t] * NSA_HG, axis=0)
    with_mask = lambda x, keep: jnp.concatenate([x, tile4(jnp.where(keep, 0.0, NEG))], axis=1).astype(BF16)
    q_sel = each(lambda x, s: with_mask(x, s > 0.0), qs, selq)
    q_far = jnp.concatenate([with_mask(x, (s > 0.0) & (bidx <= i - NEAR_BLOCKS))
                             for x, s, i in zip(qs, selq, blocks)], axis=0)
    near_w = NEAR_BLOCKS * SLC_BLOCK
    pad_s = KV_PAD
    near0 = pad_s - (NEAR_BLOCKS - 1) * SLC_BLOCK
    kn = [ks_ref[0, 0, pl.ds(pl.multiple_of(near0 + i * QB, QB), near_w), :] for i in blocks]
    vn = [vs_ref[0, 0, pl.ds(pl.multiple_of(near0 + i * QB, QB), near_w), :] for i in blocks]
    tbln = tbln_ref[0]
    s_near = each(lambda x, k: _dot_nt(x, k) + tbln, q_sel, kn)
    m_near = jnp.concatenate(each(rmax, s_near), axis=0)

    far_w = FAR_CHUNK_BLOCKS * SLC_BLOCK
    n_far = jnp.maximum(blocks[-1] - (NEAR_BLOCKS - 1) + FAR_CHUNK_BLOCKS - 1, 0) // FAR_CHUNK_BLOCKS

    def far_logits(js, mvec):
        ss = [_dot_nt(q_far, ks_ref[0, 0, pl.ds(pl.multiple_of(pad_s + j * far_w, SLC_BLOCK), far_w), :])
              for j in js]
        for j, s in zip(js, ss):
            s_ref[:, pl.ds(pl.multiple_of(j * far_w, far_w), far_w)] = s
        for s in ss:
            mvec = _lane_tile_fold(s, jnp.maximum, mvec)
        return mvec

    n_pair = n_far // 2
    odd = n_far % 2 == 1
    mvec = lax.fori_loop(0, n_pair, lambda jp, m: far_logits([2 * jp, 2 * jp + 1], m),
                         jnp.full((nb * rows, 128), NEG, F32))
    mvec = lax.cond(odd, lambda m: far_logits([n_far - 1], m), lambda m: m, mvec)
    m_s = jnp.maximum(m_near, rmax(mvec))

    e_near = [jnp.exp2(s - m_s[u * rows:(u + 1) * rows]) for u, s in enumerate(s_near)]
    acc0 = jnp.concatenate(each(lambda e, v: _dot(e.astype(BF16), v), e_near, vn), axis=0)

    def far_values(js, acc):
        es = [jnp.exp2(s_ref[:, pl.ds(pl.multiple_of(j * far_w, far_w), far_w)] - m_s).astype(BF16) for j in js]
        for j, e in zip(js, es):
            acc = acc + _dot(e, vs_ref[0, 0, pl.ds(pl.multiple_of(pad_s + j * far_w, SLC_BLOCK), far_w), :])
        return acc

    acc = lax.fori_loop(0, n_pair, lambda jp, a: far_values([2 * jp, 2 * jp + 1], a), acc0)
    o_s = pv(lax.cond(odd, lambda a: far_values([n_far - 1], a), lambda a: a, acc))

    g_hi, g_lo = _split(jax.nn.sigmoid(gate_ref[0, 0]))
    gexp = gexp_ref[...]
    ge = _dot(g_hi, gexp) + _dot(g_lo, gexp)
    for u in range(nb):
        outs = []
        for h in range(NSA_HG):
            r0 = slice(h * QB, (h + 1) * QB)
            gt = lambda br: ge[u * QB:(u + 1) * QB, (3 * h + br) * 128:(3 * h + br) * 128 + NSA_DH]
            outs.append(gt(0) * o_c[u][r0] + gt(1) * o_s[u * rows + h * QB:u * rows + (h + 1) * QB]
                        + gt(2) * o_w[u][r0])
        o_ref[u * QB:(u + 1) * QB, :] = jnp.concatenate(outs, axis=1).astype(o_ref.dtype)


def _bias_tables(rel_bias, ncmp):
    def table(dist, keep, base):
        onehot = jnp.asarray(np.eye(N_BUCKETS, dtype=np.float32)[_t5_bucket_np(dist)])
        tbl = jnp.einsum("qkb,bh->qkh", onehot, rel_bias, precision=HI)
        tbl = (tbl - base) * LOG2E
        tbl = jnp.where(jnp.asarray(keep)[..., None], tbl, NEG)
        k = dist.shape[1]
        return tbl.transpose(2, 0, 1).reshape(NSA_G, NSA_HG * QB, k)

    far = rel_bias[N_BUCKETS - 1]
    qi = np.arange(QB)[:, None]
    dist_c = qi - CMP_STRIDE * (np.arange(CMP_NEAR)[None, :] - CMP_NEAR // 2) - (CMP_BLOCK - 1)
    tblc = table(dist_c, dist_c >= 0, far)
    tblc = tblc * jnp.asarray(np.arange(CMP_NEAR) < CMP_NEAR // 2 + CMP_AHEAD, F32)
    tblc = jnp.pad(tblc, ((0, 0), (0, 0), (0, ncmp - CMP_NEAR)))
    jn = np.arange(NEAR_BLOCKS * SLC_BLOCK)[None, :]
    dist_n = (NEAR_BLOCKS - 1) * SLC_BLOCK + qi - jn
    tbln = table(dist_n, dist_n >= 0, far)
    jw = np.arange(WINDOW + QB)[None, :]
    dist_w = WINDOW + qi - jw
    tblw = table(dist_w, (dist_w >= 0) & (dist_w < WINDOW), 0.0)
    return tblc, tbln, tblw


def _nsa(proj, gate_logits, kvc, ks, vs, kw, vw, tables, batch, seq):
    nq = seq // QB
    ncmp = seq // CMP_STRIDE
    nslc = seq // SLC_BLOCK
    tblc, tbln, tblw = tables
    cstart = np.arange(ncmp) * CMP_STRIDE
    sstart = np.arange(nslc) * SLC_BLOCK
    overlap_t = ((cstart[None, :] <= sstart[:, None] + SLC_BLOCK - 1)
                 & (cstart[None, :] + CMP_BLOCK - 1 >= sstart[:, None])
                 & (cstart[None, :] + CMP_BLOCK <= seq)).astype(np.float32)
    rows = NSA_HG * QB
    n_gate = 3 * NSA_HG
    gate_spread = (np.arange(n_gate * 128)[None, :] // 128 == np.arange(n_gate)[:, None]) \
        & (np.arange(n_gate * 128)[None, :] % 128 < NSA_DH)
    kv_spec = lambda t: pl.BlockSpec((1, 1) + t.shape[2:], lambda b, g, i: (b, g, 0, 0))
    tbl_spec = lambda k: pl.BlockSpec((1, rows, k), lambda b, g, i: (g, 0, 0))
    qcol = C_Q // (NSA_HG * NSA_DH)
    return pl.pallas_call(
        functools.partial(_nsa_kernel, ncmp=ncmp),
        out_shape=jax.ShapeDtypeStruct((batch * seq, NSA_HEADS * NSA_DH), BF16),
        grid=(batch, NSA_G, nq // NSA_NB),
        in_specs=[pl.BlockSpec((NSA_NB * QB, NSA_HG * NSA_DH), lambda b, g, i: (b * (nq // NSA_NB) + i, qcol + g)),
                  pl.BlockSpec((1, 1, NSA_NB * QB, 3 * NSA_HG), lambda b, g, i: (b, g, i, 0)),
                  pl.BlockSpec((1, 1, ncmp, NSA_DH), lambda b, g, i: (b, g, 0, 0)),
                  pl.BlockSpec((1, 1, ncmp, NSA_DH), lambda b, g, i: (b, g, 0, 0)),
                  kv_spec(ks), kv_spec(vs), kv_spec(kw), kv_spec(vw),
                  tbl_spec(ncmp), tbl_spec(NEAR_BLOCKS * SLC_BLOCK), tbl_spec(WINDOW + QB),
                  pl.BlockSpec((nslc, ncmp), lambda b, g, i: (0, 0)),
                  pl.BlockSpec(gate_spread.shape, lambda b, g, i: (0, 0))],
        out_specs=pl.BlockSpec((NSA_NB * QB, NSA_HG * NSA_DH), lambda b, g, i: (b * (nq // NSA_NB) + i, g)),
        scratch_shapes=[pltpu.VMEM((NSA_NB * rows, seq), F32)],
        compiler_params=_cp(("parallel", "parallel", "arbitrary"), VMEM_LIMIT),
        name="nsa_attention",
    )(proj, gate_logits, kvc[0], kvc[1], ks, vs, kw, vw, tblc, tbln, tblw, jnp.asarray(overlap_t, BF16),
      jnp.asarray(gate_spread, BF16))


def _merge_kernel(h_ref, yp_ref, yr_ref, yn_ref, x_ref, wbp_ref, wbr_ref, wbn_ref, wm_ref, bm_ref, wo_ref,
                  gn_ref, wr_ref, br_ref, xo_ref, h2_ref, route_ref):
    d = D_MODEL
    gl = jax.nn.sigmoid(_dot(h_ref[...], wm_ref[...]) + bm_ref[...])
    merged = (gl[:, :d] * _dot(yp_ref[...], wbp_ref[...]) + gl[:, d:2 * d] * _dot(yr_ref[...], wbr_ref[...])
              + gl[:, 2 * d:] * _dot(yn_ref[...], wbn_ref[...]))
    x = x_ref[...] + _dot(merged.astype(BF16), wo_ref[...])
    xo_ref[...] = x
    h2 = _rms(x, gn_ref[...])
    h2_ref[...] = h2.astype(h2_ref.dtype)

    h2_hi, h2_lo = _split(h2)
    nl = br_ref.shape[1]
    both = _dot(h2_hi, wr_ref[...])
    logits = both[:, :nl] + (both[:, nl:] + _dot(h2_lo, wr_ref[:, :nl])) + br_ref[...]
    lane = lax.broadcasted_iota(jnp.int32, logits.shape, 1)
    big = logits.shape[1]
    lg = jnp.where(lane < N_GROUPS, logits, NEG)
    mg = jnp.max(lg, axis=1, keepdims=True)
    p_top = 1.0 / jnp.sum(jnp.exp(lg - mg), axis=1, keepdims=True)
    grp = jnp.min(jnp.where(lg == mg, lane, big), axis=1, keepdims=True)
    lo = N_GROUPS + EPG * grp
    le = jnp.where((lane >= lo) & (lane < lo + EPG), logits, NEG)
    e1 = jnp.max(le, axis=1, keepdims=True)
    i1 = jnp.min(jnp.where(le == e1, lane, big), axis=1, keepdims=True)
    le = jnp.where(lane == i1, NEG, le)
    e2 = jnp.max(le, axis=1, keepdims=True)
    i2 = jnp.min(jnp.where(le == e2, lane, big), axis=1, keepdims=True)
    t = jnp.exp(e2 - e1)
    w1 = p_top / (1.0 + t)
    w2 = p_top * t / (1.0 + t)
    route_ref[...] = jnp.where(lane == 0, (i1 - N_GROUPS).astype(F32),
                               jnp.where(lane == 1, (i2 - N_GROUPS).astype(F32),
                                         jnp.where(lane == 2, w1, jnp.where(lane == 3, w2, 0.0))))


def _merge(h, y_pool, y_rwkv, y_nsa, x, p, tm=256):
    n, d = x.shape
    row = lambda w: pl.BlockSpec((tm, w), lambda i: (i, 0))
    const = lambda a: pl.BlockSpec(a.shape, lambda i: (0, 0))
    ws = [p["wb_pool"], p["wb_rwkv"], p["wb_nsa"], p["w_merge"], p["b_merge"], p["w_out"],
          p["norm_ffn"], p["w_router"], p["b_router"]]
    return pl.pallas_call(
        _merge_kernel,
        out_shape=(jax.ShapeDtypeStruct((n, d), F32), jax.ShapeDtypeStruct((n, d), F32),
                   jax.ShapeDtypeStruct((n, 128), F32)),
        grid=(n // tm,),
        in_specs=[row(d), row(POOL_WIDTH), row(RW_WIDTH), row(d), row(d)] + [const(w) for w in ws],
        out_specs=(row(d), row(d), row(128)),
        compiler_params=_cp(("parallel",), VMEM_LIMIT),
        name="merge_router",
    )(h, y_pool, y_rwkv, y_nsa, x, *ws)


def _expert_kernel(te_ref, nt_ref, tok_ref, h_hbm, w_ref, wg_ref, wu_ref, wd_ref, o_ref,
                   xbuf, sem, wg_s, wu_s, wd_s):
    i = pl.program_id(0)
    tm = MOE_TM
    n_tiles = nt_ref[0]

    def row_copy(tile, slot, r):
        return pltpu.make_async_copy(h_hbm.at[pl.ds(tok_ref[tile * tm + r], 1), :],
                                     xbuf.at[slot, pl.ds(r, 1), :], sem.at[slot])

    def tile_wait(slot):
        pltpu.make_async_copy(h_hbm.at[pl.ds(0, tm), :], xbuf.at[slot], sem.at[slot]).wait()

    @pl.when((i == 0) & (n_tiles > 0))
    def _():
        def body(r, carry):
            row_copy(0, 0, r).start()
            return carry
        lax.fori_loop(0, tm, body, 0, unroll=8)

    @pl.when((i == 0) | (te_ref[i] != te_ref[jnp.maximum(i - 1, 0)]))
    def _():
        wg_s[...] = wg_ref[0, 0].astype(BF16)
        wu_s[...] = wu_ref[0, 0].astype(BF16)
        wd_s[...] = wd_ref[0, 0].astype(BF16)

    @pl.when(i < n_tiles)
    def _():
        slot = i % 2
        quarter = tm // 4

        def fetch_next(part):
            for r in range(part * quarter, (part + 1) * quarter):
                row_copy(i + 1, 1 - slot, r).start()

        tile_wait(slot)
        xb = xbuf[slot].astype(BF16)
        fetch_next(0)
        gate = _dot(xb, wg_s[...])
        fetch_next(1)
        up = _dot(xb, wu_s[...])
        fetch_next(2)
        hid = (gate * jax.nn.sigmoid(gate) * up).astype(BF16)
        fetch_next(3)
        o_ref[...] = w_ref[...] * _dot(hid, wd_s[...])

    @pl.when((i == n_tiles) & (n_tiles > 0))
    def _():
        tile_wait(i % 2)

    @pl.when(i >= nt_ref[0])
    def _():
        o_ref[...] = jnp.zeros_like(o_ref)


def _experts(h2, rowtok, roww, tile_expert, n_tiles, layer, wg, wu, wd):
    d = h2.shape[1]
    r = rowtok.shape[0]
    tm = MOE_TM
    return pl.pallas_call(
        _expert_kernel,
        out_shape=jax.ShapeDtypeStruct((r, d), F32),
        grid_spec=pltpu.PrefetchScalarGridSpec(
            num_scalar_prefetch=3,
            grid=(r // tm,),
            in_specs=[pl.BlockSpec(memory_space=pl.ANY),
                      pl.BlockSpec((tm, 1), lambda i, te, nt, tok: (i, 0)),
                      pl.BlockSpec((1, 1, d, D_EXPERT), lambda i, te, nt, tok: (layer, te[i], 0, 0)),
                      pl.BlockSpec((1, 1, d, D_EXPERT), lambda i, te, nt, tok: (layer, te[i], 0, 0)),
                      pl.BlockSpec((1, 1, D_EXPERT, d), lambda i, te, nt, tok: (layer, te[i], 0, 0))],
            out_specs=pl.BlockSpec((tm, d), lambda i, te, nt, tok: (i, 0)),
            scratch_shapes=[pltpu.VMEM((2, tm, d), F32), pltpu.SemaphoreType.DMA((2,)),
                            pltpu.VMEM((d, D_EXPERT), BF16), pltpu.VMEM((d, D_EXPERT), BF16),
                            pltpu.VMEM((D_EXPERT, d), BF16)]),
        compiler_params=_cp(("arbitrary",), VMEM_LIMIT),
        name="moe_experts",
    )(tile_expert, n_tiles, rowtok, h2, roww, wg, wu, wd)


def _moe(h2, route, layer, wg, wu, wd):
    n = h2.shape[0]
    tm = MOE_TM
    r = 2 * n + (N_EXPERTS + 1) * tm
    ids = route[:, 0:2].astype(jnp.int32).reshape(-1)
    wts = route[:, 2:4].reshape(-1)
    onehot = (ids[:, None] == jnp.arange(N_EXPERTS)[None, :]).astype(jnp.int32)
    rank = jnp.sum((jnp.cumsum(onehot, axis=0) - onehot) * onehot, axis=1)
    counts = jnp.sum(onehot, axis=0)
    tiles = (counts + tm - 1) // tm
    tile_end = jnp.cumsum(tiles)
    starts = (tile_end - tiles) * tm
    pos = starts[ids] + rank
    row_assign = jnp.full((r,), -1, jnp.int32).at[pos].set(
        jnp.arange(2 * n, dtype=jnp.int32), unique_indices=True, mode="promise_in_bounds")
    rowtok = jnp.maximum(row_assign, 0) // 2
    roww = jnp.where(row_assign >= 0, wts[jnp.maximum(row_assign, 0)], 0.0)
    n_tiles = tile_end[-1:].astype(jnp.int32)
    tile_expert = jnp.minimum(jnp.sum(tile_end[None, :] <= jnp.arange(r // tm)[:, None], axis=1),
                              N_EXPERTS - 1).astype(jnp.int32)
    ys = _experts(h2, rowtok, roww.reshape(r, 1), tile_expert, n_tiles, layer, wg, wu, wd)
    return ys[pos[0::2]], ys[pos[1::2]]


def _layer_params(l, a):
    f = lambda t: t[l]
    row = lambda t: t[l].reshape(1, -1)
    w_in = a["w_in"][l]
    w_in_p = jnp.concatenate([w_in[:, :SRC_RW_END], jnp.zeros((D_MODEL, C_Q - SRC_RW_END), F32),
                              w_in[:, SRC_Q:SRC_KV], w_in[:, SRC_GATE:],
                              jnp.zeros((D_MODEL, P_COLS - C_GATE - (w_in.shape[1] - SRC_GATE)), F32)], axis=1)
    mu = a["rw_mu"][l]
    wb = a["w_branch"][l].astype(BF16)
    w_router = jnp.zeros((D_MODEL, 128), F32)
    w_router = w_router.at[:, :N_GROUPS].set(a["w_router_grp"][l]).at[:, N_GROUPS:N_GROUPS + N_EXPERTS].set(
        a["w_router_exp"][l])
    b_router = jnp.zeros((1, 128), F32)
    b_router = b_router.at[0, :N_GROUPS].set(a["b_router_grp"][l]).at[0, N_GROUPS:N_GROUPS + N_EXPERTS].set(
        a["b_router_exp"][l])
    pos = jnp.stack([a["cmp_pos_k"][l].reshape(-1), a["cmp_pos_v"][l].reshape(-1)])
    return {
        "w_in": w_in_p.astype(BF16), "w_kv": w_in[:, SRC_KV:SRC_GATE].astype(BF16),
        "pool_w": f(a["pool_w"]), "pool_scale": f(a["pool_scale"]),
        "mu_r": mu[None, 0:RW_WIDTH], "mu_k": mu[None, RW_WIDTH:2 * RW_WIDTH],
        "mu_v": mu[None, 2 * RW_WIDTH:3 * RW_WIDTH],
        "mu_l": jnp.concatenate([mu[3 * RW_WIDTH:], jnp.zeros((LORA_PAD - RW_LORA,), F32)])[None],
        "w0": row(a["rw_w0"]), "w_up": f(a["rw_w_up"]), "a0": row(a["rw_a0"]), "a_up": f(a["rw_a_up"]),
        "g_up": f(a["rw_g_up"]), "k_k": row(a["rw_k_k"]), "k_a": row(a["rw_k_a"]), "r_k": row(a["rw_r_k"]),
        "gn_w": row(a["rw_gn_w"]), "gn_b": row(a["rw_gn_b"]),
        "cmp_pos": jnp.broadcast_to(pos[:, None, :], (2, 8, pos.shape[1])).astype(BF16),
        "cmp_w1": jnp.stack([a["cmp_w1_k"][l], a["cmp_w1_v"][l]]).astype(BF16),
        "cmp_w2": jnp.stack([a["cmp_w2_k"][l], a["cmp_w2_v"][l]]).astype(BF16),
        "wb_pool": wb[:POOL_WIDTH], "wb_rwkv": wb[POOL_WIDTH:POOL_WIDTH + RW_WIDTH],
        "wb_nsa": wb[POOL_WIDTH + RW_WIDTH:],
        "w_merge": a["w_merge"][l].astype(BF16), "b_merge": row(a["b_merge"]),
        "w_out": a["w_out"][l].astype(BF16), "norm_ffn": row(a["norm_ffn"]),
        "w_router": jnp.concatenate(_split(w_router), axis=1), "b_router": b_router,
    }


def _mixers(proj, kv, p, tables, batch, seq):
    kv_cmp, ks, vs, kw, vw = kv
    y_pool = _pool(proj, p["pool_w"], p["pool_scale"], batch, seq)
    y_rwkv = _rwkv(proj, p, batch, seq)
    kvc = _compress(kv_cmp, p["cmp_pos"], p["cmp_w1"], p["cmp_w2"], batch, seq)
    gate_logits = proj[:, C_GATE:C_GATE + 3 * NSA_HEADS].reshape(batch, seq, NSA_G, 3 * NSA_HG).transpose(0, 2, 1, 3)
    y_nsa = _nsa(proj, gate_logits, kvc, ks, vs, kw, vw, tables, batch, seq)
    return y_pool, y_rwkv, y_nsa


def kernel(x, rel_bias, norm_mix, w_in, pool_w, pool_scale, rw_mu, rw_w0, rw_w_up, rw_a0, rw_a_up, rw_g_up, rw_k_k, rw_k_a, rw_r_k, rw_gn_w, rw_gn_b, cmp_pos_k, cmp_w1_k, cmp_w2_k, cmp_pos_v, cmp_w1_v, cmp_w2_v, w_branch, w_merge, b_merge, w_out, norm_ffn, w_router_grp, b_router_grp, w_router_exp, b_router_exp, w_exp_gate, w_exp_up, w_exp_down, norm_final):
    a = dict(w_in=w_in, pool_w=pool_w, pool_scale=pool_scale, rw_mu=rw_mu, rw_w0=rw_w0, rw_w_up=rw_w_up,
             rw_a0=rw_a0, rw_a_up=rw_a_up, rw_g_up=rw_g_up, rw_k_k=rw_k_k, rw_k_a=rw_k_a, rw_r_k=rw_r_k,
             rw_gn_w=rw_gn_w, rw_gn_b=rw_gn_b, cmp_pos_k=cmp_pos_k, cmp_w1_k=cmp_w1_k, cmp_w2_k=cmp_w2_k,
             cmp_pos_v=cmp_pos_v, cmp_w1_v=cmp_w1_v, cmp_w2_v=cmp_w2_v, w_branch=w_branch, w_merge=w_merge,
             b_merge=b_merge, w_out=w_out, norm_ffn=norm_ffn, w_router_grp=w_router_grp,
             b_router_grp=b_router_grp, w_router_exp=w_router_exp, b_router_exp=b_router_exp,
             w_exp_gate=w_exp_gate, w_exp_up=w_exp_up, w_exp_down=w_exp_down)
    batch, seq, d = x.shape
    depth = norm_mix.shape[0]
    tables = _bias_tables(rel_bias, seq // CMP_STRIDE)
    xf = x.reshape(batch * seq, d)
    h = _norm(xf, norm_mix[0], BF16)
    for l in range(depth):
        p = _layer_params(l, a)
        proj = _matmul(h, p["w_in"])
        kv = _kv_proj(h, p["w_kv"], batch, seq)
        y_pool, y_rwkv, y_nsa = _mixers(proj, kv, p, tables, batch, seq)
        xf, h2, route = _merge(h, y_pool, y_rwkv, y_nsa, xf, p)
        y1, y2 = _moe(h2, route, l, w_exp_gate, w_exp_up, w_exp_down)
        last = l == depth - 1
        g_next = norm_final if last else norm_mix[l + 1]
        xf, h = _add_norm(xf, y1, y2, g_next, F32 if last else BF16)
    return h.reshape(batch, seq, d)
```
